```python
import jax
import jax.numpy as jnp
from jax import lax
import numpy as np

D_MODEL = 1024
BATCH = 2
SEQ = 8192
DEPTH = 4

GRID_W = 64
CTX_LEN = 256
RET_HEADS = 4
RET_DK = 64
RET_DV = 128
RET_CHUNK = 128
RET_QK_W = RET_HEADS * RET_DK
RET_V_W = RET_HEADS * RET_DV
NA_HEADS = 8
NA_DH = 64
NA_KH = 8
NA_KW = 16
NA_W = NA_HEADS * NA_DH
MLA_HEADS = 8
MLA_Q_RANK = 256
MLA_KV_RANK = 256
MLA_NOPE = 64
MLA_ROPE = 32
MLA_DV = 64
MLA_V_W = MLA_HEADS * MLA_DV
D_FF = 4 * D_MODEL
ROPE_BASE = 10000.0
Q_BLOCK = 128
EPS = 1e-5
DEEPNORM_ALPHA = (2 * DEPTH) ** 0.25
DEEPNORM_BETA = (8 * DEPTH) ** -0.25
IN_SPLITS = (RET_QK_W, RET_QK_W, RET_V_W, RET_V_W, RET_V_W, NA_W, NA_W, NA_W,
             MLA_Q_RANK, MLA_KV_RANK, MLA_ROPE, D_MODEL, D_MODEL, D_MODEL)
IN_WIDTH = sum(IN_SPLITS)

kernel_name = 'hybrid_retention_natten_mla_dit_block'


def layer_norm(x, gain, bias):
    xf = x.astype(jnp.float32)
    mu = jnp.mean(xf, axis=-1, keepdims=True)
    var = jnp.mean(jnp.square(xf - mu), axis=-1, keepdims=True)
    return ((xf - mu) * lax.rsqrt(var + EPS) * gain + bias).astype(x.dtype)


def rms_norm(x, gain):
    xf = x.astype(jnp.float32)
    return (xf * lax.rsqrt(jnp.mean(jnp.square(xf), axis=-1, keepdims=True) + EPS) * gain).astype(x.dtype)


def to_heads(a, n_heads):
    b, t, _ = a.shape
    return a.reshape(b, t, n_heads, -1).transpose(0, 2, 1, 3)


def merge_heads(a):
    b, h, t, d = a.shape
    return a.transpose(0, 2, 1, 3).reshape(b, t, h * d)


def axial_rope(n_tok, rot_dim):
    t = jnp.arange(n_tok)
    row = (t // GRID_W).astype(jnp.float32)
    col = (t % GRID_W).astype(jnp.float32)
    n_freq = rot_dim // 4
    inv_freq = ROPE_BASE ** (-2.0 * jnp.arange(n_freq, dtype=jnp.float32) / (rot_dim // 2))
    ang = jnp.concatenate([row[:, None] * inv_freq, col[:, None] * inv_freq], axis=-1)
    return jnp.cos(ang), jnp.sin(ang)


def apply_rope(x, rope):
    cos, sin = rope
    shape = (1, cos.shape[0]) + (1,) * (x.ndim - 3) + (cos.shape[1],)
    cos = cos.reshape(shape).astype(x.dtype)
    sin = sin.reshape(shape).astype(x.dtype)
    x1, x2 = jnp.split(x, 2, axis=-1)
    return jnp.concatenate([x1 * cos - x2 * sin, x1 * sin + x2 * cos], axis=-1)


def attend_blocks(q, k, v):
    b, h, t, d = q.shape
    nb = t // Q_BLOCK
    qb = jnp.moveaxis(q.reshape(b, h, nb, Q_BLOCK, d), 2, 0)

    def one_block(qi):
        s = jnp.einsum('bhqd,bhkd->bhqk', qi, k).astype(jnp.float32)
        p = jax.nn.softmax(s, axis=-1).astype(v.dtype)
        return jnp.einsum('bhqk,bhkv->bhqv', p, v)

    o = lax.map(one_block, qb)
    return jnp.moveaxis(o, 0, 2).reshape(b, h, t, v.shape[-1])


def retention_chunkwise(q, k, v, log_gamma, s0):
    b, h, t, dk = q.shape
    dv = v.shape[-1]
    n = t // RET_CHUNK
    qc = q.reshape(b, h, n, RET_CHUNK, dk)
    kc = k.reshape(b, h, n, RET_CHUNK, dk)
    vc = v.reshape(b, h, n, RET_CHUNK, dv)
    pos = jnp.arange(RET_CHUNK, dtype=jnp.float32)
    lg = log_gamma[:, None]
    diff = pos[:, None] - pos[None, :]
    decay_mat = jnp.where(diff >= 0, jnp.exp(lg[:, :, None] * jnp.maximum(diff, 0.0)), 0.0)
    q_decay = jnp.exp(lg * (pos + 1.0))
    k_decay = jnp.exp(lg * (RET_CHUNK - 1.0 - pos))
    chunk_decay = jnp.exp(log_gamma * RET_CHUNK)
    scores = jnp.einsum('bhncd,bhnsd->bhncs', qc, kc) * decay_mat[None, :, None]
    inner = jnp.einsum('bhncs,bhnsv->bhncv', scores, vc)
    kv_chunk = jnp.einsum('bhnsd,bhnsv->nbhdv', kc * k_decay[None, :, None, :, None], vc)

    def step(s, kv):
        return s * chunk_decay[None, :, None, None] + kv, s

    s_final, s_prev = lax.scan(step, s0, kv_chunk)
    cross = jnp.einsum('bhncd,nbhdv->bhncv', qc * q_decay[None, :, None, :, None], s_prev)
    return (inner + cross).reshape(b, h, t, dv), s_final


def retention_final_state(k, v, log_gamma):
    t = k.shape[2]
    w = jnp.exp(log_gamma[:, None] * (t - 1.0 - jnp.arange(t, dtype=jnp.float32)))
    return jnp.einsum('bhtd,bhtv->bhdv', k * w[None, :, :, None], v)


def head_group_norm(o, gain):
    of = o.astype(jnp.float32)
    mu = jnp.mean(of, axis=-1, keepdims=True)
    var = jnp.mean(jnp.square(of - mu), axis=-1, keepdims=True)
    return merge_heads((of - mu) * lax.rsqrt(var + EPS)) * gain


def retention_qkv(pq, pk, pv, rope):
    b, t, _ = pq.shape
    q = pq.reshape(b, t, RET_HEADS, RET_DK)
    k = pk.reshape(b, t, RET_HEADS, RET_DK) * (RET_DK ** -0.5)
    if rope is not None:
        q = apply_rope(q, rope)
        k = apply_rope(k, rope)
    return q.transpose(0, 2, 1, 3), k.transpose(0, 2, 1, 3), to_heads(pv, RET_HEADS)


def retention_branch(parts_x, parts_z, p, rope, need_ctx):
    log_gamma = jnp.log1p(-jnp.exp(p['ret_log_decay'].astype(jnp.float32)))
    qx, kx, vx = retention_qkv(parts_x[0], parts_x[1], parts_x[2], rope)
    qz, kz, vz = retention_qkv(parts_z[0], parts_z[1], parts_z[2], None)
    s_zero = jnp.zeros((qx.shape[0], RET_HEADS, RET_DK, RET_DV), jnp.float32)
    out_x, out_z = [], []
    for d in range(2):
        f = (lambda a: a) if d == 0 else (lambda a: jnp.flip(a, axis=2))
        if need_ctx:
            o_z, s_ctx = retention_chunkwise(f(qz), f(kz), f(vz), log_gamma[d], s_zero)
            out_z.append(f(o_z))
        else:
            s_ctx = retention_final_state(f(kz), f(vz), log_gamma[d])
        o_x, _ = retention_chunkwise(f(qx), f(kx), f(vx), log_gamma[d], s_ctx)
        out_x.append(f(o_x))

    def gated(outs, g_fwd, g_bwd):
        return (jax.nn.silu(g_fwd) * head_group_norm(outs[0], p['ret_gn_gain'])
                + jax.nn.silu(g_bwd) * head_group_norm(outs[1], p['ret_gn_gain']))

    y_x = gated(out_x, parts_x[3], parts_x[4])
    y_z = gated(out_z, parts_z[3], parts_z[4]) if need_ctx else None
    return y_x, y_z


def neighborhood_attention(q, k, v, k_ctx, v_ctx, rpb):
    b, h, t, d = q.shape
    rows = t // GRID_W
    kh = min(NA_KH, rows)
    kg = k.reshape(b, h, rows, GRID_W, d)
    vg = v.reshape(b, h, rows, GRID_W, d)
    qg = jnp.moveaxis(q.reshape(b, h, rows, GRID_W, d), 2, 0)
    cols = np.arange(GRID_W)
    col_start = np.clip(cols - NA_KW // 2, 0, GRID_W - NA_KW)
    col_idx = col_start[:, None] + np.arange(NA_KW)[None, :]
    dc_idx = col_idx - cols[:, None] + (NA_KW - 1)
    n_loc = kh * NA_KW

    def one_row(args):
        r, q_row = args
        r0 = jnp.clip(r - kh // 2, 0, rows - kh)
        k_rows = lax.dynamic_slice_in_dim(kg, r0, kh, axis=2)
        v_rows = lax.dynamic_slice_in_dim(vg, r0, kh, axis=2)
        k_win = k_rows[:, :, :, col_idx]
        v_win = v_rows[:, :, :, col_idx]
        dr_idx = r0 + jnp.arange(kh) - r + (NA_KH - 1)
        bias = rpb[:, dr_idx[:, None, None], dc_idx[None, :, :]]
        s_loc = jnp.einsum('bhwd,bhawkd->bhwak', q_row, k_win) + jnp.transpose(bias, (0, 2, 1, 3))[None]
        s_loc = s_loc.reshape(b, h, GRID_W, n_loc)
        s_ctx = jnp.einsum('bhwd,bhld->bhwl', q_row, k_ctx)
        s_all = jnp.concatenate([s_loc.astype(jnp.float32), s_ctx.astype(jnp.float32)], axis=-1)
        prob = jax.nn.softmax(s_all, axis=-1).astype(v.dtype)
        p_loc = prob[..., :n_loc].reshape(b, h, GRID_W, kh, NA_KW)
        p_ctx = prob[..., n_loc:]
        return (jnp.einsum('bhwak,bhawkv->bhwv', p_loc, v_win)
                + jnp.einsum('bhwl,bhlv->bhwv', p_ctx, v_ctx))

    out = lax.map(one_row, (jnp.arange(rows), qg))
    return jnp.moveaxis(out, 0, 2).reshape(b, h, t, d)


def na_branch(parts_x, parts_z, p, need_ctx):
    scale = NA_DH ** -0.5
    qx, kx, vx = [to_heads(a, NA_HEADS) for a in parts_x]
    qz, kz, vz = [to_heads(a, NA_HEADS) for a in parts_z]
    y_x = merge_heads(neighborhood_attention(qx * scale, kx, vx, kz, vz, p['na_rpb']))
    y_z = merge_heads(attend_blocks(qz * scale, kz, vz)) if need_ctx else None
    return y_x, y_z


def mla_project(pq, pkv, pkr, p, rope):
    b, t, _ = pq.shape
    q = (rms_norm(pq, p['mla_q_norm']) @ p['mla_w_qup']).reshape(b, t, MLA_HEADS, MLA_NOPE + MLA_ROPE)
    kv = (rms_norm(pkv, p['mla_kv_norm']) @ p['mla_w_kvup']).reshape(b, t, MLA_HEADS, MLA_NOPE + MLA_DV)
    q_nope, q_rope = q[..., :MLA_NOPE], q[..., MLA_NOPE:]
    k_nope, v = kv[..., :MLA_NOPE], kv[..., MLA_NOPE:]
    k_rope = pkr
    if rope is not None:
        q_rope = apply_rope(q_rope, rope)
        k_rope = apply_rope(k_rope, rope)
    q = jnp.concatenate([q_nope, q_rope], axis=-1) * ((MLA_NOPE + MLA_ROPE) ** -0.5)
    k = jnp.concatenate([k_nope, jnp.broadcast_to(k_rope[:, :, None, :], (b, t, MLA_HEADS, MLA_ROPE))], axis=-1)
    return q.transpose(0, 2, 1, 3), k.transpose(0, 2, 1, 3), v.transpose(0, 2, 1, 3)


def mla_branch(parts_x, parts_z, p, rope, need_ctx):
    qx, kx, vx = mla_project(parts_x[0], parts_x[1], parts_x[2], p, rope)
    qz, kz, vz = mla_project(parts_z[0], parts_z[1], parts_z[2], p, None)
    k_all = jnp.concatenate([kx, kz], axis=2)
    v_all = jnp.concatenate([vx, vz], axis=2)
    y_x = merge_heads(attend_blocks(qx, k_all, v_all))
    y_z = merge_heads(attend_blocks(qz, kz, vz)) if need_ctx else None
    return y_x, y_z


def token_mixers(hx, hz, p, rope_ret, rope_mla, need_ctx):
    offsets = np.cumsum(IN_SPLITS)[:-1].tolist()
    px = jnp.split(hx @ p['w_in'], offsets, axis=-1)
    pz = jnp.split(hz @ p['w_in'], offsets, axis=-1)
    ya_x, ya_z = retention_branch(px[0:5], pz[0:5], p, rope_ret, need_ctx)
    yb_x, yb_z = na_branch(px[5:8], pz[5:8], p, need_ctx)
    yc_x, yc_z = mla_branch(px[8:11], pz[8:11], p, rope_mla, need_ctx)

    def merge(parts, ya, yb, yc):
        y = (jax.nn.sigmoid(parts[11]) * (ya @ p['w_branch_ret'])
             + jax.nn.sigmoid(parts[12]) * (yb @ p['w_branch_na'])
             + jax.nn.sigmoid(parts[13]) * (yc @ p['w_branch_mla']))
        return y @ p['w_out']

    mix_x = merge(px, ya_x, yb_x, yc_x)
    mix_z = merge(pz, ya_z, yb_z, yc_z) if need_ctx else None
    return mix_x, mix_z


def sq_relu_mlp(h, p):
    return jnp.square(jax.nn.relu(h @ p['w_ff1'])) @ p['w_ff2']


def hybrid_layer(x, z, c, c_ctx, p, rope_ret, rope_mla, need_ctx):
    mod_x = jax.nn.silu(c) @ p['w_ada'] + p['b_ada']
    mod_z = jax.nn.silu(c_ctx) @ p['w_ada'] + p['b_ada']
    sh1x, sc1x, g1x, sh2x, sc2x, g2x = [m[:, None, :] for m in jnp.split(mod_x, 6, axis=-1)]
    sh1z, sc1z, g1z, sh2z, sc2z, g2z = jnp.split(mod_z, 6, axis=-1)
    hx = x * (1.0 + sc1x) + sh1x
    hz = z * (1.0 + sc1z) + sh1z
    mix_x, mix_z = token_mixers(hx, hz, p, rope_ret, rope_mla, need_ctx)
    x = layer_norm(DEEPNORM_ALPHA * x + g1x * mix_x, p['ln_gain'][0], p['ln_bias'][0])
    x = layer_norm(DEEPNORM_ALPHA * x + g2x * sq_relu_mlp(x * (1.0 + sc2x) + sh2x, p),
                   p['ln_gain'][1], p['ln_bias'][1])
    if need_ctx:
        z = layer_norm(DEEPNORM_ALPHA * z + g1z * mix_z, p['ln_gain'][0], p['ln_bias'][0])
        z = layer_norm(DEEPNORM_ALPHA * z + g2z * sq_relu_mlp(z * (1.0 + sc2z) + sh2z, p),
                       p['ln_gain'][1], p['ln_bias'][1])
    else:
        z = None
    return x, z


def setup_inputs(seed: int = 0) -> dict:
    key = jax.random.key(seed)
    ks = jax.random.split(key, 24)
    f32 = jnp.float32
    L = DEPTH
    D = D_MODEL

    def nrm(k, shape, s):
        return s * jax.random.normal(k, shape, f32)

    base_decay = -(5.0 + jnp.arange(RET_HEADS, dtype=f32)) * float(np.log(2.0))
    return {
        'x': nrm(ks[0], (BATCH, SEQ, D), 1.0),
        'c': nrm(ks[1], (BATCH, D), 1.0),
        'ctx': nrm(ks[2], (BATCH, CTX_LEN, D), 1.0),
        'c_ctx': nrm(ks[3], (D,), 1.0),
        'w_ada': nrm(ks[4], (L, D, 6 * D), 0.5 * D ** -0.5),
        'b_ada': nrm(ks[5], (L, 6 * D), 0.01),
        'w_in': nrm(ks[6], (L, D, IN_WIDTH), D ** -0.5),
        'ret_log_decay': base_decay + nrm(ks[7], (L, 2, RET_HEADS), 0.1),
        'ret_gn_gain': 1.0 + nrm(ks[8], (L, RET_V_W), 0.02),
        'na_rpb': nrm(ks[9], (L, NA_HEADS, 2 * NA_KH - 1, 2 * NA_KW - 1), 0.1),
        'mla_q_norm': 1.0 + nrm(ks[10], (L, MLA_Q_RANK), 0.02),
        'mla_w_qup': nrm(ks[11], (L, MLA_Q_RANK, MLA_HEADS * (MLA_NOPE + MLA_ROPE)), MLA_Q_RANK ** -0.5),
        'mla_kv_norm': 1.0 + nrm(ks[12], (L, MLA_KV_RANK), 0.02),
        'mla_w_kvup': nrm(ks[13], (L, MLA_KV_RANK, MLA_HEADS * (MLA_NOPE + MLA_DV)), MLA_KV_RANK ** -0.5),
        'w_branch_ret': nrm(ks[14], (L, RET_V_W, D), DEEPNORM_BETA * RET_V_W ** -0.5),
        'w_branch_na': nrm(ks[15], (L, NA_W, D), DEEPNORM_BETA * NA_W ** -0.5),
        'w_branch_mla': nrm(ks[16], (L, MLA_V_W, D), DEEPNORM_BETA * MLA_V_W ** -0.5),
        'w_out': nrm(ks[17], (L, D, D), DEEPNORM_BETA * D ** -0.5),
        'w_ff1': nrm(ks[18], (L, D, D_FF), D ** -0.5),
        'w_ff2': nrm(ks[19], (L, D_FF, D), DEEPNORM_BETA * D_FF ** -0.5),
        'ln_gain': 1.0 + nrm(ks[20], (L, 2, D), 0.02),
        'ln_bias': nrm(ks[21], (L, 2, D), 0.01),
    }


def reference(x, c, ctx, c_ctx, w_ada, b_ada, w_in, ret_log_decay, ret_gn_gain, na_rpb,
              mla_q_norm, mla_w_qup, mla_kv_norm, mla_w_kvup, w_branch_ret, w_branch_na,
              w_branch_mla, w_out, w_ff1, w_ff2, ln_gain, ln_bias):
    n_tok = x.shape[1]
    rope_ret = axial_rope(n_tok, RET_DK)
    rope_mla = axial_rope(n_tok, MLA_ROPE)
    z = ctx
    for l in range(DEPTH):
        p = {
            'w_ada': w_ada[l], 'b_ada': b_ada[l], 'w_in': w_in[l],
            'ret_log_decay': ret_log_decay[l], 'ret_gn_gain': ret_gn_gain[l], 'na_rpb': na_rpb[l],
            'mla_q_norm': mla_q_norm[l], 'mla_w_qup': mla_w_qup[l],
            'mla_kv_norm': mla_kv_norm[l], 'mla_w_kvup': mla_w_kvup[l],
            'w_branch_ret': w_branch_ret[l], 'w_branch_na': w_branch_na[l], 'w_branch_mla': w_branch_mla[l],
            'w_out': w_out[l], 'w_ff1': w_ff1[l], 'w_ff2': w_ff2[l],
            'ln_gain': ln_gain[l], 'ln_bias': ln_bias[l],
        }
        x, z = hybrid_layer(x, z, c, c_ctx, p, rope_ret, rope_mla, l < DEPTH - 1)
    return x
```

```python
import functools

import numpy as np
import jax
import jax.numpy as jnp
from jax import lax
from jax.experimental import pallas as pl
from jax.experimental.pallas import tpu as pltpu

F32 = jnp.float32
BF16 = jnp.bfloat16

D_MODEL = 1024
GRID_W = 64
RET_HEADS = 4
RET_DK = 64
RET_DV = 128
RET_CHUNK = 128
NA_HEADS = 8
NA_DH = 64
NA_KH = 8
NA_KW = 16
MLA_HEADS = 8
MLA_RANK = 256
MLA_NOPE = 64
MLA_ROPE = 32
MLA_DV = 64
MLA_DK_PAD = 128
D_FF = 4 * D_MODEL
ROPE_BASE = 10000.0
EPS = 1e-5
DEPTH_FOR_NORM = 4
DEEPNORM_ALPHA = (2 * DEPTH_FOR_NORM) ** 0.25
MASK_VALUE = -1e30

C_RET_Q, C_RET_K, C_RET_V, C_RET_GF, C_RET_GB = 0, 256, 512, 1024, 1536
C_NA_Q, C_NA_K, C_NA_V = 2048, 2560, 3072
C_MLA_Q, C_MLA_KV = 3584, 3840
C_GATE_A, C_GATE_B, C_GATE_C = 4096, 5120, 6144
C_MLA_KR = 7168
P_WIDTH = 7296
P_COL_TILE = 2432

NA_QROWS = 4
NA_KROWS = 12
NA_QB = NA_QROWS * GRID_W
NA_KB = NA_KROWS * GRID_W

VMEM_LIMIT = 56 * 1024 * 1024


def _cparams(*sem):
    return pltpu.CompilerParams(dimension_semantics=sem, vmem_limit_bytes=VMEM_LIMIT)


def _tile(n, pref):
    t = min(n, pref)
    while n % t:
        t -= 1
    return t


def _const_spec(shape):
    nd = len(shape)
    return pl.BlockSpec(shape, lambda *_: (0,) * nd)


def _ada_kernel(c_ref, w_ref, b_ref, o_ref):
    c = c_ref[...]
    a = c * jax.nn.sigmoid(c)
    o_ref[0] = jnp.dot(a.astype(BF16), w_ref[0].astype(BF16), preferred_element_type=F32) + b_ref[0]


def _ada(cc, w_ada, b_ada):
    depth, d, n = w_ada.shape
    tn = 1024
    return pl.pallas_call(
        _ada_kernel,
        grid=(depth, n // tn),
        in_specs=[
            pl.BlockSpec((8, d), lambda l, j: (0, 0)),
            pl.BlockSpec((1, d, tn), lambda l, j: (l, 0, j)),
            pl.BlockSpec((1, 1, tn), lambda l, j: (l, 0, j)),
        ],
        out_specs=pl.BlockSpec((1, 8, tn), lambda l, j: (l, 0, j)),
        out_shape=jax.ShapeDtypeStruct((depth, 8, n), F32),
        compiler_params=_cparams("parallel", "parallel"),
        name="ada",
    )(cc, w_ada, b_ada.reshape(depth, 1, n))


def _inproj_kernel(x_ref, sc_ref, sh_ref, w_ref, o_ref):
    h = x_ref[0] * (1.0 + sc_ref[0]) + sh_ref[0]
    o_ref[0] = jnp.dot(h.astype(BF16), w_ref[...], preferred_element_type=F32)


def _inproj(x, sc, sh, w):
    b, t, d = x.shape
    tm = _tile(t, 512)
    tn = P_COL_TILE
    return pl.pallas_call(
        _inproj_kernel,
        grid=(P_WIDTH // tn, b, t // tm),
        in_specs=[
            pl.BlockSpec((1, tm, d), lambda j, bi, i: (bi, i, 0)),
            pl.BlockSpec((1, 1, d), lambda j, bi, i: (bi, 0, 0)),
            pl.BlockSpec((1, 1, d), lambda j, bi, i: (bi, 0, 0)),
            pl.BlockSpec((d, tn), lambda j, bi, i: (0, j)),
        ],
        out_specs=pl.BlockSpec((1, tm, tn), lambda j, bi, i: (bi, i, j)),
        out_shape=jax.ShapeDtypeStruct((b, t, P_WIDTH), F32),
        compiler_params=_cparams("parallel", "parallel", "parallel"),
        name="inproj",
    )(x, sc, sh, w)


def _swap_halves(x, half):
    n = x.shape[-1]
    lane = lax.broadcasted_iota(jnp.int32, x.shape, x.ndim - 1)
    first = (lane % (2 * half)) < half
    return jnp.where(first, pltpu.roll(x, n - half, x.ndim - 1), pltpu.roll(x, half, x.ndim - 1))


def _ret_kernel(lg_ref, q_ref, k_ref, v_ref, g_ref, cos_ref, sin_ref, gain_ref, s0_ref,
                y_ref, sf_ref, s_scr, *, use_rope, n_chunks):
    d = pl.program_id(1)
    c = pl.program_id(2)

    @pl.when(c == 0)
    def _():
        s_scr[...] = s0_ref[0, 0]

    q = q_ref[0]
    k = k_ref[0] * (RET_DK ** -0.5)
    if use_rope:
        cos = jnp.concatenate([cos_ref[...], cos_ref[...]], axis=1)
        sin = jnp.concatenate([sin_ref[...], sin_ref[...]], axis=1)
        q = q * cos + _swap_halves(q, RET_DK // 2) * sin
        k = k * cos + _swap_halves(k, RET_DK // 2) * sin
    v = v_ref[0].astype(BF16)
    gate = g_ref[0]
    gate = gate * jax.nn.sigmoid(gate)
    gain = gain_ref[...]

    cc = RET_CHUNK
    df = d.astype(F32)
    sgn = 1.0 - 2.0 * df
    row = lax.broadcasted_iota(jnp.int32, (cc, cc), 0).astype(F32)
    col = lax.broadcasted_iota(jnp.int32, (cc, cc), 1).astype(F32)
    diff = sgn * (row - col)
    pos = lax.broadcasted_iota(jnp.int32, (cc, RET_DK), 0).astype(F32)
    q_exp = (1.0 - df) * (pos + 1.0) + df * (cc - pos)
    k_exp = (1.0 - df) * (cc - 1.0 - pos) + df * pos

    for h in range(RET_HEADS):
        lg = lg_ref[d * RET_HEADS + h]
        decay = jnp.where(diff >= 0, jnp.exp(lg * jnp.maximum(diff, 0.0)), 0.0)
        qh = q[:, h * RET_DK:(h + 1) * RET_DK]
        kh = k[:, h * RET_DK:(h + 1) * RET_DK]
        vh = v[:, h * RET_DV:(h + 1) * RET_DV]
        scores = lax.dot_general(qh.astype(BF16), kh.astype(BF16), (((1,), (1,)), ((), ())),
                                 preferred_element_type=F32) * decay
        inner = jnp.dot(scores.astype(BF16), vh, preferred_element_type=F32)
        state = s_scr[h]
        cross = jnp.dot((qh * jnp.exp(lg * q_exp)).astype(BF16), state.astype(BF16),
                        preferred_element_type=F32)
        o = inner + cross
        kd = (kh * jnp.exp(lg * k_exp)).T.astype(BF16)
        kv = jnp.dot(kd, vh, preferred_element_type=F32)
        s_scr[h] = state * jnp.exp(lg * jnp.full(state.shape, float(cc), F32)) + kv
        mu = jnp.mean(o, axis=-1, keepdims=True)
        oc = o - mu
        var = jnp.mean(oc * oc, axis=-1, keepdims=True)
        sl = slice(h * RET_DV, (h + 1) * RET_DV)
        y_ref[0, 0, :, sl] = gate[:, sl] * (oc * lax.rsqrt(var + EPS) * gain[:, sl])

    @pl.when(c == n_chunks - 1)
    def _():
        sf_ref[0, 0] = s_scr[...]


def _retention(p, lg, cos, sin, gain, s0, use_rope):
    b, t, _ = p.shape
    cc = RET_CHUNK
    n = t // cc

    def cidx(di, ci):
        return ci + di * (n - 1 - 2 * ci)

    kern = functools.partial(_ret_kernel, use_rope=use_rope, n_chunks=n)
    grid_spec = pltpu.PrefetchScalarGridSpec(
        num_scalar_prefetch=1,
        grid=(b, 2, n),
        in_specs=[
            pl.BlockSpec((1, cc, 256), lambda bi, di, ci, lg_: (bi, cidx(di, ci), C_RET_Q // 256)),
            pl.BlockSpec((1, cc, 256), lambda bi, di, ci, lg_: (bi, cidx(di, ci), C_RET_K // 256)),
            pl.BlockSpec((1, cc, 512), lambda bi, di, ci, lg_: (bi, cidx(di, ci), C_RET_V // 512)),
            pl.BlockSpec((1, cc, 512), lambda bi, di, ci, lg_: (bi, cidx(di, ci), C_RET_GF // 512 + di)),
            pl.BlockSpec((cc, 128), lambda bi, di, ci, lg_: (cidx(di, ci), 0)),
            pl.BlockSpec((cc, 128), lambda bi, di, ci, lg_: (cidx(di, ci), 0)),
            pl.BlockSpec((1, RET_HEADS * RET_DV), lambda bi, di, ci, lg_: (0, 0)),
            pl.BlockSpec((1, 1, RET_HEADS, RET_DK, RET_DV), lambda bi, di, ci, lg_: (bi, di, 0, 0, 0)),
        ],
        out_specs=[
            pl.BlockSpec((1, 1, cc, RET_HEADS * RET_DV), lambda bi, di, ci, lg_: (bi, di, cidx(di, ci), 0)),
            pl.BlockSpec((1, 1, RET_HEADS, RET_DK, RET_DV), lambda bi, di, ci, lg_: (bi, di, 0, 0, 0)),
        ],
        scratch_shapes=[pltpu.VMEM((RET_HEADS, RET_DK, RET_DV), F32)],
    )
    return pl.pallas_call(
        kern,
        grid_spec=grid_spec,
        out_shape=[
            jax.ShapeDtypeStruct((b, 2, t, RET_HEADS * RET_DV), F32),
            jax.ShapeDtypeStruct((b, 2, RET_HEADS, RET_DK, RET_DV), F32),
        ],
        compiler_params=_cparams("parallel", "parallel", "arbitrary"),
        name="retention",
    )(lg, p, p, p, p, cos, sin, gain, s0)


def _na_tables(rows):
    groups = rows // NA_QROWS
    reps = (0, 1, groups - 1)
    dr = np.zeros((3, NA_QB, NA_KB), np.int32)
    dc = np.zeros((3, NA_QB, NA_KB), np.int32)
    ok = np.zeros((3, NA_QB, NA_KB), bool)
    qr = np.repeat(np.arange(NA_QROWS), GRID_W)
    qc = np.tile(np.arange(GRID_W), NA_QROWS)
    kr = np.repeat(np.arange(NA_KROWS), GRID_W)
    kc = np.tile(np.arange(GRID_W), NA_KROWS)
    for ti, g in enumerate(reps):
        ws = int(np.clip(NA_QROWS * g - NA_KH // 2, 0, rows - NA_KROWS))
        r = NA_QROWS * g + qr
        r0 = np.clip(r - NA_KH // 2, 0, rows - NA_KH)
        c0 = np.clip(qc - NA_KW // 2, 0, GRID_W - NA_KW)
        krow = ws + kr
        valid = ((krow[None, :] >= r0[:, None]) & (krow[None, :] < r0[:, None] + NA_KH)
                 & (kc[None, :] >= c0[:, None]) & (kc[None, :] < c0[:, None] + NA_KW))
        ok[ti] = valid
        dr[ti] = np.where(valid, krow[None, :] - r[:, None] + (NA_KH - 1), 0)
        dc[ti] = np.where(valid, kc[None, :] - qc[:, None] + (NA_KW - 1), 0)
    return dr, dc, ok


def _na_kernel(q_ref, k0_ref, k1_ref, k2_ref, v0_ref, v1_ref, v2_ref, kz_ref, vz_ref, bias_ref, o_ref):
    q = (q_ref[0] * (NA_DH ** -0.5)).astype(BF16)
    ks = [r[0].astype(BF16) for r in (k0_ref, k1_ref, k2_ref, kz_ref)]
    vs = [r[0].astype(BF16) for r in (v0_ref, v1_ref, v2_ref, vz_ref)]
    kb = k0_ref.shape[1]
    outs = []
    for h in range(NA_HEADS):
        sl = slice(h * NA_DH, (h + 1) * NA_DH)
        qh = q[:, sl]
        s = []
        for j in range(4):
            sj = lax.dot_general(qh, ks[j][:, sl], (((1,), (1,)), ((), ())), preferred_element_type=F32)
            if j < 3:
                sj = sj + bias_ref[0, h, :, j * kb:(j + 1) * kb]
            s.append(sj)
        m = functools.reduce(jnp.maximum, [jnp.max(sj, axis=-1, keepdims=True) for sj in s])
        ps = [jnp.exp(sj - m) for sj in s]
        l = functools.reduce(jnp.add, [jnp.sum(pj, axis=-1, keepdims=True) for pj in ps])
        o = functools.reduce(jnp.add, [jnp.dot(ps[j].astype(BF16), vs[j][:, sl], preferred_element_type=F32)
                                       for j in range(4)])
        outs.append(o / l)
    o_ref[0] = jnp.concatenate(outs, axis=1)


def _na(px, pz, bias):
    b, t, _ = px.shape
    lz = pz.shape[1]
    groups = t // NA_QB
    w = NA_HEADS * NA_DH
    kblk = NA_KB // 3

    def kspec(col, off):
        return pl.BlockSpec((1, kblk, w), lambda bi, g: (bi, jnp.clip(g - 1, 0, groups - 3) + off, col // w))

    def tab(g):
        return jnp.where(g == 0, 0, jnp.where(g == groups - 1, 2, 1))

    return pl.pallas_call(
        _na_kernel,
        grid=(b, groups),
        in_specs=[
            pl.BlockSpec((1, NA_QB, w), lambda bi, g: (bi, g, C_NA_Q // w)),
            kspec(C_NA_K, 0), kspec(C_NA_K, 1), kspec(C_NA_K, 2),
            kspec(C_NA_V, 0), kspec(C_NA_V, 1), kspec(C_NA_V, 2),
            pl.BlockSpec((1, lz, w), lambda bi, g: (bi, 0, C_NA_K // w)),
            pl.BlockSpec((1, lz, w), lambda bi, g: (bi, 0, C_NA_V // w)),
            pl.BlockSpec((1, NA_HEADS, NA_QB, NA_KB), lambda bi, g: (tab(g), 0, 0, 0)),
        ],
        out_specs=pl.BlockSpec((1, NA_QB, w), lambda bi, g: (bi, g, 0)),
        out_shape=jax.ShapeDtypeStruct((b, t, w), F32),
        compiler_params=_cparams("parallel", "arbitrary"),
        name="na",
    )(px, px, px, px, px, px, px, pz, pz, bias)


def _rms(x, gain):
    return x * lax.rsqrt(jnp.mean(x * x, axis=-1, keepdims=True) + EPS) * gain


def _mlaprep_kernel(pq_ref, pkv_ref, pkr_ref, qn_ref, kvn_ref, wq_ref, wk_ref, wv_ref, cos_ref, sin_ref,
                    q_ref, k_ref, v_ref, *, use_rope):
    hq = _rms(pq_ref[0], qn_ref[...]).astype(BF16)
    hkv = _rms(pkv_ref[0], kvn_ref[...]).astype(BF16)
    q = jnp.dot(hq, wq_ref[...], preferred_element_type=F32)
    k = jnp.dot(hkv, wk_ref[...], preferred_element_type=F32)
    v = jnp.dot(hkv, wv_ref[...], preferred_element_type=F32)
    kr = pltpu.roll(pkr_ref[0], MLA_NOPE, 1)
    k = k + jnp.concatenate([kr] * MLA_HEADS, axis=1)
    if use_rope:
        cos = jnp.concatenate([cos_ref[...]] * MLA_HEADS, axis=1)
        sin = jnp.concatenate([sin_ref[...]] * MLA_HEADS, axis=1)
        q = q * cos + _swap_halves(q, MLA_ROPE // 2) * sin
        k = k * cos + _swap_halves(k, MLA_ROPE // 2) * sin
    q_ref[0] = (q * ((MLA_NOPE + MLA_ROPE) ** -0.5)).astype(BF16)
    k_ref[0] = k.astype(BF16)
    v_ref[0] = v.astype(BF16)


def _mlaprep(p, qn, kvn, wq, wk, wv, cos, sin, use_rope):
    b, t, _ = p.shape
    tm = _tile(t, 512)
    wqk = MLA_HEADS * MLA_DK_PAD
    wvv = MLA_HEADS * MLA_DV
    kern = functools.partial(_mlaprep_kernel, use_rope=use_rope)
    return pl.pallas_call(
        kern,
        grid=(b, t // tm),
        in_specs=[
            pl.BlockSpec((1, tm, MLA_RANK), lambda bi, i: (bi, i, C_MLA_Q // MLA_RANK)),
            pl.BlockSpec((1, tm, MLA_RANK), lambda bi, i: (bi, i, C_MLA_KV // MLA_RANK)),
            pl.BlockSpec((1, tm, 128), lambda bi, i: (bi, i, C_MLA_KR // 128)),
            _const_spec((1, MLA_RANK)), _const_spec((1, MLA_RANK)),
            _const_spec((MLA_RANK, wqk)), _const_spec((MLA_RANK, wqk)), _const_spec((MLA_RANK, wvv)),
            pl.BlockSpec((tm, 128), lambda bi, i: (i, 0)),
            pl.BlockSpec((tm, 128), lambda bi, i: (i, 0)),
        ],
        out_specs=[
            pl.BlockSpec((1, tm, wqk), lambda bi, i: (bi, i, 0)),
            pl.BlockSpec((1, tm, wqk), lambda bi, i: (bi, i, 0)),
            pl.BlockSpec((1, tm, wvv), lambda bi, i: (bi, i, 0)),
        ],
        out_shape=[
            jax.ShapeDtypeStruct((b, t, wqk), BF16),
            jax.ShapeDtypeStruct((b, t, wqk), BF16),
            jax.ShapeDtypeStruct((b, t, wvv), BF16),
        ],
        compiler_params=_cparams("parallel", "parallel"),
        name="mlaprep",
    )(p, p, p, qn, kvn, wq, wk, wv, cos, sin)


def _flash_kernel(q_ref, k_ref, v_ref, o_ref, m_scr, l_scr, acc_scr, *, heads, dk, dv, scale, nk):
    ki = pl.program_id(2)

    @pl.when(ki == 0)
    def _():
        m_scr[...] = jnp.full(m_scr.shape, -jnp.inf, F32)
        l_scr[...] = jnp.zeros(l_scr.shape, F32)
        acc_scr[...] = jnp.zeros(acc_scr.shape, F32)

    q = q_ref[0]
    if scale != 1.0:
        q = q * scale
    q = q.astype(BF16)
    k = k_ref[0].astype(BF16)
    v = v_ref[0].astype(BF16)
    for h in range(heads):
        qh = q[:, h * dk:(h + 1) * dk]
        kh = k[:, h * dk:(h + 1) * dk]
        vh = v[:, h * dv:(h + 1) * dv]
        s = lax.dot_general(qh, kh, (((1,), (1,)), ((), ())), preferred_element_type=F32)
        m_prev = m_scr[h]
        m_new = jnp.maximum(m_prev, jnp.max(s, axis=-1, keepdims=True))
        alpha = jnp.exp(m_prev - m_new)
        p = jnp.exp(s - m_new)
        l_scr[h] = alpha * l_scr[h] + jnp.sum(p, axis=-1, keepdims=True)
        acc_scr[h] = alpha * acc_scr[h] + jnp.dot(p.astype(BF16), vh, preferred_element_type=F32)
        m_scr[h] = m_new

    @pl.when(ki == nk - 1)
    def _():
        o_ref[0] = jnp.concatenate([acc_scr[h] / l_scr[h] for h in range(heads)], axis=1)


def _flash(q, k, v, qcol, kcol, vcol, heads, dk, dv, scale, tq_pref, tk_pref):
    b, tq_all, _ = q.shape
    tk_all = k.shape[1]
    tq = _tile(tq_all, tq_pref)
    tk = _tile(tk_all, tk_pref)
    nk = tk_all // tk
    kern = functools.partial(_flash_kernel, heads=heads, dk=dk, dv=dv, scale=scale, nk=nk)
    return pl.pallas_call(
        kern,
        grid=(b, tq_all // tq, nk),
        in_specs=[
            pl.BlockSpec((1, tq, heads * dk), lambda bi, i, j: (bi, i, qcol)),
            pl.BlockSpec((1, tk, heads * dk), lambda bi, i, j: (bi, j, kcol)),
            pl.BlockSpec((1, tk, heads * dv), lambda bi, i, j: (bi, j, vcol)),
        ],
        out_specs=pl.BlockSpec((1, tq, heads * dv), lambda bi, i, j: (bi, i, 0)),
        out_shape=jax.ShapeDtypeStruct((b, tq_all, heads * dv), F32),
        scratch_shapes=[
            pltpu.VMEM((heads, tq, 1), F32),
            pltpu.VMEM((heads, tq, 1), F32),
            pltpu.VMEM((heads, tq, dv), F32),
        ],
        compiler_params=_cparams("parallel", "parallel", "arbitrary"),
        name="flash",
    )(q, k, v)


def _layer_norm(r, gain, bias):
    mu = jnp.mean(r, axis=-1, keepdims=True)
    rc = r - mu
    var = jnp.mean(rc * rc, axis=-1, keepdims=True)
    return rc * lax.rsqrt(var + EPS) * gain + bias


def _merge_kernel(yaf_ref, yab_ref, yb_ref, yc_ref, ga_ref, gb_ref, gc_ref, x_ref, g1_ref,
                  wa_ref, wb_ref, wc_ref, wo_ref, lng_ref, lnb_ref, o_ref):
    ya = (yaf_ref[0, 0] + yab_ref[0, 0]).astype(BF16)
    y = (jax.nn.sigmoid(ga_ref[0]) * jnp.dot(ya, wa_ref[...], preferred_element_type=F32)
         + jax.nn.sigmoid(gb_ref[0]) * jnp.dot(yb_ref[0].astype(BF16), wb_ref[...], preferred_element_type=F32)
         + jax.nn.sigmoid(gc_ref[0]) * jnp.dot(yc_ref[0].astype(BF16), wc_ref[...], preferred_element_type=F32))
    mix = jnp.dot(y.astype(BF16), wo_ref[...], preferred_element_type=F32)
    r = DEEPNORM_ALPHA * x_ref[0] + g1_ref[0] * mix
    o_ref[0] = _layer_norm(r, lng_ref[...], lnb_ref[...])


def _merge(ya, yb, yc, p, x, g1, wa, wb, wc, wo, lng, lnb):
    b, t, d = x.shape
    tm = _tile(t, 512)
    wbr = 512
    return pl.pallas_call(
        _merge_kernel,
        grid=(b, t // tm),
        in_specs=[
            pl.BlockSpec((1, 1, tm, wbr), lambda bi, i: (bi, 0, i, 0)),
            pl.BlockSpec((1, 1, tm, wbr), lambda bi, i: (bi, 1, i, 0)),
            pl.BlockSpec((1, tm, wbr), lambda bi, i: (bi, i, 0)),
            pl.BlockSpec((1, tm, wbr), lambda bi, i: (bi, i, 0)),
            pl.BlockSpec((1, tm, d), lambda bi, i: (bi, i, C_GATE_A // d)),
            pl.BlockSpec((1, tm, d), lambda bi, i: (bi, i, C_GATE_B // d)),
            pl.BlockSpec((1, tm, d), lambda bi, i: (bi, i, C_GATE_C // d)),
            pl.BlockSpec((1, tm, d), lambda bi, i: (bi, i, 0)),
            pl.BlockSpec((1, 1, d), lambda bi, i: (bi, 0, 0)),
            _const_spec((wbr, d)), _const_spec((wbr, d)), _const_spec((wbr, d)), _const_spec((d, d)),
            _const_spec((1, d)), _const_spec((1, d)),
        ],
        out_specs=pl.BlockSpec((1, tm, d), lambda bi, i: (bi, i, 0)),
        out_shape=jax.ShapeDtypeStruct((b, t, d), F32),
        compiler_params=_cparams("parallel", "parallel"),
        name="merge",
    )(ya, ya, yb, yc, p, p, p, x, g1, wa, wb, wc, wo, lng, lnb)


def _ffn_kernel(x_ref, sc_ref, sh_ref, g2_ref, w1_ref, w2_ref, lng_ref, lnb_ref, o_ref, *, ff_chunk):
    x = x_ref[0]
    h = (x * (1.0 + sc_ref[0]) + sh_ref[0]).astype(BF16)
    acc = jnp.zeros(x.shape, F32)
    for j in range(D_FF // ff_chunk):
        u = jnp.dot(h, w1_ref[:, j * ff_chunk:(j + 1) * ff_chunk], preferred_element_type=F32)
        u = jnp.square(jnp.maximum(u, 0.0)).astype(BF16)
        acc = acc + jnp.dot(u, w2_ref[j * ff_chunk:(j + 1) * ff_chunk, :], preferred_element_type=F32)
    r = DEEPNORM_ALPHA * x + g2_ref[0] * acc
    o_ref[0] = _layer_norm(r, lng_ref[...], lnb_ref[...])


def _ffn(x, sc, sh, g2, w1, w2, lng, lnb):
    b, t, d = x.shape
    tm = _tile(t, 512)
    kern = functools.partial(_ffn_kernel, ff_chunk=1024)
    return pl.pallas_call(
        kern,
        grid=(b, t // tm),
        in_specs=[
            pl.BlockSpec((1, tm, d), lambda bi, i: (bi, i, 0)),
            pl.BlockSpec((1, 1, d), lambda bi, i: (bi, 0, 0)),
            pl.BlockSpec((1, 1, d), lambda bi, i: (bi, 0, 0)),
            pl.BlockSpec((1, 1, d), lambda bi, i: (bi, 0, 0)),
            _const_spec((d, D_FF)), _const_spec((D_FF, d)),
            _const_spec((1, d)), _const_spec((1, d)),
        ],
        out_specs=pl.BlockSpec((1, tm, d), lambda bi, i: (bi, i, 0)),
        out_shape=jax.ShapeDtypeStruct((b, t, d), F32),
        compiler_params=_cparams("parallel", "parallel"),
        name="ffn",
    )(x, sc, sh, g2, w1, w2, lng, lnb)


def _rope_tables(n_tok, rot_dim):
    t = jnp.arange(n_tok)
    row = (t // GRID_W).astype(F32)
    col = (t % GRID_W).astype(F32)
    n_freq = rot_dim // 4
    inv_freq = ROPE_BASE ** (-2.0 * jnp.arange(n_freq, dtype=F32) / (rot_dim // 2))
    ang = jnp.concatenate([row[:, None] * inv_freq, col[:, None] * inv_freq], axis=-1)
    return jnp.cos(ang), jnp.sin(ang)


def _ret_rope_tables(n_tok):
    cos, sin = _rope_tables(n_tok, RET_DK)
    cos_h = jnp.concatenate([cos, cos], axis=1)
    sin_h = jnp.concatenate([-sin, sin], axis=1)
    return jnp.tile(cos_h, (1, 2)), jnp.tile(sin_h, (1, 2))


def _mla_rope_tables(n_tok):
    cos, sin = _rope_tables(n_tok, MLA_ROPE)
    ones = jnp.ones((n_tok, MLA_NOPE), F32)
    zeros = jnp.zeros((n_tok, MLA_NOPE), F32)
    pad = jnp.zeros((n_tok, MLA_DK_PAD - MLA_NOPE - MLA_ROPE), F32)
    cos_h = jnp.concatenate([ones, cos, cos, pad], axis=1)
    sin_h = jnp.concatenate([zeros, -sin, sin, pad], axis=1)
    return cos_h, sin_h


def _pack_w_in(w):
    d = w.shape[0]
    return jnp.concatenate([w[:, :4096], w[:, 4128:7200], w[:, 4096:4128],
                            jnp.zeros((d, P_WIDTH - 7200), w.dtype)], axis=1).astype(BF16)


def _pack_mla_weights(w_qup, w_kvup):
    r = w_qup.shape[0]
    wq = w_qup.reshape(r, MLA_HEADS, MLA_NOPE + MLA_ROPE)
    wq = jnp.pad(wq, ((0, 0), (0, 0), (0, MLA_DK_PAD - MLA_NOPE - MLA_ROPE))).reshape(r, MLA_HEADS * MLA_DK_PAD)
    wkv = w_kvup.reshape(r, MLA_HEADS, MLA_NOPE + MLA_DV)
    wk = jnp.pad(wkv[:, :, :MLA_NOPE], ((0, 0), (0, 0), (0, MLA_DK_PAD - MLA_NOPE))).reshape(r, MLA_HEADS * MLA_DK_PAD)
    wv = wkv[:, :, MLA_NOPE:].reshape(r, MLA_HEADS * MLA_DV)
    return wq.astype(BF16), wk.astype(BF16), wv.astype(BF16)


def kernel(x, c, ctx, c_ctx, w_ada, b_ada, w_in, ret_log_decay, ret_gn_gain, na_rpb, mla_q_norm, mla_w_qup,
           mla_kv_norm, mla_w_kvup, w_branch_ret, w_branch_na, w_branch_mla, w_out, w_ff1, w_ff2, ln_gain, ln_bias):
    depth = w_ada.shape[0]
    b, t, d = x.shape
    lz = ctx.shape[1]
    rows = t // GRID_W

    cc = jnp.zeros((8, d), F32).at[:b].set(c).at[b].set(c_ctx)
    mod = _ada(cc, w_ada, b_ada)

    cos_r, sin_r = _ret_rope_tables(t)
    cos_m, sin_m = _mla_rope_tables(t)
    cos_rz, sin_rz = cos_r[:lz], sin_r[:lz]
    cos_mz, sin_mz = cos_m[:lz], sin_m[:lz]
    na_dr, na_dc, na_ok = _na_tables(rows)
    s_zero = jnp.zeros((b, 2, RET_HEADS, RET_DK, RET_DV), F32)

    z = ctx
    for l in range(depth):
        need_ctx = l < depth - 1
        mx = mod[l, :b].reshape(b, 6, 1, d)
        mz = jnp.broadcast_to(mod[l, b].reshape(1, 6, 1, d), (b, 6, 1, d))
        sh1x, sc1x, g1x, sh2x, sc2x, g2x = [mx[:, i] for i in range(6)]
        sh1z, sc1z, g1z, sh2z, sc2z, g2z = [mz[:, i] for i in range(6)]

        w_in_p = _pack_w_in(w_in[l])
        px = _inproj(x, sc1x, sh1x, w_in_p)
        pz = _inproj(z, sc1z, sh1z, w_in_p)

        lg = jnp.log1p(-jnp.exp(ret_log_decay[l].astype(F32))).reshape(2 * RET_HEADS)
        gn_gain = ret_gn_gain[l].reshape(1, RET_HEADS * RET_DV)
        ya_z, s_ctx = _retention(pz, lg, cos_rz, sin_rz, gn_gain, s_zero, use_rope=False)
        ya_x, _ = _retention(px, lg, cos_r, sin_r, gn_gain, s_ctx, use_rope=True)

        rpb = na_rpb[l]
        bias = jnp.where(na_ok[None], rpb[:, na_dr, na_dc], MASK_VALUE).transpose(1, 0, 2, 3)
        yb_x = _na(px, pz, bias)

        wq, wk, wv = _pack_mla_weights(mla_w_qup[l], mla_w_kvup[l])
        qn = mla_q_norm[l].reshape(1, MLA_RANK)
        kvn = mla_kv_norm[l].reshape(1, MLA_RANK)
        q_x, k_x, v_x = _mlaprep(px, qn, kvn, wq, wk, wv, cos_m, sin_m, use_rope=True)
        q_z, k_z, v_z = _mlaprep(pz, qn, kvn, wq, wk, wv, cos_mz, sin_mz, use_rope=False)
        k_all = jnp.concatenate([k_x, k_z], axis=1)
        v_all = jnp.concatenate([v_x, v_z], axis=1)
        yc_x = _flash(q_x, k_all, v_all, 0, 0, 0, MLA_HEADS, MLA_DK_PAD, MLA_DV, 1.0, 512, 768)

        wa = w_branch_ret[l].astype(BF16)
        wb = w_branch_na[l].astype(BF16)
        wc = w_branch_mla[l].astype(BF16)
        wo = w_out[l].astype(BF16)
        w1 = w_ff1[l].astype(BF16)
        w2 = w_ff2[l].astype(BF16)
        lng1, lnb1 = ln_gain[l, 0].reshape(1, d), ln_bias[l, 0].reshape(1, d)
        lng2, lnb2 = ln_gain[l, 1].reshape(1, d), ln_bias[l, 1].reshape(1, d)

        x1 = _merge(ya_x, yb_x, yc_x, px, x, g1x, wa, wb, wc, wo, lng1, lnb1)
        x = _ffn(x1, sc2x, sh2x, g2x, w1, w2, lng2, lnb2)

        if need_ctx:
            wna = NA_HEADS * NA_DH
            yb_z = _flash(pz, pz, pz, C_NA_Q // wna, C_NA_K // wna, C_NA_V // wna,
                          NA_HEADS, NA_DH, NA_DH, NA_DH ** -0.5, 256, 256)
            yc_z = _flash(q_z, k_z, v_z, 0, 0, 0, MLA_HEADS, MLA_DK_PAD, MLA_DV, 1.0, 256, 256)
            z1 = _merge(ya_z, yb_z, yc_z, pz, z, g1z, wa, wb, wc, wo, lng1, lnb1)
            z = _ffn(z1, sc2z, sh2z, g2z, w1, w2, lng2, lnb2)
    return x
```

```python
import functools

import numpy as np
import jax
import jax.numpy as jnp
from jax import lax
from jax.experimental import pallas as pl
from jax.experimental.pallas import tpu as pltpu

F32 = jnp.float32
BF16 = jnp.bfloat16

D_MODEL = 1024
GRID_W = 64
RET_HEADS = 4
RET_DK = 64
RET_DV = 128
RET_CHUNK = 128
NA_HEADS = 8
NA_DH = 64
NA_KH = 8
NA_KW = 16
MLA_HEADS = 8
MLA_RANK = 256
MLA_NOPE = 64
MLA_ROPE = 32
MLA_DV = 64
MLA_DK_PAD = 128
D_FF = 4 * D_MODEL
ROPE_BASE = 10000.0
EPS = 1e-5
DEPTH_FOR_NORM = 4
DEEPNORM_ALPHA = (2 * DEPTH_FOR_NORM) ** 0.25
MASK_VALUE = -1e30

C_RET_Q, C_RET_K, C_RET_V, C_RET_GF, C_RET_GB = 0, 256, 512, 1024, 1536
C_NA_Q, C_NA_K, C_NA_V = 2048, 2560, 3072
C_MLA_Q, C_MLA_KV = 3584, 3840
C_GATE_A, C_GATE_B, C_GATE_C = 4096, 5120, 6144
C_MLA_KR = 7168
P_WIDTH = 7296
P_COL_TILE = 2432

NA_QROWS = 4
NA_KROWS = 12
NA_QB = NA_QROWS * GRID_W
NA_KB = NA_KROWS * GRID_W

VMEM_LIMIT = 56 * 1024 * 1024


def _cparams(*sem):
    return pltpu.CompilerParams(dimension_semantics=sem, vmem_limit_bytes=VMEM_LIMIT)


def _tile(n, pref):
    t = min(n, pref)
    while n % t:
        t -= 1
    return t


def _const_spec(shape):
    nd = len(shape)
    return pl.BlockSpec(shape, lambda *_: (0,) * nd)


def _ada_kernel(c_ref, w_ref, b_ref, o_ref):
    c = c_ref[...]
    a = c * jax.nn.sigmoid(c)
    o_ref[0] = jnp.dot(a.astype(BF16), w_ref[0].astype(BF16), preferred_element_type=F32) + b_ref[0]


def _ada(cc, w_ada, b_ada):
    depth, d, n = w_ada.shape
    tn = 1024
    return pl.pallas_call(
        _ada_kernel,
        grid=(depth, n // tn),
        in_specs=[
            pl.BlockSpec((8, d), lambda l, j: (0, 0)),
            pl.BlockSpec((1, d, tn), lambda l, j: (l, 0, j)),
            pl.BlockSpec((1, 1, tn), lambda l, j: (l, 0, j)),
        ],
        out_specs=pl.BlockSpec((1, 8, tn), lambda l, j: (l, 0, j)),
        out_shape=jax.ShapeDtypeStruct((depth, 8, n), F32),
        compiler_params=_cparams("parallel", "parallel"),
        name="ada",
    )(cc, w_ada, b_ada.reshape(depth, 1, n))


def _inproj_kernel(x_ref, sc_ref, sh_ref, w_ref, o_ref):
    h = x_ref[0] * (1.0 + sc_ref[0]) + sh_ref[0]
    o_ref[0] = jnp.dot(h.astype(BF16), w_ref[...], preferred_element_type=F32)


def _inproj(x, sc, sh, w):
    b, t, d = x.shape
    tm = _tile(t, 512)
    tn = P_COL_TILE
    return pl.pallas_call(
        _inproj_kernel,
        grid=(P_WIDTH // tn, b, t // tm),
        in_specs=[
            pl.BlockSpec((1, tm, d), lambda j, bi, i: (bi, i, 0)),
            pl.BlockSpec((1, 1, d), lambda j, bi, i: (bi, 0, 0)),
            pl.BlockSpec((1, 1, d), lambda j, bi, i: (bi, 0, 0)),
            pl.BlockSpec((d, tn), lambda j, bi, i: (0, j)),
        ],
        out_specs=pl.BlockSpec((1, tm, tn), lambda j, bi, i: (bi, i, j)),
        out_shape=jax.ShapeDtypeStruct((b, t, P_WIDTH), F32),
        compiler_params=_cparams("parallel", "parallel", "parallel"),
        name="inproj",
    )(x, sc, sh, w)


def _swap_halves(x, half):
    n = x.shape[-1]
    lane = lax.broadcasted_iota(jnp.int32, x.shape, x.ndim - 1)
    first = (lane % (2 * half)) < half
    return jnp.where(first, pltpu.roll(x, n - half, x.ndim - 1), pltpu.roll(x, half, x.ndim - 1))


def _ret_kernel(lg_ref, q_ref, k_ref, v_ref, g_ref, cos_ref, sin_ref, gain_ref, s0_ref,
                y_ref, sf_ref, s_scr, *, use_rope, n_chunks):
    d = pl.program_id(1)
    c = pl.program_id(2)

    @pl.when(c == 0)
    def _():
        s_scr[...] = s0_ref[0, 0]

    q = q_ref[0]
    k = k_ref[0] * (RET_DK ** -0.5)
    if use_rope:
        cos = jnp.concatenate([cos_ref[...], cos_ref[...]], axis=1)
        sin = jnp.concatenate([sin_ref[...], sin_ref[...]], axis=1)
        q = q * cos + _swap_halves(q, RET_DK // 2) * sin
        k = k * cos + _swap_halves(k, RET_DK // 2) * sin
    v = v_ref[0].astype(BF16)
    gate = g_ref[0]
    gate = gate * jax.nn.sigmoid(gate)
    gain = gain_ref[...]

    cc = RET_CHUNK
    df = d.astype(F32)
    sgn = 1.0 - 2.0 * df
    row = lax.broadcasted_iota(jnp.int32, (cc, cc), 0).astype(F32)
    col = lax.broadcasted_iota(jnp.int32, (cc, cc), 1).astype(F32)
    diff = sgn * (row - col)
    pos = lax.broadcasted_iota(jnp.int32, (cc, RET_DK), 0).astype(F32)
    q_exp = (1.0 - df) * (pos + 1.0) + df * (cc - pos)
    k_exp = (1.0 - df) * (cc - 1.0 - pos) + df * pos

    for h in range(RET_HEADS):
        lg = lg_ref[d * RET_HEADS + h]
        decay = jnp.where(diff >= 0, jnp.exp(lg * jnp.maximum(diff, 0.0)), 0.0)
        qh = q[:, h * RET_DK:(h + 1) * RET_DK]
        kh = k[:, h * RET_DK:(h + 1) * RET_DK]
        vh = v[:, h * RET_DV:(h + 1) * RET_DV]
        scores = lax.dot_general(qh.astype(BF16), kh.astype(BF16), (((1,), (1,)), ((), ())),
                                 preferred_element_type=F32) * decay
        inner = jnp.dot(scores.astype(BF16), vh, preferred_element_type=F32)
        state = s_scr[h]
        cross = jnp.dot((qh * jnp.exp(lg * q_exp)).astype(BF16), state.astype(BF16),
                        preferred_element_type=F32)
        o = inner + cross
        kd = (kh * jnp.exp(lg * k_exp)).T.astype(BF16)
        kv = jnp.dot(kd, vh, preferred_element_type=F32)
        s_scr[h] = state * jnp.exp(lg * jnp.full(state.shape, float(cc), F32)) + kv
        mu = jnp.mean(o, axis=-1, keepdims=True)
        oc = o - mu
        var = jnp.mean(oc * oc, axis=-1, keepdims=True)
        sl = slice(h * RET_DV, (h + 1) * RET_DV)
        y_ref[0, 0, :, sl] = gate[:, sl] * (oc * lax.rsqrt(var + EPS) * gain[:, sl])

    @pl.when(c == n_chunks - 1)
    def _():
        sf_ref[0, 0] = s_scr[...]


def _retention(p, lg, cos, sin, gain, s0, use_rope):
    b, t, _ = p.shape
    cc = RET_CHUNK
    n = t // cc

    def cidx(di, ci):
        return ci + di * (n - 1 - 2 * ci)

    kern = functools.partial(_ret_kernel, use_rope=use_rope, n_chunks=n)
    grid_spec = pltpu.PrefetchScalarGridSpec(
        num_scalar_prefetch=1,
        grid=(b, 2, n),
        in_specs=[
            pl.BlockSpec((1, cc, 256), lambda bi, di, ci, lg_: (bi, cidx(di, ci), C_RET_Q // 256)),
            pl.BlockSpec((1, cc, 256), lambda bi, di, ci, lg_: (bi, cidx(di, ci), C_RET_K // 256)),
            pl.BlockSpec((1, cc, 512), lambda bi, di, ci, lg_: (bi, cidx(di, ci), C_RET_V // 512)),
            pl.BlockSpec((1, cc, 512), lambda bi, di, ci, lg_: (bi, cidx(di, ci), C_RET_GF // 512 + di)),
            pl.BlockSpec((cc, 128), lambda bi, di, ci, lg_: (cidx(di, ci), 0)),
            pl.BlockSpec((cc, 128), lambda bi, di, ci, lg_: (cidx(di, ci), 0)),
            pl.BlockSpec((1, RET_HEADS * RET_DV), lambda bi, di, ci, lg_: (0, 0)),
            pl.BlockSpec((1, 1, RET_HEADS, RET_DK, RET_DV), lambda bi, di, ci, lg_: (bi, di, 0, 0, 0)),
        ],
        out_specs=[
            pl.BlockSpec((1, 1, cc, RET_HEADS * RET_DV), lambda bi, di, ci, lg_: (bi, di, cidx(di, ci), 0)),
            pl.BlockSpec((1, 1, RET_HEADS, RET_DK, RET_DV), lambda bi, di, ci, lg_: (bi, di, 0, 0, 0)),
        ],
        scratch_shapes=[pltpu.VMEM((RET_HEADS, RET_DK, RET_DV), F32)],
    )
    return pl.pallas_call(
        kern,
        grid_spec=grid_spec,
        out_shape=[
            jax.ShapeDtypeStruct((b, 2, t, RET_HEADS * RET_DV), F32),
            jax.ShapeDtypeStruct((b, 2, RET_HEADS, RET_DK, RET_DV), F32),
        ],
        compiler_params=_cparams("parallel", "parallel", "arbitrary"),
        name="retention",
    )(lg, p, p, p, p, cos, sin, gain, s0)


def _na_tables(rows):
    groups = rows // NA_QROWS
    cols = np.arange(GRID_W)
    c0 = np.clip(cols - NA_KW // 2, 0, GRID_W - NA_KW)
    col_ok = (cols[None, :] >= c0[:, None]) & (cols[None, :] < c0[:, None] + NA_KW)
    dc = cols[None, :] - cols[:, None] + (NA_KW - 1)
    onehot = (dc[None] == np.arange(2 * NA_KW - 1)[:, None, None]) & col_ok[None]
    row_bias = np.full((3, NA_QROWS, NA_KROWS), 2 * NA_KH - 1, np.int64)
    for ti, g in enumerate((0, 1, groups - 1)):
        ws = int(np.clip(NA_QROWS * g - NA_KH // 2, 0, rows - NA_KROWS))
        for lr in range(NA_QROWS):
            r = NA_QROWS * g + lr
            r0 = int(np.clip(r - NA_KH // 2, 0, rows - NA_KH))
            for kr in range(NA_KROWS):
                if r0 <= ws + kr < r0 + NA_KH:
                    row_bias[ti, lr, kr] = ws + kr - r + (NA_KH - 1)
    return onehot, col_ok, row_bias


def _na_bias_tables(na_rpb, rows):
    onehot, col_ok, row_bias = _na_tables(rows)
    toep = jnp.sum(jnp.where(onehot, na_rpb[..., None, None], 0.0), axis=3)
    toep = jnp.where(col_ok, toep, MASK_VALUE)
    masked = jnp.full(toep.shape[:2] + (1, GRID_W, GRID_W), MASK_VALUE, F32)
    toep = jnp.concatenate([toep, masked], axis=2)
    tabs = []
    for ti in range(3):
        slab = [jnp.concatenate([toep[:, :, int(row_bias[ti, lr, kr])] for kr in range(NA_KROWS)], axis=-1)
                for lr in range(NA_QROWS)]
        tabs.append(jnp.concatenate(slab, axis=-2))
    return jnp.stack(tabs, axis=1)


def _na_kernel(q_ref, k0_ref, k1_ref, k2_ref, v0_ref, v1_ref, v2_ref, kz_ref, vz_ref, bias_ref, o_ref):
    q = (q_ref[0] * (NA_DH ** -0.5)).astype(BF16)
    ks = [r[0].astype(BF16) for r in (k0_ref, k1_ref, k2_ref, kz_ref)]
    vs = [r[0].astype(BF16) for r in (v0_ref, v1_ref, v2_ref, vz_ref)]
    kb = k0_ref.shape[1]
    outs = []
    for h in range(NA_HEADS):
        sl = slice(h * NA_DH, (h + 1) * NA_DH)
        qh = q[:, sl]
        s = []
        for j in range(4):
            sj = lax.dot_general(qh, ks[j][:, sl], (((1,), (1,)), ((), ())), preferred_element_type=F32)
            if j < 3:
                sj = sj + bias_ref[0, h, :, j * kb:(j + 1) * kb]
            s.append(sj)
        m = functools.reduce(jnp.maximum, [jnp.max(sj, axis=-1, keepdims=True) for sj in s])
        ps = [jnp.exp(sj - m) for sj in s]
        l = functools.reduce(jnp.add, [jnp.sum(pj, axis=-1, keepdims=True) for pj in ps])
        o = functools.reduce(jnp.add, [jnp.dot(ps[j].astype(BF16), vs[j][:, sl], preferred_element_type=F32)
                                       for j in range(4)])
        outs.append(o / l)
    o_ref[0] = jnp.concatenate(outs, axis=1)


def _na(px, pz, bias):
    b, t, _ = px.shape
    lz = pz.shape[1]
    groups = t // NA_QB
    w = NA_HEADS * NA_DH
    kblk = NA_KB // 3

    def kspec(col, off):
        return pl.BlockSpec((1, kblk, w), lambda bi, g: (bi, jnp.clip(g - 1, 0, groups - 3) + off, col // w))

    def tab(g):
        return jnp.where(g == 0, 0, jnp.where(g == groups - 1, 2, 1))

    return pl.pallas_call(
        _na_kernel,
        grid=(b, groups),
        in_specs=[
            pl.BlockSpec((1, NA_QB, w), lambda bi, g: (bi, g, C_NA_Q // w)),
            kspec(C_NA_K, 0), kspec(C_NA_K, 1), kspec(C_NA_K, 2),
            kspec(C_NA_V, 0), kspec(C_NA_V, 1), kspec(C_NA_V, 2),
            pl.BlockSpec((1, lz, w), lambda bi, g: (bi, 0, C_NA_K // w)),
            pl.BlockSpec((1, lz, w), lambda bi, g: (bi, 0, C_NA_V // w)),
            pl.BlockSpec((1, NA_HEADS, NA_QB, NA_KB), lambda bi, g: (tab(g), 0, 0, 0)),
        ],
        out_specs=pl.BlockSpec((1, NA_QB, w), lambda bi, g: (bi, g, 0)),
        out_shape=jax.ShapeDtypeStruct((b, t, w), F32),
        compiler_params=_cparams("parallel", "arbitrary"),
        name="na",
    )(px, px, px, px, px, px, px, pz, pz, bias)


def _rms(x, gain):
    return x * lax.rsqrt(jnp.mean(x * x, axis=-1, keepdims=True) + EPS) * gain


def _mlaprep_kernel(pq_ref, pkv_ref, pkr_ref, qn_ref, kvn_ref, wq_ref, wk_ref, wv_ref, cos_ref, sin_ref,
                    q_ref, k_ref, v_ref, *, use_rope):
    hq = _rms(pq_ref[0], qn_ref[...]).astype(BF16)
    hkv = _rms(pkv_ref[0], kvn_ref[...]).astype(BF16)
    q = jnp.dot(hq, wq_ref[...], preferred_element_type=F32)
    k = jnp.dot(hkv, wk_ref[...], preferred_element_type=F32)
    v = jnp.dot(hkv, wv_ref[...], preferred_element_type=F32)
    kr = pltpu.roll(pkr_ref[0], MLA_NOPE, 1)
    k = k + jnp.concatenate([kr] * MLA_HEADS, axis=1)
    if use_rope:
        cos = jnp.concatenate([cos_ref[...]] * MLA_HEADS, axis=1)
        sin = jnp.concatenate([sin_ref[...]] * MLA_HEADS, axis=1)
        q = q * cos + _swap_halves(q, MLA_ROPE // 2) * sin
        k = k * cos + _swap_halves(k, MLA_ROPE // 2) * sin
    q_ref[0] = (q * ((MLA_NOPE + MLA_ROPE) ** -0.5)).astype(BF16)
    k_ref[0] = k.astype(BF16)
    v_ref[0] = v.astype(BF16)


def _mlaprep(p, qn, kvn, wq, wk, wv, cos, sin, use_rope):
    b, t, _ = p.shape
    tm = _tile(t, 512)
    wqk = MLA_HEADS * MLA_DK_PAD
    wvv = MLA_HEADS * MLA_DV
    kern = functools.partial(_mlaprep_kernel, use_rope=use_rope)
    return pl.pallas_call(
        kern,
        grid=(b, t // tm),
        in_specs=[
            pl.BlockSpec((1, tm, MLA_RANK), lambda bi, i: (bi, i, C_MLA_Q // MLA_RANK)),
            pl.BlockSpec((1, tm, MLA_RANK), lambda bi, i: (bi, i, C_MLA_KV // MLA_RANK)),
            pl.BlockSpec((1, tm, 128), lambda bi, i: (bi, i, C_MLA_KR // 128)),
            _const_spec((1, MLA_RANK)), _const_spec((1, MLA_RANK)),
            _const_spec((MLA_RANK, wqk)), _const_spec((MLA_RANK, wqk)), _const_spec((MLA_RANK, wvv)),
            pl.BlockSpec((tm, 128), lambda bi, i: (i, 0)),
            pl.BlockSpec((tm, 128), lambda bi, i: (i, 0)),
        ],
        out_specs=[
            pl.BlockSpec((1, tm, wqk), lambda bi, i: (bi, i, 0)),
            pl.BlockSpec((1, tm, wqk), lambda bi, i: (bi, i, 0)),
            pl.BlockSpec((1, tm, wvv), lambda bi, i: (bi, i, 0)),
        ],
        out_shape=[
            jax.ShapeDtypeStruct((b, t, wqk), BF16),
            jax.ShapeDtypeStruct((b, t, wqk), BF16),
            jax.ShapeDtypeStruct((b, t, wvv), BF16),
        ],
        compiler_params=_cparams("parallel", "parallel"),
        name="mlaprep",
    )(p, p, p, qn, kvn, wq, wk, wv, cos, sin)


def _flash_kernel(q_ref, k_ref, v_ref, o_ref, m_scr, l_scr, acc_scr, *, heads, dk, dv, scale, nk):
    ki = pl.program_id(2)
    tq = q_ref.shape[1]
    tk = k_ref.shape[1]
    lanes = 2 * dv

    @pl.when(ki == 0)
    def _():
        m_scr[...] = jnp.full(m_scr.shape, -jnp.inf, F32)
        l_scr[...] = jnp.zeros(l_scr.shape, F32)
        acc_scr[...] = jnp.zeros(acc_scr.shape, F32)

    low = lax.broadcasted_iota(jnp.int32, (tq, lanes), 1) < dv
    for pr in range(heads // 2):
        vp = v_ref[0, :, pr * lanes:(pr + 1) * lanes].astype(BF16)
        alphas, pvs = [], []
        for h in (2 * pr, 2 * pr + 1):
            qh = q_ref[0, :, h * dk:(h + 1) * dk]
            if scale != 1.0:
                qh = qh * scale
            kh = k_ref[0, :, h * dk:(h + 1) * dk]
            s = lax.dot_general(qh.astype(BF16), kh.astype(BF16), (((1,), (1,)), ((), ())),
                                preferred_element_type=F32)
            cols = [s[:, j * lanes:(j + 1) * lanes] for j in range(tk // lanes)]
            m_prev = m_scr[h]
            m_tile = jnp.max(functools.reduce(jnp.maximum, cols), axis=-1, keepdims=True)
            m_new = jnp.maximum(m_prev, m_tile)
            alpha = jnp.exp(m_prev - m_new)
            ps = [jnp.exp(cj - m_new) for cj in cols]
            l_scr[h] = alpha * l_scr[h] + functools.reduce(jnp.add, ps)
            m_scr[h] = m_new
            p = jnp.concatenate([pj.astype(BF16) for pj in ps], axis=1)
            pvs.append(jnp.dot(p, vp, preferred_element_type=F32))
            alphas.append(alpha)
        acc_scr[pr] = acc_scr[pr] * jnp.where(low, alphas[0], alphas[1]) + jnp.where(low, pvs[0], pvs[1])

    @pl.when(ki == nk - 1)
    def _():
        for pr in range(heads // 2):
            l0 = jnp.sum(l_scr[2 * pr], axis=-1, keepdims=True)
            l1 = jnp.sum(l_scr[2 * pr + 1], axis=-1, keepdims=True)
            o_ref[0, :, pr * lanes:(pr + 1) * lanes] = acc_scr[pr] / jnp.where(low, l0, l1)


def _flash(q, k, v, qcol, kcol, vcol, heads, dk, dv, scale, tq_pref, tk_pref):
    b, tq_all, _ = q.shape
    tk_all = k.shape[1]
    tq = _tile(tq_all, tq_pref)
    tk = _tile(tk_all, tk_pref)
    nk = tk_all // tk
    assert heads % 2 == 0 and 2 * dv == 128 and tk % 128 == 0
    kern = functools.partial(_flash_kernel, heads=heads, dk=dk, dv=dv, scale=scale, nk=nk)
    return pl.pallas_call(
        kern,
        grid=(b, tq_all // tq, nk),
        in_specs=[
            pl.BlockSpec((1, tq, heads * dk), lambda bi, i, j: (bi, i, qcol)),
            pl.BlockSpec((1, tk, heads * dk), lambda bi, i, j: (bi, j, kcol)),
            pl.BlockSpec((1, tk, heads * dv), lambda bi, i, j: (bi, j, vcol)),
        ],
        out_specs=pl.BlockSpec((1, tq, heads * dv), lambda bi, i, j: (bi, i, 0)),
        out_shape=jax.ShapeDtypeStruct((b, tq_all, heads * dv), F32),
        scratch_shapes=[
            pltpu.VMEM((heads, tq, 2 * dv), F32),
            pltpu.VMEM((heads, tq, 2 * dv), F32),
            pltpu.VMEM((heads // 2, tq, 2 * dv), F32),
        ],
        compiler_params=_cparams("parallel", "parallel", "arbitrary"),
        name="flash",
    )(q, k, v)


def _layer_norm(r, gain, bias):
    mu = jnp.mean(r, axis=-1, keepdims=True)
    rc = r - mu
    var = jnp.mean(rc * rc, axis=-1, keepdims=True)
    return rc * lax.rsqrt(var + EPS) * gain + bias


def _merge_kernel(yaf_ref, yab_ref, yb_ref, yc_ref, ga_ref, gb_ref, gc_ref, x_ref, g1_ref,
                  wa_ref, wb_ref, wc_ref, wo_ref, lng_ref, lnb_ref, o_ref):
    ya = (yaf_ref[0, 0] + yab_ref[0, 0]).astype(BF16)
    y = (jax.nn.sigmoid(ga_ref[0]) * jnp.dot(ya, wa_ref[...], preferred_element_type=F32)
         + jax.nn.sigmoid(gb_ref[0]) * jnp.dot(yb_ref[0].astype(BF16), wb_ref[...], preferred_element_type=F32)
         + jax.nn.sigmoid(gc_ref[0]) * jnp.dot(yc_ref[0].astype(BF16), wc_ref[...], preferred_element_type=F32))
    mix = jnp.dot(y.astype(BF16), wo_ref[...], preferred_element_type=F32)
    r = DEEPNORM_ALPHA * x_ref[0] + g1_ref[0] * mix
    o_ref[0] = _layer_norm(r, lng_ref[...], lnb_ref[...])


def _merge(ya, yb, yc, p, x, g1, wa, wb, wc, wo, lng, lnb):
    b, t, d = x.shape
    tm = _tile(t, 512)
    wbr = 512
    return pl.pallas_call(
        _merge_kernel,
        grid=(b, t // tm),
        in_specs=[
            pl.BlockSpec((1, 1, tm, wbr), lambda bi, i: (bi, 0, i, 0)),
            pl.BlockSpec((1, 1, tm, wbr), lambda bi, i: (bi, 1, i, 0)),
            pl.BlockSpec((1, tm, wbr), lambda bi, i: (bi, i, 0)),
            pl.BlockSpec((1, tm, wbr), lambda bi, i: (bi, i, 0)),
            pl.BlockSpec((1, tm, d), lambda bi, i: (bi, i, C_GATE_A // d)),
            pl.BlockSpec((1, tm, d), lambda bi, i: (bi, i, C_GATE_B // d)),
            pl.BlockSpec((1, tm, d), lambda bi, i: (bi, i, C_GATE_C // d)),
            pl.BlockSpec((1, tm, d), lambda bi, i: (bi, i, 0)),
            pl.BlockSpec((1, 1, d), lambda bi, i: (bi, 0, 0)),
            _const_spec((wbr, d)), _const_spec((wbr, d)), _const_spec((wbr, d)), _const_spec((d, d)),
            _const_spec((1, d)), _const_spec((1, d)),
        ],
        out_specs=pl.BlockSpec((1, tm, d), lambda bi, i: (bi, i, 0)),
        out_shape=jax.ShapeDtypeStruct((b, t, d), F32),
        compiler_params=_cparams("parallel", "parallel"),
        name="merge",
    )(ya, ya, yb, yc, p, p, p, x, g1, wa, wb, wc, wo, lng, lnb)


def _ffn_kernel(x_ref, sc_ref, sh_ref, g2_ref, w1_ref, w2_ref, lng_ref, lnb_ref, o_ref, *, ff_chunk):
    x = x_ref[0]
    h = (x * (1.0 + sc_ref[0]) + sh_ref[0]).astype(BF16)
    acc = jnp.zeros(x.shape, F32)
    for j in range(D_FF // ff_chunk):
        u = jnp.dot(h, w1_ref[:, j * ff_chunk:(j + 1) * ff_chunk], preferred_element_type=F32)
        u = jnp.square(jnp.maximum(u, 0.0)).astype(BF16)
        acc = acc + jnp.dot(u, w2_ref[j * ff_chunk:(j + 1) * ff_chunk, :], preferred_element_type=F32)
    r = DEEPNORM_ALPHA * x + g2_ref[0] * acc
    o_ref[0] = _layer_norm(r, lng_ref[...], lnb_ref[...])


def _ffn(x, sc, sh, g2, w1, w2, lng, lnb):
    b, t, d = x.shape
    tm = _tile(t, 512)
    kern = functools.partial(_ffn_kernel, ff_chunk=1024)
    return pl.pallas_call(
        kern,
        grid=(b, t // tm),
        in_specs=[
            pl.BlockSpec((1, tm, d), lambda bi, i: (bi, i, 0)),
            pl.BlockSpec((1, 1, d), lambda bi, i: (bi, 0, 0)),
            pl.BlockSpec((1, 1, d), lambda bi, i: (bi, 0, 0)),
            pl.BlockSpec((1, 1, d), lambda bi, i: (bi, 0, 0)),
            _const_spec((d, D_FF)), _const_spec((D_FF, d)),
            _const_spec((1, d)), _const_spec((1, d)),
        ],
        out_specs=pl.BlockSpec((1, tm, d), lambda bi, i: (bi, i, 0)),
        out_shape=jax.ShapeDtypeStruct((b, t, d), F32),
        compiler_params=_cparams("parallel", "parallel"),
        name="ffn",
    )(x, sc, sh, g2, w1, w2, lng, lnb)


def _rope_tables(n_tok, rot_dim):
    t = jnp.arange(n_tok)
    row = (t // GRID_W).astype(F32)
    col = (t % GRID_W).astype(F32)
    n_freq = rot_dim // 4
    inv_freq = ROPE_BASE ** (-2.0 * jnp.arange(n_freq, dtype=F32) / (rot_dim // 2))
    ang = jnp.concatenate([row[:, None] * inv_freq, col[:, None] * inv_freq], axis=-1)
    return jnp.cos(ang), jnp.sin(ang)


def _ret_rope_tables(n_tok):
    cos, sin = _rope_tables(n_tok, RET_DK)
    cos_h = jnp.concatenate([cos, cos], axis=1)
    sin_h = jnp.concatenate([-sin, sin], axis=1)
    return jnp.tile(cos_h, (1, 2)), jnp.tile(sin_h, (1, 2))


def _mla_rope_tables(n_tok):
    cos, sin = _rope_tables(n_tok, MLA_ROPE)
    ones = jnp.ones((n_tok, MLA_NOPE), F32)
    zeros = jnp.zeros((n_tok, MLA_NOPE), F32)
    pad = jnp.zeros((n_tok, MLA_DK_PAD - MLA_NOPE - MLA_ROPE), F32)
    cos_h = jnp.concatenate([ones, cos, cos, pad], axis=1)
    sin_h = jnp.concatenate([zeros, -sin, sin, pad], axis=1)
    return cos_h, sin_h


def _pack_w_in(w):
    d = w.shape[0]
    return jnp.concatenate([w[:, :4096], w[:, 4128:7200], w[:, 4096:4128],
                            jnp.zeros((d, P_WIDTH - 7200), w.dtype)], axis=1).astype(BF16)


def _pack_mla_weights(w_qup, w_kvup):
    r = w_qup.shape[0]
    wq = w_qup.reshape(r, MLA_HEADS, MLA_NOPE + MLA_ROPE)
    wq = jnp.pad(wq, ((0, 0), (0, 0), (0, MLA_DK_PAD - MLA_NOPE - MLA_ROPE))).reshape(r, MLA_HEADS * MLA_DK_PAD)
    wkv = w_kvup.reshape(r, MLA_HEADS, MLA_NOPE + MLA_DV)
    wk = jnp.pad(wkv[:, :, :MLA_NOPE], ((0, 0), (0, 0), (0, MLA_DK_PAD - MLA_NOPE))).reshape(r, MLA_HEADS * MLA_DK_PAD)
    wv = wkv[:, :, MLA_NOPE:].reshape(r, MLA_HEADS * MLA_DV)
    return wq.astype(BF16), wk.astype(BF16), wv.astype(BF16)


def kernel(x, c, ctx, c_ctx, w_ada, b_ada, w_in, ret_log_decay, ret_gn_gain, na_rpb, mla_q_norm, mla_w_qup,
           mla_kv_norm, mla_w_kvup, w_branch_ret, w_branch_na, w_branch_mla, w_out, w_ff1, w_ff2, ln_gain, ln_bias):
    depth = w_ada.shape[0]
    b, t, d = x.shape
    lz = ctx.shape[1]
    rows = t // GRID_W

    cc = jnp.zeros((8, d), F32).at[:b].set(c).at[b].set(c_ctx)
    mod = _ada(cc, w_ada, b_ada)

    cos_r, sin_r = _ret_rope_tables(t)
    cos_m, sin_m = _mla_rope_tables(t)
    cos_rz, sin_rz = cos_r[:lz], sin_r[:lz]
    cos_mz, sin_mz = cos_m[:lz], sin_m[:lz]
    na_bias = _na_bias_tables(na_rpb, rows)
    s_zero = jnp.zeros((b, 2, RET_HEADS, RET_DK, RET_DV), F32)

    z = ctx
    for l in range(depth):
        need_ctx = l < depth - 1
        mx = mod[l, :b].reshape(b, 6, 1, d)
        mz = jnp.broadcast_to(mod[l, b].reshape(1, 6, 1, d), (b, 6, 1, d))
        sh1x, sc1x, g1x, sh2x, sc2x, g2x = [mx[:, i] for i in range(6)]
        sh1z, sc1z, g1z, sh2z, sc2z, g2z = [mz[:, i] for i in range(6)]

        w_in_p = _pack_w_in(w_in[l])
        px = _inproj(x, sc1x, sh1x, w_in_p)
        pz = _inproj(z, sc1z, sh1z, w_in_p)

        lg = jnp.log1p(-jnp.exp(ret_log_decay[l].astype(F32))).reshape(2 * RET_HEADS)
        gn_gain = ret_gn_gain[l].reshape(1, RET_HEADS * RET_DV)
        ya_z, s_ctx = _retention(pz, lg, cos_rz, sin_rz, gn_gain, s_zero, use_rope=False)
        ya_x, _ = _retention(px, lg, cos_r, sin_r, gn_gain, s_ctx, use_rope=True)

        yb_x = _na(px, pz, na_bias[l])

        wq, wk, wv = _pack_mla_weights(mla_w_qup[l], mla_w_kvup[l])
        qn = mla_q_norm[l].reshape(1, MLA_RANK)
        kvn = mla_kv_norm[l].reshape(1, MLA_RANK)
        q_x, k_x, v_x = _mlaprep(px, qn, kvn, wq, wk, wv, cos_m, sin_m, use_rope=True)
        q_z, k_z, v_z = _mlaprep(pz, qn, kvn, wq, wk, wv, cos_mz, sin_mz, use_rope=False)
        k_all = jnp.concatenate([k_x, k_z], axis=1)
        v_all = jnp.concatenate([v_x, v_z], axis=1)
        yc_x = _flash(q_x, k_all, v_all, 0, 0, 0, MLA_HEADS, MLA_DK_PAD, MLA_DV, 1.0, 512, 768)

        wa = w_branch_ret[l].astype(BF16)
        wb = w_branch_na[l].astype(BF16)
        wc = w_branch_mla[l].astype(BF16)
        wo = w_out[l].astype(BF16)
        w1 = w_ff1[l].astype(BF16)
        w2 = w_ff2[l].astype(BF16)
        lng1, lnb1 = ln_gain[l, 0].reshape(1, d), ln_bias[l, 0].reshape(1, d)
        lng2, lnb2 = ln_gain[l, 1].reshape(1, d), ln_bias[l, 1].reshape(1, d)

        x1 = _merge(ya_x, yb_x, yc_x, px, x, g1x, wa, wb, wc, wo, lng1, lnb1)
        x = _ffn(x1, sc2x, sh2x, g2x, w1, w2, lng2, lnb2)

        if need_ctx:
            wna = NA_HEADS * NA_DH
            yb_z = _flash(pz, pz, pz, C_NA_Q // wna, C_NA_K // wna, C_NA_V // wna,
                          NA_HEADS, NA_DH, NA_DH, NA_DH ** -0.5, 256, 256)
            yc_z = _flash(q_z, k_z, v_z, 0, 0, 0, MLA_HEADS, MLA_DK_PAD, MLA_DV, 1.0, 256, 256)
            z1 = _merge(ya_z, yb_z, yc_z, pz, z, g1z, wa, wb, wc, wo, lng1, lnb1)
            z = _ffn(z1, sc2z, sh2z, g2z, w1, w2, lng2, lnb2)
    return x
```

```python
import functools

import numpy as np
import jax
import jax.numpy as jnp
from jax import lax
from jax.experimental import pallas as pl
from jax.experimental.pallas import tpu as pltpu

F32 = jnp.float32
BF16 = jnp.bfloat16

D_MODEL = 1024
GRID_W = 64
RET_HEADS = 4
RET_DK = 64
RET_DV = 128
RET_CHUNK = 128
NA_HEADS = 8
NA_DH = 64
NA_KH = 8
NA_KW = 16
MLA_HEADS = 8
MLA_RANK = 256
MLA_NOPE = 64
MLA_ROPE = 32
MLA_DV = 64
MLA_DK_PAD = 128
D_FF = 4 * D_MODEL
ROPE_BASE = 10000.0
EPS = 1e-5
DEPTH_FOR_NORM = 4
DEEPNORM_ALPHA = (2 * DEPTH_FOR_NORM) ** 0.25
MASK_VALUE = -1e30

C_RET_Q, C_RET_K, C_RET_V, C_RET_GF, C_RET_GB = 0, 256, 512, 1024, 1536
C_NA_Q, C_NA_K, C_NA_V = 2048, 2560, 3072
C_MLA_Q, C_MLA_KV = 3584, 3840
C_GATE_A, C_GATE_B, C_GATE_C = 4096, 5120, 6144
C_MLA_KR = 7168
P_WIDTH = 7296
P_COL_TILE = 2432

NA_QROWS = 4
NA_KROWS = 12
NA_QB = NA_QROWS * GRID_W
NA_KB = NA_KROWS * GRID_W

VMEM_LIMIT = 56 * 1024 * 1024


def _cparams(*sem):
    return pltpu.CompilerParams(dimension_semantics=sem, vmem_limit_bytes=VMEM_LIMIT)


def _tile(n, pref):
    t = min(n, pref)
    while n % t:
        t -= 1
    return t


def _const_spec(shape):
    nd = len(shape)
    return pl.BlockSpec(shape, lambda *_: (0,) * nd)


def _ada_kernel(c_ref, w_ref, b_ref, o_ref):
    c = c_ref[...]
    a = c * jax.nn.sigmoid(c)
    o_ref[0] = jnp.dot(a.astype(BF16), w_ref[0].astype(BF16), preferred_element_type=F32) + b_ref[0]


def _ada(cc, w_ada, b_ada):
    depth, d, n = w_ada.shape
    tn = 1024
    return pl.pallas_call(
        _ada_kernel,
        grid=(depth, n // tn),
        in_specs=[
            pl.BlockSpec((8, d), lambda l, j: (0, 0)),
            pl.BlockSpec((1, d, tn), lambda l, j: (l, 0, j)),
            pl.BlockSpec((1, 1, tn), lambda l, j: (l, 0, j)),
        ],
        out_specs=pl.BlockSpec((1, 8, tn), lambda l, j: (l, 0, j)),
        out_shape=jax.ShapeDtypeStruct((depth, 8, n), F32),
        compiler_params=_cparams("parallel", "parallel"),
        name="ada",
    )(cc, w_ada, b_ada.reshape(depth, 1, n))


def _inproj_kernel(x_ref, sc_ref, sh_ref, w_ref, o_ref):
    h = x_ref[0] * (1.0 + sc_ref[0]) + sh_ref[0]
    o_ref[0] = jnp.dot(h.astype(BF16), w_ref[...], preferred_element_type=F32)


def _inproj(x, sc, sh, w):
    b, t, d = x.shape
    tm = _tile(t, 512)
    tn = P_COL_TILE
    return pl.pallas_call(
        _inproj_kernel,
        grid=(P_WIDTH // tn, b, t // tm),
        in_specs=[
            pl.BlockSpec((1, tm, d), lambda j, bi, i: (bi, i, 0)),
            pl.BlockSpec((1, 1, d), lambda j, bi, i: (bi, 0, 0)),
            pl.BlockSpec((1, 1, d), lambda j, bi, i: (bi, 0, 0)),
            pl.BlockSpec((d, tn), lambda j, bi, i: (0, j)),
        ],
        out_specs=pl.BlockSpec((1, tm, tn), lambda j, bi, i: (bi, i, j)),
        out_shape=jax.ShapeDtypeStruct((b, t, P_WIDTH), F32),
        compiler_params=_cparams("parallel", "parallel", "parallel"),
        name="inproj",
    )(x, sc, sh, w)


def _swap_halves(x, half):
    n = x.shape[-1]
    lane = lax.broadcasted_iota(jnp.int32, x.shape, x.ndim - 1)
    first = (lane % (2 * half)) < half
    return jnp.where(first, pltpu.roll(x, n - half, x.ndim - 1), pltpu.roll(x, half, x.ndim - 1))


def _ret_kernel(lg_ref, qf_ref, qb_ref, kf_ref, kb_ref, vf_ref, vb_ref, gf_ref, gb_ref,
                cosf_ref, cosb_ref, sinf_ref, sinb_ref, gain_ref, s0_ref,
                yf_ref, yb_ref, sf_ref, s_scr, *, use_rope, n_chunks):
    c = pl.program_id(0)
    batch = qf_ref.shape[0]
    cc = RET_CHUNK

    @pl.when(c == 0)
    def _():
        s_scr[...] = s0_ref[...]

    gain = gain_ref[...]
    row = lax.broadcasted_iota(jnp.int32, (cc, cc), 0).astype(F32)
    col = lax.broadcasted_iota(jnp.int32, (cc, cc), 1).astype(F32)
    pos = lax.broadcasted_iota(jnp.int32, (cc, RET_DK), 0).astype(F32)
    dirs = (
        (qf_ref, kf_ref, vf_ref, gf_ref, cosf_ref, sinf_ref, yf_ref, row - col, pos + 1.0, cc - 1.0 - pos),
        (qb_ref, kb_ref, vb_ref, gb_ref, cosb_ref, sinb_ref, yb_ref, col - row, cc - pos, pos),
    )
    for d, (q_ref, k_ref, v_ref, g_ref, cos_ref, sin_ref, y_ref, diff, q_exp, k_exp) in enumerate(dirs):
        if use_rope:
            cos = jnp.concatenate([cos_ref[...], cos_ref[...]], axis=1)
            sin = jnp.concatenate([sin_ref[...], sin_ref[...]], axis=1)
        qs, ks = [], []
        for b in range(batch):
            q = q_ref[b]
            k = k_ref[b] * (RET_DK ** -0.5)
            if use_rope:
                q = q * cos + _swap_halves(q, RET_DK // 2) * sin
                k = k * cos + _swap_halves(k, RET_DK // 2) * sin
            qs.append(q)
            ks.append(k)
        for h in range(RET_HEADS):
            lg = lg_ref[d * RET_HEADS + h]
            decay = jnp.where(diff >= 0, jnp.exp(lg * jnp.maximum(diff, 0.0)), 0.0)
            q_decay = jnp.exp(lg * q_exp)
            k_decay = jnp.exp(lg * k_exp)
            chunk_decay = jnp.exp(lg * jnp.full((RET_DK, RET_DV), float(cc), F32))
            sl = slice(h * RET_DV, (h + 1) * RET_DV)
            for b in range(batch):
                qh = qs[b][:, h * RET_DK:(h + 1) * RET_DK]
                kh = ks[b][:, h * RET_DK:(h + 1) * RET_DK]
                vh = v_ref[b, :, sl].astype(BF16)
                scores = lax.dot_general(qh.astype(BF16), kh.astype(BF16), (((1,), (1,)), ((), ())),
                                         preferred_element_type=F32) * decay
                inner = jnp.dot(scores.astype(BF16), vh, preferred_element_type=F32)
                state = s_scr[b, d, h]
                cross = jnp.dot((qh * q_decay).astype(BF16), state.astype(BF16), preferred_element_type=F32)
                o = inner + cross
                kv = jnp.dot((kh * k_decay).T.astype(BF16), vh, preferred_element_type=F32)
                s_scr[b, d, h] = state * chunk_decay + kv
                mu = jnp.mean(o, axis=-1, keepdims=True)
                oc = o - mu
                var = jnp.mean(oc * oc, axis=-1, keepdims=True)
                gate = g_ref[b, :, sl]
                y_ref[b, :, sl] = (gate * jax.nn.sigmoid(gate)) * (oc * lax.rsqrt(var + EPS) * gain[:, sl])

    @pl.when(c == n_chunks - 1)
    def _():
        sf_ref[...] = s_scr[...]


def _retention(p, lg, cos, sin, gain, s0, use_rope):
    b, t, _ = p.shape
    cc = RET_CHUNK
    n = t // cc
    wv = RET_HEADS * RET_DV

    def pspec(width, col, backward):
        if backward:
            return pl.BlockSpec((b, cc, width), lambda ci, lg_: (0, n - 1 - ci, col // width))
        return pl.BlockSpec((b, cc, width), lambda ci, lg_: (0, ci, col // width))

    def tspec(backward):
        if backward:
            return pl.BlockSpec((cc, 128), lambda ci, lg_: (n - 1 - ci, 0))
        return pl.BlockSpec((cc, 128), lambda ci, lg_: (ci, 0))

    state_spec = pl.BlockSpec((b, 2, RET_HEADS, RET_DK, RET_DV), lambda ci, lg_: (0, 0, 0, 0, 0))
    kern = functools.partial(_ret_kernel, use_rope=use_rope, n_chunks=n)
    grid_spec = pltpu.PrefetchScalarGridSpec(
        num_scalar_prefetch=1,
        grid=(n,),
        in_specs=[
            pspec(256, C_RET_Q, False), pspec(256, C_RET_Q, True),
            pspec(256, C_RET_K, False), pspec(256, C_RET_K, True),
            pspec(wv, C_RET_V, False), pspec(wv, C_RET_V, True),
            pspec(wv, C_RET_GF, False), pspec(wv, C_RET_GB, True),
            tspec(False), tspec(True), tspec(False), tspec(True),
            pl.BlockSpec((1, wv), lambda ci, lg_: (0, 0)),
            state_spec,
        ],
        out_specs=[
            pl.BlockSpec((b, cc, wv), lambda ci, lg_: (0, ci, 0)),
            pl.BlockSpec((b, cc, wv), lambda ci, lg_: (0, n - 1 - ci, 0)),
            state_spec,
        ],
        scratch_shapes=[pltpu.VMEM((b, 2, RET_HEADS, RET_DK, RET_DV), F32)],
    )
    return pl.pallas_call(
        kern,
        grid_spec=grid_spec,
        out_shape=[
            jax.ShapeDtypeStruct((b, t, wv), F32),
            jax.ShapeDtypeStruct((b, t, wv), F32),
            jax.ShapeDtypeStruct((b, 2, RET_HEADS, RET_DK, RET_DV), F32),
        ],
        compiler_params=_cparams("arbitrary"),
        name="retention",
    )(lg, p, p, p, p, p, p, p, p, cos, cos, sin, sin, gain, s0)


def _na_tables(rows):
    groups = rows // NA_QROWS
    cols = np.arange(GRID_W)
    c0 = np.clip(cols - NA_KW // 2, 0, GRID_W - NA_KW)
    col_ok = (cols[None, :] >= c0[:, None]) & (cols[None, :] < c0[:, None] + NA_KW)
    dc = cols[None, :] - cols[:, None] + (NA_KW - 1)
    onehot = (dc[None] == np.arange(2 * NA_KW - 1)[:, None, None]) & col_ok[None]
    row_bias = np.full((3, NA_QROWS, NA_KROWS), 2 * NA_KH - 1, np.int64)
    for ti, g in enumerate((0, 1, groups - 1)):
        ws = int(np.clip(NA_QROWS * g - NA_KH // 2, 0, rows - NA_KROWS))
        for lr in range(NA_QROWS):
            r = NA_QROWS * g + lr
            r0 = int(np.clip(r - NA_KH // 2, 0, rows - NA_KH))
            for kr in range(NA_KROWS):
                if r0 <= ws + kr < r0 + NA_KH:
                    row_bias[ti, lr, kr] = ws + kr - r + (NA_KH - 1)
    return onehot, col_ok, row_bias


def _na_bias_tables(na_rpb, rows):
    onehot, col_ok, row_bias = _na_tables(rows)
    toep = jnp.sum(jnp.where(onehot, na_rpb[..., None, None], 0.0), axis=3)
    toep = jnp.where(col_ok, toep, MASK_VALUE)
    masked = jnp.full(toep.shape[:2] + (1, GRID_W, GRID_W), MASK_VALUE, F32)
    toep = jnp.concatenate([toep, masked], axis=2)
    tabs = []
    for ti in range(3):
        slab = [jnp.concatenate([toep[:, :, int(row_bias[ti, lr, kr])] for kr in range(NA_KROWS)], axis=-1)
                for lr in range(NA_QROWS)]
        tabs.append(jnp.concatenate(slab, axis=-2))
    return jnp.stack(tabs, axis=1)


def _na_kernel(q_ref, k0_ref, k1_ref, k2_ref, v0_ref, v1_ref, v2_ref, kz_ref, vz_ref, bias_ref, o_ref):
    kb = k0_ref.shape[1]
    lanes = 2 * NA_DH
    low = lax.broadcasted_iota(jnp.int32, (q_ref.shape[1], lanes), 1) < NA_DH
    for pr in range(NA_HEADS // 2):
        sl = slice(pr * lanes, (pr + 1) * lanes)
        qp = q_ref[0, :, sl] * (NA_DH ** -0.5)
        qm = (jnp.where(low, qp, 0.0).astype(BF16), jnp.where(low, 0.0, qp).astype(BF16))
        ks = [r[0, :, sl].astype(BF16) for r in (k0_ref, k1_ref, k2_ref, kz_ref)]
        vs = [r[0, :, sl].astype(BF16) for r in (v0_ref, v1_ref, v2_ref, vz_ref)]
        outs = []
        for hh in range(2):
            h = 2 * pr + hh
            cols = []
            for j in range(4):
                sj = lax.dot_general(qm[hh], ks[j], (((1,), (1,)), ((), ())), preferred_element_type=F32)
                if j < 3:
                    sj = sj + bias_ref[0, h, :, j * kb:(j + 1) * kb]
                cols += [sj[:, i * lanes:(i + 1) * lanes] for i in range(sj.shape[1] // lanes)]
            m = jnp.max(functools.reduce(jnp.maximum, cols), axis=-1, keepdims=True)
            ps = [jnp.exp(cj - m) for cj in cols]
            l = jnp.sum(functools.reduce(jnp.add, ps), axis=-1, keepdims=True)
            o = None
            per = len(cols) // 4
            for j in range(4):
                pj = jnp.concatenate([pc.astype(BF16) for pc in ps[j * per:(j + 1) * per]], axis=1)
                oj = jnp.dot(pj, vs[j], preferred_element_type=F32)
                o = oj if o is None else o + oj
            outs.append(o / l)
        o_ref[0, :, sl] = jnp.where(low, outs[0], outs[1])


def _na(px, pz, bias):
    b, t, _ = px.shape
    lz = pz.shape[1]
    groups = t // NA_QB
    w = NA_HEADS * NA_DH
    kblk = NA_KB // 3
    assert lz == kblk and groups >= 3

    def kspec(col, off):
        return pl.BlockSpec((1, kblk, w), lambda bi, g: (bi, jnp.clip(g - 1, 0, groups - 3) + off, col // w))

    def tab(g):
        return jnp.where(g == 0, 0, jnp.where(g == groups - 1, 2, 1))

    return pl.pallas_call(
        _na_kernel,
        grid=(b, groups),
        in_specs=[
            pl.BlockSpec((1, NA_QB, w), lambda bi, g: (bi, g, C_NA_Q // w)),
            kspec(C_NA_K, 0), kspec(C_NA_K, 1), kspec(C_NA_K, 2),
            kspec(C_NA_V, 0), kspec(C_NA_V, 1), kspec(C_NA_V, 2),
            pl.BlockSpec((1, lz, w), lambda bi, g: (bi, 0, C_NA_K // w)),
            pl.BlockSpec((1, lz, w), lambda bi, g: (bi, 0, C_NA_V // w)),
            pl.BlockSpec((1, NA_HEADS, NA_QB, NA_KB), lambda bi, g: (tab(g), 0, 0, 0)),
        ],
        out_specs=pl.BlockSpec((1, NA_QB, w), lambda bi, g: (bi, g, 0)),
        out_shape=jax.ShapeDtypeStruct((b, t, w), F32),
        compiler_params=_cparams("parallel", "arbitrary"),
        name="na",
    )(px, px, px, px, px, px, px, pz, pz, bias)


def _rms(x, gain):
    return x * lax.rsqrt(jnp.mean(x * x, axis=-1, keepdims=True) + EPS) * gain


def _mlaprep_kernel(pq_ref, pkv_ref, pkr_ref, qn_ref, kvn_ref, wq_ref, wk_ref, wv_ref, cos_ref, sin_ref,
                    q_ref, k_ref, v_ref, *, use_rope):
    hq = _rms(pq_ref[0], qn_ref[...]).astype(BF16)
    hkv = _rms(pkv_ref[0], kvn_ref[...]).astype(BF16)
    q = jnp.dot(hq, wq_ref[...], preferred_element_type=F32)
    k = jnp.dot(hkv, wk_ref[...], preferred_element_type=F32)
    v = jnp.dot(hkv, wv_ref[...], preferred_element_type=F32)
    kr = pltpu.roll(pkr_ref[0], MLA_NOPE, 1)
    k = k + jnp.concatenate([kr] * MLA_HEADS, axis=1)
    if use_rope:
        cos = jnp.concatenate([cos_ref[...]] * MLA_HEADS, axis=1)
        sin = jnp.concatenate([sin_ref[...]] * MLA_HEADS, axis=1)
        q = q * cos + _swap_halves(q, MLA_ROPE // 2) * sin
        k = k * cos + _swap_halves(k, MLA_ROPE // 2) * sin
    q_ref[0] = (q * ((MLA_NOPE + MLA_ROPE) ** -0.5)).astype(BF16)
    k_ref[0] = k.astype(BF16)
    v_ref[0] = v.astype(BF16)


def _mlaprep(p, qn, kvn, wq, wk, wv, cos, sin, use_rope):
    b, t, _ = p.shape
    tm = _tile(t, 512)
    wqk = MLA_HEADS * MLA_DK_PAD
    wvv = MLA_HEADS * MLA_DV
    kern = functools.partial(_mlaprep_kernel, use_rope=use_rope)
    return pl.pallas_call(
        kern,
        grid=(b, t // tm),
        in_specs=[
            pl.BlockSpec((1, tm, MLA_RANK), lambda bi, i: (bi, i, C_MLA_Q // MLA_RANK)),
            pl.BlockSpec((1, tm, MLA_RANK), lambda bi, i: (bi, i, C_MLA_KV // MLA_RANK)),
            pl.BlockSpec((1, tm, 128), lambda bi, i: (bi, i, C_MLA_KR // 128)),
            _const_spec((1, MLA_RANK)), _const_spec((1, MLA_RANK)),
            _const_spec((MLA_RANK, wqk)), _const_spec((MLA_RANK, wqk)), _const_spec((MLA_RANK, wvv)),
            pl.BlockSpec((tm, 128), lambda bi, i: (i, 0)),
            pl.BlockSpec((tm, 128), lambda bi, i: (i, 0)),
        ],
        out_specs=[
            pl.BlockSpec((1, tm, wqk), lambda bi, i: (bi, i, 0)),
            pl.BlockSpec((1, tm, wqk), lambda bi, i: (bi, i, 0)),
            pl.BlockSpec((1, tm, wvv), lambda bi, i: (bi, i, 0)),
        ],
        out_shape=[
            jax.ShapeDtypeStruct((b, t, wqk), BF16),
            jax.ShapeDtypeStruct((b, t, wqk), BF16),
            jax.ShapeDtypeStruct((b, t, wvv), BF16),
        ],
        compiler_params=_cparams("parallel", "parallel"),
        name="mlaprep",
    )(p, p, p, qn, kvn, wq, wk, wv, cos, sin)


def _flash_kernel(q_ref, k_ref, v_ref, o_ref, m_scr, l_scr, acc_scr, *, heads, dk, dv, scale, nk):
    ki = pl.program_id(2)
    tq = q_ref.shape[1]
    tk = k_ref.shape[1]
    lanes = 2 * dv

    @pl.when(ki == 0)
    def _():
        m_scr[...] = jnp.full(m_scr.shape, -jnp.inf, F32)
        l_scr[...] = jnp.zeros(l_scr.shape, F32)
        acc_scr[...] = jnp.zeros(acc_scr.shape, F32)

    low = lax.broadcasted_iota(jnp.int32, (tq, lanes), 1) < dv
    for pr in range(heads // 2):
        vp = v_ref[0, :, pr * lanes:(pr + 1) * lanes].astype(BF16)
        alphas, pvs = [], []
        for h in (2 * pr, 2 * pr + 1):
            qh = q_ref[0, :, h * dk:(h + 1) * dk]
            if scale != 1.0:
                qh = qh * scale
            kh = k_ref[0, :, h * dk:(h + 1) * dk]
            s = lax.dot_general(qh.astype(BF16), kh.astype(BF16), (((1,), (1,)), ((), ())),
                                preferred_element_type=F32)
            cols = [s[:, j * lanes:(j + 1) * lanes] for j in range(tk // lanes)]
            m_prev = m_scr[h]
            m_tile = jnp.max(functools.reduce(jnp.maximum, cols), axis=-1, keepdims=True)
            m_new = jnp.maximum(m_prev, m_tile)
            alpha = jnp.exp(m_prev - m_new)
            ps = [jnp.exp(cj - m_new) for cj in cols]
            l_scr[h] = alpha * l_scr[h] + functools.reduce(jnp.add, ps)
            m_scr[h] = m_new
            p = jnp.concatenate([pj.astype(BF16) for pj in ps], axis=1)
            pvs.append(jnp.dot(p, vp, preferred_element_type=F32))
            alphas.append(alpha)
        acc_scr[pr] = acc_scr[pr] * jnp.where(low, alphas[0], alphas[1]) + jnp.where(low, pvs[0], pvs[1])

    @pl.when(ki == nk - 1)
    def _():
        for pr in range(heads // 2):
            l0 = jnp.sum(l_scr[2 * pr], axis=-1, keepdims=True)
            l1 = jnp.sum(l_scr[2 * pr + 1], axis=-1, keepdims=True)
            o_ref[0, :, pr * lanes:(pr + 1) * lanes] = acc_scr[pr] / jnp.where(low, l0, l1)


def _flash(q, k, v, qcol, kcol, vcol, heads, dk, dv, scale, tq_pref, tk_pref):
    b, tq_all, _ = q.shape
    tk_all = k.shape[1]
    tq = _tile(tq_all, tq_pref)
    tk = _tile(tk_all, tk_pref)
    nk = tk_all // tk
    assert heads % 2 == 0 and 2 * dv == 128 and tk % 128 == 0
    kern = functools.partial(_flash_kernel, heads=heads, dk=dk, dv=dv, scale=scale, nk=nk)
    return pl.pallas_call(
        kern,
        grid=(b, tq_all // tq, nk),
        in_specs=[
            pl.BlockSpec((1, tq, heads * dk), lambda bi, i, j: (bi, i, qcol)),
            pl.BlockSpec((1, tk, heads * dk), lambda bi, i, j: (bi, j, kcol)),
            pl.BlockSpec((1, tk, heads * dv), lambda bi, i, j: (bi, j, vcol)),
        ],
        out_specs=pl.BlockSpec((1, tq, heads * dv), lambda bi, i, j: (bi, i, 0)),
        out_shape=jax.ShapeDtypeStruct((b, tq_all, heads * dv), F32),
        scratch_shapes=[
            pltpu.VMEM((heads, tq, 2 * dv), F32),
            pltpu.VMEM((heads, tq, 2 * dv), F32),
            pltpu.VMEM((heads // 2, tq, 2 * dv), F32),
        ],
        compiler_params=_cparams("parallel", "parallel", "arbitrary"),
        name="flash",
    )(q, k, v)


def _layer_norm(r, gain, bias):
    mu = jnp.mean(r, axis=-1, keepdims=True)
    rc = r - mu
    var = jnp.mean(rc * rc, axis=-1, keepdims=True)
    return rc * lax.rsqrt(var + EPS) * gain + bias


def _merge_kernel(yaf_ref, yab_ref, yb_ref, yc_ref, ga_ref, gb_ref, gc_ref, x_ref, g1_ref,
                  wa_ref, wb_ref, wc_ref, wo_ref, lng_ref, lnb_ref, o_ref):
    ya = (yaf_ref[0] + yab_ref[0]).astype(BF16)
    y = (jax.nn.sigmoid(ga_ref[0]) * jnp.dot(ya, wa_ref[...], preferred_element_type=F32)
         + jax.nn.sigmoid(gb_ref[0]) * jnp.dot(yb_ref[0].astype(BF16), wb_ref[...], preferred_element_type=F32)
         + jax.nn.sigmoid(gc_ref[0]) * jnp.dot(yc_ref[0].astype(BF16), wc_ref[...], preferred_element_type=F32))
    mix = jnp.dot(y.astype(BF16), wo_ref[...], preferred_element_type=F32)
    r = DEEPNORM_ALPHA * x_ref[0] + g1_ref[0] * mix
    o_ref[0] = _layer_norm(r, lng_ref[...], lnb_ref[...])


def _merge(yaf, yab, yb, yc, p, x, g1, wa, wb, wc, wo, lng, lnb):
    b, t, d = x.shape
    tm = _tile(t, 512)
    wbr = 512
    return pl.pallas_call(
        _merge_kernel,
        grid=(b, t // tm),
        in_specs=[
            pl.BlockSpec((1, tm, wbr), lambda bi, i: (bi, i, 0)),
            pl.BlockSpec((1, tm, wbr), lambda bi, i: (bi, i, 0)),
            pl.BlockSpec((1, tm, wbr), lambda bi, i: (bi, i, 0)),
            pl.BlockSpec((1, tm, wbr), lambda bi, i: (bi, i, 0)),
            pl.BlockSpec((1, tm, d), lambda bi, i: (bi, i, C_GATE_A // d)),
            pl.BlockSpec((1, tm, d), lambda bi, i: (bi, i, C_GATE_B // d)),
            pl.BlockSpec((1, tm, d), lambda bi, i: (bi, i, C_GATE_C // d)),
            pl.BlockSpec((1, tm, d), lambda bi, i: (bi, i, 0)),
            pl.BlockSpec((1, 1, d), lambda bi, i: (bi, 0, 0)),
            _const_spec((wbr, d)), _const_spec((wbr, d)), _const_spec((wbr, d)), _const_spec((d, d)),
            _const_spec((1, d)), _const_spec((1, d)),
        ],
        out_specs=pl.BlockSpec((1, tm, d), lambda bi, i: (bi, i, 0)),
        out_shape=jax.ShapeDtypeStruct((b, t, d), F32),
        compiler_params=_cparams("parallel", "parallel"),
        name="merge",
    )(yaf, yab, yb, yc, p, p, p, x, g1, wa, wb, wc, wo, lng, lnb)


def _ffn_kernel(x_ref, sc_ref, sh_ref, g2_ref, w1_ref, w2_ref, lng_ref, lnb_ref, o_ref, *, ff_chunk):
    x = x_ref[0]
    h = (x * (1.0 + sc_ref[0]) + sh_ref[0]).astype(BF16)
    acc = jnp.zeros(x.shape, F32)
    for j in range(D_FF // ff_chunk):
        u = jnp.dot(h, w1_ref[:, j * ff_chunk:(j + 1) * ff_chunk], preferred_element_type=F32)
        u = jnp.square(jnp.maximum(u, 0.0)).astype(BF16)
        acc = acc + jnp.dot(u, w2_ref[j * ff_chunk:(j + 1) * ff_chunk, :], preferred_element_type=F32)
    r = DEEPNORM_ALPHA * x + g2_ref[0] * acc
    o_ref[0] = _layer_norm(r, lng_ref[...], lnb_ref[...])


def _ffn(x, sc, sh, g2, w1, w2, lng, lnb):
    b, t, d = x.shape
    tm = _tile(t, 512)
    kern = functools.partial(_ffn_kernel, ff_chunk=1024)
    return pl.pallas_call(
        kern,
        grid=(b, t // tm),
        in_specs=[
            pl.BlockSpec((1, tm, d), lambda bi, i: (bi, i, 0)),
            pl.BlockSpec((1, 1, d), lambda bi, i: (bi, 0, 0)),
            pl.BlockSpec((1, 1, d), lambda bi, i: (bi, 0, 0)),
            pl.BlockSpec((1, 1, d), lambda bi, i: (bi, 0, 0)),
            _const_spec((d, D_FF)), _const_spec((D_FF, d)),
            _const_spec((1, d)), _const_spec((1, d)),
        ],
        out_specs=pl.BlockSpec((1, tm, d), lambda bi, i: (bi, i, 0)),
        out_shape=jax.ShapeDtypeStruct((b, t, d), F32),
        compiler_params=_cparams("parallel", "parallel"),
        name="ffn",
    )(x, sc, sh, g2, w1, w2, lng, lnb)


def _rope_tables(n_tok, rot_dim):
    t = jnp.arange(n_tok)
    row = (t // GRID_W).astype(F32)
    col = (t % GRID_W).astype(F32)
    n_freq = rot_dim // 4
    inv_freq = ROPE_BASE ** (-2.0 * jnp.arange(n_freq, dtype=F32) / (rot_dim // 2))
    ang = jnp.concatenate([row[:, None] * inv_freq, col[:, None] * inv_freq], axis=-1)
    return jnp.cos(ang), jnp.sin(ang)


def _ret_rope_tables(n_tok):
    cos, sin = _rope_tables(n_tok, RET_DK)
    cos_h = jnp.concatenate([cos, cos], axis=1)
    sin_h = jnp.concatenate([-sin, sin], axis=1)
    return jnp.tile(cos_h, (1, 2)), jnp.tile(sin_h, (1, 2))


def _mla_rope_tables(n_tok):
    cos, sin = _rope_tables(n_tok, MLA_ROPE)
    ones = jnp.ones((n_tok, MLA_NOPE), F32)
    zeros = jnp.zeros((n_tok, MLA_NOPE), F32)
    pad = jnp.zeros((n_tok, MLA_DK_PAD - MLA_NOPE - MLA_ROPE), F32)
    cos_h = jnp.concatenate([ones, cos, cos, pad], axis=1)
    sin_h = jnp.concatenate([zeros, -sin, sin, pad], axis=1)
    return cos_h, sin_h


def _pack_w_in(w):
    d = w.shape[0]
    return jnp.concatenate([w[:, :4096], w[:, 4128:7200], w[:, 4096:4128],
                            jnp.zeros((d, P_WIDTH - 7200), w.dtype)], axis=1).astype(BF16)


def _pack_mla_weights(w_qup, w_kvup):
    r = w_qup.shape[0]
    wq = w_qup.reshape(r, MLA_HEADS, MLA_NOPE + MLA_ROPE)
    wq = jnp.pad(wq, ((0, 0), (0, 0), (0, MLA_DK_PAD - MLA_NOPE - MLA_ROPE))).reshape(r, MLA_HEADS * MLA_DK_PAD)
    wkv = w_kvup.reshape(r, MLA_HEADS, MLA_NOPE + MLA_DV)
    wk = jnp.pad(wkv[:, :, :MLA_NOPE], ((0, 0), (0, 0), (0, MLA_DK_PAD - MLA_NOPE))).reshape(r, MLA_HEADS * MLA_DK_PAD)
    wv = wkv[:, :, MLA_NOPE:].reshape(r, MLA_HEADS * MLA_DV)
    return wq.astype(BF16), wk.astype(BF16), wv.astype(BF16)


def kernel(x, c, ctx, c_ctx, w_ada, b_ada, w_in, ret_log_decay, ret_gn_gain, na_rpb, mla_q_norm, mla_w_qup,
           mla_kv_norm, mla_w_kvup, w_branch_ret, w_branch_na, w_branch_mla, w_out, w_ff1, w_ff2, ln_gain, ln_bias):
    depth = w_ada.shape[0]
    b, t, d = x.shape
    lz = ctx.shape[1]
    rows = t // GRID_W

    cc = jnp.zeros((8, d), F32).at[:b].set(c).at[b].set(c_ctx)
    mod = _ada(cc, w_ada, b_ada)

    cos_r, sin_r = _ret_rope_tables(t)
    cos_m, sin_m = _mla_rope_tables(t)
    cos_rz, sin_rz = cos_r[:lz], sin_r[:lz]
    cos_mz, sin_mz = cos_m[:lz], sin_m[:lz]
    na_bias = _na_bias_tables(na_rpb, rows)
    s_zero = jnp.zeros((b, 2, RET_HEADS, RET_DK, RET_DV), F32)

    z = ctx
    for l in range(depth):
        need_ctx = l < depth - 1
        mx = mod[l, :b].reshape(b, 6, 1, d)
        mz = jnp.broadcast_to(mod[l, b].reshape(1, 6, 1, d), (b, 6, 1, d))
        sh1x, sc1x, g1x, sh2x, sc2x, g2x = [mx[:, i] for i in range(6)]
        sh1z, sc1z, g1z, sh2z, sc2z, g2z = [mz[:, i] for i in range(6)]

        w_in_p = _pack_w_in(w_in[l])
        px = _inproj(x, sc1x, sh1x, w_in_p)
        pz = _inproj(z, sc1z, sh1z, w_in_p)

        lg = jnp.log1p(-jnp.exp(ret_log_decay[l].astype(F32))).reshape(2 * RET_HEADS)
        gn_gain = ret_gn_gain[l].reshape(1, RET_HEADS * RET_DV)
        yaf_z, yab_z, s_ctx = _retention(pz, lg, cos_rz, sin_rz, gn_gain, s_zero, use_rope=False)
        yaf_x, yab_x, _ = _retention(px, lg, cos_r, sin_r, gn_gain, s_ctx, use_rope=True)

        yb_x = _na(px, pz, na_bias[l])

        wq, wk, wv = _pack_mla_weights(mla_w_qup[l], mla_w_kvup[l])
        qn = mla_q_norm[l].reshape(1, MLA_RANK)
        kvn = mla_kv_norm[l].reshape(1, MLA_RANK)
        q_x, k_x, v_x = _mlaprep(px, qn, kvn, wq, wk, wv, cos_m, sin_m, use_rope=True)
        q_z, k_z, v_z = _mlaprep(pz, qn, kvn, wq, wk, wv, cos_mz, sin_mz, use_rope=False)
        k_all = jnp.concatenate([k_x, k_z], axis=1)
        v_all = jnp.concatenate([v_x, v_z], axis=1)
        yc_x = _flash(q_x, k_all, v_all, 0, 0, 0, MLA_HEADS, MLA_DK_PAD, MLA_DV, 1.0, 512, 768)

        wa = w_branch_ret[l].astype(BF16)
        wb = w_branch_na[l].astype(BF16)
        wc = w_branch_mla[l].astype(BF16)
        wo = w_out[l].astype(BF16)
        w1 = w_ff1[l].astype(BF16)
        w2 = w_ff2[l].astype(BF16)
        lng1, lnb1 = ln_gain[l, 0].reshape(1, d), ln_bias[l, 0].reshape(1, d)
        lng2, lnb2 = ln_gain[l, 1].reshape(1, d), ln_bias[l, 1].reshape(1, d)

        x1 = _merge(yaf_x, yab_x, yb_x, yc_x, px, x, g1x, wa, wb, wc, wo, lng1, lnb1)
        x = _ffn(x1, sc2x, sh2x, g2x, w1, w2, lng2, lnb2)

        if need_ctx:
            wna = NA_HEADS * NA_DH
            yb_z = _flash(pz, pz, pz, C_NA_Q // wna, C_NA_K // wna, C_NA_V // wna,
                          NA_HEADS, NA_DH, NA_DH, NA_DH ** -0.5, 256, 256)
            yc_z = _flash(q_z, k_z, v_z, 0, 0, 0, MLA_HEADS, MLA_DK_PAD, MLA_DV, 1.0, 256, 256)
            z1 = _merge(yaf_z, yab_z, yb_z, yc_z, pz, z, g1z, wa, wb, wc, wo, lng1, lnb1)
            z = _ffn(z1, sc2z, sh2z, g2z, w1, w2, lng2, lnb2)
    return x
```

```python
import functools

import numpy as np
import jax
import jax.numpy as jnp
from jax import lax
from jax.experimental import pallas as pl
from jax.experimental.pallas import tpu as pltpu

F32 = jnp.float32
BF16 = jnp.bfloat16

D_MODEL = 1024
GRID_W = 64
RET_HEADS = 4
RET_DK = 64
RET_DV = 128
RET_CHUNK = 128
NA_HEADS = 8
NA_DH = 64
NA_KH = 8
NA_KW = 16
MLA_HEADS = 8
MLA_RANK = 256
MLA_NOPE = 64
MLA_ROPE = 32
MLA_DV = 64
MLA_DK_PAD = 128
D_FF = 4 * D_MODEL
ROPE_BASE = 10000.0
EPS = 1e-5
DEPTH_FOR_NORM = 4
DEEPNORM_ALPHA = (2 * DEPTH_FOR_NORM) ** 0.25
MASK_VALUE = -1e30
LOG2_E = 1.4426950408889634

C_RET_Q, C_RET_K, C_RET_V, C_RET_GF, C_RET_GB = 0, 256, 512, 1024, 1536
C_NA_Q, C_NA_K, C_NA_V = 2048, 2560, 3072
C_MLA_Q, C_MLA_KV = 3584, 3840
C_GATE_A, C_GATE_B, C_GATE_C = 4096, 5120, 6144
C_MLA_KR = 7168
P_WIDTH = 7296
P_COL_TILE = 2432

NA_QROWS = 4
NA_KROWS = 12
NA_QB = NA_QROWS * GRID_W
NA_KB = NA_KROWS * GRID_W

VMEM_LIMIT = 56 * 1024 * 1024


def _cparams(*sem):
    return pltpu.CompilerParams(dimension_semantics=sem, vmem_limit_bytes=VMEM_LIMIT)


def _tile(n, pref):
    t = min(n, pref)
    while n % t:
        t -= 1
    return t


def _const_spec(shape):
    nd = len(shape)
    return pl.BlockSpec(shape, lambda *_: (0,) * nd)


def _ada_kernel(c_ref, w_ref, b_ref, o_ref):
    c = c_ref[...]
    a = c * jax.nn.sigmoid(c)
    o_ref[0] = jnp.dot(a.astype(BF16), w_ref[0].astype(BF16), preferred_element_type=F32) + b_ref[0]


def _ada(cc, w_ada, b_ada):
    depth, d, n = w_ada.shape
    tn = 1024
    return pl.pallas_call(
        _ada_kernel,
        grid=(depth, n // tn),
        in_specs=[
            pl.BlockSpec((8, d), lambda l, j: (0, 0)),
            pl.BlockSpec((1, d, tn), lambda l, j: (l, 0, j)),
            pl.BlockSpec((1, 1, tn), lambda l, j: (l, 0, j)),
        ],
        out_specs=pl.BlockSpec((1, 8, tn), lambda l, j: (l, 0, j)),
        out_shape=jax.ShapeDtypeStruct((depth, 8, n), F32),
        compiler_params=_cparams("parallel", "parallel"),
        name="ada",
    )(cc, w_ada, b_ada.reshape(depth, 1, n))


def _inproj_kernel(x_ref, sc_ref, sh_ref, w_ref, o_ref):
    h = x_ref[0] * (1.0 + sc_ref[0]) + sh_ref[0]
    o_ref[0] = jnp.dot(h.astype(BF16), w_ref[...], preferred_element_type=F32)


def _inproj(x, sc, sh, w):
    b, t, d = x.shape
    tm = _tile(t, 512)
    tn = P_COL_TILE
    return pl.pallas_call(
        _inproj_kernel,
        grid=(P_WIDTH // tn, b, t // tm),
        in_specs=[
            pl.BlockSpec((1, tm, d), lambda j, bi, i: (bi, i, 0)),
            pl.BlockSpec((1, 1, d), lambda j, bi, i: (bi, 0, 0)),
            pl.BlockSpec((1, 1, d), lambda j, bi, i: (bi, 0, 0)),
            pl.BlockSpec((d, tn), lambda j, bi, i: (0, j)),
        ],
        out_specs=pl.BlockSpec((1, tm, tn), lambda j, bi, i: (bi, i, j)),
        out_shape=jax.ShapeDtypeStruct((b, t, P_WIDTH), F32),
        compiler_params=_cparams("parallel", "parallel", "parallel"),
        name="inproj",
    )(x, sc, sh, w)


def _swap_halves(x, half):
    n = x.shape[-1]
    lane = lax.broadcasted_iota(jnp.int32, x.shape, x.ndim - 1)
    first = (lane % (2 * half)) < half
    return jnp.where(first, pltpu.roll(x, n - half, x.ndim - 1), pltpu.roll(x, half, x.ndim - 1))


def _ret_kernel(lg_ref, qf_ref, qb_ref, kf_ref, kb_ref, vf_ref, vb_ref, gf_ref, gb_ref,
                cosf_ref, cosb_ref, sinf_ref, sinb_ref, gain_ref, s0_ref,
                yf_ref, yb_ref, sf_ref, s_scr, *, use_rope, n_chunks):
    c = pl.program_id(0)
    batch = qf_ref.shape[0]
    cc = RET_CHUNK

    @pl.when(c == 0)
    def _():
        s_scr[...] = s0_ref[...]

    gain = gain_ref[...]
    row = lax.broadcasted_iota(jnp.int32, (cc, cc), 0).astype(F32)
    col = lax.broadcasted_iota(jnp.int32, (cc, cc), 1).astype(F32)
    pos = lax.broadcasted_iota(jnp.int32, (cc, RET_DK), 0).astype(F32)
    dirs = (
        (qf_ref, kf_ref, vf_ref, gf_ref, cosf_ref, sinf_ref, yf_ref, row - col, pos + 1.0, cc - 1.0 - pos),
        (qb_ref, kb_ref, vb_ref, gb_ref, cosb_ref, sinb_ref, yb_ref, col - row, cc - pos, pos),
    )
    for d, (q_ref, k_ref, v_ref, g_ref, cos_ref, sin_ref, y_ref, diff, q_exp, k_exp) in enumerate(dirs):
        if use_rope:
            cos = jnp.concatenate([cos_ref[...], cos_ref[...]], axis=1)
            sin = jnp.concatenate([sin_ref[...], sin_ref[...]], axis=1)
        qs, ks = [], []
        for b in range(batch):
            q = q_ref[b]
            k = k_ref[b] * (RET_DK ** -0.5)
            if use_rope:
                q = q * cos + _swap_halves(q, RET_DK // 2) * sin
                k = k * cos + _swap_halves(k, RET_DK // 2) * sin
            qs.append(q)
            ks.append(k)
        for h in range(RET_HEADS):
            lg = lg_ref[d * RET_HEADS + h]
            decay = jnp.where(diff >= 0, jnp.exp(lg * jnp.maximum(diff, 0.0)), 0.0)
            q_decay = jnp.exp(lg * q_exp)
            k_decay = jnp.exp(lg * k_exp)
            chunk_decay = jnp.exp(lg * jnp.full((RET_DK, RET_DV), float(cc), F32))
            sl = slice(h * RET_DV, (h + 1) * RET_DV)
            for b in range(batch):
                qh = qs[b][:, h * RET_DK:(h + 1) * RET_DK]
                kh = ks[b][:, h * RET_DK:(h + 1) * RET_DK]
                vh = v_ref[b, :, sl].astype(BF16)
                scores = lax.dot_general(qh.astype(BF16), kh.astype(BF16), (((1,), (1,)), ((), ())),
                                         preferred_element_type=F32) * decay
                inner = jnp.dot(scores.astype(BF16), vh, preferred_element_type=F32)
                state = s_scr[b, d, h]
                cross = jnp.dot((qh * q_decay).astype(BF16), state.astype(BF16), preferred_element_type=F32)
                o = inner + cross
                kv = jnp.dot((kh * k_decay).T.astype(BF16), vh, preferred_element_type=F32)
                s_scr[b, d, h] = state * chunk_decay + kv
                mu = jnp.mean(o, axis=-1, keepdims=True)
                oc = o - mu
                var = jnp.mean(oc * oc, axis=-1, keepdims=True)
                gate = g_ref[b, :, sl]
                y_ref[b, :, sl] = (gate * jax.nn.sigmoid(gate)) * (oc * lax.rsqrt(var + EPS) * gain[:, sl])

    @pl.when(c == n_chunks - 1)
    def _():
        sf_ref[...] = s_scr[...]


def _retention(p, lg, cos, sin, gain, s0, use_rope):
    b, t, _ = p.shape
    cc = RET_CHUNK
    n = t // cc
    wv = RET_HEADS * RET_DV

    def pspec(width, col, backward):
        if backward:
            return pl.BlockSpec((b, cc, width), lambda ci, lg_: (0, n - 1 - ci, col // width))
        return pl.BlockSpec((b, cc, width), lambda ci, lg_: (0, ci, col // width))

    def tspec(backward):
        if backward:
            return pl.BlockSpec((cc, 128), lambda ci, lg_: (n - 1 - ci, 0))
        return pl.BlockSpec((cc, 128), lambda ci, lg_: (ci, 0))

    state_spec = pl.BlockSpec((b, 2, RET_HEADS, RET_DK, RET_DV), lambda ci, lg_: (0, 0, 0, 0, 0))
    kern = functools.partial(_ret_kernel, use_rope=use_rope, n_chunks=n)
    grid_spec = pltpu.PrefetchScalarGridSpec(
        num_scalar_prefetch=1,
        grid=(n,),
        in_specs=[
            pspec(256, C_RET_Q, False), pspec(256, C_RET_Q, True),
            pspec(256, C_RET_K, False), pspec(256, C_RET_K, True),
            pspec(wv, C_RET_V, False), pspec(wv, C_RET_V, True),
            pspec(wv, C_RET_GF, False), pspec(wv, C_RET_GB, True),
            tspec(False), tspec(True), tspec(False), tspec(True),
            pl.BlockSpec((1, wv), lambda ci, lg_: (0, 0)),
            state_spec,
        ],
        out_specs=[
            pl.BlockSpec((b, cc, wv), lambda ci, lg_: (0, ci, 0)),
            pl.BlockSpec((b, cc, wv), lambda ci, lg_: (0, n - 1 - ci, 0)),
            state_spec,
        ],
        scratch_shapes=[pltpu.VMEM((b, 2, RET_HEADS, RET_DK, RET_DV), F32)],
    )
    return pl.pallas_call(
        kern,
        grid_spec=grid_spec,
        out_shape=[
            jax.ShapeDtypeStruct((b, t, wv), F32),
            jax.ShapeDtypeStruct((b, t, wv), F32),
            jax.ShapeDtypeStruct((b, 2, RET_HEADS, RET_DK, RET_DV), F32),
        ],
        compiler_params=_cparams("arbitrary"),
        name="retention",
    )(lg, p, p, p, p, p, p, p, p, cos, cos, sin, sin, gain, s0)


def _na_tables(rows):
    groups = rows // NA_QROWS
    cols = np.arange(GRID_W)
    c0 = np.clip(cols - NA_KW // 2, 0, GRID_W - NA_KW)
    col_ok = (cols[None, :] >= c0[:, None]) & (cols[None, :] < c0[:, None] + NA_KW)
    dc = cols[None, :] - cols[:, None] + (NA_KW - 1)
    onehot = (dc[None] == np.arange(2 * NA_KW - 1)[:, None, None]) & col_ok[None]
    row_bias = np.full((3, NA_QROWS, NA_KROWS), 2 * NA_KH - 1, np.int64)
    for ti, g in enumerate((0, 1, groups - 1)):
        ws = int(np.clip(NA_QROWS * g - NA_KH // 2, 0, rows - NA_KROWS))
        for lr in range(NA_QROWS):
            r = NA_QROWS * g + lr
            r0 = int(np.clip(r - NA_KH // 2, 0, rows - NA_KH))
            for kr in range(NA_KROWS):
                if r0 <= ws + kr < r0 + NA_KH:
                    row_bias[ti, lr, kr] = ws + kr - r + (NA_KH - 1)
    return onehot, col_ok, row_bias


def _na_bias_tables(na_rpb, rows):
    onehot, col_ok, row_bias = _na_tables(rows)
    toep = jnp.sum(jnp.where(onehot, na_rpb[..., None, None], 0.0), axis=3)
    toep = jnp.where(col_ok, toep, MASK_VALUE)
    masked = jnp.full(toep.shape[:2] + (1, GRID_W, GRID_W), MASK_VALUE, F32)
    toep = jnp.concatenate([toep, masked], axis=2)
    tabs = []
    for ti in range(3):
        slab = [jnp.concatenate([toep[:, :, int(row_bias[ti, lr, kr])] for kr in range(NA_KROWS)], axis=-1)
                for lr in range(NA_QROWS)]
        tabs.append(jnp.concatenate(slab, axis=-2))
    return jnp.stack(tabs, axis=1)


def _na_kernel(q_ref, k0_ref, k1_ref, k2_ref, v0_ref, v1_ref, v2_ref, kz_ref, vz_ref, bias_ref, o_ref):
    kb = k0_ref.shape[1]
    lanes = 2 * NA_DH
    low = lax.broadcasted_iota(jnp.int32, (q_ref.shape[1], lanes), 1) < NA_DH
    for pr in range(NA_HEADS // 2):
        sl = slice(pr * lanes, (pr + 1) * lanes)
        qp = q_ref[0, :, sl] * (NA_DH ** -0.5)
        qm = (jnp.where(low, qp, 0.0).astype(BF16), jnp.where(low, 0.0, qp).astype(BF16))
        ks = [r[0, :, sl].astype(BF16) for r in (k0_ref, k1_ref, k2_ref, kz_ref)]
        vs = [r[0, :, sl].astype(BF16) for r in (v0_ref, v1_ref, v2_ref, vz_ref)]
        outs = []
        for hh in range(2):
            h = 2 * pr + hh
            cols = []
            for j in range(4):
                sj = lax.dot_general(qm[hh], ks[j], (((1,), (1,)), ((), ())), preferred_element_type=F32)
                if j < 3:
                    sj = sj + bias_ref[0, h, :, j * kb:(j + 1) * kb]
                cols += [sj[:, i * lanes:(i + 1) * lanes] for i in range(sj.shape[1] // lanes)]
            m = jnp.max(functools.reduce(jnp.maximum, cols), axis=-1, keepdims=True)
            ps = [jnp.exp(cj - m) for cj in cols]
            l = jnp.sum(functools.reduce(jnp.add, ps), axis=-1, keepdims=True)
            o = None
            per = len(cols) // 4
            for j in range(4):
                pj = jnp.concatenate([pc.astype(BF16) for pc in ps[j * per:(j + 1) * per]], axis=1)
                oj = jnp.dot(pj, vs[j], preferred_element_type=F32)
                o = oj if o is None else o + oj
            outs.append(o / l)
        o_ref[0, :, sl] = jnp.where(low, outs[0], outs[1])


def _na(px, pz, bias):
    b, t, _ = px.shape
    lz = pz.shape[1]
    groups = t // NA_QB
    w = NA_HEADS * NA_DH
    kblk = NA_KB // 3
    assert lz == kblk and groups >= 3

    def kspec(col, off):
        return pl.BlockSpec((1, kblk, w), lambda bi, g: (bi, jnp.clip(g - 1, 0, groups - 3) + off, col // w))

    def tab(g):
        return jnp.where(g == 0, 0, jnp.where(g == groups - 1, 2, 1))

    return pl.pallas_call(
        _na_kernel,
        grid=(b, groups),
        in_specs=[
            pl.BlockSpec((1, NA_QB, w), lambda bi, g: (bi, g, C_NA_Q // w)),
            kspec(C_NA_K, 0), kspec(C_NA_K, 1), kspec(C_NA_K, 2),
            kspec(C_NA_V, 0), kspec(C_NA_V, 1), kspec(C_NA_V, 2),
            pl.BlockSpec((1, lz, w), lambda bi, g: (bi, 0, C_NA_K // w)),
            pl.BlockSpec((1, lz, w), lambda bi, g: (bi, 0, C_NA_V // w)),
            pl.BlockSpec((1, NA_HEADS, NA_QB, NA_KB), lambda bi, g: (tab(g), 0, 0, 0)),
        ],
        out_specs=pl.BlockSpec((1, NA_QB, w), lambda bi, g: (bi, g, 0)),
        out_shape=jax.ShapeDtypeStruct((b, t, w), F32),
        compiler_params=_cparams("parallel", "arbitrary"),
        name="na",
    )(px, px, px, px, px, px, px, pz, pz, bias)


def _rms(x, gain):
    return x * lax.rsqrt(jnp.mean(x * x, axis=-1, keepdims=True) + EPS) * gain


def _mlaprep_kernel(xq_ref, xkv_ref, xkr_ref, zq_ref, zkv_ref, zkr_ref, qn_ref, kvn_ref, wq_ref, wk_ref, wv_ref,
                    cos_ref, sin_ref, q_ref, k_ref, v_ref, *, nx):
    latent = pl.program_id(1) < nx
    pq = jnp.where(latent, xq_ref[0], zq_ref[0])
    pkv = jnp.where(latent, xkv_ref[0], zkv_ref[0])
    pkr = jnp.where(latent, xkr_ref[0], zkr_ref[0])
    hq = _rms(pq, qn_ref[...]).astype(BF16)
    hkv = _rms(pkv, kvn_ref[...]).astype(BF16)
    q = jnp.dot(hq, wq_ref[...], preferred_element_type=F32)
    k = jnp.dot(hkv, wk_ref[...], preferred_element_type=F32)
    v = jnp.dot(hkv, wv_ref[...], preferred_element_type=F32)
    kr = pltpu.roll(pkr, MLA_NOPE, 1)
    k = k + jnp.concatenate([kr] * MLA_HEADS, axis=1)
    cos = jnp.concatenate([cos_ref[...]] * MLA_HEADS, axis=1)
    sin = jnp.concatenate([sin_ref[...]] * MLA_HEADS, axis=1)
    q = q * cos + _swap_halves(q, MLA_ROPE // 2) * sin
    k = k * cos + _swap_halves(k, MLA_ROPE // 2) * sin
    q_ref[0] = (q * ((MLA_NOPE + MLA_ROPE) ** -0.5 * LOG2_E)).astype(BF16)
    k_ref[0] = k.astype(BF16)
    lane = lax.broadcasted_iota(jnp.int32, v.shape, 1)
    v_ref[0] = jnp.where(lane % MLA_DK_PAD == MLA_DV, 1.0, v).astype(BF16)


def _mlaprep(px, pz, qn, kvn, wq, wk, wv, cos, sin):
    b, t, _ = px.shape
    lz = pz.shape[1]
    tm = lz
    nx = t // tm
    wqk = MLA_HEADS * MLA_DK_PAD

    def xspec(width, col):
        return pl.BlockSpec((1, tm, width), lambda bi, i: (bi, jnp.minimum(i, nx - 1), col // width))

    def zspec(width, col):
        return pl.BlockSpec((1, tm, width), lambda bi, i: (bi, 0, col // width))

    out_spec = pl.BlockSpec((1, tm, wqk), lambda bi, i: (bi, i, 0))
    kern = functools.partial(_mlaprep_kernel, nx=nx)
    return pl.pallas_call(
        kern,
        grid=(b, nx + 1),
        in_specs=[
            xspec(MLA_RANK, C_MLA_Q), xspec(MLA_RANK, C_MLA_KV), xspec(128, C_MLA_KR),
            zspec(MLA_RANK, C_MLA_Q), zspec(MLA_RANK, C_MLA_KV), zspec(128, C_MLA_KR),
            _const_spec((1, MLA_RANK)), _const_spec((1, MLA_RANK)),
            _const_spec((MLA_RANK, wqk)), _const_spec((MLA_RANK, wqk)), _const_spec((MLA_RANK, wqk)),
            pl.BlockSpec((tm, 128), lambda bi, i: (i, 0)),
            pl.BlockSpec((tm, 128), lambda bi, i: (i, 0)),
        ],
        out_specs=[out_spec, out_spec, out_spec],
        out_shape=[jax.ShapeDtypeStruct((b, t + lz, wqk), BF16)] * 3,
        compiler_params=_cparams("parallel", "parallel"),
        name="mlaprep",
    )(px, px, px, pz, pz, pz, qn, kvn, wq, wk, wv, cos, sin)


def _flash_kernel(q_ref, k_ref, v_ref, o_ref, m_scr, l_scr, acc_scr, *, heads, dk, dv, scale, nk):
    ki = pl.program_id(2)
    tq = q_ref.shape[1]
    tk = k_ref.shape[1]
    lanes = 2 * dv

    @pl.when(ki == 0)
    def _():
        m_scr[...] = jnp.full(m_scr.shape, -jnp.inf, F32)
        l_scr[...] = jnp.zeros(l_scr.shape, F32)
        acc_scr[...] = jnp.zeros(acc_scr.shape, F32)

    low = lax.broadcasted_iota(jnp.int32, (tq, lanes), 1) < dv
    for pr in range(heads // 2):
        vp = v_ref[0, :, pr * lanes:(pr + 1) * lanes].astype(BF16)
        alphas, pvs = [], []
        for h in (2 * pr, 2 * pr + 1):
            qh = q_ref[0, :, h * dk:(h + 1) * dk]
            if scale != 1.0:
                qh = qh * scale
            kh = k_ref[0, :, h * dk:(h + 1) * dk]
            s = lax.dot_general(qh.astype(BF16), kh.astype(BF16), (((1,), (1,)), ((), ())),
                                preferred_element_type=F32)
            cols = [s[:, j * lanes:(j + 1) * lanes] for j in range(tk // lanes)]
            m_prev = m_scr[h]
            m_tile = jnp.max(functools.reduce(jnp.maximum, cols), axis=-1, keepdims=True)
            m_new = jnp.maximum(m_prev, m_tile)
            alpha = jnp.exp2(m_prev - m_new)
            ps = [jnp.exp2(cj - m_new) for cj in cols]
            l_scr[h] = alpha * l_scr[h] + functools.reduce(jnp.add, ps)
            m_scr[h] = m_new
            p = jnp.concatenate([pj.astype(BF16) for pj in ps], axis=1)
            pvs.append(jnp.dot(p, vp, preferred_element_type=F32))
            alphas.append(alpha)
        acc_scr[pr] = acc_scr[pr] * jnp.where(low, alphas[0], alphas[1]) + jnp.where(low, pvs[0], pvs[1])

    @pl.when(ki == nk - 1)
    def _():
        for pr in range(heads // 2):
            l0 = jnp.sum(l_scr[2 * pr], axis=-1, keepdims=True)
            l1 = jnp.sum(l_scr[2 * pr + 1], axis=-1, keepdims=True)
            o_ref[0, :, pr * lanes:(pr + 1) * lanes] = acc_scr[pr] / jnp.where(low, l0, l1)


def _flash(q, k, v, qcol, kcol, vcol, heads, dk, dv, scale, tq_pref, tk_pref):
    b, tq_all, _ = q.shape
    tk_all = k.shape[1]
    tq = _tile(tq_all, tq_pref)
    tk = _tile(tk_all, tk_pref)
    nk = tk_all // tk
    assert heads % 2 == 0 and 2 * dv == 128 and tk % 128 == 0
    kern = functools.partial(_flash_kernel, heads=heads, dk=dk, dv=dv, scale=scale, nk=nk)
    return pl.pallas_call(
        kern,
        grid=(b, tq_all // tq, nk),
        in_specs=[
            pl.BlockSpec((1, tq, heads * dk), lambda bi, i, j: (bi, i, qcol)),
            pl.BlockSpec((1, tk, heads * dk), lambda bi, i, j: (bi, j, kcol)),
            pl.BlockSpec((1, tk, heads * dv), lambda bi, i, j: (bi, j, vcol)),
        ],
        out_specs=pl.BlockSpec((1, tq, heads * dv), lambda bi, i, j: (bi, i, 0)),
        out_shape=jax.ShapeDtypeStruct((b, tq_all, heads * dv), F32),
        scratch_shapes=[
            pltpu.VMEM((heads, tq, 2 * dv), F32),
            pltpu.VMEM((heads, tq, 2 * dv), F32),
            pltpu.VMEM((heads // 2, tq, 2 * dv), F32),
        ],
        compiler_params=_cparams("parallel", "parallel", "arbitrary"),
        name="flash",
    )(q, k, v)


def _mla_attn_kernel(q_ref, k_ref, v_ref, o_ref, m_scr, acc_scr, *, heads, nk):
    ki = pl.program_id(2)
    w = MLA_DK_PAD

    @pl.when(ki == 0)
    def _():
        m_scr[...] = jnp.full(m_scr.shape, -jnp.inf, F32)
        acc_scr[...] = jnp.zeros(acc_scr.shape, F32)

    for h in range(heads):
        sl = slice(h * w, (h + 1) * w)
        s = lax.dot_general(q_ref[0, :, sl], k_ref[0, :, sl], (((1,), (1,)), ((), ())),
                            preferred_element_type=F32)
        cols = [s[:, j * w:(j + 1) * w] for j in range(s.shape[1] // w)]
        m_prev = m_scr[h]
        m_new = jnp.maximum(m_prev, jnp.max(functools.reduce(jnp.maximum, cols), axis=-1, keepdims=True))
        p = jnp.concatenate([jnp.exp2(cj - m_new).astype(BF16) for cj in cols], axis=1)
        m_scr[h] = m_new
        acc_scr[h] = (jnp.exp2(m_prev - m_new) * acc_scr[h]
                      + jnp.dot(p, v_ref[0, :, sl], preferred_element_type=F32))

    @pl.when(ki == nk - 1)
    def _():
        low = lax.broadcasted_iota(jnp.int32, (q_ref.shape[1], w), 1) < MLA_DV
        for pr in range(heads // 2):
            a0 = acc_scr[2 * pr]
            a1 = acc_scr[2 * pr + 1]
            o0 = a0 / a0[:, MLA_DV:MLA_DV + 1]
            o1 = a1 / a1[:, MLA_DV:MLA_DV + 1]
            o_ref[0, :, pr * w:(pr + 1) * w] = jnp.where(low, o0, pltpu.roll(o1, MLA_DV, 1))


def _mla_attn(q, k, v, q_start, q_len, k_start, k_len, tq_pref, tk_pref):
    b, _, wq = q.shape
    heads = wq // MLA_DK_PAD
    tq = _tile(q_len, tq_pref)
    tk = _tile(k_len, tk_pref)
    assert q_start % tq == 0 and k_start % tk == 0
    q_off, k_off = q_start // tq, k_start // tk
    nk = k_len // tk
    kern = functools.partial(_mla_attn_kernel, heads=heads, nk=nk)
    return pl.pallas_call(
        kern,
        grid=(b, q_len // tq, nk),
        in_specs=[
            pl.BlockSpec((1, tq, wq), lambda bi, i, j: (bi, q_off + i, 0)),
            pl.BlockSpec((1, tk, wq), lambda bi, i, j: (bi, k_off + j, 0)),
            pl.BlockSpec((1, tk, wq), lambda bi, i, j: (bi, k_off + j, 0)),
        ],
        out_specs=pl.BlockSpec((1, tq, heads * MLA_DV), lambda bi, i, j: (bi, i, 0)),
        out_shape=jax.ShapeDtypeStruct((b, q_len, heads * MLA_DV), F32),
        scratch_shapes=[
            pltpu.VMEM((heads, tq, MLA_DK_PAD), F32),
            pltpu.VMEM((heads, tq, MLA_DK_PAD), F32),
        ],
        compiler_params=_cparams("parallel", "parallel", "arbitrary"),
        name="mla_attn",
    )(q, k, v)


def _layer_norm(r, gain, bias):
    mu = jnp.mean(r, axis=-1, keepdims=True)
    rc = r - mu
    var = jnp.mean(rc * rc, axis=-1, keepdims=True)
    return rc * lax.rsqrt(var + EPS) * gain + bias


def _merge_kernel(yaf_ref, yab_ref, yb_ref, yc_ref, ga_ref, gb_ref, gc_ref, x_ref, g1_ref,
                  wa_ref, wb_ref, wc_ref, wo_ref, lng_ref, lnb_ref, o_ref):
    ya = (yaf_ref[0] + yab_ref[0]).astype(BF16)
    y = (jax.nn.sigmoid(ga_ref[0]) * jnp.dot(ya, wa_ref[...], preferred_element_type=F32)
         + jax.nn.sigmoid(gb_ref[0]) * jnp.dot(yb_ref[0].astype(BF16), wb_ref[...], preferred_element_type=F32)
         + jax.nn.sigmoid(gc_ref[0]) * jnp.dot(yc_ref[0].astype(BF16), wc_ref[...], preferred_element_type=F32))
    mix = jnp.dot(y.astype(BF16), wo_ref[...], preferred_element_type=F32)
    r = DEEPNORM_ALPHA * x_ref[0] + g1_ref[0] * mix
    o_ref[0] = _layer_norm(r, lng_ref[...], lnb_ref[...])


def _merge(yaf, yab, yb, yc, p, x, g1, wa, wb, wc, wo, lng, lnb):
    b, t, d = x.shape
    tm = _tile(t, 512)
    wbr = 512
    return pl.pallas_call(
        _merge_kernel,
        grid=(b, t // tm),
        in_specs=[
            pl.BlockSpec((1, tm, wbr), lambda bi, i: (bi, i, 0)),
            pl.BlockSpec((1, tm, wbr), lambda bi, i: (bi, i, 0)),
            pl.BlockSpec((1, tm, wbr), lambda bi, i: (bi, i, 0)),
            pl.BlockSpec((1, tm, wbr), lambda bi, i: (bi, i, 0)),
            pl.BlockSpec((1, tm, d), lambda bi, i: (bi, i, C_GATE_A // d)),
            pl.BlockSpec((1, tm, d), lambda bi, i: (bi, i, C_GATE_B // d)),
            pl.BlockSpec((1, tm, d), lambda bi, i: (bi, i, C_GATE_C // d)),
            pl.BlockSpec((1, tm, d), lambda bi, i: (bi, i, 0)),
            pl.BlockSpec((1, 1, d), lambda bi, i: (bi, 0, 0)),
            _const_spec((wbr, d)), _const_spec((wbr, d)), _const_spec((wbr, d)), _const_spec((d, d)),
            _const_spec((1, d)), _const_spec((1, d)),
        ],
        out_specs=pl.BlockSpec((1, tm, d), lambda bi, i: (bi, i, 0)),
        out_shape=jax.ShapeDtypeStruct((b, t, d), F32),
        compiler_params=_cparams("parallel", "parallel"),
        name="merge",
    )(yaf, yab, yb, yc, p, p, p, x, g1, wa, wb, wc, wo, lng, lnb)


def _ffn_kernel(x_ref, sc_ref, sh_ref, g2_ref, w1_ref, w2_ref, lng_ref, lnb_ref, o_ref, *, ff_chunk):
    x = x_ref[0]
    h = (x * (1.0 + sc_ref[0]) + sh_ref[0]).astype(BF16)
    acc = jnp.zeros(x.shape, F32)
    for j in range(D_FF // ff_chunk):
        u = jnp.dot(h, w1_ref[:, j * ff_chunk:(j + 1) * ff_chunk], preferred_element_type=F32)
        u = jnp.square(jnp.maximum(u, 0.0)).astype(BF16)
        acc = acc + jnp.dot(u, w2_ref[j * ff_chunk:(j + 1) * ff_chunk, :], preferred_element_type=F32)
    r = DEEPNORM_ALPHA * x + g2_ref[0] * acc
    o_ref[0] = _layer_norm(r, lng_ref[...], lnb_ref[...])


def _ffn(x, sc, sh, g2, w1, w2, lng, lnb):
    b, t, d = x.shape
    tm = _tile(t, 512)
    kern = functools.partial(_ffn_kernel, ff_chunk=1024)
    return pl.pallas_call(
        kern,
        grid=(b, t // tm),
        in_specs=[
            pl.BlockSpec((1, tm, d), lambda bi, i: (bi, i, 0)),
            pl.BlockSpec((1, 1, d), lambda bi, i: (bi, 0, 0)),
            pl.BlockSpec((1, 1, d), lambda bi, i: (bi, 0, 0)),
            pl.BlockSpec((1, 1, d), lambda bi, i: (bi, 0, 0)),
            _const_spec((d, D_FF)), _const_spec((D_FF, d)),
            _const_spec((1, d)), _const_spec((1, d)),
        ],
        out_specs=pl.BlockSpec((1, tm, d), lambda bi, i: (bi, i, 0)),
        out_shape=jax.ShapeDtypeStruct((b, t, d), F32),
        compiler_params=_cparams("parallel", "parallel"),
        name="ffn",
    )(x, sc, sh, g2, w1, w2, lng, lnb)


def _rope_tables(n_tok, rot_dim):
    t = jnp.arange(n_tok)
    row = (t // GRID_W).astype(F32)
    col = (t % GRID_W).astype(F32)
    n_freq = rot_dim // 4
    inv_freq = ROPE_BASE ** (-2.0 * jnp.arange(n_freq, dtype=F32) / (rot_dim // 2))
    ang = jnp.concatenate([row[:, None] * inv_freq, col[:, None] * inv_freq], axis=-1)
    return jnp.cos(ang), jnp.sin(ang)


def _ret_rope_tables(n_tok):
    cos, sin = _rope_tables(n_tok, RET_DK)
    cos_h = jnp.concatenate([cos, cos], axis=1)
    sin_h = jnp.concatenate([-sin, sin], axis=1)
    return jnp.tile(cos_h, (1, 2)), jnp.tile(sin_h, (1, 2))


def _mla_rope_tables(n_tok):
    cos, sin = _rope_tables(n_tok, MLA_ROPE)
    ones = jnp.ones((n_tok, MLA_NOPE), F32)
    zeros = jnp.zeros((n_tok, MLA_NOPE), F32)
    pad = jnp.zeros((n_tok, MLA_DK_PAD - MLA_NOPE - MLA_ROPE), F32)
    cos_h = jnp.concatenate([ones, cos, cos, pad], axis=1)
    sin_h = jnp.concatenate([zeros, -sin, sin, pad], axis=1)
    return cos_h, sin_h


def _pack_w_in(w):
    d = w.shape[0]
    return jnp.concatenate([w[:, :4096], w[:, 4128:7200], w[:, 4096:4128],
                            jnp.zeros((d, P_WIDTH - 7200), w.dtype)], axis=1).astype(BF16)


def _pack_mla_weights(w_qup, w_kvup):
    r = w_qup.shape[0]
    wq = w_qup.reshape(r, MLA_HEADS, MLA_NOPE + MLA_ROPE)
    wq = jnp.pad(wq, ((0, 0), (0, 0), (0, MLA_DK_PAD - MLA_NOPE - MLA_ROPE))).reshape(r, MLA_HEADS * MLA_DK_PAD)
    wkv = w_kvup.reshape(r, MLA_HEADS, MLA_NOPE + MLA_DV)
    wk = jnp.pad(wkv[:, :, :MLA_NOPE], ((0, 0), (0, 0), (0, MLA_DK_PAD - MLA_NOPE))).reshape(r, MLA_HEADS * MLA_DK_PAD)
    wv = jnp.pad(wkv[:, :, MLA_NOPE:], ((0, 0), (0, 0), (0, MLA_DK_PAD - MLA_DV))).reshape(r, MLA_HEADS * MLA_DK_PAD)
    return wq.astype(BF16), wk.astype(BF16), wv.astype(BF16)


def kernel(x, c, ctx, c_ctx, w_ada, b_ada, w_in, ret_log_decay, ret_gn_gain, na_rpb, mla_q_norm, mla_w_qup,
           mla_kv_norm, mla_w_kvup, w_branch_ret, w_branch_na, w_branch_mla, w_out, w_ff1, w_ff2, ln_gain, ln_bias):
    depth = w_ada.shape[0]
    b, t, d = x.shape
    lz = ctx.shape[1]
    rows = t // GRID_W

    cc = jnp.zeros((8, d), F32).at[:b].set(c).at[b].set(c_ctx)
    mod = _ada(cc, w_ada, b_ada)

    cos_r, sin_r = _ret_rope_tables(t)
    cos_m, sin_m = _mla_rope_tables(t)
    cos_m = jnp.concatenate([cos_m, jnp.ones((lz, MLA_DK_PAD), F32)], axis=0)
    sin_m = jnp.concatenate([sin_m, jnp.zeros((lz, MLA_DK_PAD), F32)], axis=0)
    cos_rz, sin_rz = cos_r[:lz], sin_r[:lz]
    na_bias = _na_bias_tables(na_rpb, rows)
    s_zero = jnp.zeros((b, 2, RET_HEADS, RET_DK, RET_DV), F32)

    z = ctx
    for l in range(depth):
        need_ctx = l < depth - 1
        mx = mod[l, :b].reshape(b, 6, 1, d)
        mz = jnp.broadcast_to(mod[l, b].reshape(1, 6, 1, d), (b, 6, 1, d))
        sh1x, sc1x, g1x, sh2x, sc2x, g2x = [mx[:, i] for i in range(6)]
        sh1z, sc1z, g1z, sh2z, sc2z, g2z = [mz[:, i] for i in range(6)]

        w_in_p = _pack_w_in(w_in[l])
        px = _inproj(x, sc1x, sh1x, w_in_p)
        pz = _inproj(z, sc1z, sh1z, w_in_p)

        lg = jnp.log1p(-jnp.exp(ret_log_decay[l].astype(F32))).reshape(2 * RET_HEADS)
        gn_gain = ret_gn_gain[l].reshape(1, RET_HEADS * RET_DV)
        yaf_z, yab_z, s_ctx = _retention(pz, lg, cos_rz, sin_rz, gn_gain, s_zero, use_rope=False)
        yaf_x, yab_x, _ = _retention(px, lg, cos_r, sin_r, gn_gain, s_ctx, use_rope=True)

        yb_x = _na(px, pz, na_bias[l])

        wq, wk, wv = _pack_mla_weights(mla_w_qup[l], mla_w_kvup[l])
        qn = mla_q_norm[l].reshape(1, MLA_RANK)
        kvn = mla_kv_norm[l].reshape(1, MLA_RANK)
        q_all, k_all, v_all = _mlaprep(px, pz, qn, kvn, wq, wk, wv, cos_m, sin_m)
        yc_x = _mla_attn(q_all, k_all, v_all, 0, t, 0, t + lz, 512, 768)

        wa = w_branch_ret[l].astype(BF16)
        wb = w_branch_na[l].astype(BF16)
        wc = w_branch_mla[l].astype(BF16)
        wo = w_out[l].astype(BF16)
        w1 = w_ff1[l].astype(BF16)
        w2 = w_ff2[l].astype(BF16)
        lng1, lnb1 = ln_gain[l, 0].reshape(1, d), ln_bias[l, 0].reshape(1, d)
        lng2, lnb2 = ln_gain[l, 1].reshape(1, d), ln_bias[l, 1].reshape(1, d)

        x1 = _merge(yaf_x, yab_x, yb_x, yc_x, px, x, g1x, wa, wb, wc, wo, lng1, lnb1)
        x = _ffn(x1, sc2x, sh2x, g2x, w1, w2, lng2, lnb2)

        if need_ctx:
            wna = NA_HEADS * NA_DH
            yb_z = _flash(pz, pz, pz, C_NA_Q // wna, C_NA_K // wna, C_NA_V // wna,
                          NA_HEADS, NA_DH, NA_DH, NA_DH ** -0.5 * LOG2_E, 256, 256)
            yc_z = _mla_attn(q_all, k_all, v_all, t, lz, t, lz, lz, lz)
            z1 = _merge(yaf_z, yab_z, yb_z, yc_z, pz, z, g1z, wa, wb, wc, wo, lng1, lnb1)
            z = _ffn(z1, sc2z, sh2z, g2z, w1, w2, lng2, lnb2)
    return x
```

```python
import functools

import numpy as np
import jax
import jax.numpy as jnp
from jax import lax
from jax.experimental import pallas as pl
from jax.experimental.pallas import tpu as pltpu

F32 = jnp.float32
BF16 = jnp.bfloat16

D_MODEL = 1024
GRID_W = 64
RET_HEADS = 4
RET_DK = 64
RET_DV = 128
RET_CHUNK = 128
NA_HEADS = 8
NA_DH = 64
NA_KH = 8
NA_KW = 16
MLA_HEADS = 8
MLA_RANK = 256
MLA_NOPE = 64
MLA_ROPE = 32
MLA_DV = 64
MLA_DK_PAD = 128
D_FF = 4 * D_MODEL
ROPE_BASE = 10000.0
EPS = 1e-5
DEPTH_FOR_NORM = 4
DEEPNORM_ALPHA = (2 * DEPTH_FOR_NORM) ** 0.25
MASK_VALUE = -1e30
LOG2_E = 1.4426950408889634

C_RET_Q, C_RET_K, C_RET_V, C_RET_GF, C_RET_GB = 0, 256, 512, 1024, 1536
C_NA_Q, C_NA_K, C_NA_V = 2048, 2560, 3072
C_MLA_Q, C_MLA_KV = 3584, 3840
C_GATE_A, C_GATE_B, C_GATE_C = 4096, 5120, 6144
C_MLA_KR = 7168
P_WIDTH = 7296
P_COL_TILE = 2432

NA_QROWS = 4
NA_KROWS = 12
NA_QB = NA_QROWS * GRID_W
NA_KB = NA_KROWS * GRID_W

VMEM_LIMIT = 56 * 1024 * 1024


def _cparams(*sem):
    return pltpu.CompilerParams(dimension_semantics=sem, vmem_limit_bytes=VMEM_LIMIT)


def _tile(n, pref, mult=1):
    t = min(n, pref) // mult * mult
    while n % t:
        t -= mult
    return t


def _const_spec(shape):
    nd = len(shape)
    return pl.BlockSpec(shape, lambda *_: (0,) * nd)


def _ada_kernel(c_ref, w_ref, b_ref, o_ref):
    c = c_ref[...]
    a = c * jax.nn.sigmoid(c)
    o_ref[0] = jnp.dot(a.astype(BF16), w_ref[0].astype(BF16), preferred_element_type=F32) + b_ref[0]


def _ada(cc, w_ada, b_ada):
    depth, d, n = w_ada.shape
    tn = 1024
    return pl.pallas_call(
        _ada_kernel,
        grid=(depth, n // tn),
        in_specs=[
            pl.BlockSpec((8, d), lambda l, j: (0, 0)),
            pl.BlockSpec((1, d, tn), lambda l, j: (l, 0, j)),
            pl.BlockSpec((1, 1, tn), lambda l, j: (l, 0, j)),
        ],
        out_specs=pl.BlockSpec((1, 8, tn), lambda l, j: (l, 0, j)),
        out_shape=jax.ShapeDtypeStruct((depth, 8, n), F32),
        compiler_params=_cparams("parallel", "parallel"),
        name="ada",
    )(cc, w_ada, b_ada.reshape(depth, 1, n))


def _inproj_kernel(x_ref, sc_ref, sh_ref, w_ref, o_ref):
    h = x_ref[0] * (1.0 + sc_ref[0]) + sh_ref[0]
    o_ref[0] = jnp.dot(h.astype(BF16), w_ref[...], preferred_element_type=F32)


def _inproj(x, sc, sh, w):
    b, t, d = x.shape
    tm = _tile(t, 512)
    tn = P_COL_TILE
    return pl.pallas_call(
        _inproj_kernel,
        grid=(P_WIDTH // tn, b, t // tm),
        in_specs=[
            pl.BlockSpec((1, tm, d), lambda j, bi, i: (bi, i, 0)),
            pl.BlockSpec((1, 1, d), lambda j, bi, i: (bi, 0, 0)),
            pl.BlockSpec((1, 1, d), lambda j, bi, i: (bi, 0, 0)),
            pl.BlockSpec((d, tn), lambda j, bi, i: (0, j)),
        ],
        out_specs=pl.BlockSpec((1, tm, tn), lambda j, bi, i: (bi, i, j)),
        out_shape=jax.ShapeDtypeStruct((b, t, P_WIDTH), F32),
        compiler_params=_cparams("parallel", "parallel", "parallel"),
        name="inproj",
    )(x, sc, sh, w)


def _swap_halves(x, half):
    n = x.shape[-1]
    lane = lax.broadcasted_iota(jnp.int32, x.shape, x.ndim - 1)
    first = (lane % (2 * half)) < half
    return jnp.where(first, pltpu.roll(x, n - half, x.ndim - 1), pltpu.roll(x, half, x.ndim - 1))


def _ret_kernel(lg_ref, qf_ref, qb_ref, kf_ref, kb_ref, vf_ref, vb_ref, gf_ref, gb_ref,
                cosf_ref, cosb_ref, sinf_ref, sinb_ref, gain_ref, s0_ref,
                yf_ref, yb_ref, sf_ref, s_scr, *, use_rope, n_chunks):
    c = pl.program_id(0)
    batch = qf_ref.shape[0]
    cc = RET_CHUNK

    @pl.when(c == 0)
    def _():
        s_scr[...] = s0_ref[...]

    gain = gain_ref[...]
    row = lax.broadcasted_iota(jnp.int32, (cc, cc), 0).astype(F32)
    col = lax.broadcasted_iota(jnp.int32, (cc, cc), 1).astype(F32)
    pos = lax.broadcasted_iota(jnp.int32, (cc, RET_DK), 0).astype(F32)
    dirs = (
        (qf_ref, kf_ref, vf_ref, gf_ref, cosf_ref, sinf_ref, yf_ref, row - col, pos + 1.0, cc - 1.0 - pos),
        (qb_ref, kb_ref, vb_ref, gb_ref, cosb_ref, sinb_ref, yb_ref, col - row, cc - pos, pos),
    )
    for d, (q_ref, k_ref, v_ref, g_ref, cos_ref, sin_ref, y_ref, diff, q_exp, k_exp) in enumerate(dirs):
        if use_rope:
            cos = jnp.concatenate([cos_ref[...], cos_ref[...]], axis=1)
            sin = jnp.concatenate([sin_ref[...], sin_ref[...]], axis=1)
        qs, ks = [], []
        for b in range(batch):
            q = q_ref[b]
            k = k_ref[b] * (RET_DK ** -0.5)
            if use_rope:
                q = q * cos + _swap_halves(q, RET_DK // 2) * sin
                k = k * cos + _swap_halves(k, RET_DK // 2) * sin
            qs.append(q)
            ks.append(k)
        for h in range(RET_HEADS):
            lg = lg_ref[d * RET_HEADS + h]
            decay = jnp.where(diff >= 0, jnp.exp(lg * jnp.maximum(diff, 0.0)), 0.0)
            q_decay = jnp.exp(lg * q_exp)
            k_decay = jnp.exp(lg * k_exp)
            chunk_decay = jnp.exp(lg * jnp.full((RET_DK, RET_DV), float(cc), F32))
            sl = slice(h * RET_DV, (h + 1) * RET_DV)
            for b in range(batch):
                qh = qs[b][:, h * RET_DK:(h + 1) * RET_DK]
                kh = ks[b][:, h * RET_DK:(h + 1) * RET_DK]
                vh = v_ref[b, :, sl].astype(BF16)
                scores = lax.dot_general(qh.astype(BF16), kh.astype(BF16), (((1,), (1,)), ((), ())),
                                         preferred_element_type=F32) * decay
                inner = jnp.dot(scores.astype(BF16), vh, preferred_element_type=F32)
                state = s_scr[b, d, h]
                cross = jnp.dot((qh * q_decay).astype(BF16), state.astype(BF16), preferred_element_type=F32)
                o = inner + cross
                kv = jnp.dot((kh * k_decay).T.astype(BF16), vh, preferred_element_type=F32)
                s_scr[b, d, h] = state * chunk_decay + kv
                mu = jnp.mean(o, axis=-1, keepdims=True)
                oc = o - mu
                var = jnp.mean(oc * oc, axis=-1, keepdims=True)
                gate = g_ref[b, :, sl]
                y_ref[b, :, sl] = (gate * jax.nn.sigmoid(gate)) * (oc * lax.rsqrt(var + EPS) * gain[:, sl])

    @pl.when(c == n_chunks - 1)
    def _():
        sf_ref[...] = s_scr[...]


def _retention(p, lg, cos, sin, gain, s0, use_rope):
    b, t, _ = p.shape
    cc = RET_CHUNK
    n = t // cc
    wv = RET_HEADS * RET_DV

    def pspec(width, col, backward):
        if backward:
            return pl.BlockSpec((b, cc, width), lambda ci, lg_: (0, n - 1 - ci, col // width))
        return pl.BlockSpec((b, cc, width), lambda ci, lg_: (0, ci, col // width))

    def tspec(backward):
        if backward:
            return pl.BlockSpec((cc, 128), lambda ci, lg_: (n - 1 - ci, 0))
        return pl.BlockSpec((cc, 128), lambda ci, lg_: (ci, 0))

    state_spec = pl.BlockSpec((b, 2, RET_HEADS, RET_DK, RET_DV), lambda ci, lg_: (0, 0, 0, 0, 0))
    kern = functools.partial(_ret_kernel, use_rope=use_rope, n_chunks=n)
    grid_spec = pltpu.PrefetchScalarGridSpec(
        num_scalar_prefetch=1,
        grid=(n,),
        in_specs=[
            pspec(256, C_RET_Q, False), pspec(256, C_RET_Q, True),
            pspec(256, C_RET_K, False), pspec(256, C_RET_K, True),
            pspec(wv, C_RET_V, False), pspec(wv, C_RET_V, True),
            pspec(wv, C_RET_GF, False), pspec(wv, C_RET_GB, True),
            tspec(False), tspec(True), tspec(False), tspec(True),
            pl.BlockSpec((1, wv), lambda ci, lg_: (0, 0)),
            state_spec,
        ],
        out_specs=[
            pl.BlockSpec((b, cc, wv), lambda ci, lg_: (0, ci, 0)),
            pl.BlockSpec((b, cc, wv), lambda ci, lg_: (0, n - 1 - ci, 0)),
            state_spec,
        ],
        scratch_shapes=[pltpu.VMEM((b, 2, RET_HEADS, RET_DK, RET_DV), F32)],
    )
    return pl.pallas_call(
        kern,
        grid_spec=grid_spec,
        out_shape=[
            jax.ShapeDtypeStruct((b, t, wv), F32),
            jax.ShapeDtypeStruct((b, t, wv), F32),
            jax.ShapeDtypeStruct((b, 2, RET_HEADS, RET_DK, RET_DV), F32),
        ],
        compiler_params=_cparams("arbitrary"),
        name="retention",
    )(lg, p, p, p, p, p, p, p, p, cos, cos, sin, sin, gain, s0)


def _na_tables(rows):
    groups = rows // NA_QROWS
    cols = np.arange(GRID_W)
    c0 = np.clip(cols - NA_KW // 2, 0, GRID_W - NA_KW)
    col_ok = (cols[None, :] >= c0[:, None]) & (cols[None, :] < c0[:, None] + NA_KW)
    dc = cols[None, :] - cols[:, None] + (NA_KW - 1)
    onehot = (dc[None] == np.arange(2 * NA_KW - 1)[:, None, None]) & col_ok[None]
    row_bias = np.full((3, NA_QROWS, NA_KROWS), 2 * NA_KH - 1, np.int64)
    for ti, g in enumerate((0, 1, groups - 1)):
        ws = int(np.clip(NA_QROWS * g - NA_KH // 2, 0, rows - NA_KROWS))
        for lr in range(NA_QROWS):
            r = NA_QROWS * g + lr
            r0 = int(np.clip(r - NA_KH // 2, 0, rows - NA_KH))
            for kr in range(NA_KROWS):
                if r0 <= ws + kr < r0 + NA_KH:
                    row_bias[ti, lr, kr] = ws + kr - r + (NA_KH - 1)
    return onehot, col_ok, row_bias


def _na_bias_tables(na_rpb, rows):
    onehot, col_ok, row_bias = _na_tables(rows)
    depth, heads = na_rpb.shape[:2]
    toep = jnp.einsum('lhdj,jck->lhdck', na_rpb, onehot.astype(np.float32), precision=lax.Precision.HIGHEST)
    toep = jnp.where(col_ok, toep, MASK_VALUE)
    masked = jnp.full(toep.shape[:2] + (1, GRID_W, GRID_W), MASK_VALUE, F32)
    toep = jnp.concatenate([toep, masked], axis=2)
    pick = (row_bias.reshape(-1, 1) == np.arange(2 * NA_KH)[None, :]).astype(np.float32)
    tab = jnp.einsum('nd,lhdck->lhnck', pick, toep, precision=lax.Precision.HIGHEST)
    tab = tab.reshape(depth, heads, 3, NA_QROWS, NA_KROWS, GRID_W, GRID_W)
    return tab.transpose(0, 2, 1, 3, 5, 4, 6).reshape(depth, 3, heads, NA_QB, NA_KB)


def _na_kernel(q_ref, k0_ref, k1_ref, k2_ref, v0_ref, v1_ref, v2_ref, kz_ref, vz_ref, bias_ref, o_ref):
    kb = k0_ref.shape[1]
    lanes = 2 * NA_DH
    low = lax.broadcasted_iota(jnp.int32, (q_ref.shape[1], lanes), 1) < NA_DH
    for pr in range(NA_HEADS // 2):
        sl = slice(pr * lanes, (pr + 1) * lanes)
        qp = q_ref[0, :, sl] * (NA_DH ** -0.5)
        qm = (jnp.where(low, qp, 0.0).astype(BF16), jnp.where(low, 0.0, qp).astype(BF16))
        ks = [r[0, :, sl].astype(BF16) for r in (k0_ref, k1_ref, k2_ref, kz_ref)]
        vs = [r[0, :, sl].astype(BF16) for r in (v0_ref, v1_ref, v2_ref, vz_ref)]
        outs = []
        for hh in range(2):
            h = 2 * pr + hh
            cols = []
            for j in range(4):
                sj = lax.dot_general(qm[hh], ks[j], (((1,), (1,)), ((), ())), preferred_element_type=F32)
                if j < 3:
                    sj = sj + bias_ref[0, 0, h, :, j * kb:(j + 1) * kb]
                cols += [sj[:, i * lanes:(i + 1) * lanes] for i in range(sj.shape[1] // lanes)]
            m = jnp.max(functools.reduce(jnp.maximum, cols), axis=-1, keepdims=True)
            ps = [jnp.exp(cj - m) for cj in cols]
            l = jnp.sum(functools.reduce(jnp.add, ps), axis=-1, keepdims=True)
            o = None
            per = len(cols) // 4
            for j in range(4):
                pj = jnp.concatenate([pc.astype(BF16) for pc in ps[j * per:(j + 1) * per]], axis=1)
                oj = jnp.dot(pj, vs[j], preferred_element_type=F32)
                o = oj if o is None else o + oj
            outs.append(o / l)
        o_ref[0, :, sl] = jnp.where(low, outs[0], outs[1]).astype(o_ref.dtype)


def _na(px, pz, bias, layer):
    b, t, _ = px.shape
    lz = pz.shape[1]
    groups = t // NA_QB
    w = NA_HEADS * NA_DH
    kblk = NA_KB // 3
    assert lz == kblk and groups >= 3

    def kspec(col, off):
        return pl.BlockSpec((1, kblk, w), lambda bi, g: (bi, jnp.clip(g - 1, 0, groups - 3) + off, col // w))

    def tab(g):
        return jnp.where(g == 0, 0, jnp.where(g == groups - 1, 2, 1))

    return pl.pallas_call(
        _na_kernel,
        grid=(b, groups),
        in_specs=[
            pl.BlockSpec((1, NA_QB, w), lambda bi, g: (bi, g, C_NA_Q // w)),
            kspec(C_NA_K, 0), kspec(C_NA_K, 1), kspec(C_NA_K, 2),
            kspec(C_NA_V, 0), kspec(C_NA_V, 1), kspec(C_NA_V, 2),
            pl.BlockSpec((1, lz, w), lambda bi, g: (bi, 0, C_NA_K // w)),
            pl.BlockSpec((1, lz, w), lambda bi, g: (bi, 0, C_NA_V // w)),
            pl.BlockSpec((1, 1, NA_HEADS, NA_QB, NA_KB), lambda bi, g: (layer, tab(g), 0, 0, 0)),
        ],
        out_specs=pl.BlockSpec((1, NA_QB, w), lambda bi, g: (bi, g, 0)),
        out_shape=jax.ShapeDtypeStruct((b, t, w), BF16),
        compiler_params=_cparams("parallel", "arbitrary"),
        name="na",
    )(px, px, px, px, px, px, px, pz, pz, bias)


def _rms(x, gain):
    return x * lax.rsqrt(jnp.mean(x * x, axis=-1, keepdims=True) + EPS) * gain


def _mlaprep_kernel(xq_ref, xkv_ref, xkr_ref, zq_ref, zkv_ref, zkr_ref, qn_ref, kvn_ref, wq_ref, wk_ref, wv_ref,
                    cos_ref, sin_ref, q_ref, k_ref, v_ref, *, nx):
    latent = pl.program_id(1) < nx
    pq = jnp.where(latent, xq_ref[0], zq_ref[0])
    pkv = jnp.where(latent, xkv_ref[0], zkv_ref[0])
    pkr = jnp.where(latent, xkr_ref[0], zkr_ref[0])
    hq = _rms(pq, qn_ref[...]).astype(BF16)
    hkv = _rms(pkv, kvn_ref[...]).astype(BF16)
    q = jnp.dot(hq, wq_ref[...], preferred_element_type=F32)
    k = jnp.dot(hkv, wk_ref[...], preferred_element_type=F32)
    v = jnp.dot(hkv, wv_ref[...], preferred_element_type=F32)
    kr = pltpu.roll(pkr, MLA_NOPE, 1)
    k = k + jnp.concatenate([kr] * MLA_HEADS, axis=1)
    cos = jnp.concatenate([cos_ref[...]] * MLA_HEADS, axis=1)
    sin = jnp.concatenate([sin_ref[...]] * MLA_HEADS, axis=1)
    q = q * cos + _swap_halves(q, MLA_ROPE // 2) * sin
    k = k * cos + _swap_halves(k, MLA_ROPE // 2) * sin
    q_ref[0] = (q * ((MLA_NOPE + MLA_ROPE) ** -0.5 * LOG2_E)).astype(BF16)
    k_ref[0] = k.astype(BF16)
    v_ref[0] = v.astype(BF16)


def _mlaprep(px, pz, qn, kvn, wq, wk, wv, cos, sin):
    b, t, _ = px.shape
    lz = pz.shape[1]
    tm = lz
    nx = t // tm
    wqk = MLA_HEADS * MLA_DK_PAD
    wvv = MLA_HEADS * MLA_DV

    def xspec(width, col):
        return pl.BlockSpec((1, tm, width), lambda bi, i: (bi, jnp.minimum(i, nx - 1), col // width))

    def zspec(width, col):
        return pl.BlockSpec((1, tm, width), lambda bi, i: (bi, 0, col // width))

    out_spec = pl.BlockSpec((1, tm, wqk), lambda bi, i: (bi, i, 0))
    kern = functools.partial(_mlaprep_kernel, nx=nx)
    return pl.pallas_call(
        kern,
        grid=(b, nx + 1),
        in_specs=[
            xspec(MLA_RANK, C_MLA_Q), xspec(MLA_RANK, C_MLA_KV), xspec(128, C_MLA_KR),
            zspec(MLA_RANK, C_MLA_Q), zspec(MLA_RANK, C_MLA_KV), zspec(128, C_MLA_KR),
            _const_spec((1, MLA_RANK)), _const_spec((1, MLA_RANK)),
            _const_spec((MLA_RANK, wqk)), _const_spec((MLA_RANK, wqk)), _const_spec((MLA_RANK, wvv)),
            pl.BlockSpec((tm, 128), lambda bi, i: (i, 0)),
            pl.BlockSpec((tm, 128), lambda bi, i: (i, 0)),
        ],
        out_specs=[out_spec, out_spec, pl.BlockSpec((1, tm, wvv), lambda bi, i: (bi, i, 0))],
        out_shape=[jax.ShapeDtypeStruct((b, t + lz, wqk), BF16)] * 2
        + [jax.ShapeDtypeStruct((b, t + lz, wvv), BF16)],
        compiler_params=_cparams("parallel", "parallel"),
        name="mlaprep",
    )(px, px, px, pz, pz, pz, qn, kvn, wq, wk, wv, cos, sin)


def _flash_kernel(q_ref, k_ref, v_ref, o_ref, m_scr, l_scr, acc_scr, *, heads, dk, dv, scale, nk):
    ki = pl.program_id(2)
    tq = q_ref.shape[1]
    tk = k_ref.shape[1]
    lanes = 2 * dv

    @pl.when(ki == 0)
    def _():
        m_scr[...] = jnp.full(m_scr.shape, -jnp.inf, F32)
        l_scr[...] = jnp.zeros(l_scr.shape, F32)
        acc_scr[...] = jnp.zeros(acc_scr.shape, F32)

    low = lax.broadcasted_iota(jnp.int32, (tq, lanes), 1) < dv
    for pr in range(heads // 2):
        vp = v_ref[0, :, pr * lanes:(pr + 1) * lanes].astype(BF16)
        alphas, pvs = [], []
        for h in (2 * pr, 2 * pr + 1):
            qh = q_ref[0, :, h * dk:(h + 1) * dk]
            if scale != 1.0:
                qh = qh * scale
            kh = k_ref[0, :, h * dk:(h + 1) * dk]
            s = lax.dot_general(qh.astype(BF16), kh.astype(BF16), (((1,), (1,)), ((), ())),
                                preferred_element_type=F32)
            cols = [s[:, j * lanes:(j + 1) * lanes] for j in range(tk // lanes)]
            m_prev = m_scr[h]
            m_tile = jnp.max(functools.reduce(jnp.maximum, cols), axis=-1, keepdims=True)
            m_new = jnp.maximum(m_prev, m_tile)
            alpha = jnp.exp2(m_prev - m_new)
            ps = [jnp.exp2(cj - m_new) for cj in cols]
            l_scr[h] = alpha * l_scr[h] + functools.reduce(jnp.add, ps)
            m_scr[h] = m_new
            p = jnp.concatenate([pj.astype(BF16) for pj in ps], axis=1)
            pvs.append(jnp.dot(p, vp, preferred_element_type=F32))
            alphas.append(alpha)
        acc_scr[pr] = acc_scr[pr] * jnp.where(low, alphas[0], alphas[1]) + jnp.where(low, pvs[0], pvs[1])

    @pl.when(ki == nk - 1)
    def _():
        for pr in range(heads // 2):
            l0 = jnp.sum(l_scr[2 * pr], axis=-1, keepdims=True)
            l1 = jnp.sum(l_scr[2 * pr + 1], axis=-1, keepdims=True)
            o_ref[0, :, pr * lanes:(pr + 1) * lanes] = (acc_scr[pr] / jnp.where(low, l0, l1)).astype(o_ref.dtype)


def _flash(q, k, v, qcol, kcol, vcol, heads, dk, dv, scale, tq_pref, tk_pref):
    b, tq_all, _ = q.shape
    tk_all = k.shape[1]
    tq = _tile(tq_all, tq_pref)
    tk = _tile(tk_all, tk_pref)
    nk = tk_all // tk
    assert heads % 2 == 0 and 2 * dv == 128 and tk % 128 == 0
    kern = functools.partial(_flash_kernel, heads=heads, dk=dk, dv=dv, scale=scale, nk=nk)
    return pl.pallas_call(
        kern,
        grid=(b, tq_all // tq, nk),
        in_specs=[
            pl.BlockSpec((1, tq, heads * dk), lambda bi, i, j: (bi, i, qcol)),
            pl.BlockSpec((1, tk, heads * dk), lambda bi, i, j: (bi, j, kcol)),
            pl.BlockSpec((1, tk, heads * dv), lambda bi, i, j: (bi, j, vcol)),
        ],
        out_specs=pl.BlockSpec((1, tq, heads * dv), lambda bi, i, j: (bi, i, 0)),
        out_shape=jax.ShapeDtypeStruct((b, tq_all, heads * dv), BF16),
        scratch_shapes=[
            pltpu.VMEM((heads, tq, 2 * dv), F32),
            pltpu.VMEM((heads, tq, 2 * dv), F32),
            pltpu.VMEM((heads // 2, tq, 2 * dv), F32),
        ],
        compiler_params=_cparams("parallel", "parallel", "arbitrary"),
        name="flash",
    )(q, k, v)


def _mla_attn_kernel(q_ref, k_ref, v_ref, o_ref, m_scr, l_scr, acc_scr, *, heads, nk):
    ki = pl.program_id(2)
    w = MLA_DK_PAD
    tq = q_ref.shape[1]

    @pl.when(ki == 0)
    def _():
        m_scr[...] = jnp.full(m_scr.shape, -jnp.inf, F32)
        l_scr[...] = jnp.zeros(l_scr.shape, F32)
        acc_scr[...] = jnp.zeros(acc_scr.shape, F32)

    low = lax.broadcasted_iota(jnp.int32, (tq, w), 1) < MLA_DV
    for pr in range(heads // 2):
        vp = v_ref[0, :, pr * w:(pr + 1) * w]
        alphas, pvs = [], []
        for h in (2 * pr, 2 * pr + 1):
            sl = slice(h * w, (h + 1) * w)
            s = lax.dot_general(q_ref[0, :, sl], k_ref[0, :, sl], (((1,), (1,)), ((), ())),
                                preferred_element_type=F32)
            cols = [s[:, j * w:(j + 1) * w] for j in range(s.shape[1] // w)]
            m_prev = m_scr[h]
            m_new = jnp.maximum(m_prev, jnp.max(functools.reduce(jnp.maximum, cols), axis=-1, keepdims=True))
            alpha = jnp.exp2(m_prev - m_new)
            ps = [jnp.exp2(cj - m_new) for cj in cols]
            l_scr[h] = alpha * l_scr[h] + functools.reduce(jnp.add, ps)
            m_scr[h] = m_new
            p = jnp.concatenate([pj.astype(BF16) for pj in ps], axis=1)
            pvs.append(jnp.dot(p, vp, preferred_element_type=F32))
            alphas.append(alpha)
        acc_scr[pr] = acc_scr[pr] * jnp.where(low, alphas[0], alphas[1]) + jnp.where(low, pvs[0], pvs[1])

    @pl.when(ki == nk - 1)
    def _():
        for pr in range(heads // 2):
            l0 = jnp.sum(l_scr[2 * pr], axis=-1, keepdims=True)
            l1 = jnp.sum(l_scr[2 * pr + 1], axis=-1, keepdims=True)
            o_ref[0, :, pr * w:(pr + 1) * w] = (acc_scr[pr] / jnp.where(low, l0, l1)).astype(o_ref.dtype)


def _mla_attn(q, k, v, q_start, q_len, k_start, k_len, tq_pref, tk_pref):
    b, _, wq = q.shape
    heads = wq // MLA_DK_PAD
    tq = _tile(q_len, tq_pref, 8)
    tk = _tile(k_len, tk_pref, 128)
    assert q_start % tq == 0 and k_start % tk == 0
    q_off, k_off = q_start // tq, k_start // tk
    nk = k_len // tk
    kern = functools.partial(_mla_attn_kernel, heads=heads, nk=nk)
    return pl.pallas_call(
        kern,
        grid=(b, q_len // tq, nk),
        in_specs=[
            pl.BlockSpec((1, tq, wq), lambda bi, i, j: (bi, q_off + i, 0)),
            pl.BlockSpec((1, tk, wq), lambda bi, i, j: (bi, k_off + j, 0)),
            pl.BlockSpec((1, tk, heads * MLA_DV), lambda bi, i, j: (bi, k_off + j, 0)),
        ],
        out_specs=pl.BlockSpec((1, tq, heads * MLA_DV), lambda bi, i, j: (bi, i, 0)),
        out_shape=jax.ShapeDtypeStruct((b, q_len, heads * MLA_DV), BF16),
        scratch_shapes=[
            pltpu.VMEM((heads, tq, MLA_DK_PAD), F32),
            pltpu.VMEM((heads, tq, MLA_DK_PAD), F32),
            pltpu.VMEM((heads // 2, tq, MLA_DK_PAD), F32),
        ],
        compiler_params=_cparams("parallel", "parallel", "arbitrary"),
        name="mla_attn",
    )(q, k, v)


def _layer_norm(r, gain, bias):
    mu = jnp.mean(r, axis=-1, keepdims=True)
    rc = r - mu
    var = jnp.mean(rc * rc, axis=-1, keepdims=True)
    return rc * lax.rsqrt(var + EPS) * gain + bias


def _merge_kernel(yaf_ref, yab_ref, yb_ref, yc_ref, ga_ref, gb_ref, gc_ref, x_ref, g1_ref,
                  wa_ref, wb_ref, wc_ref, wo_ref, lng_ref, lnb_ref, o_ref):
    ya = (yaf_ref[0] + yab_ref[0]).astype(BF16)
    y = (jax.nn.sigmoid(ga_ref[0]) * jnp.dot(ya, wa_ref[...], preferred_element_type=F32)
         + jax.nn.sigmoid(gb_ref[0]) * jnp.dot(yb_ref[0].astype(BF16), wb_ref[...], preferred_element_type=F32)
         + jax.nn.sigmoid(gc_ref[0]) * jnp.dot(yc_ref[0].astype(BF16), wc_ref[...], preferred_element_type=F32))
    mix = jnp.dot(y.astype(BF16), wo_ref[...], preferred_element_type=F32)
    r = DEEPNORM_ALPHA * x_ref[0] + g1_ref[0] * mix
    o_ref[0] = _layer_norm(r, lng_ref[...], lnb_ref[...])


def _merge(yaf, yab, yb, yc, p, x, g1, wa, wb, wc, wo, lng, lnb):
    b, t, d = x.shape
    tm = _tile(t, 512)
    wbr = 512
    return pl.pallas_call(
        _merge_kernel,
        grid=(b, t // tm),
        in_specs=[
            pl.BlockSpec((1, tm, wbr), lambda bi, i: (bi, i, 0)),
            pl.BlockSpec((1, tm, wbr), lambda bi, i: (bi, i, 0)),
            pl.BlockSpec((1, tm, wbr), lambda bi, i: (bi, i, 0)),
            pl.BlockSpec((1, tm, wbr), lambda bi, i: (bi, i, 0)),
            pl.BlockSpec((1, tm, d), lambda bi, i: (bi, i, C_GATE_A // d)),
            pl.BlockSpec((1, tm, d), lambda bi, i: (bi, i, C_GATE_B // d)),
            pl.BlockSpec((1, tm, d), lambda bi, i: (bi, i, C_GATE_C // d)),
            pl.BlockSpec((1, tm, d), lambda bi, i: (bi, i, 0)),
            pl.BlockSpec((1, 1, d), lambda bi, i: (bi, 0, 0)),
            _const_spec((wbr, d)), _const_spec((wbr, d)), _const_spec((wbr, d)), _const_spec((d, d)),
            _const_spec((1, d)), _const_spec((1, d)),
        ],
        out_specs=pl.BlockSpec((1, tm, d), lambda bi, i: (bi, i, 0)),
        out_shape=jax.ShapeDtypeStruct((b, t, d), F32),
        compiler_params=_cparams("parallel", "parallel"),
        name="merge",
    )(yaf, yab, yb, yc, p, p, p, x, g1, wa, wb, wc, wo, lng, lnb)


def _ffn_kernel(x_ref, sc_ref, sh_ref, g2_ref, w1_ref, w2_ref, lng_ref, lnb_ref, o_ref, *, ff_chunk):
    x = x_ref[0]
    h = (x * (1.0 + sc_ref[0]) + sh_ref[0]).astype(BF16)
    acc = jnp.zeros(x.shape, F32)
    for j in range(D_FF // ff_chunk):
        u = jnp.dot(h, w1_ref[:, j * ff_chunk:(j + 1) * ff_chunk], preferred_element_type=F32)
        u = jnp.square(jnp.maximum(u, 0.0)).astype(BF16)
        acc = acc + jnp.dot(u, w2_ref[j * ff_chunk:(j + 1) * ff_chunk, :], preferred_element_type=F32)
    r = DEEPNORM_ALPHA * x + g2_ref[0] * acc
    o_ref[0] = _layer_norm(r, lng_ref[...], lnb_ref[...])


def _ffn(x, sc, sh, g2, w1, w2, lng, lnb):
    b, t, d = x.shape
    tm = _tile(t, 512)
    kern = functools.partial(_ffn_kernel, ff_chunk=1024)
    return pl.pallas_call(
        kern,
        grid=(b, t // tm),
        in_specs=[
            pl.BlockSpec((1, tm, d), lambda bi, i: (bi, i, 0)),
            pl.BlockSpec((1, 1, d), lambda bi, i: (bi, 0, 0)),
            pl.BlockSpec((1, 1, d), lambda bi, i: (bi, 0, 0)),
            pl.BlockSpec((1, 1, d), lambda bi, i: (bi, 0, 0)),
            _const_spec((d, D_FF)), _const_spec((D_FF, d)),
            _const_spec((1, d)), _const_spec((1, d)),
        ],
        out_specs=pl.BlockSpec((1, tm, d), lambda bi, i: (bi, i, 0)),
        out_shape=jax.ShapeDtypeStruct((b, t, d), F32),
        compiler_params=_cparams("parallel", "parallel"),
        name="ffn",
    )(x, sc, sh, g2, w1, w2, lng, lnb)


def _rope_tables(n_tok, rot_dim):
    t = jnp.arange(n_tok)
    row = (t // GRID_W).astype(F32)
    col = (t % GRID_W).astype(F32)
    n_freq = rot_dim // 4
    inv_freq = ROPE_BASE ** (-2.0 * jnp.arange(n_freq, dtype=F32) / (rot_dim // 2))
    ang = jnp.concatenate([row[:, None] * inv_freq, col[:, None] * inv_freq], axis=-1)
    return jnp.cos(ang), jnp.sin(ang)


def _ret_rope_tables(n_tok):
    cos, sin = _rope_tables(n_tok, RET_DK)
    cos_h = jnp.concatenate([cos, cos], axis=1)
    sin_h = jnp.concatenate([-sin, sin], axis=1)
    return jnp.tile(cos_h, (1, 2)), jnp.tile(sin_h, (1, 2))


def _mla_rope_tables(n_tok):
    cos, sin = _rope_tables(n_tok, MLA_ROPE)
    ones = jnp.ones((n_tok, MLA_NOPE), F32)
    zeros = jnp.zeros((n_tok, MLA_NOPE), F32)
    pad = jnp.zeros((n_tok, MLA_DK_PAD - MLA_NOPE - MLA_ROPE), F32)
    cos_h = jnp.concatenate([ones, cos, cos, pad], axis=1)
    sin_h = jnp.concatenate([zeros, -sin, sin, pad], axis=1)
    return cos_h, sin_h


def _pack_w_in(w):
    d = w.shape[0]
    return jnp.concatenate([w[:, :4096], w[:, 4128:7200], w[:, 4096:4128],
                            jnp.zeros((d, P_WIDTH - 7200), w.dtype)], axis=1).astype(BF16)


def _pack_mla_weights(w_qup, w_kvup):
    r = w_qup.shape[0]
    wq = w_qup.reshape(r, MLA_HEADS, MLA_NOPE + MLA_ROPE)
    wq = jnp.pad(wq, ((0, 0), (0, 0), (0, MLA_DK_PAD - MLA_NOPE - MLA_ROPE))).reshape(r, MLA_HEADS * MLA_DK_PAD)
    wkv = w_kvup.reshape(r, MLA_HEADS, MLA_NOPE + MLA_DV)
    wk = jnp.pad(wkv[:, :, :MLA_NOPE], ((0, 0), (0, 0), (0, MLA_DK_PAD - MLA_NOPE))).reshape(r, MLA_HEADS * MLA_DK_PAD)
    wv = wkv[:, :, MLA_NOPE:].reshape(r, MLA_HEADS * MLA_DV)
    return wq.astype(BF16), wk.astype(BF16), wv.astype(BF16)


def kernel(x, c, ctx, c_ctx, w_ada, b_ada, w_in, ret_log_decay, ret_gn_gain, na_rpb, mla_q_norm, mla_w_qup,
           mla_kv_norm, mla_w_kvup, w_branch_ret, w_branch_na, w_branch_mla, w_out, w_ff1, w_ff2, ln_gain, ln_bias):
    depth = w_ada.shape[0]
    b, t, d = x.shape
    lz = ctx.shape[1]
    rows = t // GRID_W

    cc = jnp.zeros((8, d), F32).at[:b].set(c).at[b].set(c_ctx)
    mod = _ada(cc, w_ada, b_ada)

    cos_r, sin_r = _ret_rope_tables(t)
    cos_m, sin_m = _mla_rope_tables(t)
    cos_m = jnp.concatenate([cos_m, jnp.ones((lz, MLA_DK_PAD), F32)], axis=0)
    sin_m = jnp.concatenate([sin_m, jnp.zeros((lz, MLA_DK_PAD), F32)], axis=0)
    cos_rz, sin_rz = cos_r[:lz], sin_r[:lz]
    na_bias = _na_bias_tables(na_rpb, rows)
    s_zero = jnp.zeros((b, 2, RET_HEADS, RET_DK, RET_DV), F32)

    z = ctx
    for l in range(depth):
        need_ctx = l < depth - 1
        mx = mod[l, :b].reshape(b, 6, 1, d)
        mz = jnp.broadcast_to(mod[l, b].reshape(1, 6, 1, d), (b, 6, 1, d))
        sh1x, sc1x, g1x, sh2x, sc2x, g2x = [mx[:, i] for i in range(6)]
        sh1z, sc1z, g1z, sh2z, sc2z, g2z = [mz[:, i] for i in range(6)]

        w_in_p = _pack_w_in(w_in[l])
        px = _inproj(x, sc1x, sh1x, w_in_p)
        pz = _inproj(z, sc1z, sh1z, w_in_p)

        lg = jnp.log1p(-jnp.exp(ret_log_decay[l].astype(F32))).reshape(2 * RET_HEADS)
        gn_gain = ret_gn_gain[l].reshape(1, RET_HEADS * RET_DV)
        yaf_z, yab_z, s_ctx = _retention(pz, lg, cos_rz, sin_rz, gn_gain, s_zero, use_rope=False)
        yaf_x, yab_x, _ = _retention(px, lg, cos_r, sin_r, gn_gain, s_ctx, use_rope=True)

        yb_x = _na(px, pz, na_bias, l)

        wq, wk, wv = _pack_mla_weights(mla_w_qup[l], mla_w_kvup[l])
        qn = mla_q_norm[l].reshape(1, MLA_RANK)
        kvn = mla_kv_norm[l].reshape(1, MLA_RANK)
        q_all, k_all, v_all = _mlaprep(px, pz, qn, kvn, wq, wk, wv, cos_m, sin_m)
        yc_x = _mla_attn(q_all, k_all, v_all, 0, t, 0, t + lz, 1024, 384)

        wa = w_branch_ret[l].astype(BF16)
        wb = w_branch_na[l].astype(BF16)
        wc = w_branch_mla[l].astype(BF16)
        wo = w_out[l].astype(BF16)
        w1 = w_ff1[l].astype(BF16)
        w2 = w_ff2[l].astype(BF16)
        lng1, lnb1 = ln_gain[l, 0].reshape(1, d), ln_bias[l, 0].reshape(1, d)
        lng2, lnb2 = ln_gain[l, 1].reshape(1, d), ln_bias[l, 1].reshape(1, d)

        x1 = _merge(yaf_x, yab_x, yb_x, yc_x, px, x, g1x, wa, wb, wc, wo, lng1, lnb1)
        x = _ffn(x1, sc2x, sh2x, g2x, w1, w2, lng2, lnb2)

        if need_ctx:
            wna = NA_HEADS * NA_DH
            yb_z = _flash(pz, pz, pz, C_NA_Q // wna, C_NA_K // wna, C_NA_V // wna,
                          NA_HEADS, NA_DH, NA_DH, NA_DH ** -0.5 * LOG2_E, 256, 256)
            yc_z = _mla_attn(q_all, k_all, v_all, t, lz, t, lz, lz, lz)
            z1 = _merge(yaf_z, yab_z, yb_z, yc_z, pz, z, g1z, wa, wb, wc, wo, lng1, lnb1)
            z = _ffn(z1, sc2z, sh2z, g2z, w1, w2, lng2, lnb2)
    return x
```

```python
import functools

import numpy as np
import jax
import jax.numpy as jnp
from jax import lax
from jax.experimental import pallas as pl
from jax.experimental.pallas import tpu as pltpu

F32 = jnp.float32
BF16 = jnp.bfloat16

D_MODEL = 1024
GRID_W = 64
RET_HEADS = 4
RET_DK = 64
RET_DV = 128
RET_CHUNK = 128
NA_HEADS = 8
NA_DH = 64
NA_KH = 8
NA_KW = 16
MLA_HEADS = 8
MLA_RANK = 256
MLA_NOPE = 64
MLA_ROPE = 32
MLA_DV = 64
MLA_DK_PAD = 128
D_FF = 4 * D_MODEL
ROPE_BASE = 10000.0
EPS = 1e-5
DEPTH_FOR_NORM = 4
DEEPNORM_ALPHA = (2 * DEPTH_FOR_NORM) ** 0.25
MASK_VALUE = -1e30
LOG2_E = 1.4426950408889634

C_RET_Q, C_RET_K, C_RET_V, C_RET_GF, C_RET_GB = 0, 256, 512, 1024, 1536
C_NA_Q, C_NA_K, C_NA_V = 2048, 2560, 3072
C_MLA_Q, C_MLA_KV = 3584, 3840
C_GATE_A, C_GATE_B, C_GATE_C = 4096, 5120, 6144
C_MLA_KR = 7168
P_WIDTH = 7296
P_COL_TILE = 2432

NA_QROWS = 4
NA_KROWS = 12
NA_QB = NA_QROWS * GRID_W
NA_KB = NA_KROWS * GRID_W

VMEM_LIMIT = 56 * 1024 * 1024


def _cparams(*sem):
    return pltpu.CompilerParams(dimension_semantics=sem, vmem_limit_bytes=VMEM_LIMIT)


def _tile(n, pref, mult=1):
    t = min(n, pref) // mult * mult
    while n % t:
        t -= mult
    return t


def _const_spec(shape):
    nd = len(shape)
    return pl.BlockSpec(shape, lambda *_: (0,) * nd)


def _ada_kernel(c_ref, w_ref, b_ref, o_ref):
    c = c_ref[...]
    a = c * jax.nn.sigmoid(c)
    o_ref[0] = jnp.dot(a.astype(BF16), w_ref[0].astype(BF16), preferred_element_type=F32) + b_ref[0]


def _ada(cc, w_ada, b_ada):
    depth, d, n = w_ada.shape
    tn = 1024
    return pl.pallas_call(
        _ada_kernel,
        grid=(depth, n // tn),
        in_specs=[
            pl.BlockSpec((8, d), lambda l, j: (0, 0)),
            pl.BlockSpec((1, d, tn), lambda l, j: (l, 0, j)),
            pl.BlockSpec((1, 1, tn), lambda l, j: (l, 0, j)),
        ],
        out_specs=pl.BlockSpec((1, 8, tn), lambda l, j: (l, 0, j)),
        out_shape=jax.ShapeDtypeStruct((depth, 8, n), F32),
        compiler_params=_cparams("parallel", "parallel"),
        name="ada",
    )(cc, w_ada, b_ada.reshape(depth, 1, n))


def _inproj_kernel(x_ref, sc_ref, sh_ref, w_ref, o_ref):
    h = x_ref[0] * (1.0 + sc_ref[0]) + sh_ref[0]
    o_ref[0] = jnp.dot(h.astype(BF16), w_ref[...], preferred_element_type=F32)


def _inproj(x, sc, sh, w):
    b, t, d = x.shape
    tm = _tile(t, 512)
    tn = P_COL_TILE
    return pl.pallas_call(
        _inproj_kernel,
        grid=(P_WIDTH // tn, b, t // tm),
        in_specs=[
            pl.BlockSpec((1, tm, d), lambda j, bi, i: (bi, i, 0)),
            pl.BlockSpec((1, 1, d), lambda j, bi, i: (bi, 0, 0)),
            pl.BlockSpec((1, 1, d), lambda j, bi, i: (bi, 0, 0)),
            pl.BlockSpec((d, tn), lambda j, bi, i: (0, j)),
        ],
        out_specs=pl.BlockSpec((1, tm, tn), lambda j, bi, i: (bi, i, j)),
        out_shape=jax.ShapeDtypeStruct((b, t, P_WIDTH), F32),
        compiler_params=_cparams("parallel", "parallel", "parallel"),
        name="inproj",
    )(x, sc, sh, w)


def _swap_halves(x, half):
    n = x.shape[-1]
    lane = lax.broadcasted_iota(jnp.int32, x.shape, x.ndim - 1)
    first = (lane % (2 * half)) < half
    return jnp.where(first, pltpu.roll(x, n - half, x.ndim - 1), pltpu.roll(x, half, x.ndim - 1))


def _ret_kernel(lg_ref, qf_ref, qb_ref, kf_ref, kb_ref, vf_ref, vb_ref, gf_ref, gb_ref,
                cosf_ref, cosb_ref, sinf_ref, sinb_ref, gain_ref, s0_ref,
                yf_ref, yb_ref, sf_ref, s_scr, *, use_rope, n_chunks):
    c = pl.program_id(0)
    batch = qf_ref.shape[0]
    cc = RET_CHUNK

    @pl.when(c == 0)
    def _():
        s_scr[...] = s0_ref[...]

    gain = gain_ref[...]
    row = lax.broadcasted_iota(jnp.int32, (cc, cc), 0).astype(F32)
    col = lax.broadcasted_iota(jnp.int32, (cc, cc), 1).astype(F32)
    pos = lax.broadcasted_iota(jnp.int32, (cc, RET_DK), 0).astype(F32)
    dirs = (
        (qf_ref, kf_ref, vf_ref, gf_ref, cosf_ref, sinf_ref, yf_ref, row - col, pos + 1.0, cc - 1.0 - pos),
        (qb_ref, kb_ref, vb_ref, gb_ref, cosb_ref, sinb_ref, yb_ref, col - row, cc - pos, pos),
    )
    for d, (q_ref, k_ref, v_ref, g_ref, cos_ref, sin_ref, y_ref, diff, q_exp, k_exp) in enumerate(dirs):
        if use_rope:
            cos = jnp.concatenate([cos_ref[...], cos_ref[...]], axis=1)
            sin = jnp.concatenate([sin_ref[...], sin_ref[...]], axis=1)
        qs, ks = [], []
        for b in range(batch):
            q = q_ref[b]
            k = k_ref[b] * (RET_DK ** -0.5)
            if use_rope:
                q = q * cos + _swap_halves(q, RET_DK // 2) * sin
                k = k * cos + _swap_halves(k, RET_DK // 2) * sin
            qs.append(q)
            ks.append(k)
        for h in range(RET_HEADS):
            lg = lg_ref[d * RET_HEADS + h]
            decay = jnp.where(diff >= 0, jnp.exp(lg * jnp.maximum(diff, 0.0)), 0.0)
            q_decay = jnp.exp(lg * q_exp)
            k_decay = jnp.exp(lg * k_exp)
            chunk_decay = jnp.exp(lg * jnp.full((RET_DK, RET_DV), float(cc), F32))
            sl = slice(h * RET_DV, (h + 1) * RET_DV)
            for b in range(batch):
                qh = qs[b][:, h * RET_DK:(h + 1) * RET_DK]
                kh = ks[b][:, h * RET_DK:(h + 1) * RET_DK]
                vh = v_ref[b, :, sl].astype(BF16)
                scores = lax.dot_general(qh.astype(BF16), kh.astype(BF16), (((1,), (1,)), ((), ())),
                                         preferred_element_type=F32) * decay
                inner = jnp.dot(scores.astype(BF16), vh, preferred_element_type=F32)
                state = s_scr[b, d, h]
                cross = jnp.dot((qh * q_decay).astype(BF16), state.astype(BF16), preferred_element_type=F32)
                o = inner + cross
                kv = jnp.dot((kh * k_decay).T.astype(BF16), vh, preferred_element_type=F32)
                s_scr[b, d, h] = state * chunk_decay + kv
                mu = jnp.mean(o, axis=-1, keepdims=True)
                oc = o - mu
                var = jnp.mean(oc * oc, axis=-1, keepdims=True)
                gate = g_ref[b, :, sl]
                y_ref[b, :, sl] = (gate * jax.nn.sigmoid(gate)) * (oc * lax.rsqrt(var + EPS) * gain[:, sl])

    @pl.when(c == n_chunks - 1)
    def _():
        sf_ref[...] = s_scr[...]


def _retention(p, lg, cos, sin, gain, s0, use_rope):
    b, t, _ = p.shape
    cc = RET_CHUNK
    n = t // cc
    wv = RET_HEADS * RET_DV

    def pspec(width, col, backward):
        if backward:
            return pl.BlockSpec((b, cc, width), lambda ci, lg_: (0, n - 1 - ci, col // width))
        return pl.BlockSpec((b, cc, width), lambda ci, lg_: (0, ci, col // width))

    def tspec(backward):
        if backward:
            return pl.BlockSpec((cc, 128), lambda ci, lg_: (n - 1 - ci, 0))
        return pl.BlockSpec((cc, 128), lambda ci, lg_: (ci, 0))

    state_spec = pl.BlockSpec((b, 2, RET_HEADS, RET_DK, RET_DV), lambda ci, lg_: (0, 0, 0, 0, 0))
    kern = functools.partial(_ret_kernel, use_rope=use_rope, n_chunks=n)
    grid_spec = pltpu.PrefetchScalarGridSpec(
        num_scalar_prefetch=1,
        grid=(n,),
        in_specs=[
            pspec(256, C_RET_Q, False), pspec(256, C_RET_Q, True),
            pspec(256, C_RET_K, False), pspec(256, C_RET_K, True),
            pspec(wv, C_RET_V, False), pspec(wv, C_RET_V, True),
            pspec(wv, C_RET_GF, False), pspec(wv, C_RET_GB, True),
            tspec(False), tspec(True), tspec(False), tspec(True),
            pl.BlockSpec((1, wv), lambda ci, lg_: (0, 0)),
            state_spec,
        ],
        out_specs=[
            pl.BlockSpec((b, cc, wv), lambda ci, lg_: (0, ci, 0)),
            pl.BlockSpec((b, cc, wv), lambda ci, lg_: (0, n - 1 - ci, 0)),
            state_spec,
        ],
        scratch_shapes=[pltpu.VMEM((b, 2, RET_HEADS, RET_DK, RET_DV), F32)],
    )
    return pl.pallas_call(
        kern,
        grid_spec=grid_spec,
        out_shape=[
            jax.ShapeDtypeStruct((b, t, wv), F32),
            jax.ShapeDtypeStruct((b, t, wv), F32),
            jax.ShapeDtypeStruct((b, 2, RET_HEADS, RET_DK, RET_DV), F32),
        ],
        compiler_params=_cparams("arbitrary"),
        name="retention",
    )(lg, p, p, p, p, p, p, p, p, cos, cos, sin, sin, gain, s0)


def _na_tables(rows):
    groups = rows // NA_QROWS
    cols = np.arange(GRID_W)
    c0 = np.clip(cols - NA_KW // 2, 0, GRID_W - NA_KW)
    col_ok = (cols[None, :] >= c0[:, None]) & (cols[None, :] < c0[:, None] + NA_KW)
    dc = cols[None, :] - cols[:, None] + (NA_KW - 1)
    onehot = (dc[None] == np.arange(2 * NA_KW - 1)[:, None, None]) & col_ok[None]
    row_bias = np.full((3, NA_QROWS, NA_KROWS), 2 * NA_KH - 1, np.int64)
    for ti, g in enumerate((0, 1, groups - 1)):
        ws = int(np.clip(NA_QROWS * g - NA_KH // 2, 0, rows - NA_KROWS))
        for lr in range(NA_QROWS):
            r = NA_QROWS * g + lr
            r0 = int(np.clip(r - NA_KH // 2, 0, rows - NA_KH))
            for kr in range(NA_KROWS):
                if r0 <= ws + kr < r0 + NA_KH:
                    row_bias[ti, lr, kr] = ws + kr - r + (NA_KH - 1)
    return onehot, col_ok, row_bias


def _na_bias_tables(na_rpb, rows):
    onehot, col_ok, row_bias = _na_tables(rows)
    depth, heads = na_rpb.shape[:2]
    nd = 2 * NA_KH
    toep = jnp.einsum('lhdj,jck->lhdck', na_rpb, onehot.astype(np.float32), precision=lax.Precision.HIGHEST)
    toep = jnp.where(col_ok, toep, MASK_VALUE)
    masked = jnp.full(toep.shape[:2] + (1, GRID_W, GRID_W), MASK_VALUE, F32)
    toep = jnp.concatenate([toep, masked], axis=2)
    sides = jnp.concatenate([jnp.pad(toep, ((0, 0),) * 4 + ((0, GRID_W),)),
                             jnp.pad(toep, ((0, 0),) * 4 + ((GRID_W, 0),))], axis=2)
    pairs = row_bias.reshape(-1, 2)
    pick = ((pairs[:, :1] == np.arange(nd)[None, :]).astype(np.float32),
            (pairs[:, 1:] == np.arange(nd)[None, :]).astype(np.float32))
    pick = np.concatenate(pick, axis=1)
    tab = jnp.einsum('nd,lhdck->lhnck', pick, sides, precision=lax.Precision.HIGHEST)
    return tab.reshape(depth, heads, 3, NA_QROWS * NA_KROWS // 2, GRID_W, 2 * GRID_W)


def _na_kernel(q_ref, k0_ref, k1_ref, k2_ref, v0_ref, v1_ref, v2_ref, kz_ref, vz_ref, bias_ref, o_ref):
    lanes = 2 * NA_DH
    npair = NA_KROWS // 2
    low = lax.broadcasted_iota(jnp.int32, (q_ref.shape[1], lanes), 1) < NA_DH
    for pr in range(NA_HEADS // 2):
        sl = slice(pr * lanes, (pr + 1) * lanes)
        qp = q_ref[0, :, sl] * (NA_DH ** -0.5)
        qm = (jnp.where(low, qp, 0.0).astype(BF16), jnp.where(low, 0.0, qp).astype(BF16))
        ks = [r[0, :, sl].astype(BF16) for r in (k0_ref, k1_ref, k2_ref, kz_ref)]
        vs = [r[0, :, sl].astype(BF16) for r in (v0_ref, v1_ref, v2_ref, vz_ref)]
        outs = []
        for hh in range(2):
            h = 2 * pr + hh
            cols = []
            for j in range(4):
                sj = lax.dot_general(qm[hh], ks[j], (((1,), (1,)), ((), ())), preferred_element_type=F32)
                halves = [sj[:, i * lanes:(i + 1) * lanes] for i in range(sj.shape[1] // lanes)]
                if j < 3:
                    halves = [hv + jnp.concatenate([bias_ref[0, h, 0, lr * npair + j * len(halves) + i]
                                                    for lr in range(NA_QROWS)], axis=0)
                              for i, hv in enumerate(halves)]
                cols += halves
            m = jnp.max(functools.reduce(jnp.maximum, cols), axis=-1, keepdims=True)
            ps = [jnp.exp(cj - m) for cj in cols]
            l = jnp.sum(functools.reduce(jnp.add, ps), axis=-1, keepdims=True)
            o = None
            per = len(cols) // 4
            for j in range(4):
                pj = jnp.concatenate([pc.astype(BF16) for pc in ps[j * per:(j + 1) * per]], axis=1)
                oj = jnp.dot(pj, vs[j], preferred_element_type=F32)
                o = oj if o is None else o + oj
            outs.append(o / l)
        o_ref[0, :, sl] = jnp.where(low, outs[0], outs[1]).astype(o_ref.dtype)


def _na(px, pz, bias, layer):
    b, t, _ = px.shape
    lz = pz.shape[1]
    groups = t // NA_QB
    w = NA_HEADS * NA_DH
    kblk = NA_KB // 3
    assert lz == kblk and groups >= 3

    def kspec(col, off):
        return pl.BlockSpec((1, kblk, w), lambda bi, g: (bi, jnp.clip(g - 1, 0, groups - 3) + off, col // w))

    def tab(g):
        return jnp.where(g == 0, 0, jnp.where(g == groups - 1, 2, 1))

    return pl.pallas_call(
        _na_kernel,
        grid=(b, groups),
        in_specs=[
            pl.BlockSpec((1, NA_QB, w), lambda bi, g: (bi, g, C_NA_Q // w)),
            kspec(C_NA_K, 0), kspec(C_NA_K, 1), kspec(C_NA_K, 2),
            kspec(C_NA_V, 0), kspec(C_NA_V, 1), kspec(C_NA_V, 2),
            pl.BlockSpec((1, lz, w), lambda bi, g: (bi, 0, C_NA_K // w)),
            pl.BlockSpec((1, lz, w), lambda bi, g: (bi, 0, C_NA_V // w)),
            pl.BlockSpec((1, NA_HEADS, 1) + bias.shape[3:], lambda bi, g: (layer, 0, tab(g), 0, 0, 0)),
        ],
        out_specs=pl.BlockSpec((1, NA_QB, w), lambda bi, g: (bi, g, 0)),
        out_shape=jax.ShapeDtypeStruct((b, t, w), BF16),
        compiler_params=_cparams("parallel", "arbitrary"),
        name="na",
    )(px, px, px, px, px, px, px, pz, pz, bias)


def _rms(x, gain):
    return x * lax.rsqrt(jnp.mean(x * x, axis=-1, keepdims=True) + EPS) * gain


def _mlaprep_kernel(xq_ref, xkv_ref, xkr_ref, zq_ref, zkv_ref, zkr_ref, qn_ref, kvn_ref, wq_ref, wk_ref, wv_ref,
                    cos_ref, sin_ref, q_ref, k_ref, v_ref, *, nx):
    latent = pl.program_id(1) < nx
    pq = jnp.where(latent, xq_ref[0], zq_ref[0])
    pkv = jnp.where(latent, xkv_ref[0], zkv_ref[0])
    pkr = jnp.where(latent, xkr_ref[0], zkr_ref[0])
    hq = _rms(pq, qn_ref[...]).astype(BF16)
    hkv = _rms(pkv, kvn_ref[...]).astype(BF16)
    q = jnp.dot(hq, wq_ref[...], preferred_element_type=F32)
    k = jnp.dot(hkv, wk_ref[...], preferred_element_type=F32)
    v = jnp.dot(hkv, wv_ref[...], preferred_element_type=F32)
    kr = pltpu.roll(pkr, MLA_NOPE, 1)
    k = k + jnp.concatenate([kr] * MLA_HEADS, axis=1)
    cos = jnp.concatenate([cos_ref[...]] * MLA_HEADS, axis=1)
    sin = jnp.concatenate([sin_ref[...]] * MLA_HEADS, axis=1)
    q = q * cos + _swap_halves(q, MLA_ROPE // 2) * sin
    k = k * cos + _swap_halves(k, MLA_ROPE // 2) * sin
    q_ref[0] = (q * ((MLA_NOPE + MLA_ROPE) ** -0.5 * LOG2_E)).astype(BF16)
    k_ref[0] = k.astype(BF16)
    v_ref[0] = v.astype(BF16)


def _mlaprep(px, pz, qn, kvn, wq, wk, wv, cos, sin):
    b, t, _ = px.shape
    lz = pz.shape[1]
    tm = lz
    nx = t // tm
    wqk = MLA_HEADS * MLA_DK_PAD
    wvv = MLA_HEADS * MLA_DV

    def xspec(width, col):
        return pl.BlockSpec((1, tm, width), lambda bi, i: (bi, jnp.minimum(i, nx - 1), col // width))

    def zspec(width, col):
        return pl.BlockSpec((1, tm, width), lambda bi, i: (bi, 0, col // width))

    out_spec = pl.BlockSpec((1, tm, wqk), lambda bi, i: (bi, i, 0))
    kern = functools.partial(_mlaprep_kernel, nx=nx)
    return pl.pallas_call(
        kern,
        grid=(b, nx + 1),
        in_specs=[
            xspec(MLA_RANK, C_MLA_Q), xspec(MLA_RANK, C_MLA_KV), xspec(128, C_MLA_KR),
            zspec(MLA_RANK, C_MLA_Q), zspec(MLA_RANK, C_MLA_KV), zspec(128, C_MLA_KR),
            _const_spec((1, MLA_RANK)), _const_spec((1, MLA_RANK)),
            _const_spec((MLA_RANK, wqk)), _const_spec((MLA_RANK, wqk)), _const_spec((MLA_RANK, wvv)),
            pl.BlockSpec((tm, 128), lambda bi, i: (i, 0)),
            pl.BlockSpec((tm, 128), lambda bi, i: (i, 0)),
        ],
        out_specs=[out_spec, out_spec, pl.BlockSpec((1, tm, wvv), lambda bi, i: (bi, i, 0))],
        out_shape=[jax.ShapeDtypeStruct((b, t + lz, wqk), BF16)] * 2
        + [jax.ShapeDtypeStruct((b, t + lz, wvv), BF16)],
        compiler_params=_cparams("parallel", "parallel"),
        name="mlaprep",
    )(px, px, px, pz, pz, pz, qn, kvn, wq, wk, wv, cos, sin)


def _flash_kernel(q_ref, k_ref, v_ref, o_ref, m_scr, l_scr, acc_scr, *, heads, dk, dv, scale, nk):
    ki = pl.program_id(2)
    tq = q_ref.shape[1]
    tk = k_ref.shape[1]
    lanes = 2 * dv

    @pl.when(ki == 0)
    def _():
        m_scr[...] = jnp.full(m_scr.shape, -jnp.inf, F32)
        l_scr[...] = jnp.zeros(l_scr.shape, F32)
        acc_scr[...] = jnp.zeros(acc_scr.shape, F32)

    low = lax.broadcasted_iota(jnp.int32, (tq, lanes), 1) < dv
    for pr in range(heads // 2):
        vp = v_ref[0, :, pr * lanes:(pr + 1) * lanes].astype(BF16)
        alphas, pvs = [], []
        for h in (2 * pr, 2 * pr + 1):
            qh = q_ref[0, :, h * dk:(h + 1) * dk]
            if scale != 1.0:
                qh = qh * scale
            kh = k_ref[0, :, h * dk:(h + 1) * dk]
            s = lax.dot_general(qh.astype(BF16), kh.astype(BF16), (((1,), (1,)), ((), ())),
                                preferred_element_type=F32)
            cols = [s[:, j * lanes:(j + 1) * lanes] for j in range(tk // lanes)]
            m_prev = m_scr[h]
            m_tile = jnp.max(functools.reduce(jnp.maximum, cols), axis=-1, keepdims=True)
            m_new = jnp.maximum(m_prev, m_tile)
            alpha = jnp.exp2(m_prev - m_new)
            ps = [jnp.exp2(cj - m_new) for cj in cols]
            l_scr[h] = alpha * l_scr[h] + functools.reduce(jnp.add, ps)
            m_scr[h] = m_new
            p = jnp.concatenate([pj.astype(BF16) for pj in ps], axis=1)
            pvs.append(jnp.dot(p, vp, preferred_element_type=F32))
            alphas.append(alpha)
        acc_scr[pr] = acc_scr[pr] * jnp.where(low, alphas[0], alphas[1]) + jnp.where(low, pvs[0], pvs[1])

    @pl.when(ki == nk - 1)
    def _():
        for pr in range(heads // 2):
            l0 = jnp.sum(l_scr[2 * pr], axis=-1, keepdims=True)
            l1 = jnp.sum(l_scr[2 * pr + 1], axis=-1, keepdims=True)
            o_ref[0, :, pr * lanes:(pr + 1) * lanes] = (acc_scr[pr] / jnp.where(low, l0, l1)).astype(o_ref.dtype)


def _flash(q, k, v, qcol, kcol, vcol, heads, dk, dv, scale, tq_pref, tk_pref):
    b, tq_all, _ = q.shape
    tk_all = k.shape[1]
    tq = _tile(tq_all, tq_pref)
    tk = _tile(tk_all, tk_pref)
    nk = tk_all // tk
    assert heads % 2 == 0 and 2 * dv == 128 and tk % 128 == 0
    kern = functools.partial(_flash_kernel, heads=heads, dk=dk, dv=dv, scale=scale, nk=nk)
    return pl.pallas_call(
        kern,
        grid=(b, tq_all // tq, nk),
        in_specs=[
            pl.BlockSpec((1, tq, heads * dk), lambda bi, i, j: (bi, i, qcol)),
            pl.BlockSpec((1, tk, heads * dk), lambda bi, i, j: (bi, j, kcol)),
            pl.BlockSpec((1, tk, heads * dv), lambda bi, i, j: (bi, j, vcol)),
        ],
        out_specs=pl.BlockSpec((1, tq, heads * dv), lambda bi, i, j: (bi, i, 0)),
        out_shape=jax.ShapeDtypeStruct((b, tq_all, heads * dv), BF16),
        scratch_shapes=[
            pltpu.VMEM((heads, tq, 2 * dv), F32),
            pltpu.VMEM((heads, tq, 2 * dv), F32),
            pltpu.VMEM((heads // 2, tq, 2 * dv), F32),
        ],
        compiler_params=_cparams("parallel", "parallel", "arbitrary"),
        name="flash",
    )(q, k, v)


def _mla_attn_kernel(q_ref, k_ref, v_ref, o_ref, m_scr, l_scr, acc_scr, *, heads, nk):
    ki = pl.program_id(2)
    w = MLA_DK_PAD
    tq = q_ref.shape[1]

    @pl.when(ki == 0)
    def _():
        m_scr[...] = jnp.full(m_scr.shape, -jnp.inf, F32)
        l_scr[...] = jnp.zeros(l_scr.shape, F32)
        acc_scr[...] = jnp.zeros(acc_scr.shape, F32)

    low = lax.broadcasted_iota(jnp.int32, (tq, w), 1) < MLA_DV
    for pr in range(heads // 2):
        vp = v_ref[0, :, pr * w:(pr + 1) * w]
        alphas, pvs = [], []
        for h in (2 * pr, 2 * pr + 1):
            sl = slice(h * w, (h + 1) * w)
            s = lax.dot_general(q_ref[0, :, sl], k_ref[0, :, sl], (((1,), (1,)), ((), ())),
                                preferred_element_type=F32)
            cols = [s[:, j * w:(j + 1) * w] for j in range(s.shape[1] // w)]
            m_prev = m_scr[h]
            m_new = jnp.maximum(m_prev, jnp.max(functools.reduce(jnp.maximum, cols), axis=-1, keepdims=True))
            alpha = jnp.exp2(m_prev - m_new)
            ps = [jnp.exp2(cj - m_new) for cj in cols]
            l_scr[h] = alpha * l_scr[h] + functools.reduce(jnp.add, ps)
            m_scr[h] = m_new
            p = jnp.concatenate([pj.astype(BF16) for pj in ps], axis=1)
            pvs.append(jnp.dot(p, vp, preferred_element_type=F32))
            alphas.append(alpha)
        acc_scr[pr] = acc_scr[pr] * jnp.where(low, alphas[0], alphas[1]) + jnp.where(low, pvs[0], pvs[1])

    @pl.when(ki == nk - 1)
    def _():
        for pr in range(heads // 2):
            l0 = jnp.sum(l_scr[2 * pr], axis=-1, keepdims=True)
            l1 = jnp.sum(l_scr[2 * pr + 1], axis=-1, keepdims=True)
            o_ref[0, :, pr * w:(pr + 1) * w] = (acc_scr[pr] / jnp.where(low, l0, l1)).astype(o_ref.dtype)


def _mla_attn(q, k, v, q_start, q_len, k_start, k_len, tq_pref, tk_pref):
    b, _, wq = q.shape
    heads = wq // MLA_DK_PAD
    tq = _tile(q_len, tq_pref, 8)
    tk = _tile(k_len, tk_pref, 128)
    assert q_start % tq == 0 and k_start % tk == 0
    q_off, k_off = q_start // tq, k_start // tk
    nk = k_len // tk
    kern = functools.partial(_mla_attn_kernel, heads=heads, nk=nk)
    return pl.pallas_call(
        kern,
        grid=(b, q_len // tq, nk),
        in_specs=[
            pl.BlockSpec((1, tq, wq), lambda bi, i, j: (bi, q_off + i, 0)),
            pl.BlockSpec((1, tk, wq), lambda bi, i, j: (bi, k_off + j, 0)),
            pl.BlockSpec((1, tk, heads * MLA_DV), lambda bi, i, j: (bi, k_off + j, 0)),
        ],
        out_specs=pl.BlockSpec((1, tq, heads * MLA_DV), lambda bi, i, j: (bi, i, 0)),
        out_shape=jax.ShapeDtypeStruct((b, q_len, heads * MLA_DV), BF16),
        scratch_shapes=[
            pltpu.VMEM((heads, tq, MLA_DK_PAD), F32),
            pltpu.VMEM((heads, tq, MLA_DK_PAD), F32),
            pltpu.VMEM((heads // 2, tq, MLA_DK_PAD), F32),
        ],
        compiler_params=_cparams("parallel", "parallel", "arbitrary"),
        name="mla_attn",
    )(q, k, v)


def _layer_norm(r, gain, bias):
    mu = jnp.mean(r, axis=-1, keepdims=True)
    rc = r - mu
    var = jnp.mean(rc * rc, axis=-1, keepdims=True)
    return rc * lax.rsqrt(var + EPS) * gain + bias


def _merge_kernel(yaf_ref, yab_ref, yb_ref, yc_ref, ga_ref, gb_ref, gc_ref, x_ref, g1_ref,
                  wa_ref, wb_ref, wc_ref, wo_ref, lng_ref, lnb_ref, o_ref):
    ya = (yaf_ref[0] + yab_ref[0]).astype(BF16)
    y = (jax.nn.sigmoid(ga_ref[0]) * jnp.dot(ya, wa_ref[...], preferred_element_type=F32)
         + jax.nn.sigmoid(gb_ref[0]) * jnp.dot(yb_ref[0].astype(BF16), wb_ref[...], preferred_element_type=F32)
         + jax.nn.sigmoid(gc_ref[0]) * jnp.dot(yc_ref[0].astype(BF16), wc_ref[...], preferred_element_type=F32))
    mix = jnp.dot(y.astype(BF16), wo_ref[...], preferred_element_type=F32)
    r = DEEPNORM_ALPHA * x_ref[0] + g1_ref[0] * mix
    o_ref[0] = _layer_norm(r, lng_ref[...], lnb_ref[...])


def _merge(yaf, yab, yb, yc, p, x, g1, wa, wb, wc, wo, lng, lnb):
    b, t, d = x.shape
    tm = _tile(t, 512)
    wbr = 512
    return pl.pallas_call(
        _merge_kernel,
        grid=(b, t // tm),
        in_specs=[
            pl.BlockSpec((1, tm, wbr), lambda bi, i: (bi, i, 0)),
            pl.BlockSpec((1, tm, wbr), lambda bi, i: (bi, i, 0)),
            pl.BlockSpec((1, tm, wbr), lambda bi, i: (bi, i, 0)),
            pl.BlockSpec((1, tm, wbr), lambda bi, i: (bi, i, 0)),
            pl.BlockSpec((1, tm, d), lambda bi, i: (bi, i, C_GATE_A // d)),
            pl.BlockSpec((1, tm, d), lambda bi, i: (bi, i, C_GATE_B // d)),
            pl.BlockSpec((1, tm, d), lambda bi, i: (bi, i, C_GATE_C // d)),
            pl.BlockSpec((1, tm, d), lambda bi, i: (bi, i, 0)),
            pl.BlockSpec((1, 1, d), lambda bi, i: (bi, 0, 0)),
            _const_spec((wbr, d)), _const_spec((wbr, d)), _const_spec((wbr, d)), _const_spec((d, d)),
            _const_spec((1, d)), _const_spec((1, d)),
        ],
        out_specs=pl.BlockSpec((1, tm, d), lambda bi, i: (bi, i, 0)),
        out_shape=jax.ShapeDtypeStruct((b, t, d), F32),
        compiler_params=_cparams("parallel", "parallel"),
        name="merge",
    )(yaf, yab, yb, yc, p, p, p, x, g1, wa, wb, wc, wo, lng, lnb)


def _ffn_kernel(x_ref, sc_ref, sh_ref, g2_ref, w1_ref, w2_ref, lng_ref, lnb_ref, o_ref, *, ff_chunk):
    x = x_ref[0]
    h = (x * (1.0 + sc_ref[0]) + sh_ref[0]).astype(BF16)
    acc = jnp.zeros(x.shape, F32)
    for j in range(D_FF // ff_chunk):
        u = jnp.dot(h, w1_ref[:, j * ff_chunk:(j + 1) * ff_chunk], preferred_element_type=F32)
        u = jnp.square(jnp.maximum(u, 0.0)).astype(BF16)
        acc = acc + jnp.dot(u, w2_ref[j * ff_chunk:(j + 1) * ff_chunk, :], preferred_element_type=F32)
    r = DEEPNORM_ALPHA * x + g2_ref[0] * acc
    o_ref[0] = _layer_norm(r, lng_ref[...], lnb_ref[...])


def _ffn(x, sc, sh, g2, w1, w2, lng, lnb):
    b, t, d = x.shape
    tm = _tile(t, 512)
    kern = functools.partial(_ffn_kernel, ff_chunk=1024)
    return pl.pallas_call(
        kern,
        grid=(b, t // tm),
        in_specs=[
            pl.BlockSpec((1, tm, d), lambda bi, i: (bi, i, 0)),
            pl.BlockSpec((1, 1, d), lambda bi, i: (bi, 0, 0)),
            pl.BlockSpec((1, 1, d), lambda bi, i: (bi, 0, 0)),
            pl.BlockSpec((1, 1, d), lambda bi, i: (bi, 0, 0)),
            _const_spec((d, D_FF)), _const_spec((D_FF, d)),
            _const_spec((1, d)), _const_spec((1, d)),
        ],
        out_specs=pl.BlockSpec((1, tm, d), lambda bi, i: (bi, i, 0)),
        out_shape=jax.ShapeDtypeStruct((b, t, d), F32),
        compiler_params=_cparams("parallel", "parallel"),
        name="ffn",
    )(x, sc, sh, g2, w1, w2, lng, lnb)


def _rope_tables(n_tok, rot_dim):
    t = jnp.arange(n_tok)
    row = (t // GRID_W).astype(F32)
    col = (t % GRID_W).astype(F32)
    n_freq = rot_dim // 4
    inv_freq = ROPE_BASE ** (-2.0 * jnp.arange(n_freq, dtype=F32) / (rot_dim // 2))
    ang = jnp.concatenate([row[:, None] * inv_freq, col[:, None] * inv_freq], axis=-1)
    return jnp.cos(ang), jnp.sin(ang)


def _ret_rope_tables(n_tok):
    cos, sin = _rope_tables(n_tok, RET_DK)
    cos_h = jnp.concatenate([cos, cos], axis=1)
    sin_h = jnp.concatenate([-sin, sin], axis=1)
    return jnp.tile(cos_h, (1, 2)), jnp.tile(sin_h, (1, 2))


def _mla_rope_tables(n_tok):
    cos, sin = _rope_tables(n_tok, MLA_ROPE)
    ones = jnp.ones((n_tok, MLA_NOPE), F32)
    zeros = jnp.zeros((n_tok, MLA_NOPE), F32)
    pad = jnp.zeros((n_tok, MLA_DK_PAD - MLA_NOPE - MLA_ROPE), F32)
    cos_h = jnp.concatenate([ones, cos, cos, pad], axis=1)
    sin_h = jnp.concatenate([zeros, -sin, sin, pad], axis=1)
    return cos_h, sin_h


def _pack_w_in(w):
    d = w.shape[0]
    return jnp.concatenate([w[:, :4096], w[:, 4128:7200], w[:, 4096:4128],
                            jnp.zeros((d, P_WIDTH - 7200), w.dtype)], axis=1).astype(BF16)


def _pack_mla_weights(w_qup, w_kvup):
    r = w_qup.shape[0]
    wq = w_qup.reshape(r, MLA_HEADS, MLA_NOPE + MLA_ROPE)
    wq = jnp.pad(wq, ((0, 0), (0, 0), (0, MLA_DK_PAD - MLA_NOPE - MLA_ROPE))).reshape(r, MLA_HEADS * MLA_DK_PAD)
    wkv = w_kvup.reshape(r, MLA_HEADS, MLA_NOPE + MLA_DV)
    wk = jnp.pad(wkv[:, :, :MLA_NOPE], ((0, 0), (0, 0), (0, MLA_DK_PAD - MLA_NOPE))).reshape(r, MLA_HEADS * MLA_DK_PAD)
    wv = wkv[:, :, MLA_NOPE:].reshape(r, MLA_HEADS * MLA_DV)
    return wq.astype(BF16), wk.astype(BF16), wv.astype(BF16)


def kernel(x, c, ctx, c_ctx, w_ada, b_ada, w_in, ret_log_decay, ret_gn_gain, na_rpb, mla_q_norm, mla_w_qup,
           mla_kv_norm, mla_w_kvup, w_branch_ret, w_branch_na, w_branch_mla, w_out, w_ff1, w_ff2, ln_gain, ln_bias):
    depth = w_ada.shape[0]
    b, t, d = x.shape
    lz = ctx.shape[1]
    rows = t // GRID_W

    cc = jnp.zeros((8, d), F32).at[:b].set(c).at[b].set(c_ctx)
    mod = _ada(cc, w_ada, b_ada)

    cos_r, sin_r = _ret_rope_tables(t)
    cos_m, sin_m = _mla_rope_tables(t)
    cos_m = jnp.concatenate([cos_m, jnp.ones((lz, MLA_DK_PAD), F32)], axis=0)
    sin_m = jnp.concatenate([sin_m, jnp.zeros((lz, MLA_DK_PAD), F32)], axis=0)
    cos_rz, sin_rz = cos_r[:lz], sin_r[:lz]
    na_bias = _na_bias_tables(na_rpb, rows)
    s_zero = jnp.zeros((b, 2, RET_HEADS, RET_DK, RET_DV), F32)

    z = ctx
    for l in range(depth):
        need_ctx = l < depth - 1
        mx = mod[l, :b].reshape(b, 6, 1, d)
        mz = jnp.broadcast_to(mod[l, b].reshape(1, 6, 1, d), (b, 6, 1, d))
        sh1x, sc1x, g1x, sh2x, sc2x, g2x = [mx[:, i] for i in range(6)]
        sh1z, sc1z, g1z, sh2z, sc2z, g2z = [mz[:, i] for i in range(6)]

        w_in_p = _pack_w_in(w_in[l])
        px = _inproj(x, sc1x, sh1x, w_in_p)
        pz = _inproj(z, sc1z, sh1z, w_in_p)

        lg = jnp.log1p(-jnp.exp(ret_log_decay[l].astype(F32))).reshape(2 * RET_HEADS)
        gn_gain = ret_gn_gain[l].reshape(1, RET_HEADS * RET_DV)
        yaf_z, yab_z, s_ctx = _retention(pz, lg, cos_rz, sin_rz, gn_gain, s_zero, use_rope=False)
        yaf_x, yab_x, _ = _retention(px, lg, cos_r, sin_r, gn_gain, s_ctx, use_rope=True)

        yb_x = _na(px, pz, na_bias, l)

        wq, wk, wv = _pack_mla_weights(mla_w_qup[l], mla_w_kvup[l])
        qn = mla_q_norm[l].reshape(1, MLA_RANK)
        kvn = mla_kv_norm[l].reshape(1, MLA_RANK)
        q_all, k_all, v_all = _mlaprep(px, pz, qn, kvn, wq, wk, wv, cos_m, sin_m)
        yc_x = _mla_attn(q_all, k_all, v_all, 0, t, 0, t + lz, 1024, 768)

        wa = w_branch_ret[l].astype(BF16)
        wb = w_branch_na[l].astype(BF16)
        wc = w_branch_mla[l].astype(BF16)
        wo = w_out[l].astype(BF16)
        w1 = w_ff1[l].astype(BF16)
        w2 = w_ff2[l].astype(BF16)
        lng1, lnb1 = ln_gain[l, 0].reshape(1, d), ln_bias[l, 0].reshape(1, d)
        lng2, lnb2 = ln_gain[l, 1].reshape(1, d), ln_bias[l, 1].reshape(1, d)

        x1 = _merge(yaf_x, yab_x, yb_x, yc_x, px, x, g1x, wa, wb, wc, wo, lng1, lnb1)
        x = _ffn(x1, sc2x, sh2x, g2x, w1, w2, lng2, lnb2)

        if need_ctx:
            wna = NA_HEADS * NA_DH
            yb_z = _flash(pz, pz, pz, C_NA_Q // wna, C_NA_K // wna, C_NA_V // wna,
                          NA_HEADS, NA_DH, NA_DH, NA_DH ** -0.5 * LOG2_E, 256, 256)
            yc_z = _mla_attn(q_all, k_all, v_all, t, lz, t, lz, lz, lz)
            z1 = _merge(yaf_z, yab_z, yb_z, yc_z, pz, z, g1z, wa, wb, wc, wo, lng1, lnb1)
            z = _ffn(z1, sc2z, sh2z, g2z, w1, w2, lng2, lnb2)
    return x
```

```python
import functools

import numpy as np
import jax
import jax.numpy as jnp
from jax import lax
from jax.experimental import pallas as pl
from jax.experimental.pallas import tpu as pltpu

F32 = jnp.float32
BF16 = jnp.bfloat16

D_MODEL = 1024
GRID_W = 64
RET_HEADS = 4
RET_DK = 64
RET_DV = 128
RET_CHUNK = 128
NA_HEADS = 8
NA_DH = 64
NA_KH = 8
NA_KW = 16
MLA_HEADS = 8
MLA_RANK = 256
MLA_NOPE = 64
MLA_ROPE = 32
MLA_DV = 64
MLA_DK_PAD = 128
D_FF = 4 * D_MODEL
ROPE_BASE = 10000.0
EPS = 1e-5
DEPTH_FOR_NORM = 4
DEEPNORM_ALPHA = (2 * DEPTH_FOR_NORM) ** 0.25
MASK_VALUE = -1e30
LOG2_E = 1.4426950408889634

C_RET_Q, C_RET_K, C_RET_V, C_RET_GF, C_RET_GB = 0, 256, 512, 1024, 1536
C_NA_Q, C_NA_K, C_NA_V = 2048, 2560, 3072
C_MLA_Q, C_MLA_KV = 3584, 3840
C_GATE_A, C_GATE_B, C_GATE_C = 4096, 5120, 6144
C_MLA_KR = 7168
P_WIDTH = 7296
P_COL_TILE = 2432

NA_QROWS = 4
NA_KROWS = 12
NA_QB = NA_QROWS * GRID_W
NA_KB = NA_KROWS * GRID_W

VMEM_LIMIT = 56 * 1024 * 1024


def _cparams(*sem):
    return pltpu.CompilerParams(dimension_semantics=sem, vmem_limit_bytes=VMEM_LIMIT)


def _tile(n, pref, mult=1):
    t = min(n, pref) // mult * mult
    while n % t:
        t -= mult
    return t


def _const_spec(shape):
    nd = len(shape)
    return pl.BlockSpec(shape, lambda *_: (0,) * nd)


def _ada_kernel(c_ref, w_ref, b_ref, o_ref):
    c = c_ref[...]
    a = c * jax.nn.sigmoid(c)
    o_ref[0] = jnp.dot(a.astype(BF16), w_ref[0].astype(BF16), preferred_element_type=F32) + b_ref[0]


def _ada(cc, w_ada, b_ada):
    depth, d, n = w_ada.shape
    tn = 1024
    return pl.pallas_call(
        _ada_kernel,
        grid=(depth, n // tn),
        in_specs=[
            pl.BlockSpec((8, d), lambda l, j: (0, 0)),
            pl.BlockSpec((1, d, tn), lambda l, j: (l, 0, j)),
            pl.BlockSpec((1, 1, tn), lambda l, j: (l, 0, j)),
        ],
        out_specs=pl.BlockSpec((1, 8, tn), lambda l, j: (l, 0, j)),
        out_shape=jax.ShapeDtypeStruct((depth, 8, n), F32),
        compiler_params=_cparams("parallel", "parallel"),
        name="ada",
    )(cc, w_ada, b_ada.reshape(depth, 1, n))


def _inproj_kernel(x_ref, sc_ref, sh_ref, w_ref, o_ref):
    h = x_ref[0] * (1.0 + sc_ref[0]) + sh_ref[0]
    o_ref[0] = jnp.dot(h.astype(BF16), w_ref[...], preferred_element_type=F32)


def _inproj(x, sc, sh, w):
    b, t, d = x.shape
    tm = _tile(t, 512)
    tn = P_COL_TILE
    return pl.pallas_call(
        _inproj_kernel,
        grid=(P_WIDTH // tn, b, t // tm),
        in_specs=[
            pl.BlockSpec((1, tm, d), lambda j, bi, i: (bi, i, 0)),
            pl.BlockSpec((1, 1, d), lambda j, bi, i: (bi, 0, 0)),
            pl.BlockSpec((1, 1, d), lambda j, bi, i: (bi, 0, 0)),
            pl.BlockSpec((d, tn), lambda j, bi, i: (0, j)),
        ],
        out_specs=pl.BlockSpec((1, tm, tn), lambda j, bi, i: (bi, i, j)),
        out_shape=jax.ShapeDtypeStruct((b, t, P_WIDTH), F32),
        compiler_params=_cparams("parallel", "parallel", "parallel"),
        name="inproj",
    )(x, sc, sh, w)


def _swap_halves(x, half):
    n = x.shape[-1]
    lane = lax.broadcasted_iota(jnp.int32, x.shape, x.ndim - 1)
    first = (lane % (2 * half)) < half
    return jnp.where(first, pltpu.roll(x, n - half, x.ndim - 1), pltpu.roll(x, half, x.ndim - 1))


def _ret_kernel(lg_ref, qf_ref, qb_ref, kf_ref, kb_ref, vf_ref, vb_ref, gf_ref, gb_ref,
                cosf_ref, cosb_ref, sinf_ref, sinb_ref, gain_ref, s0_ref,
                yf_ref, yb_ref, sf_ref, s_scr, *, use_rope, n_chunks):
    c = pl.program_id(0)
    batch = qf_ref.shape[0]
    cc = RET_CHUNK

    @pl.when(c == 0)
    def _():
        s_scr[...] = s0_ref[...]

    gain = gain_ref[...]
    row = lax.broadcasted_iota(jnp.int32, (cc, cc), 0).astype(F32)
    col = lax.broadcasted_iota(jnp.int32, (cc, cc), 1).astype(F32)
    wqk = RET_HEADS * RET_DK
    pos = lax.broadcasted_iota(jnp.int32, (cc, wqk), 0).astype(F32)
    head_of_lane = lax.broadcasted_iota(jnp.int32, (cc, wqk), 1) // RET_DK
    dirs = (
        (qf_ref, kf_ref, vf_ref, gf_ref, cosf_ref, sinf_ref, yf_ref, row - col, pos + 1.0, cc - 1.0 - pos),
        (qb_ref, kb_ref, vb_ref, gb_ref, cosb_ref, sinb_ref, yb_ref, col - row, cc - pos, pos),
    )
    for d, (q_ref, k_ref, v_ref, g_ref, cos_ref, sin_ref, y_ref, diff, q_exp, k_exp) in enumerate(dirs):
        if use_rope:
            cos = jnp.concatenate([cos_ref[...], cos_ref[...]], axis=1)
            sin = jnp.concatenate([sin_ref[...], sin_ref[...]], axis=1)
        lgs = [lg_ref[d * RET_HEADS + h] for h in range(RET_HEADS)]
        lg_lanes = jnp.full((cc, wqk), lgs[RET_HEADS - 1], F32)
        for h in range(RET_HEADS - 2, -1, -1):
            lg_lanes = jnp.where(head_of_lane == h, lgs[h], lg_lanes)
        q_decay = jnp.exp(lg_lanes * q_exp)
        k_decay = jnp.exp(lg_lanes * k_exp)
        for b in range(batch):
            q = q_ref[b]
            k = k_ref[b] * (RET_DK ** -0.5)
            if use_rope:
                q = q * cos + _swap_halves(q, RET_DK // 2) * sin
                k = k * cos + _swap_halves(k, RET_DK // 2) * sin
            k_bf = k.astype(BF16)
            kt_decayed = (k * k_decay).T.astype(BF16)
            q_decayed = q * q_decay
            state_bf = s_scr[b, d].astype(BF16)
            for h in range(RET_HEADS):
                mine = head_of_lane == h
                sl = slice(h * RET_DV, (h + 1) * RET_DV)
                rows = slice(h * RET_DK, (h + 1) * RET_DK)
                decay = jnp.where(diff >= 0, jnp.exp(lgs[h] * jnp.maximum(diff, 0.0)), 0.0)
                vh = v_ref[b, :, sl].astype(BF16)
                scores = lax.dot_general(jnp.where(mine, q, 0.0).astype(BF16), k_bf, (((1,), (1,)), ((), ())),
                                         preferred_element_type=F32) * decay
                inner = jnp.dot(scores.astype(BF16), vh, preferred_element_type=F32)
                cross = jnp.dot(jnp.where(mine, q_decayed, 0.0).astype(BF16), state_bf, preferred_element_type=F32)
                o = inner + cross
                kv = jnp.dot(kt_decayed[rows], vh, preferred_element_type=F32)
                chunk_decay = jnp.exp(lgs[h] * jnp.full((RET_DK, RET_DV), float(cc), F32))
                s_scr[b, d, rows] = s_scr[b, d, rows] * chunk_decay + kv
                mu = jnp.mean(o, axis=-1, keepdims=True)
                oc = o - mu
                var = jnp.mean(oc * oc, axis=-1, keepdims=True)
                gate = g_ref[b, :, sl]
                y_ref[b, :, sl] = (gate * jax.nn.sigmoid(gate)) * (oc * lax.rsqrt(var + EPS) * gain[:, sl])

    @pl.when(c == n_chunks - 1)
    def _():
        sf_ref[...] = s_scr[...]


def _retention(p, lg, cos, sin, gain, s0, use_rope):
    b, t, _ = p.shape
    cc = RET_CHUNK
    n = t // cc
    wv = RET_HEADS * RET_DV

    def pspec(width, col, backward):
        if backward:
            return pl.BlockSpec((b, cc, width), lambda ci, lg_: (0, n - 1 - ci, col // width))
        return pl.BlockSpec((b, cc, width), lambda ci, lg_: (0, ci, col // width))

    def tspec(backward):
        if backward:
            return pl.BlockSpec((cc, 128), lambda ci, lg_: (n - 1 - ci, 0))
        return pl.BlockSpec((cc, 128), lambda ci, lg_: (ci, 0))

    state_spec = pl.BlockSpec((b, 2, RET_HEADS * RET_DK, RET_DV), lambda ci, lg_: (0, 0, 0, 0))
    kern = functools.partial(_ret_kernel, use_rope=use_rope, n_chunks=n)
    grid_spec = pltpu.PrefetchScalarGridSpec(
        num_scalar_prefetch=1,
        grid=(n,),
        in_specs=[
            pspec(256, C_RET_Q, False), pspec(256, C_RET_Q, True),
            pspec(256, C_RET_K, False), pspec(256, C_RET_K, True),
            pspec(wv, C_RET_V, False), pspec(wv, C_RET_V, True),
            pspec(wv, C_RET_GF, False), pspec(wv, C_RET_GB, True),
            tspec(False), tspec(True), tspec(False), tspec(True),
            pl.BlockSpec((1, wv), lambda ci, lg_: (0, 0)),
            state_spec,
        ],
        out_specs=[
            pl.BlockSpec((b, cc, wv), lambda ci, lg_: (0, ci, 0)),
            pl.BlockSpec((b, cc, wv), lambda ci, lg_: (0, n - 1 - ci, 0)),
            state_spec,
        ],
        scratch_shapes=[pltpu.VMEM((b, 2, RET_HEADS * RET_DK, RET_DV), F32)],
    )
    return pl.pallas_call(
        kern,
        grid_spec=grid_spec,
        out_shape=[
            jax.ShapeDtypeStruct((b, t, wv), F32),
            jax.ShapeDtypeStruct((b, t, wv), F32),
            jax.ShapeDtypeStruct((b, 2, RET_HEADS * RET_DK, RET_DV), F32),
        ],
        compiler_params=_cparams("arbitrary"),
        name="retention",
    )(lg, p, p, p, p, p, p, p, p, cos, cos, sin, sin, gain, s0)


def _na_tables(rows):
    groups = rows // NA_QROWS
    cols = np.arange(GRID_W)
    c0 = np.clip(cols - NA_KW // 2, 0, GRID_W - NA_KW)
    col_ok = (cols[None, :] >= c0[:, None]) & (cols[None, :] < c0[:, None] + NA_KW)
    dc = cols[None, :] - cols[:, None] + (NA_KW - 1)
    onehot = (dc[None] == np.arange(2 * NA_KW - 1)[:, None, None]) & col_ok[None]
    row_bias = np.full((3, NA_QROWS, NA_KROWS), 2 * NA_KH - 1, np.int64)
    for ti, g in enumerate((0, 1, groups - 1)):
        ws = int(np.clip(NA_QROWS * g - NA_KH // 2, 0, rows - NA_KROWS))
        for lr in range(NA_QROWS):
            r = NA_QROWS * g + lr
            r0 = int(np.clip(r - NA_KH // 2, 0, rows - NA_KH))
            for kr in range(NA_KROWS):
                if r0 <= ws + kr < r0 + NA_KH:
                    row_bias[ti, lr, kr] = ws + kr - r + (NA_KH - 1)
    return onehot, col_ok, row_bias


def _na_bias_tables(na_rpb, rows):
    onehot, col_ok, row_bias = _na_tables(rows)
    depth, heads = na_rpb.shape[:2]
    nd = 2 * NA_KH
    toep = jnp.einsum('lhdj,jck->lhdck', na_rpb, onehot.astype(np.float32), precision=lax.Precision.HIGHEST)
    toep = jnp.where(col_ok, toep, MASK_VALUE)
    masked = jnp.full(toep.shape[:2] + (1, GRID_W, GRID_W), MASK_VALUE, F32)
    toep = jnp.concatenate([toep, masked], axis=2)
    sides = jnp.concatenate([jnp.pad(toep, ((0, 0),) * 4 + ((0, GRID_W),)),
                             jnp.pad(toep, ((0, 0),) * 4 + ((GRID_W, 0),))], axis=2)
    pairs = row_bias.reshape(-1, 2)
    pick = ((pairs[:, :1] == np.arange(nd)[None, :]).astype(np.float32),
            (pairs[:, 1:] == np.arange(nd)[None, :]).astype(np.float32))
    pick = np.concatenate(pick, axis=1)
    tab = jnp.einsum('nd,lhdck->lhnck', pick, sides, precision=lax.Precision.HIGHEST)
    return tab.reshape(depth, heads, 3, NA_QROWS * NA_KROWS // 2, GRID_W, 2 * GRID_W)


def _na_kernel(q_ref, k0_ref, k1_ref, k2_ref, v0_ref, v1_ref, v2_ref, kz_ref, vz_ref, bias_ref, o_ref):
    lanes = 2 * NA_DH
    npair = NA_KROWS // 2
    low = lax.broadcasted_iota(jnp.int32, (q_ref.shape[1], lanes), 1) < NA_DH
    for pr in range(NA_HEADS // 2):
        sl = slice(pr * lanes, (pr + 1) * lanes)
        qp = q_ref[0, :, sl] * (NA_DH ** -0.5)
        qm = (jnp.where(low, qp, 0.0).astype(BF16), jnp.where(low, 0.0, qp).astype(BF16))
        ks = [r[0, :, sl].astype(BF16) for r in (k0_ref, k1_ref, k2_ref, kz_ref)]
        vs = [r[0, :, sl].astype(BF16) for r in (v0_ref, v1_ref, v2_ref, vz_ref)]
        outs = []
        for hh in range(2):
            h = 2 * pr + hh
            cols = []
            for j in range(4):
                sj = lax.dot_general(qm[hh], ks[j], (((1,), (1,)), ((), ())), preferred_element_type=F32)
                halves = [sj[:, i * lanes:(i + 1) * lanes] for i in range(sj.shape[1] // lanes)]
                if j < 3:
                    halves = [hv + jnp.concatenate([bias_ref[0, h, 0, lr * npair + j * len(halves) + i]
                                                    for lr in range(NA_QROWS)], axis=0)
                              for i, hv in enumerate(halves)]
                cols += halves
            m = jnp.max(functools.reduce(jnp.maximum, cols), axis=-1, keepdims=True)
            ps = [jnp.exp(cj - m) for cj in cols]
            l = jnp.sum(functools.reduce(jnp.add, ps), axis=-1, keepdims=True)
            o = None
            per = len(cols) // 4
            for j in range(4):
                pj = jnp.concatenate([pc.astype(BF16) for pc in ps[j * per:(j + 1) * per]], axis=1)
                oj = jnp.dot(pj, vs[j], preferred_element_type=F32)
                o = oj if o is None else o + oj
            outs.append(o / l)
        o_ref[0, :, sl] = jnp.where(low, outs[0], outs[1]).astype(o_ref.dtype)


def _na(px, pz, bias, layer):
    b, t, _ = px.shape
    lz = pz.shape[1]
    groups = t // NA_QB
    w = NA_HEADS * NA_DH
    kblk = NA_KB // 3
    assert lz == kblk and groups >= 3

    def kspec(col, off):
        return pl.BlockSpec((1, kblk, w), lambda bi, g: (bi, jnp.clip(g - 1, 0, groups - 3) + off, col // w))

    def tab(g):
        return jnp.where(g == 0, 0, jnp.where(g == groups - 1, 2, 1))

    return pl.pallas_call(
        _na_kernel,
        grid=(b, groups),
        in_specs=[
            pl.BlockSpec((1, NA_QB, w), lambda bi, g: (bi, g, C_NA_Q // w)),
            kspec(C_NA_K, 0), kspec(C_NA_K, 1), kspec(C_NA_K, 2),
            kspec(C_NA_V, 0), kspec(C_NA_V, 1), kspec(C_NA_V, 2),
            pl.BlockSpec((1, lz, w), lambda bi, g: (bi, 0, C_NA_K // w)),
            pl.BlockSpec((1, lz, w), lambda bi, g: (bi, 0, C_NA_V // w)),
            pl.BlockSpec((1, NA_HEADS, 1) + bias.shape[3:], lambda bi, g: (layer, 0, tab(g), 0, 0, 0)),
        ],
        out_specs=pl.BlockSpec((1, NA_QB, w), lambda bi, g: (bi, g, 0)),
        out_shape=jax.ShapeDtypeStruct((b, t, w), BF16),
        compiler_params=_cparams("parallel", "arbitrary"),
        name="na",
    )(px, px, px, px, px, px, px, pz, pz, bias)


def _rms(x, gain):
    return x * lax.rsqrt(jnp.mean(x * x, axis=-1, keepdims=True) + EPS) * gain


def _mlaprep_kernel(xq_ref, xkv_ref, xkr_ref, zq_ref, zkv_ref, zkr_ref, qn_ref, kvn_ref, wq_ref, wqs_ref, wk_ref,
                    wv_ref, cos_ref, sin_ref, q_ref, k_ref, v_ref, *, nx):
    latent = pl.program_id(1) < nx
    pq = jnp.where(latent, xq_ref[0], zq_ref[0])
    pkv = jnp.where(latent, xkv_ref[0], zkv_ref[0])
    pkr = jnp.where(latent, xkr_ref[0], zkr_ref[0])
    hq = _rms(pq, qn_ref[...]).astype(BF16)
    hkv = _rms(pkv, kvn_ref[...]).astype(BF16)
    q = jnp.dot(hq, wq_ref[...], preferred_element_type=F32)
    q_swapped = jnp.dot(hq, wqs_ref[...], preferred_element_type=F32)
    k = jnp.dot(hkv, wk_ref[...], preferred_element_type=F32)
    v = jnp.dot(hkv, wv_ref[...], preferred_element_type=F32)
    kr = pltpu.roll(pkr, MLA_NOPE, 1)
    k = k + jnp.concatenate([kr] * MLA_HEADS, axis=1)
    k_swapped = jnp.concatenate([_swap_halves(kr, MLA_ROPE // 2)] * MLA_HEADS, axis=1)
    cos = jnp.concatenate([cos_ref[...]] * MLA_HEADS, axis=1)
    sin = jnp.concatenate([sin_ref[...]] * MLA_HEADS, axis=1)
    q = q * cos + q_swapped * sin
    k = k * cos + k_swapped * sin
    q_ref[0] = (q * ((MLA_NOPE + MLA_ROPE) ** -0.5 * LOG2_E)).astype(BF16)
    k_ref[0] = k.astype(BF16)
    v_ref[0] = v.astype(BF16)


def _mlaprep(px, pz, qn, kvn, wq, wqs, wk, wv, cos, sin):
    b, t, _ = px.shape
    lz = pz.shape[1]
    tm = lz
    nx = t // tm
    wqk = MLA_HEADS * MLA_DK_PAD
    wvv = MLA_HEADS * MLA_DV

    def xspec(width, col):
        return pl.BlockSpec((1, tm, width), lambda bi, i: (bi, jnp.minimum(i, nx - 1), col // width))

    def zspec(width, col):
        return pl.BlockSpec((1, tm, width), lambda bi, i: (bi, 0, col // width))

    out_spec = pl.BlockSpec((1, tm, wqk), lambda bi, i: (bi, i, 0))
    kern = functools.partial(_mlaprep_kernel, nx=nx)
    return pl.pallas_call(
        kern,
        grid=(b, nx + 1),
        in_specs=[
            xspec(MLA_RANK, C_MLA_Q), xspec(MLA_RANK, C_MLA_KV), xspec(128, C_MLA_KR),
            zspec(MLA_RANK, C_MLA_Q), zspec(MLA_RANK, C_MLA_KV), zspec(128, C_MLA_KR),
            _const_spec((1, MLA_RANK)), _const_spec((1, MLA_RANK)),
            _const_spec((MLA_RANK, wqk)), _const_spec((MLA_RANK, wqk)), _const_spec((MLA_RANK, wqk)),
            _const_spec((MLA_RANK, wvv)),
            pl.BlockSpec((tm, 128), lambda bi, i: (i, 0)),
            pl.BlockSpec((tm, 128), lambda bi, i: (i, 0)),
        ],
        out_specs=[out_spec, out_spec, pl.BlockSpec((1, tm, wvv), lambda bi, i: (bi, i, 0))],
        out_shape=[jax.ShapeDtypeStruct((b, t + lz, wqk), BF16)] * 2
        + [jax.ShapeDtypeStruct((b, t + lz, wvv), BF16)],
        compiler_params=_cparams("parallel", "parallel"),
        name="mlaprep",
    )(px, px, px, pz, pz, pz, qn, kvn, wq, wqs, wk, wv, cos, sin)


def _flash_kernel(q_ref, k_ref, v_ref, o_ref, m_scr, l_scr, acc_scr, *, heads, dk, dv, scale, nk):
    ki = pl.program_id(2)
    tq = q_ref.shape[1]
    tk = k_ref.shape[1]
    lanes = 2 * dv

    @pl.when(ki == 0)
    def _():
        m_scr[...] = jnp.full(m_scr.shape, -jnp.inf, F32)
        l_scr[...] = jnp.zeros(l_scr.shape, F32)
        acc_scr[...] = jnp.zeros(acc_scr.shape, F32)

    low = lax.broadcasted_iota(jnp.int32, (tq, lanes), 1) < dv
    for pr in range(heads // 2):
        vp = v_ref[0, :, pr * lanes:(pr + 1) * lanes].astype(BF16)
        alphas, pvs = [], []
        for h in (2 * pr, 2 * pr + 1):
            qh = q_ref[0, :, h * dk:(h + 1) * dk]
            if scale != 1.0:
                qh = qh * scale
            kh = k_ref[0, :, h * dk:(h + 1) * dk]
            s = lax.dot_general(qh.astype(BF16), kh.astype(BF16), (((1,), (1,)), ((), ())),
                                preferred_element_type=F32)
            cols = [s[:, j * lanes:(j + 1) * lanes] for j in range(tk // lanes)]
            m_prev = m_scr[h]
            m_tile = jnp.max(functools.reduce(jnp.maximum, cols), axis=-1, keepdims=True)
            m_new = jnp.maximum(m_prev, m_tile)
            alpha = jnp.exp2(m_prev - m_new)
            ps = [jnp.exp2(cj - m_new) for cj in cols]
            l_scr[h] = alpha * l_scr[h] + functools.reduce(jnp.add, ps)
            m_scr[h] = m_new
            p = jnp.concatenate([pj.astype(BF16) for pj in ps], axis=1)
            pvs.append(jnp.dot(p, vp, preferred_element_type=F32))
            alphas.append(alpha)
        acc_scr[pr] = acc_scr[pr] * jnp.where(low, alphas[0], alphas[1]) + jnp.where(low, pvs[0], pvs[1])

    @pl.when(ki == nk - 1)
    def _():
        for pr in range(heads // 2):
            l0 = jnp.sum(l_scr[2 * pr], axis=-1, keepdims=True)
            l1 = jnp.sum(l_scr[2 * pr + 1], axis=-1, keepdims=True)
            o_ref[0, :, pr * lanes:(pr + 1) * lanes] = (acc_scr[pr] / jnp.where(low, l0, l1)).astype(o_ref.dtype)


def _flash(q, k, v, qcol, kcol, vcol, heads, dk, dv, scale, tq_pref, tk_pref):
    b, tq_all, _ = q.shape
    tk_all = k.shape[1]
    tq = _tile(tq_all, tq_pref)
    tk = _tile(tk_all, tk_pref)
    nk = tk_all // tk
    assert heads % 2 == 0 and 2 * dv == 128 and tk % 128 == 0
    kern = functools.partial(_flash_kernel, heads=heads, dk=dk, dv=dv, scale=scale, nk=nk)
    return pl.pallas_call(
        kern,
        grid=(b, tq_all // tq, nk),
        in_specs=[
            pl.BlockSpec((1, tq, heads * dk), lambda bi, i, j: (bi, i, qcol)),
            pl.BlockSpec((1, tk, heads * dk), lambda bi, i, j: (bi, j, kcol)),
            pl.BlockSpec((1, tk, heads * dv), lambda bi, i, j: (bi, j, vcol)),
        ],
        out_specs=pl.BlockSpec((1, tq, heads * dv), lambda bi, i, j: (bi, i, 0)),
        out_shape=jax.ShapeDtypeStruct((b, tq_all, heads * dv), BF16),
        scratch_shapes=[
            pltpu.VMEM((heads, tq, 2 * dv), F32),
            pltpu.VMEM((heads, tq, 2 * dv), F32),
            pltpu.VMEM((heads // 2, tq, 2 * dv), F32),
        ],
        compiler_params=_cparams("parallel", "parallel", "arbitrary"),
        name="flash",
    )(q, k, v)


def _mla_attn_kernel(q_ref, k_ref, v_ref, o_ref, m_scr, l_scr, acc_scr, *, heads, nk):
    ki = pl.program_id(2)
    w = MLA_DK_PAD
    tq = q_ref.shape[1]

    @pl.when(ki == 0)
    def _():
        m_scr[...] = jnp.full(m_scr.shape, -jnp.inf, F32)
        l_scr[...] = jnp.zeros(l_scr.shape, F32)
        acc_scr[...] = jnp.zeros(acc_scr.shape, F32)

    low = lax.broadcasted_iota(jnp.int32, (tq, w), 1) < MLA_DV
    for pr in range(heads // 2):
        vp = v_ref[0, :, pr * w:(pr + 1) * w]
        alphas, pvs = [], []
        for h in (2 * pr, 2 * pr + 1):
            sl = slice(h * w, (h + 1) * w)
            s = lax.dot_general(q_ref[0, :, sl], k_ref[0, :, sl], (((1,), (1,)), ((), ())),
                                preferred_element_type=F32)
            cols = [s[:, j * w:(j + 1) * w] for j in range(s.shape[1] // w)]
            m_prev = m_scr[h]
            m_new = jnp.maximum(m_prev, jnp.max(functools.reduce(jnp.maximum, cols), axis=-1, keepdims=True))
            alpha = jnp.exp2(m_prev - m_new)
            ps = [jnp.exp2(cj - m_new) for cj in cols]
            l_scr[h] = alpha * l_scr[h] + functools.reduce(jnp.add, ps)
            m_scr[h] = m_new
            p = jnp.concatenate([pj.astype(BF16) for pj in ps], axis=1)
            pvs.append(jnp.dot(p, vp, preferred_element_type=F32))
            alphas.append(alpha)
        acc_scr[pr] = acc_scr[pr] * jnp.where(low, alphas[0], alphas[1]) + jnp.where(low, pvs[0], pvs[1])

    @pl.when(ki == nk - 1)
    def _():
        for pr in range(heads // 2):
            l0 = jnp.sum(l_scr[2 * pr], axis=-1, keepdims=True)
            l1 = jnp.sum(l_scr[2 * pr + 1], axis=-1, keepdims=True)
            o_ref[0, :, pr * w:(pr + 1) * w] = (acc_scr[pr] / jnp.where(low, l0, l1)).astype(o_ref.dtype)


def _mla_attn(q, k, v, q_start, q_len, k_start, k_len, tq_pref, tk_pref):
    b, _, wq = q.shape
    heads = wq // MLA_DK_PAD
    tq = _tile(q_len, tq_pref, 8)
    tk = _tile(k_len, tk_pref, 128)
    assert q_start % tq == 0 and k_start % tk == 0
    q_off, k_off = q_start // tq, k_start // tk
    nk = k_len // tk
    kern = functools.partial(_mla_attn_kernel, heads=heads, nk=nk)
    return pl.pallas_call(
        kern,
        grid=(b, q_len // tq, nk),
        in_specs=[
            pl.BlockSpec((1, tq, wq), lambda bi, i, j: (bi, q_off + i, 0)),
            pl.BlockSpec((1, tk, wq), lambda bi, i, j: (bi, k_off + j, 0)),
            pl.BlockSpec((1, tk, heads * MLA_DV), lambda bi, i, j: (bi, k_off + j, 0)),
        ],
        out_specs=pl.BlockSpec((1, tq, heads * MLA_DV), lambda bi, i, j: (bi, i, 0)),
        out_shape=jax.ShapeDtypeStruct((b, q_len, heads * MLA_DV), BF16),
        scratch_shapes=[
            pltpu.VMEM((heads, tq, MLA_DK_PAD), F32),
            pltpu.VMEM((heads, tq, MLA_DK_PAD), F32),
            pltpu.VMEM((heads // 2, tq, MLA_DK_PAD), F32),
        ],
        compiler_params=_cparams("parallel", "parallel", "arbitrary"),
        name="mla_attn",
    )(q, k, v)


def _layer_norm(r, gain, bias):
    mu = jnp.mean(r, axis=-1, keepdims=True)
    rc = r - mu
    var = jnp.mean(rc * rc, axis=-1, keepdims=True)
    return rc * lax.rsqrt(var + EPS) * gain + bias


def _merge_kernel(yaf_ref, yab_ref, yb_ref, yc_ref, ga_ref, gb_ref, gc_ref, x_ref, g1_ref,
                  wa_ref, wb_ref, wc_ref, wo_ref, lng_ref, lnb_ref, o_ref):
    ya = (yaf_ref[0] + yab_ref[0]).astype(BF16)
    y = (jax.nn.sigmoid(ga_ref[0]) * jnp.dot(ya, wa_ref[...], preferred_element_type=F32)
         + jax.nn.sigmoid(gb_ref[0]) * jnp.dot(yb_ref[0].astype(BF16), wb_ref[...], preferred_element_type=F32)
         + jax.nn.sigmoid(gc_ref[0]) * jnp.dot(yc_ref[0].astype(BF16), wc_ref[...], preferred_element_type=F32))
    mix = jnp.dot(y.astype(BF16), wo_ref[...], preferred_element_type=F32)
    r = DEEPNORM_ALPHA * x_ref[0] + g1_ref[0] * mix
    o_ref[0] = _layer_norm(r, lng_ref[...], lnb_ref[...])


def _merge(yaf, yab, yb, yc, p, x, g1, wa, wb, wc, wo, lng, lnb):
    b, t, d = x.shape
    tm = _tile(t, 512)
    wbr = 512
    return pl.pallas_call(
        _merge_kernel,
        grid=(b, t // tm),
        in_specs=[
            pl.BlockSpec((1, tm, wbr), lambda bi, i: (bi, i, 0)),
            pl.BlockSpec((1, tm, wbr), lambda bi, i: (bi, i, 0)),
            pl.BlockSpec((1, tm, wbr), lambda bi, i: (bi, i, 0)),
            pl.BlockSpec((1, tm, wbr), lambda bi, i: (bi, i, 0)),
            pl.BlockSpec((1, tm, d), lambda bi, i: (bi, i, C_GATE_A // d)),
            pl.BlockSpec((1, tm, d), lambda bi, i: (bi, i, C_GATE_B // d)),
            pl.BlockSpec((1, tm, d), lambda bi, i: (bi, i, C_GATE_C // d)),
            pl.BlockSpec((1, tm, d), lambda bi, i: (bi, i, 0)),
            pl.BlockSpec((1, 1, d), lambda bi, i: (bi, 0, 0)),
            _const_spec((wbr, d)), _const_spec((wbr, d)), _const_spec((wbr, d)), _const_spec((d, d)),
            _const_spec((1, d)), _const_spec((1, d)),
        ],
        out_specs=pl.BlockSpec((1, tm, d), lambda bi, i: (bi, i, 0)),
        out_shape=jax.ShapeDtypeStruct((b, t, d), F32),
        compiler_params=_cparams("parallel", "parallel"),
        name="merge",
    )(yaf, yab, yb, yc, p, p, p, x, g1, wa, wb, wc, wo, lng, lnb)


def _ffn_kernel(x_ref, sc_ref, sh_ref, g2_ref, w1_ref, w2_ref, lng_ref, lnb_ref, o_ref, *, ff_chunk):
    x = x_ref[0]
    h = (x * (1.0 + sc_ref[0]) + sh_ref[0]).astype(BF16)
    acc = jnp.zeros(x.shape, F32)
    for j in range(D_FF // ff_chunk):
        u = jnp.dot(h, w1_ref[:, j * ff_chunk:(j + 1) * ff_chunk], preferred_element_type=F32)
        u = jnp.square(jnp.maximum(u, 0.0)).astype(BF16)
        acc = acc + jnp.dot(u, w2_ref[j * ff_chunk:(j + 1) * ff_chunk, :], preferred_element_type=F32)
    r = DEEPNORM_ALPHA * x + g2_ref[0] * acc
    o_ref[0] = _layer_norm(r, lng_ref[...], lnb_ref[...])


def _ffn(x, sc, sh, g2, w1, w2, lng, lnb):
    b, t, d = x.shape
    tm = _tile(t, 512)
    kern = functools.partial(_ffn_kernel, ff_chunk=1024)
    return pl.pallas_call(
        kern,
        grid=(b, t // tm),
        in_specs=[
            pl.BlockSpec((1, tm, d), lambda bi, i: (bi, i, 0)),
            pl.BlockSpec((1, 1, d), lambda bi, i: (bi, 0, 0)),
            pl.BlockSpec((1, 1, d), lambda bi, i: (bi, 0, 0)),
            pl.BlockSpec((1, 1, d), lambda bi, i: (bi, 0, 0)),
            _const_spec((d, D_FF)), _const_spec((D_FF, d)),
            _const_spec((1, d)), _const_spec((1, d)),
        ],
        out_specs=pl.BlockSpec((1, tm, d), lambda bi, i: (bi, i, 0)),
        out_shape=jax.ShapeDtypeStruct((b, t, d), F32),
        compiler_params=_cparams("parallel", "parallel"),
        name="ffn",
    )(x, sc, sh, g2, w1, w2, lng, lnb)


def _rope_tables(n_tok, rot_dim):
    t = jnp.arange(n_tok)
    row = (t // GRID_W).astype(F32)
    col = (t % GRID_W).astype(F32)
    n_freq = rot_dim // 4
    inv_freq = ROPE_BASE ** (-2.0 * jnp.arange(n_freq, dtype=F32) / (rot_dim // 2))
    ang = jnp.concatenate([row[:, None] * inv_freq, col[:, None] * inv_freq], axis=-1)
    return jnp.cos(ang), jnp.sin(ang)


def _ret_rope_tables(n_tok):
    cos, sin = _rope_tables(n_tok, RET_DK)
    cos_h = jnp.concatenate([cos, cos], axis=1)
    sin_h = jnp.concatenate([-sin, sin], axis=1)
    return jnp.tile(cos_h, (1, 2)), jnp.tile(sin_h, (1, 2))


def _mla_rope_tables(n_tok):
    cos, sin = _rope_tables(n_tok, MLA_ROPE)
    ones = jnp.ones((n_tok, MLA_NOPE), F32)
    zeros = jnp.zeros((n_tok, MLA_NOPE), F32)
    pad = jnp.zeros((n_tok, MLA_DK_PAD - MLA_NOPE - MLA_ROPE), F32)
    cos_h = jnp.concatenate([ones, cos, cos, pad], axis=1)
    sin_h = jnp.concatenate([zeros, -sin, sin, pad], axis=1)
    return cos_h, sin_h


def _pack_w_in(w):
    d = w.shape[0]
    return jnp.concatenate([w[:, :4096], w[:, 4128:7200], w[:, 4096:4128],
                            jnp.zeros((d, P_WIDTH - 7200), w.dtype)], axis=1).astype(BF16)


def _pack_mla_weights(w_qup, w_kvup):
    r = w_qup.shape[0]
    wq = w_qup.reshape(r, MLA_HEADS, MLA_NOPE + MLA_ROPE)
    half = MLA_ROPE // 2
    wqs = jnp.concatenate([jnp.zeros((r, MLA_HEADS, MLA_NOPE), wq.dtype), wq[:, :, MLA_NOPE + half:],
                           wq[:, :, MLA_NOPE:MLA_NOPE + half]], axis=2)
    pad_q = ((0, 0), (0, 0), (0, MLA_DK_PAD - MLA_NOPE - MLA_ROPE))
    wq = jnp.pad(wq, pad_q).reshape(r, MLA_HEADS * MLA_DK_PAD)
    wqs = jnp.pad(wqs, pad_q).reshape(r, MLA_HEADS * MLA_DK_PAD)
    wkv = w_kvup.reshape(r, MLA_HEADS, MLA_NOPE + MLA_DV)
    wk = jnp.pad(wkv[:, :, :MLA_NOPE], ((0, 0), (0, 0), (0, MLA_DK_PAD - MLA_NOPE))).reshape(r, MLA_HEADS * MLA_DK_PAD)
    wv = wkv[:, :, MLA_NOPE:].reshape(r, MLA_HEADS * MLA_DV)
    return wq.astype(BF16), wqs.astype(BF16), wk.astype(BF16), wv.astype(BF16)


def kernel(x, c, ctx, c_ctx, w_ada, b_ada, w_in, ret_log_decay, ret_gn_gain, na_rpb, mla_q_norm, mla_w_qup,
           mla_kv_norm, mla_w_kvup, w_branch_ret, w_branch_na, w_branch_mla, w_out, w_ff1, w_ff2, ln_gain, ln_bias):
    depth = w_ada.shape[0]
    b, t, d = x.shape
    lz = ctx.shape[1]
    rows = t // GRID_W

    cc = jnp.zeros((8, d), F32).at[:b].set(c).at[b].set(c_ctx)
    mod = _ada(cc, w_ada, b_ada)

    cos_r, sin_r = _ret_rope_tables(t)
    cos_m, sin_m = _mla_rope_tables(t)
    cos_m = jnp.concatenate([cos_m, jnp.ones((lz, MLA_DK_PAD), F32)], axis=0)
    sin_m = jnp.concatenate([sin_m, jnp.zeros((lz, MLA_DK_PAD), F32)], axis=0)
    cos_rz, sin_rz = cos_r[:lz], sin_r[:lz]
    na_bias = _na_bias_tables(na_rpb, rows)
    s_zero = jnp.zeros((b, 2, RET_HEADS * RET_DK, RET_DV), F32)

    z = ctx
    for l in range(depth):
        need_ctx = l < depth - 1
        mx = mod[l, :b].reshape(b, 6, 1, d)
        mz = jnp.broadcast_to(mod[l, b].reshape(1, 6, 1, d), (b, 6, 1, d))
        sh1x, sc1x, g1x, sh2x, sc2x, g2x = [mx[:, i] for i in range(6)]
        sh1z, sc1z, g1z, sh2z, sc2z, g2z = [mz[:, i] for i in range(6)]

        w_in_p = _pack_w_in(w_in[l])
        px = _inproj(x, sc1x, sh1x, w_in_p)
        pz = _inproj(z, sc1z, sh1z, w_in_p)

        lg = jnp.log1p(-jnp.exp(ret_log_decay[l].astype(F32))).reshape(2 * RET_HEADS)
        gn_gain = ret_gn_gain[l].reshape(1, RET_HEADS * RET_DV)
        yaf_z, yab_z, s_ctx = _retention(pz, lg, cos_rz, sin_rz, gn_gain, s_zero, use_rope=False)
        yaf_x, yab_x, _ = _retention(px, lg, cos_r, sin_r, gn_gain, s_ctx, use_rope=True)

        yb_x = _na(px, pz, na_bias, l)

        wq, wqs, wk, wv = _pack_mla_weights(mla_w_qup[l], mla_w_kvup[l])
        qn = mla_q_norm[l].reshape(1, MLA_RANK)
        kvn = mla_kv_norm[l].reshape(1, MLA_RANK)
        q_all, k_all, v_all = _mlaprep(px, pz, qn, kvn, wq, wqs, wk, wv, cos_m, sin_m)
        yc_x = _mla_attn(q_all, k_all, v_all, 0, t, 0, t + lz, 1024, 768)

        wa = w_branch_ret[l].astype(BF16)
        wb = w_branch_na[l].astype(BF16)
        wc = w_branch_mla[l].astype(BF16)
        wo = w_out[l].astype(BF16)
        w1 = w_ff1[l].astype(BF16)
        w2 = w_ff2[l].astype(BF16)
        lng1, lnb1 = ln_gain[l, 0].reshape(1, d), ln_bias[l, 0].reshape(1, d)
        lng2, lnb2 = ln_gain[l, 1].reshape(1, d), ln_bias[l, 1].reshape(1, d)

        x1 = _merge(yaf_x, yab_x, yb_x, yc_x, px, x, g1x, wa, wb, wc, wo, lng1, lnb1)
        x = _ffn(x1, sc2x, sh2x, g2x, w1, w2, lng2, lnb2)

        if need_ctx:
            wna = NA_HEADS * NA_DH
            yb_z = _flash(pz, pz, pz, C_NA_Q // wna, C_NA_K // wna, C_NA_V // wna,
                          NA_HEADS, NA_DH, NA_DH, NA_DH ** -0.5 * LOG2_E, 256, 256)
            yc_z = _mla_attn(q_all, k_all, v_all, t, lz, t, lz, lz, lz)
            z1 = _merge(yaf_z, yab_z, yb_z, yc_z, pz, z, g1z, wa, wb, wc, wo, lng1, lnb1)
            z = _ffn(z1, sc2z, sh2z, g2z, w1, w2, lng2, lnb2)
    return x
```

```python
import functools

import numpy as np
import jax
import jax.numpy as jnp
from jax import lax
from jax.experimental import pallas as pl
from jax.experimental.pallas import tpu as pltpu

F32 = jnp.float32
BF16 = jnp.bfloat16

D_MODEL = 1024
GRID_W = 64
RET_HEADS = 4
RET_DK = 64
RET_DV = 128
RET_CHUNK = 128
NA_HEADS = 8
NA_DH = 64
NA_KH = 8
NA_KW = 16
MLA_HEADS = 8
MLA_RANK = 256
MLA_NOPE = 64
MLA_ROPE = 32
MLA_DV = 64
MLA_DK_PAD = 128
D_FF = 4 * D_MODEL
ROPE_BASE = 10000.0
EPS = 1e-5
DEPTH_FOR_NORM = 4
DEEPNORM_ALPHA = (2 * DEPTH_FOR_NORM) ** 0.25
MASK_VALUE = -1e30
LOG2_E = 1.4426950408889634

C_RET_Q, C_RET_K, C_RET_V, C_RET_GF, C_RET_GB = 0, 256, 512, 1024, 1536
C_NA_Q, C_NA_K, C_NA_V = 2048, 2560, 3072
C_MLA_Q, C_MLA_KV = 3584, 3840
C_GATE_A, C_GATE_B, C_GATE_C = 4096, 5120, 6144
C_MLA_KR = 7168
P_WIDTH = 7296
P_COL_TILE = 2432

NA_QROWS = 4
NA_KROWS = 12
NA_QB = NA_QROWS * GRID_W
NA_KB = NA_KROWS * GRID_W

VMEM_LIMIT = 56 * 1024 * 1024


def _cparams(*sem):
    return pltpu.CompilerParams(dimension_semantics=sem, vmem_limit_bytes=VMEM_LIMIT)


def _tile(n, pref, mult=1):
    t = min(n, pref) // mult * mult
    while n % t:
        t -= mult
    return t


def _const_spec(shape):
    nd = len(shape)
    return pl.BlockSpec(shape, lambda *_: (0,) * nd)


def _ada_kernel(c_ref, w_ref, b_ref, o_ref):
    c = c_ref[...]
    a = c * jax.nn.sigmoid(c)
    o_ref[0] = jnp.dot(a.astype(BF16), w_ref[0].astype(BF16), preferred_element_type=F32) + b_ref[0]


def _ada(cc, w_ada, b_ada):
    depth, d, n = w_ada.shape
    tn = 1024
    return pl.pallas_call(
        _ada_kernel,
        grid=(depth, n // tn),
        in_specs=[
            pl.BlockSpec((8, d), lambda l, j: (0, 0)),
            pl.BlockSpec((1, d, tn), lambda l, j: (l, 0, j)),
            pl.BlockSpec((1, 1, tn), lambda l, j: (l, 0, j)),
        ],
        out_specs=pl.BlockSpec((1, 8, tn), lambda l, j: (l, 0, j)),
        out_shape=jax.ShapeDtypeStruct((depth, 8, n), F32),
        compiler_params=_cparams("parallel", "parallel"),
        name="ada",
    )(cc, w_ada, b_ada.reshape(depth, 1, n))


def _inproj_kernel(x_ref, sc_ref, sh_ref, w_ref, o_ref):
    h = x_ref[0] * (1.0 + sc_ref[0]) + sh_ref[0]
    o_ref[0] = jnp.dot(h.astype(BF16), w_ref[...], preferred_element_type=F32)


def _inproj(x, sc, sh, w):
    b, t, d = x.shape
    tm = _tile(t, 512)
    tn = P_COL_TILE
    return pl.pallas_call(
        _inproj_kernel,
        grid=(P_WIDTH // tn, b, t // tm),
        in_specs=[
            pl.BlockSpec((1, tm, d), lambda j, bi, i: (bi, i, 0)),
            pl.BlockSpec((1, 1, d), lambda j, bi, i: (bi, 0, 0)),
            pl.BlockSpec((1, 1, d), lambda j, bi, i: (bi, 0, 0)),
            pl.BlockSpec((d, tn), lambda j, bi, i: (0, j)),
        ],
        out_specs=pl.BlockSpec((1, tm, tn), lambda j, bi, i: (bi, i, j)),
        out_shape=jax.ShapeDtypeStruct((b, t, P_WIDTH), F32),
        compiler_params=_cparams("parallel", "parallel", "parallel"),
        name="inproj",
    )(x, sc, sh, w)


def _swap_halves(x, half):
    n = x.shape[-1]
    lane = lax.broadcasted_iota(jnp.int32, x.shape, x.ndim - 1)
    first = (lane % (2 * half)) < half
    return jnp.where(first, pltpu.roll(x, n - half, x.ndim - 1), pltpu.roll(x, half, x.ndim - 1))


def _ret_kernel(lg_ref, qf_ref, qb_ref, kf_ref, kb_ref, vf_ref, vb_ref, gf_ref, gb_ref,
                cosf_ref, cosb_ref, sinf_ref, sinb_ref, gain_ref, s0_ref,
                yf_ref, yb_ref, sf_ref, s_scr, *, use_rope, n_chunks):
    c = pl.program_id(0)
    batch = qf_ref.shape[0]
    cc = RET_CHUNK

    @pl.when(c == 0)
    def _():
        s_scr[...] = s0_ref[...]

    gain = gain_ref[...]
    row = lax.broadcasted_iota(jnp.int32, (cc, cc), 0).astype(F32)
    col = lax.broadcasted_iota(jnp.int32, (cc, cc), 1).astype(F32)
    wqk = RET_HEADS * RET_DK
    pos = lax.broadcasted_iota(jnp.int32, (cc, wqk), 0).astype(F32)
    head_of_lane = lax.broadcasted_iota(jnp.int32, (cc, wqk), 1) // RET_DK
    dirs = (
        (qf_ref, kf_ref, vf_ref, gf_ref, cosf_ref, sinf_ref, yf_ref, row - col, pos + 1.0, cc - 1.0 - pos),
        (qb_ref, kb_ref, vb_ref, gb_ref, cosb_ref, sinb_ref, yb_ref, col - row, cc - pos, pos),
    )
    for d, (q_ref, k_ref, v_ref, g_ref, cos_ref, sin_ref, y_ref, diff, q_exp, k_exp) in enumerate(dirs):
        if use_rope:
            cos = jnp.concatenate([cos_ref[...], cos_ref[...]], axis=1)
            sin = jnp.concatenate([sin_ref[...], sin_ref[...]], axis=1)
        lgs = [lg_ref[d * RET_HEADS + h] for h in range(RET_HEADS)]
        lg_lanes = jnp.full((cc, wqk), lgs[RET_HEADS - 1], F32)
        for h in range(RET_HEADS - 2, -1, -1):
            lg_lanes = jnp.where(head_of_lane == h, lgs[h], lg_lanes)
        q_decay = jnp.exp(lg_lanes * q_exp)
        k_decay = jnp.exp(lg_lanes * k_exp)
        for b in range(batch):
            q = q_ref[b]
            k = k_ref[b] * (RET_DK ** -0.5)
            if use_rope:
                q = q * cos + _swap_halves(q, RET_DK // 2) * sin
                k = k * cos + _swap_halves(k, RET_DK // 2) * sin
            k_bf = k.astype(BF16)
            kt_decayed = (k * k_decay).T.astype(BF16)
            q_decayed = q * q_decay
            state_bf = s_scr[b, d].astype(BF16)
            for h in range(RET_HEADS):
                mine = head_of_lane == h
                sl = slice(h * RET_DV, (h + 1) * RET_DV)
                rows = slice(h * RET_DK, (h + 1) * RET_DK)
                decay = jnp.where(diff >= 0, jnp.exp(lgs[h] * jnp.maximum(diff, 0.0)), 0.0)
                vh = v_ref[b, :, sl].astype(BF16)
                scores = lax.dot_general(jnp.where(mine, q, 0.0).astype(BF16), k_bf, (((1,), (1,)), ((), ())),
                                         preferred_element_type=F32) * decay
                inner = jnp.dot(scores.astype(BF16), vh, preferred_element_type=F32)
                cross = jnp.dot(jnp.where(mine, q_decayed, 0.0).astype(BF16), state_bf, preferred_element_type=F32)
                o = inner + cross
                kv = jnp.dot(kt_decayed[rows], vh, preferred_element_type=F32)
                chunk_decay = jnp.exp(lgs[h] * jnp.full((RET_DK, RET_DV), float(cc), F32))
                s_scr[b, d, rows] = s_scr[b, d, rows] * chunk_decay + kv
                mu = jnp.mean(o, axis=-1, keepdims=True)
                oc = o - mu
                var = jnp.mean(oc * oc, axis=-1, keepdims=True)
                gate = g_ref[b, :, sl]
                y_ref[b, :, sl] = (gate * jax.nn.sigmoid(gate)) * (oc * lax.rsqrt(var + EPS) * gain[:, sl])

    @pl.when(c == n_chunks - 1)
    def _():
        sf_ref[...] = s_scr[...]


def _retention(p, lg, cos, sin, gain, s0, use_rope):
    b, t, _ = p.shape
    cc = RET_CHUNK
    n = t // cc
    wv = RET_HEADS * RET_DV

    def pspec(width, col, backward):
        if backward:
            return pl.BlockSpec((b, cc, width), lambda ci, lg_: (0, n - 1 - ci, col // width))
        return pl.BlockSpec((b, cc, width), lambda ci, lg_: (0, ci, col // width))

    def tspec(backward):
        if backward:
            return pl.BlockSpec((cc, 128), lambda ci, lg_: (n - 1 - ci, 0))
        return pl.BlockSpec((cc, 128), lambda ci, lg_: (ci, 0))

    state_spec = pl.BlockSpec((b, 2, RET_HEADS * RET_DK, RET_DV), lambda ci, lg_: (0, 0, 0, 0))
    kern = functools.partial(_ret_kernel, use_rope=use_rope, n_chunks=n)
    grid_spec = pltpu.PrefetchScalarGridSpec(
        num_scalar_prefetch=1,
        grid=(n,),
        in_specs=[
            pspec(256, C_RET_Q, False), pspec(256, C_RET_Q, True),
            pspec(256, C_RET_K, False), pspec(256, C_RET_K, True),
            pspec(wv, C_RET_V, False), pspec(wv, C_RET_V, True),
            pspec(wv, C_RET_GF, False), pspec(wv, C_RET_GB, True),
            tspec(False), tspec(True), tspec(False), tspec(True),
            pl.BlockSpec((1, wv), lambda ci, lg_: (0, 0)),
            state_spec,
        ],
        out_specs=[
            pl.BlockSpec((b, cc, wv), lambda ci, lg_: (0, ci, 0)),
            pl.BlockSpec((b, cc, wv), lambda ci, lg_: (0, n - 1 - ci, 0)),
            state_spec,
        ],
        scratch_shapes=[pltpu.VMEM((b, 2, RET_HEADS * RET_DK, RET_DV), F32)],
    )
    return pl.pallas_call(
        kern,
        grid_spec=grid_spec,
        out_shape=[
            jax.ShapeDtypeStruct((b, t, wv), F32),
            jax.ShapeDtypeStruct((b, t, wv), F32),
            jax.ShapeDtypeStruct((b, 2, RET_HEADS * RET_DK, RET_DV), F32),
        ],
        compiler_params=_cparams("arbitrary"),
        name="retention",
    )(lg, p, p, p, p, p, p, p, p, cos, cos, sin, sin, gain, s0)


def _na_tables(rows):
    groups = rows // NA_QROWS
    cols = np.arange(GRID_W)
    c0 = np.clip(cols - NA_KW // 2, 0, GRID_W - NA_KW)
    col_ok = (cols[None, :] >= c0[:, None]) & (cols[None, :] < c0[:, None] + NA_KW)
    dc = cols[None, :] - cols[:, None] + (NA_KW - 1)
    onehot = (dc[None] == np.arange(2 * NA_KW - 1)[:, None, None]) & col_ok[None]
    row_bias = np.full((3, NA_QROWS, NA_KROWS), 2 * NA_KH - 1, np.int64)
    for ti, g in enumerate((0, 1, groups - 1)):
        ws = int(np.clip(NA_QROWS * g - NA_KH // 2, 0, rows - NA_KROWS))
        for lr in range(NA_QROWS):
            r = NA_QROWS * g + lr
            r0 = int(np.clip(r - NA_KH // 2, 0, rows - NA_KH))
            for kr in range(NA_KROWS):
                if r0 <= ws + kr < r0 + NA_KH:
                    row_bias[ti, lr, kr] = ws + kr - r + (NA_KH - 1)
    return onehot, col_ok, row_bias


def _na_bias_tables(na_rpb, rows):
    onehot, col_ok, row_bias = _na_tables(rows)
    depth, heads = na_rpb.shape[:2]
    nd = 2 * NA_KH
    toep = jnp.einsum('lhdj,jck->lhdck', na_rpb, onehot.astype(np.float32), precision=lax.Precision.HIGHEST)
    toep = jnp.where(col_ok, toep, MASK_VALUE)
    masked = jnp.full(toep.shape[:2] + (1, GRID_W, GRID_W), MASK_VALUE, F32)
    toep = jnp.concatenate([toep, masked], axis=2)
    sides = jnp.concatenate([jnp.pad(toep, ((0, 0),) * 4 + ((0, GRID_W),)),
                             jnp.pad(toep, ((0, 0),) * 4 + ((GRID_W, 0),))], axis=2)
    pairs = row_bias.reshape(-1, 2)
    pick = ((pairs[:, :1] == np.arange(nd)[None, :]).astype(np.float32),
            (pairs[:, 1:] == np.arange(nd)[None, :]).astype(np.float32))
    pick = np.concatenate(pick, axis=1)
    tab = jnp.einsum('nd,lhdck->lhnck', pick, sides, precision=lax.Precision.HIGHEST)
    return tab.reshape(depth, heads, 3, NA_QROWS * NA_KROWS // 2, GRID_W, 2 * GRID_W)


def _na_kernel(q_ref, k0_ref, k1_ref, k2_ref, v0_ref, v1_ref, v2_ref, kz_ref, vz_ref, bias_ref, o_ref):
    lanes = 2 * NA_DH
    npair = NA_KROWS // 2
    low = lax.broadcasted_iota(jnp.int32, (q_ref.shape[1], lanes), 1) < NA_DH
    for pr in range(NA_HEADS // 2):
        sl = slice(pr * lanes, (pr + 1) * lanes)
        qp = q_ref[0, :, sl] * (NA_DH ** -0.5)
        qm = (jnp.where(low, qp, 0.0).astype(BF16), jnp.where(low, 0.0, qp).astype(BF16))
        ks = [r[0, :, sl].astype(BF16) for r in (k0_ref, k1_ref, k2_ref, kz_ref)]
        vs = [r[0, :, sl].astype(BF16) for r in (v0_ref, v1_ref, v2_ref, vz_ref)]
        outs = []
        for hh in range(2):
            h = 2 * pr + hh
            cols = []
            for j in range(4):
                sj = lax.dot_general(qm[hh], ks[j], (((1,), (1,)), ((), ())), preferred_element_type=F32)
                halves = [sj[:, i * lanes:(i + 1) * lanes] for i in range(sj.shape[1] // lanes)]
                if j < 3:
                    halves = [hv + jnp.concatenate([bias_ref[0, h, 0, lr * npair + j * len(halves) + i]
                                                    for lr in range(NA_QROWS)], axis=0)
                              for i, hv in enumerate(halves)]
                cols += halves
            m = jnp.max(functools.reduce(jnp.maximum, cols), axis=-1, keepdims=True)
            ps = [jnp.exp(cj - m) for cj in cols]
            l = jnp.sum(functools.reduce(jnp.add, ps), axis=-1, keepdims=True)
            o = None
            per = len(cols) // 4
            for j in range(4):
                pj = jnp.concatenate([pc.astype(BF16) for pc in ps[j * per:(j + 1) * per]], axis=1)
                oj = jnp.dot(pj, vs[j], preferred_element_type=F32)
                o = oj if o is None else o + oj
            outs.append(o / l)
        o_ref[0, :, sl] = jnp.where(low, outs[0], outs[1]).astype(o_ref.dtype)


def _na(px, pz, bias, layer):
    b, t, _ = px.shape
    lz = pz.shape[1]
    groups = t // NA_QB
    w = NA_HEADS * NA_DH
    kblk = NA_KB // 3
    assert lz == kblk and groups >= 3

    def kspec(col, off):
        return pl.BlockSpec((1, kblk, w), lambda bi, g: (bi, jnp.clip(g - 1, 0, groups - 3) + off, col // w))

    def tab(g):
        return jnp.where(g == 0, 0, jnp.where(g == groups - 1, 2, 1))

    return pl.pallas_call(
        _na_kernel,
        grid=(b, groups),
        in_specs=[
            pl.BlockSpec((1, NA_QB, w), lambda bi, g: (bi, g, C_NA_Q // w)),
            kspec(C_NA_K, 0), kspec(C_NA_K, 1), kspec(C_NA_K, 2),
            kspec(C_NA_V, 0), kspec(C_NA_V, 1), kspec(C_NA_V, 2),
            pl.BlockSpec((1, lz, w), lambda bi, g: (bi, 0, C_NA_K // w)),
            pl.BlockSpec((1, lz, w), lambda bi, g: (bi, 0, C_NA_V // w)),
            pl.BlockSpec((1, NA_HEADS, 1) + bias.shape[3:], lambda bi, g: (layer, 0, tab(g), 0, 0, 0)),
        ],
        out_specs=pl.BlockSpec((1, NA_QB, w), lambda bi, g: (bi, g, 0)),
        out_shape=jax.ShapeDtypeStruct((b, t, w), BF16),
        compiler_params=_cparams("parallel", "arbitrary"),
        name="na",
    )(px, px, px, px, px, px, px, pz, pz, bias)


def _rms(x, gain):
    return x * lax.rsqrt(jnp.mean(x * x, axis=-1, keepdims=True) + EPS) * gain


def _mlaprep_kernel(xq_ref, xkv_ref, xkr_ref, zq_ref, zkv_ref, zkr_ref, qn_ref, kvn_ref, wq_ref, wqs_ref, wk_ref,
                    wv_ref, cos_ref, sin_ref, q_ref, k_ref, v_ref, *, nx):
    latent = pl.program_id(1) < nx
    pq = jnp.where(latent, xq_ref[0], zq_ref[0])
    pkv = jnp.where(latent, xkv_ref[0], zkv_ref[0])
    pkr = jnp.where(latent, xkr_ref[0], zkr_ref[0])
    hq = _rms(pq, qn_ref[...]).astype(BF16)
    hkv = _rms(pkv, kvn_ref[...]).astype(BF16)
    q = jnp.dot(hq, wq_ref[...], preferred_element_type=F32)
    q_swapped = jnp.dot(hq, wqs_ref[...], preferred_element_type=F32)
    k = jnp.dot(hkv, wk_ref[...], preferred_element_type=F32)
    v = jnp.dot(hkv, wv_ref[...], preferred_element_type=F32)
    kr = pltpu.roll(pkr, MLA_NOPE, 1)
    k = k + jnp.concatenate([kr] * MLA_HEADS, axis=1)
    k_swapped = jnp.concatenate([_swap_halves(kr, MLA_ROPE // 2)] * MLA_HEADS, axis=1)
    cos = jnp.concatenate([cos_ref[...]] * MLA_HEADS, axis=1)
    sin = jnp.concatenate([sin_ref[...]] * MLA_HEADS, axis=1)
    q = q * cos + q_swapped * sin
    k = k * cos + k_swapped * sin
    q_ref[0] = (q * ((MLA_NOPE + MLA_ROPE) ** -0.5 * LOG2_E)).astype(BF16)
    k_ref[0] = k.astype(BF16)
    v_ref[0] = v.astype(BF16)


def _mlaprep(px, pz, qn, kvn, wq, wqs, wk, wv, cos, sin):
    b, t, _ = px.shape
    lz = pz.shape[1]
    tm = lz
    nx = t // tm
    wqk = MLA_HEADS * MLA_DK_PAD
    wvv = MLA_HEADS * MLA_DV

    def xspec(width, col):
        return pl.BlockSpec((1, tm, width), lambda bi, i: (bi, jnp.minimum(i, nx - 1), col // width))

    def zspec(width, col):
        return pl.BlockSpec((1, tm, width), lambda bi, i: (bi, 0, col // width))

    out_spec = pl.BlockSpec((1, tm, wqk), lambda bi, i: (bi, i, 0))
    kern = functools.partial(_mlaprep_kernel, nx=nx)
    return pl.pallas_call(
        kern,
        grid=(b, nx + 1),
        in_specs=[
            xspec(MLA_RANK, C_MLA_Q), xspec(MLA_RANK, C_MLA_KV), xspec(128, C_MLA_KR),
            zspec(MLA_RANK, C_MLA_Q), zspec(MLA_RANK, C_MLA_KV), zspec(128, C_MLA_KR),
            _const_spec((1, MLA_RANK)), _const_spec((1, MLA_RANK)),
            _const_spec((MLA_RANK, wqk)), _const_spec((MLA_RANK, wqk)), _const_spec((MLA_RANK, wqk)),
            _const_spec((MLA_RANK, wvv)),
            pl.BlockSpec((tm, 128), lambda bi, i: (i, 0)),
            pl.BlockSpec((tm, 128), lambda bi, i: (i, 0)),
        ],
        out_specs=[out_spec, out_spec, pl.BlockSpec((1, tm, wvv), lambda bi, i: (bi, i, 0))],
        out_shape=[jax.ShapeDtypeStruct((b, t + lz, wqk), BF16)] * 2
        + [jax.ShapeDtypeStruct((b, t + lz, wvv), BF16)],
        compiler_params=_cparams("parallel", "parallel"),
        name="mlaprep",
    )(px, px, px, pz, pz, pz, qn, kvn, wq, wqs, wk, wv, cos, sin)


def _flash_kernel(q_ref, k_ref, v_ref, o_ref, m_scr, l_scr, acc_scr, *, heads, dk, dv, scale, nk):
    ki = pl.program_id(2)
    tq = q_ref.shape[1]
    tk = k_ref.shape[1]
    lanes = 2 * dv

    @pl.when(ki == 0)
    def _():
        m_scr[...] = jnp.full(m_scr.shape, -jnp.inf, F32)
        l_scr[...] = jnp.zeros(l_scr.shape, F32)
        acc_scr[...] = jnp.zeros(acc_scr.shape, F32)

    low = lax.broadcasted_iota(jnp.int32, (tq, lanes), 1) < dv
    for pr in range(heads // 2):
        vp = v_ref[0, :, pr * lanes:(pr + 1) * lanes].astype(BF16)
        alphas, pvs = [], []
        for h in (2 * pr, 2 * pr + 1):
            qh = q_ref[0, :, h * dk:(h + 1) * dk]
            if scale != 1.0:
                qh = qh * scale
            kh = k_ref[0, :, h * dk:(h + 1) * dk]
            s = lax.dot_general(qh.astype(BF16), kh.astype(BF16), (((1,), (1,)), ((), ())),
                                preferred_element_type=F32)
            cols = [s[:, j * lanes:(j + 1) * lanes] for j in range(tk // lanes)]
            m_prev = m_scr[h]
            m_tile = jnp.max(functools.reduce(jnp.maximum, cols), axis=-1, keepdims=True)
            m_new = jnp.maximum(m_prev, m_tile)
            alpha = jnp.exp2(m_prev - m_new)
            ps = [jnp.exp2(cj - m_new) for cj in cols]
            l_scr[h] = alpha * l_scr[h] + functools.reduce(jnp.add, ps)
            m_scr[h] = m_new
            p = jnp.concatenate([pj.astype(BF16) for pj in ps], axis=1)
            pvs.append(jnp.dot(p, vp, preferred_element_type=F32))
            alphas.append(alpha)
        acc_scr[pr] = acc_scr[pr] * jnp.where(low, alphas[0], alphas[1]) + jnp.where(low, pvs[0], pvs[1])

    @pl.when(ki == nk - 1)
    def _():
        for pr in range(heads // 2):
            l0 = jnp.sum(l_scr[2 * pr], axis=-1, keepdims=True)
            l1 = jnp.sum(l_scr[2 * pr + 1], axis=-1, keepdims=True)
            o_ref[0, :, pr * lanes:(pr + 1) * lanes] = (acc_scr[pr] / jnp.where(low, l0, l1)).astype(o_ref.dtype)


def _flash(q, k, v, qcol, kcol, vcol, heads, dk, dv, scale, tq_pref, tk_pref):
    b, tq_all, _ = q.shape
    tk_all = k.shape[1]
    tq = _tile(tq_all, tq_pref)
    tk = _tile(tk_all, tk_pref)
    nk = tk_all // tk
    assert heads % 2 == 0 and 2 * dv == 128 and tk % 128 == 0
    kern = functools.partial(_flash_kernel, heads=heads, dk=dk, dv=dv, scale=scale, nk=nk)
    return pl.pallas_call(
        kern,
        grid=(b, tq_all // tq, nk),
        in_specs=[
            pl.BlockSpec((1, tq, heads * dk), lambda bi, i, j: (bi, i, qcol)),
            pl.BlockSpec((1, tk, heads * dk), lambda bi, i, j: (bi, j, kcol)),
            pl.BlockSpec((1, tk, heads * dv), lambda bi, i, j: (bi, j, vcol)),
        ],
        out_specs=pl.BlockSpec((1, tq, heads * dv), lambda bi, i, j: (bi, i, 0)),
        out_shape=jax.ShapeDtypeStruct((b, tq_all, heads * dv), BF16),
        scratch_shapes=[
            pltpu.VMEM((heads, tq, 2 * dv), F32),
            pltpu.VMEM((heads, tq, 2 * dv), F32),
            pltpu.VMEM((heads // 2, tq, 2 * dv), F32),
        ],
        compiler_params=_cparams("parallel", "parallel", "arbitrary"),
        name="flash",
    )(q, k, v)


def _mla_attn_kernel(q_ref, k_ref, v_ref, o_ref, m_scr, l_scr, acc_scr, *, heads, nk):
    ki = pl.program_id(2)
    w = MLA_DK_PAD
    tq = q_ref.shape[1]

    @pl.when(ki == 0)
    def _():
        m_scr[...] = jnp.full(m_scr.shape, -jnp.inf, F32)
        l_scr[...] = jnp.zeros(l_scr.shape, F32)
        acc_scr[...] = jnp.zeros(acc_scr.shape, F32)

    low = lax.broadcasted_iota(jnp.int32, (tq, w), 1) < MLA_DV
    for pr in range(heads // 2):
        vp = v_ref[0, :, pr * w:(pr + 1) * w]
        alphas, pvs = [], []
        for h in (2 * pr, 2 * pr + 1):
            sl = slice(h * w, (h + 1) * w)
            s = lax.dot_general(q_ref[0, :, sl], k_ref[0, :, sl], (((1,), (1,)), ((), ())),
                                preferred_element_type=F32)
            cols = [s[:, j * w:(j + 1) * w] for j in range(s.shape[1] // w)]
            m_prev = m_scr[h]
            m_new = jnp.maximum(m_prev, jnp.max(functools.reduce(jnp.maximum, cols), axis=-1, keepdims=True))
            alpha = jnp.exp2(m_prev - m_new)
            ps = [jnp.exp2(cj - m_new) for cj in cols]
            l_scr[h] = alpha * l_scr[h] + functools.reduce(jnp.add, ps)
            m_scr[h] = m_new
            p = jnp.concatenate([pj.astype(BF16) for pj in ps], axis=1)
            pvs.append(jnp.dot(p, vp, preferred_element_type=F32))
            alphas.append(alpha)
        acc_scr[pr] = acc_scr[pr] * jnp.where(low, alphas[0], alphas[1]) + jnp.where(low, pvs[0], pvs[1])

    @pl.when(ki == nk - 1)
    def _():
        for pr in range(heads // 2):
            l0 = jnp.sum(l_scr[2 * pr], axis=-1, keepdims=True)
            l1 = jnp.sum(l_scr[2 * pr + 1], axis=-1, keepdims=True)
            o_ref[0, :, pr * w:(pr + 1) * w] = (acc_scr[pr] / jnp.where(low, l0, l1)).astype(o_ref.dtype)


def _mla_attn(q, k, v, q_start, q_len, k_start, k_len, tq_pref, tk_pref):
    b, _, wq = q.shape
    heads = wq // MLA_DK_PAD
    tq = _tile(q_len, tq_pref, 8)
    tk = _tile(k_len, tk_pref, 128)
    assert q_start % tq == 0 and k_start % tk == 0
    q_off, k_off = q_start // tq, k_start // tk
    nk = k_len // tk
    kern = functools.partial(_mla_attn_kernel, heads=heads, nk=nk)
    return pl.pallas_call(
        kern,
        grid=(b, q_len // tq, nk),
        in_specs=[
            pl.BlockSpec((1, tq, wq), lambda bi, i, j: (bi, q_off + i, 0)),
            pl.BlockSpec((1, tk, wq), lambda bi, i, j: (bi, k_off + j, 0)),
            pl.BlockSpec((1, tk, heads * MLA_DV), lambda bi, i, j: (bi, k_off + j, 0)),
        ],
        out_specs=pl.BlockSpec((1, tq, heads * MLA_DV), lambda bi, i, j: (bi, i, 0)),
        out_shape=jax.ShapeDtypeStruct((b, q_len, heads * MLA_DV), BF16),
        scratch_shapes=[
            pltpu.VMEM((heads, tq, MLA_DK_PAD), F32),
            pltpu.VMEM((heads, tq, MLA_DK_PAD), F32),
            pltpu.VMEM((heads // 2, tq, MLA_DK_PAD), F32),
        ],
        compiler_params=_cparams("parallel", "parallel", "arbitrary"),
        name="mla_attn",
    )(q, k, v)


def _layer_norm(r, gain, bias):
    mu = jnp.mean(r, axis=-1, keepdims=True)
    rc = r - mu
    var = jnp.mean(rc * rc, axis=-1, keepdims=True)
    return rc * lax.rsqrt(var + EPS) * gain + bias


def _merge_kernel(yaf_ref, yab_ref, yb_ref, yc_ref, ga_ref, gb_ref, gc_ref, x_ref, g1_ref,
                  wa_ref, wb_ref, wc_ref, wo_ref, lng_ref, lnb_ref, o_ref):
    ya = (yaf_ref[0] + yab_ref[0]).astype(BF16)
    y = (jax.nn.sigmoid(ga_ref[0]) * jnp.dot(ya, wa_ref[...], preferred_element_type=F32)
         + jax.nn.sigmoid(gb_ref[0]) * jnp.dot(yb_ref[0].astype(BF16), wb_ref[...], preferred_element_type=F32)
         + jax.nn.sigmoid(gc_ref[0]) * jnp.dot(yc_ref[0].astype(BF16), wc_ref[...], preferred_element_type=F32))
    mix = jnp.dot(y.astype(BF16), wo_ref[...], preferred_element_type=F32)
    r = DEEPNORM_ALPHA * x_ref[0] + g1_ref[0] * mix
    o_ref[0] = _layer_norm(r, lng_ref[...], lnb_ref[...])


def _merge(yaf, yab, yb, yc, p, x, g1, wa, wb, wc, wo, lng, lnb):
    b, t, d = x.shape
    tm = _tile(t, 512)
    wbr = 512
    return pl.pallas_call(
        _merge_kernel,
        grid=(b, t // tm),
        in_specs=[
            pl.BlockSpec((1, tm, wbr), lambda bi, i: (bi, i, 0)),
            pl.BlockSpec((1, tm, wbr), lambda bi, i: (bi, i, 0)),
            pl.BlockSpec((1, tm, wbr), lambda bi, i: (bi, i, 0)),
            pl.BlockSpec((1, tm, wbr), lambda bi, i: (bi, i, 0)),
            pl.BlockSpec((1, tm, d), lambda bi, i: (bi, i, C_GATE_A // d)),
            pl.BlockSpec((1, tm, d), lambda bi, i: (bi, i, C_GATE_B // d)),
            pl.BlockSpec((1, tm, d), lambda bi, i: (bi, i, C_GATE_C // d)),
            pl.BlockSpec((1, tm, d), lambda bi, i: (bi, i, 0)),
            pl.BlockSpec((1, 1, d), lambda bi, i: (bi, 0, 0)),
            _const_spec((wbr, d)), _const_spec((wbr, d)), _const_spec((wbr, d)), _const_spec((d, d)),
            _const_spec((1, d)), _const_spec((1, d)),
        ],
        out_specs=pl.BlockSpec((1, tm, d), lambda bi, i: (bi, i, 0)),
        out_shape=jax.ShapeDtypeStruct((b, t, d), F32),
        compiler_params=_cparams("parallel", "parallel"),
        name="merge",
    )(yaf, yab, yb, yc, p, p, p, x, g1, wa, wb, wc, wo, lng, lnb)


def _ffn_kernel(x_ref, sc_ref, sh_ref, g2_ref, w1_ref, w2_ref, lng_ref, lnb_ref, o_ref, *, ff_chunk):
    x = x_ref[0]
    h = (x * (1.0 + sc_ref[0]) + sh_ref[0]).astype(BF16)
    acc = jnp.zeros(x.shape, F32)
    for j in range(D_FF // ff_chunk):
        u = jnp.dot(h, w1_ref[:, j * ff_chunk:(j + 1) * ff_chunk], preferred_element_type=F32)
        u = jnp.square(jnp.maximum(u, 0.0)).astype(BF16)
        acc = acc + jnp.dot(u, w2_ref[j * ff_chunk:(j + 1) * ff_chunk, :], preferred_element_type=F32)
    r = DEEPNORM_ALPHA * x + g2_ref[0] * acc
    o_ref[0] = _layer_norm(r, lng_ref[...], lnb_ref[...])


def _ffn(x, sc, sh, g2, w1, w2, lng, lnb):
    b, t, d = x.shape
    tm = _tile(t, 512)
    kern = functools.partial(_ffn_kernel, ff_chunk=1024)
    return pl.pallas_call(
        kern,
        grid=(b, t // tm),
        in_specs=[
            pl.BlockSpec((1, tm, d), lambda bi, i: (bi, i, 0)),
            pl.BlockSpec((1, 1, d), lambda bi, i: (bi, 0, 0)),
            pl.BlockSpec((1, 1, d), lambda bi, i: (bi, 0, 0)),
            pl.BlockSpec((1, 1, d), lambda bi, i: (bi, 0, 0)),
            _const_spec((d, D_FF)), _const_spec((D_FF, d)),
            _const_spec((1, d)), _const_spec((1, d)),
        ],
        out_specs=pl.BlockSpec((1, tm, d), lambda bi, i: (bi, i, 0)),
        out_shape=jax.ShapeDtypeStruct((b, t, d), F32),
        compiler_params=_cparams("parallel", "parallel"),
        name="ffn",
    )(x, sc, sh, g2, w1, w2, lng, lnb)


def _rope_tables(n_tok, rot_dim):
    t = jnp.arange(n_tok)
    row = (t // GRID_W).astype(F32)
    col = (t % GRID_W).astype(F32)
    n_freq = rot_dim // 4
    inv_freq = ROPE_BASE ** (-2.0 * jnp.arange(n_freq, dtype=F32) / (rot_dim // 2))
    ang = jnp.concatenate([row[:, None] * inv_freq, col[:, None] * inv_freq], axis=-1)
    return jnp.cos(ang), jnp.sin(ang)


def _ret_rope_tables(n_tok):
    cos, sin = _rope_tables(n_tok, RET_DK)
    cos_h = jnp.concatenate([cos, cos], axis=1)
    sin_h = jnp.concatenate([-sin, sin], axis=1)
    return jnp.tile(cos_h, (1, 2)), jnp.tile(sin_h, (1, 2))


def _mla_rope_tables(n_tok):
    cos, sin = _rope_tables(n_tok, MLA_ROPE)
    ones = jnp.ones((n_tok, MLA_NOPE), F32)
    zeros = jnp.zeros((n_tok, MLA_NOPE), F32)
    pad = jnp.zeros((n_tok, MLA_DK_PAD - MLA_NOPE - MLA_ROPE), F32)
    cos_h = jnp.concatenate([ones, cos, cos, pad], axis=1)
    sin_h = jnp.concatenate([zeros, -sin, sin, pad], axis=1)
    return cos_h, sin_h


def _pack_w_in(w):
    d = w.shape[0]
    return jnp.concatenate([w[:, :4096], w[:, 4128:7200], w[:, 4096:4128],
                            jnp.zeros((d, P_WIDTH - 7200), w.dtype)], axis=1).astype(BF16)


def _pack_mla_weights(w_qup, w_kvup):
    r = w_qup.shape[0]
    wq = w_qup.reshape(r, MLA_HEADS, MLA_NOPE + MLA_ROPE)
    half = MLA_ROPE // 2
    wqs = jnp.concatenate([jnp.zeros((r, MLA_HEADS, MLA_NOPE), wq.dtype), wq[:, :, MLA_NOPE + half:],
                           wq[:, :, MLA_NOPE:MLA_NOPE + half]], axis=2)
    pad_q = ((0, 0), (0, 0), (0, MLA_DK_PAD - MLA_NOPE - MLA_ROPE))
    wq = jnp.pad(wq, pad_q).reshape(r, MLA_HEADS * MLA_DK_PAD)
    wqs = jnp.pad(wqs, pad_q).reshape(r, MLA_HEADS * MLA_DK_PAD)
    wkv = w_kvup.reshape(r, MLA_HEADS, MLA_NOPE + MLA_DV)
    wk = jnp.pad(wkv[:, :, :MLA_NOPE], ((0, 0), (0, 0), (0, MLA_DK_PAD - MLA_NOPE))).reshape(r, MLA_HEADS * MLA_DK_PAD)
    wv = wkv[:, :, MLA_NOPE:].reshape(r, MLA_HEADS * MLA_DV)
    return wq.astype(BF16), wqs.astype(BF16), wk.astype(BF16), wv.astype(BF16)


def kernel(x, c, ctx, c_ctx, w_ada, b_ada, w_in, ret_log_decay, ret_gn_gain, na_rpb, mla_q_norm, mla_w_qup,
           mla_kv_norm, mla_w_kvup, w_branch_ret, w_branch_na, w_branch_mla, w_out, w_ff1, w_ff2, ln_gain, ln_bias):
    depth = w_ada.shape[0]
    b, t, d = x.shape
    lz = ctx.shape[1]
    rows = t // GRID_W

    cc = jnp.zeros((8, d), F32).at[:b].set(c).at[b].set(c_ctx)
    mod = _ada(cc, w_ada, b_ada)

    cos_r, sin_r = _ret_rope_tables(t)
    cos_m, sin_m = _mla_rope_tables(t)
    cos_m = jnp.concatenate([cos_m, jnp.ones((lz, MLA_DK_PAD), F32)], axis=0)
    sin_m = jnp.concatenate([sin_m, jnp.zeros((lz, MLA_DK_PAD), F32)], axis=0)
    cos_rz, sin_rz = cos_r[:lz], sin_r[:lz]
    na_bias = _na_bias_tables(na_rpb, rows)
    s_zero = jnp.zeros((b, 2, RET_HEADS * RET_DK, RET_DV), F32)

    z = ctx
    for l in range(depth):
        need_ctx = l < depth - 1
        mx = mod[l, :b].reshape(b, 6, 1, d)
        mz = jnp.broadcast_to(mod[l, b].reshape(1, 6, 1, d), (b, 6, 1, d))
        sh1x, sc1x, g1x, sh2x, sc2x, g2x = [mx[:, i] for i in range(6)]
        sh1z, sc1z, g1z, sh2z, sc2z, g2z = [mz[:, i] for i in range(6)]

        w_in_p = _pack_w_in(w_in[l])
        px = _inproj(x, sc1x, sh1x, w_in_p)
        pz = _inproj(z, sc1z, sh1z, w_in_p)

        lg = jnp.log1p(-jnp.exp(ret_log_decay[l].astype(F32))).reshape(2 * RET_HEADS)
        gn_gain = ret_gn_gain[l].reshape(1, RET_HEADS * RET_DV)
        yaf_z, yab_z, s_ctx = _retention(pz, lg, cos_rz, sin_rz, gn_gain, s_zero, use_rope=False)
        yaf_x, yab_x, _ = _retention(px, lg, cos_r, sin_r, gn_gain, s_ctx, use_rope=True)

        yb_x = _na(px, pz, na_bias, l)

        wq, wqs, wk, wv = _pack_mla_weights(mla_w_qup[l], mla_w_kvup[l])
        qn = mla_q_norm[l].reshape(1, MLA_RANK)
        kvn = mla_kv_norm[l].reshape(1, MLA_RANK)
        q_all, k_all, v_all = _mlaprep(px, pz, qn, kvn, wq, wqs, wk, wv, cos_m, sin_m)
        yc_x = _mla_attn(q_all, k_all, v_all, 0, t, 0, t + lz, 1024, 1408)

        wa = w_branch_ret[l].astype(BF16)
        wb = w_branch_na[l].astype(BF16)
        wc = w_branch_mla[l].astype(BF16)
        wo = w_out[l].astype(BF16)
        w1 = w_ff1[l].astype(BF16)
        w2 = w_ff2[l].astype(BF16)
        lng1, lnb1 = ln_gain[l, 0].reshape(1, d), ln_bias[l, 0].reshape(1, d)
        lng2, lnb2 = ln_gain[l, 1].reshape(1, d), ln_bias[l, 1].reshape(1, d)

        x1 = _merge(yaf_x, yab_x, yb_x, yc_x, px, x, g1x, wa, wb, wc, wo, lng1, lnb1)
        x = _ffn(x1, sc2x, sh2x, g2x, w1, w2, lng2, lnb2)

        if need_ctx:
            wna = NA_HEADS * NA_DH
            yb_z = _flash(pz, pz, pz, C_NA_Q // wna, C_NA_K // wna, C_NA_V // wna,
                          NA_HEADS, NA_DH, NA_DH, NA_DH ** -0.5 * LOG2_E, 256, 256)
            yc_z = _mla_attn(q_all, k_all, v_all, t, lz, t, lz, lz, lz)
            z1 = _merge(yaf_z, yab_z, yb_z, yc_z, pz, z, g1z, wa, wb, wc, wo, lng1, lnb1)
            z = _ffn(z1, sc2z, sh2z, g2z, w1, w2, lng2, lnb2)
    return x
```

```python
import functools

import numpy as np
import jax
import jax.numpy as jnp
from jax import lax
from jax.experimental import pallas as pl
from jax.experimental.pallas import tpu as pltpu

F32 = jnp.float32
BF16 = jnp.bfloat16

D_MODEL = 1024
GRID_W = 64
RET_HEADS = 4
RET_DK = 64
RET_DV = 128
RET_CHUNK = 128
NA_HEADS = 8
NA_DH = 64
NA_KH = 8
NA_KW = 16
MLA_HEADS = 8
MLA_RANK = 256
MLA_NOPE = 64
MLA_ROPE = 32
MLA_DV = 64
MLA_DK_PAD = 128
D_FF = 4 * D_MODEL
ROPE_BASE = 10000.0
EPS = 1e-5
DEPTH_FOR_NORM = 4
DEEPNORM_ALPHA = (2 * DEPTH_FOR_NORM) ** 0.25
MASK_VALUE = -1e30
LOG2_E = 1.4426950408889634

C_RET_Q, C_RET_K, C_RET_V, C_RET_GF, C_RET_GB = 0, 256, 512, 1024, 1536
C_NA_Q, C_NA_K, C_NA_V = 2048, 2560, 3072
C_MLA_Q, C_MLA_KV = 3584, 3840
C_GATE_A, C_GATE_B, C_GATE_C = 4096, 5120, 6144
C_MLA_KR = 7168
P_WIDTH = 7296
P_COL_TILE = 2432

NA_QROWS = 4
NA_KROWS = 12
NA_QB = NA_QROWS * GRID_W
NA_KB = NA_KROWS * GRID_W

VMEM_LIMIT = 56 * 1024 * 1024


def _cparams(*sem):
    return pltpu.CompilerParams(dimension_semantics=sem, vmem_limit_bytes=VMEM_LIMIT)


def _tile(n, pref, mult=1):
    t = min(n, pref) // mult * mult
    while n % t:
        t -= mult
    return t


def _const_spec(shape):
    nd = len(shape)
    return pl.BlockSpec(shape, lambda *_: (0,) * nd)


def _ada_kernel(c_ref, w_ref, b_ref, o_ref):
    c = c_ref[...]
    a = c * jax.nn.sigmoid(c)
    o_ref[0] = jnp.dot(a.astype(BF16), w_ref[0].astype(BF16), preferred_element_type=F32) + b_ref[0]


def _ada(cc, w_ada, b_ada):
    depth, d, n = w_ada.shape
    tn = 1024
    return pl.pallas_call(
        _ada_kernel,
        grid=(depth, n // tn),
        in_specs=[
            pl.BlockSpec((8, d), lambda l, j: (0, 0)),
            pl.BlockSpec((1, d, tn), lambda l, j: (l, 0, j)),
            pl.BlockSpec((1, 1, tn), lambda l, j: (l, 0, j)),
        ],
        out_specs=pl.BlockSpec((1, 8, tn), lambda l, j: (l, 0, j)),
        out_shape=jax.ShapeDtypeStruct((depth, 8, n), F32),
        compiler_params=_cparams("parallel", "parallel"),
        name="ada",
    )(cc, w_ada, b_ada.reshape(depth, 1, n))


def _inproj_kernel(x_ref, sc_ref, sh_ref, w_ref, o_ref):
    h = x_ref[0] * (1.0 + sc_ref[0]) + sh_ref[0]
    o_ref[0] = jnp.dot(h.astype(BF16), w_ref[...], preferred_element_type=F32)


def _inproj(x, sc, sh, w):
    b, t, d = x.shape
    tm = _tile(t, 512)
    tn = P_COL_TILE
    return pl.pallas_call(
        _inproj_kernel,
        grid=(P_WIDTH // tn, b, t // tm),
        in_specs=[
            pl.BlockSpec((1, tm, d), lambda j, bi, i: (bi, i, 0)),
            pl.BlockSpec((1, 1, d), lambda j, bi, i: (bi, 0, 0)),
            pl.BlockSpec((1, 1, d), lambda j, bi, i: (bi, 0, 0)),
            pl.BlockSpec((d, tn), lambda j, bi, i: (0, j)),
        ],
        out_specs=pl.BlockSpec((1, tm, tn), lambda j, bi, i: (bi, i, j)),
        out_shape=jax.ShapeDtypeStruct((b, t, P_WIDTH), F32),
        compiler_params=_cparams("parallel", "parallel", "parallel"),
        name="inproj",
    )(x, sc, sh, w)


def _swap_halves(x, half):
    n = x.shape[-1]
    lane = lax.broadcasted_iota(jnp.int32, x.shape, x.ndim - 1)
    first = (lane % (2 * half)) < half
    return jnp.where(first, pltpu.roll(x, n - half, x.ndim - 1), pltpu.roll(x, half, x.ndim - 1))


def _ret_kernel(lg_ref, qf_ref, qb_ref, kf_ref, kb_ref, vf_ref, vb_ref, gf_ref, gb_ref,
                cosf_ref, cosb_ref, sinf_ref, sinb_ref, gain_ref, s0_ref,
                yf_ref, yb_ref, sf_ref, s_scr, *, use_rope, n_chunks):
    c = pl.program_id(0)
    batch = qf_ref.shape[0]
    cc = RET_CHUNK

    @pl.when(c == 0)
    def _():
        s_scr[...] = s0_ref[...]

    gain = gain_ref[...]
    row = lax.broadcasted_iota(jnp.int32, (cc, cc), 0).astype(F32)
    col = lax.broadcasted_iota(jnp.int32, (cc, cc), 1).astype(F32)
    wqk = RET_HEADS * RET_DK
    pos = lax.broadcasted_iota(jnp.int32, (cc, wqk), 0).astype(F32)
    head_of_lane = lax.broadcasted_iota(jnp.int32, (cc, wqk), 1) // RET_DK
    dirs = (
        (qf_ref, kf_ref, vf_ref, gf_ref, cosf_ref, sinf_ref, yf_ref, row - col, pos + 1.0, cc - 1.0 - pos),
        (qb_ref, kb_ref, vb_ref, gb_ref, cosb_ref, sinb_ref, yb_ref, col - row, cc - pos, pos),
    )
    for d, (q_ref, k_ref, v_ref, g_ref, cos_ref, sin_ref, y_ref, diff, q_exp, k_exp) in enumerate(dirs):
        if use_rope:
            cos = jnp.concatenate([cos_ref[...], cos_ref[...]], axis=1)
            sin = jnp.concatenate([sin_ref[...], sin_ref[...]], axis=1)
        lgs = [lg_ref[d * RET_HEADS + h] for h in range(RET_HEADS)]
        lg_lanes = jnp.full((cc, wqk), lgs[RET_HEADS - 1], F32)
        for h in range(RET_HEADS - 2, -1, -1):
            lg_lanes = jnp.where(head_of_lane == h, lgs[h], lg_lanes)
        q_decay = jnp.exp(lg_lanes * q_exp)
        k_decay = jnp.exp(lg_lanes * k_exp)
        for b in range(batch):
            q = q_ref[b]
            k = k_ref[b] * (RET_DK ** -0.5)
            if use_rope:
                q = q * cos + _swap_halves(q, RET_DK // 2) * sin
                k = k * cos + _swap_halves(k, RET_DK // 2) * sin
            k_bf = k.astype(BF16)
            kt_decayed = (k * k_decay).T.astype(BF16)
            q_decayed = q * q_decay
            state_bf = s_scr[b, d].astype(BF16)
            for h in range(RET_HEADS):
                mine = head_of_lane == h
                sl = slice(h * RET_DV, (h + 1) * RET_DV)
                rows = slice(h * RET_DK, (h + 1) * RET_DK)
                decay = jnp.where(diff >= 0, jnp.exp(lgs[h] * jnp.maximum(diff, 0.0)), 0.0)
                vh = v_ref[b, :, sl].astype(BF16)
                scores = lax.dot_general(jnp.where(mine, q, 0.0).astype(BF16), k_bf, (((1,), (1,)), ((), ())),
                                         preferred_element_type=F32) * decay
                inner = jnp.dot(scores.astype(BF16), vh, preferred_element_type=F32)
                cross = jnp.dot(jnp.where(mine, q_decayed, 0.0).astype(BF16), state_bf, preferred_element_type=F32)
                o = inner + cross
                kv = jnp.dot(kt_decayed[rows], vh, preferred_element_type=F32)
                chunk_decay = jnp.exp(lgs[h] * jnp.full((RET_DK, RET_DV), float(cc), F32))
                s_scr[b, d, rows] = s_scr[b, d, rows] * chunk_decay + kv
                mu = jnp.mean(o, axis=-1, keepdims=True)
                oc = o - mu
                var = jnp.mean(oc * oc, axis=-1, keepdims=True)
                gate = g_ref[b, :, sl]
                y_ref[b, :, sl] = (gate * jax.nn.sigmoid(gate)) * (oc * lax.rsqrt(var + EPS) * gain[:, sl])

    @pl.when(c == n_chunks - 1)
    def _():
        sf_ref[...] = s_scr[...]


def _retention(p, lg, cos, sin, gain, s0, use_rope):
    b, t, _ = p.shape
    cc = RET_CHUNK
    n = t // cc
    wv = RET_HEADS * RET_DV

    def pspec(width, col, backward):
        if backward:
            return pl.BlockSpec((b, cc, width), lambda ci, lg_: (0, n - 1 - ci, col // width))
        return pl.BlockSpec((b, cc, width), lambda ci, lg_: (0, ci, col // width))

    def tspec(backward):
        if backward:
            return pl.BlockSpec((cc, 128), lambda ci, lg_: (n - 1 - ci, 0))
        return pl.BlockSpec((cc, 128), lambda ci, lg_: (ci, 0))

    state_spec = pl.BlockSpec((b, 2, RET_HEADS * RET_DK, RET_DV), lambda ci, lg_: (0, 0, 0, 0))
    kern = functools.partial(_ret_kernel, use_rope=use_rope, n_chunks=n)
    grid_spec = pltpu.PrefetchScalarGridSpec(
        num_scalar_prefetch=1,
        grid=(n,),
        in_specs=[
            pspec(256, C_RET_Q, False), pspec(256, C_RET_Q, True),
            pspec(256, C_RET_K, False), pspec(256, C_RET_K, True),
            pspec(wv, C_RET_V, False), pspec(wv, C_RET_V, True),
            pspec(wv, C_RET_GF, False), pspec(wv, C_RET_GB, True),
            tspec(False), tspec(True), tspec(False), tspec(True),
            pl.BlockSpec((1, wv), lambda ci, lg_: (0, 0)),
            state_spec,
        ],
        out_specs=[
            pl.BlockSpec((b, cc, wv), lambda ci, lg_: (0, ci, 0)),
            pl.BlockSpec((b, cc, wv), lambda ci, lg_: (0, n - 1 - ci, 0)),
            state_spec,
        ],
        scratch_shapes=[pltpu.VMEM((b, 2, RET_HEADS * RET_DK, RET_DV), F32)],
    )
    return pl.pallas_call(
        kern,
        grid_spec=grid_spec,
        out_shape=[
            jax.ShapeDtypeStruct((b, t, wv), F32),
            jax.ShapeDtypeStruct((b, t, wv), F32),
            jax.ShapeDtypeStruct((b, 2, RET_HEADS * RET_DK, RET_DV), F32),
        ],
        compiler_params=_cparams("arbitrary"),
        name="retention",
    )(lg, p, p, p, p, p, p, p, p, cos, cos, sin, sin, gain, s0)


def _na_tables(rows):
    groups = rows // NA_QROWS
    cols = np.arange(GRID_W)
    c0 = np.clip(cols - NA_KW // 2, 0, GRID_W - NA_KW)
    col_ok = (cols[None, :] >= c0[:, None]) & (cols[None, :] < c0[:, None] + NA_KW)
    dc = cols[None, :] - cols[:, None] + (NA_KW - 1)
    onehot = (dc[None] == np.arange(2 * NA_KW - 1)[:, None, None]) & col_ok[None]
    row_bias = np.full((3, NA_QROWS, NA_KROWS), 2 * NA_KH - 1, np.int64)
    for ti, g in enumerate((0, 1, groups - 1)):
        ws = int(np.clip(NA_QROWS * g - NA_KH // 2, 0, rows - NA_KROWS))
        for lr in range(NA_QROWS):
            r = NA_QROWS * g + lr
            r0 = int(np.clip(r - NA_KH // 2, 0, rows - NA_KH))
            for kr in range(NA_KROWS):
                if r0 <= ws + kr < r0 + NA_KH:
                    row_bias[ti, lr, kr] = ws + kr - r + (NA_KH - 1)
    return onehot, col_ok, row_bias


def _na_bias_tables(na_rpb, rows):
    onehot, col_ok, row_bias = _na_tables(rows)
    depth, heads = na_rpb.shape[:2]
    nd = 2 * NA_KH
    toep = jnp.einsum('lhdj,jck->lhdck', na_rpb, onehot.astype(np.float32), precision=lax.Precision.HIGHEST)
    toep = jnp.where(col_ok, toep, MASK_VALUE)
    masked = jnp.full(toep.shape[:2] + (1, GRID_W, GRID_W), MASK_VALUE, F32)
    toep = jnp.concatenate([toep, masked], axis=2)
    sides = jnp.concatenate([jnp.pad(toep, ((0, 0),) * 4 + ((0, GRID_W),)),
                             jnp.pad(toep, ((0, 0),) * 4 + ((GRID_W, 0),))], axis=2)
    pairs = row_bias.reshape(-1, 2)
    pick = ((pairs[:, :1] == np.arange(nd)[None, :]).astype(np.float32),
            (pairs[:, 1:] == np.arange(nd)[None, :]).astype(np.float32))
    pick = np.concatenate(pick, axis=1)
    tab = jnp.einsum('nd,lhdck->lhnck', pick, sides, precision=lax.Precision.HIGHEST)
    return tab.reshape(depth, heads, 3, NA_QROWS * NA_KROWS // 2, GRID_W, 2 * GRID_W)


def _na_kernel(q_ref, k0_ref, k1_ref, k2_ref, v0_ref, v1_ref, v2_ref, kz_ref, vz_ref, bias_ref, o_ref):
    lanes = 2 * NA_DH
    npair = NA_KROWS // 2
    low = lax.broadcasted_iota(jnp.int32, (q_ref.shape[1], lanes), 1) < NA_DH
    for pr in range(NA_HEADS // 2):
        sl = slice(pr * lanes, (pr + 1) * lanes)
        qp = q_ref[0, :, sl] * (NA_DH ** -0.5)
        qm = (jnp.where(low, qp, 0.0).astype(BF16), jnp.where(low, 0.0, qp).astype(BF16))
        ks = [r[0, :, sl].astype(BF16) for r in (k0_ref, k1_ref, k2_ref, kz_ref)]
        vs = [r[0, :, sl].astype(BF16) for r in (v0_ref, v1_ref, v2_ref, vz_ref)]
        outs = []
        for hh in range(2):
            h = 2 * pr + hh
            cols = []
            for j in range(4):
                sj = lax.dot_general(qm[hh], ks[j], (((1,), (1,)), ((), ())), preferred_element_type=F32)
                halves = [sj[:, i * lanes:(i + 1) * lanes] for i in range(sj.shape[1] // lanes)]
                if j < 3:
                    halves = [hv + jnp.concatenate([bias_ref[0, h, 0, lr * npair + j * len(halves) + i]
                                                    for lr in range(NA_QROWS)], axis=0)
                              for i, hv in enumerate(halves)]
                cols += halves
            m = jnp.max(functools.reduce(jnp.maximum, cols), axis=-1, keepdims=True)
            ps = [jnp.exp(cj - m) for cj in cols]
            l = jnp.sum(functools.reduce(jnp.add, ps), axis=-1, keepdims=True)
            o = None
            per = len(cols) // 4
            for j in range(4):
                pj = jnp.concatenate([pc.astype(BF16) for pc in ps[j * per:(j + 1) * per]], axis=1)
                oj = jnp.dot(pj, vs[j], preferred_element_type=F32)
                o = oj if o is None else o + oj
            outs.append(o / l)
        o_ref[0, :, sl] = jnp.where(low, outs[0], outs[1]).astype(o_ref.dtype)


def _na(px, pz, bias, layer):
    b, t, _ = px.shape
    lz = pz.shape[1]
    groups = t // NA_QB
    w = NA_HEADS * NA_DH
    kblk = NA_KB // 3
    assert lz == kblk and groups >= 3

    def kspec(col, off):
        return pl.BlockSpec((1, kblk, w), lambda bi, g: (bi, jnp.clip(g - 1, 0, groups - 3) + off, col // w))

    def tab(g):
        return jnp.where(g == 0, 0, jnp.where(g == groups - 1, 2, 1))

    return pl.pallas_call(
        _na_kernel,
        grid=(b, groups),
        in_specs=[
            pl.BlockSpec((1, NA_QB, w), lambda bi, g: (bi, g, C_NA_Q // w)),
            kspec(C_NA_K, 0), kspec(C_NA_K, 1), kspec(C_NA_K, 2),
            kspec(C_NA_V, 0), kspec(C_NA_V, 1), kspec(C_NA_V, 2),
            pl.BlockSpec((1, lz, w), lambda bi, g: (bi, 0, C_NA_K // w)),
            pl.BlockSpec((1, lz, w), lambda bi, g: (bi, 0, C_NA_V // w)),
            pl.BlockSpec((1, NA_HEADS, 1) + bias.shape[3:], lambda bi, g: (layer, 0, tab(g), 0, 0, 0)),
        ],
        out_specs=pl.BlockSpec((1, NA_QB, w), lambda bi, g: (bi, g, 0)),
        out_shape=jax.ShapeDtypeStruct((b, t, w), BF16),
        compiler_params=_cparams("parallel", "arbitrary"),
        name="na",
    )(px, px, px, px, px, px, px, pz, pz, bias)


def _rms(x, gain):
    return x * lax.rsqrt(jnp.mean(x * x, axis=-1, keepdims=True) + EPS) * gain


def _mlaprep_kernel(xq_ref, xkv_ref, xkr_ref, zq_ref, zkv_ref, zkr_ref, qn_ref, kvn_ref, wq_ref, wqs_ref, wk_ref,
                    wv_ref, cos_ref, sin_ref, q_ref, k_ref, v_ref, *, nx):
    latent = pl.program_id(1) < nx
    pq = jnp.where(latent, xq_ref[0], zq_ref[0])
    pkv = jnp.where(latent, xkv_ref[0], zkv_ref[0])
    pkr = jnp.where(latent, xkr_ref[0], zkr_ref[0])
    hq = _rms(pq, qn_ref[...]).astype(BF16)
    hkv = _rms(pkv, kvn_ref[...]).astype(BF16)
    q = jnp.dot(hq, wq_ref[...], preferred_element_type=F32)
    q_swapped = jnp.dot(hq, wqs_ref[...], preferred_element_type=F32)
    k = jnp.dot(hkv, wk_ref[...], preferred_element_type=F32)
    v = jnp.dot(hkv, wv_ref[...], preferred_element_type=F32)
    kr = pltpu.roll(pkr, MLA_NOPE, 1)
    k = k + jnp.concatenate([kr] * MLA_HEADS, axis=1)
    k_swapped = jnp.concatenate([_swap_halves(kr, MLA_ROPE // 2)] * MLA_HEADS, axis=1)
    cos = jnp.concatenate([cos_ref[...]] * MLA_HEADS, axis=1)
    sin = jnp.concatenate([sin_ref[...]] * MLA_HEADS, axis=1)
    q = q * cos + q_swapped * sin
    k = k * cos + k_swapped * sin
    q_ref[0] = (q * ((MLA_NOPE + MLA_ROPE) ** -0.5 * LOG2_E)).astype(BF16)
    k_ref[0] = k.astype(BF16)
    v_ref[0] = v.astype(BF16)


def _mlaprep(px, pz, qn, kvn, wq, wqs, wk, wv, cos, sin):
    b, t, _ = px.shape
    lz = pz.shape[1]
    tm = lz
    nx = t // tm
    wqk = MLA_HEADS * MLA_DK_PAD
    wvv = MLA_HEADS * MLA_DV

    def xspec(width, col):
        return pl.BlockSpec((1, tm, width), lambda bi, i: (bi, jnp.minimum(i, nx - 1), col // width))

    def zspec(width, col):
        return pl.BlockSpec((1, tm, width), lambda bi, i: (bi, 0, col // width))

    out_spec = pl.BlockSpec((1, tm, wqk), lambda bi, i: (bi, i, 0))
    kern = functools.partial(_mlaprep_kernel, nx=nx)
    return pl.pallas_call(
        kern,
        grid=(b, nx + 1),
        in_specs=[
            xspec(MLA_RANK, C_MLA_Q), xspec(MLA_RANK, C_MLA_KV), xspec(128, C_MLA_KR),
            zspec(MLA_RANK, C_MLA_Q), zspec(MLA_RANK, C_MLA_KV), zspec(128, C_MLA_KR),
            _const_spec((1, MLA_RANK)), _const_spec((1, MLA_RANK)),
            _const_spec((MLA_RANK, wqk)), _const_spec((MLA_RANK, wqk)), _const_spec((MLA_RANK, wqk)),
            _const_spec((MLA_RANK, wvv)),
            pl.BlockSpec((tm, 128), lambda bi, i: (i, 0)),
            pl.BlockSpec((tm, 128), lambda bi, i: (i, 0)),
        ],
        out_specs=[out_spec, out_spec, pl.BlockSpec((1, tm, wvv), lambda bi, i: (bi, i, 0))],
        out_shape=[jax.ShapeDtypeStruct((b, t + lz, wqk), BF16)] * 2
        + [jax.ShapeDtypeStruct((b, t + lz, wvv), BF16)],
        compiler_params=_cparams("parallel", "parallel"),
        name="mlaprep",
    )(px, px, px, pz, pz, pz, qn, kvn, wq, wqs, wk, wv, cos, sin)


def _flash_kernel(q_ref, k_ref, v_ref, o_ref, m_scr, l_scr, acc_scr, *, heads, dk, dv, scale, nk):
    ki = pl.program_id(2)
    tq = q_ref.shape[1]
    tk = k_ref.shape[1]
    lanes = 2 * dv

    @pl.when(ki == 0)
    def _():
        m_scr[...] = jnp.full(m_scr.shape, -jnp.inf, F32)
        l_scr[...] = jnp.zeros(l_scr.shape, F32)
        acc_scr[...] = jnp.zeros(acc_scr.shape, F32)

    low = lax.broadcasted_iota(jnp.int32, (tq, lanes), 1) < dv
    for pr in range(heads // 2):
        vp = v_ref[0, :, pr * lanes:(pr + 1) * lanes].astype(BF16)
        alphas, pvs = [], []
        for h in (2 * pr, 2 * pr + 1):
            qh = q_ref[0, :, h * dk:(h + 1) * dk]
            if scale != 1.0:
                qh = qh * scale
            kh = k_ref[0, :, h * dk:(h + 1) * dk]
            s = lax.dot_general(qh.astype(BF16), kh.astype(BF16), (((1,), (1,)), ((), ())),
                                preferred_element_type=F32)
            cols = [s[:, j * lanes:(j + 1) * lanes] for j in range(tk // lanes)]
            m_prev = m_scr[h]
            m_tile = jnp.max(functools.reduce(jnp.maximum, cols), axis=-1, keepdims=True)
            m_new = jnp.maximum(m_prev, m_tile)
            alpha = jnp.exp2(m_prev - m_new)
            ps = [jnp.exp2(cj - m_new) for cj in cols]
            l_scr[h] = alpha * l_scr[h] + functools.reduce(jnp.add, ps)
            m_scr[h] = m_new
            p = jnp.concatenate([pj.astype(BF16) for pj in ps], axis=1)
            pvs.append(jnp.dot(p, vp, preferred_element_type=F32))
            alphas.append(alpha)
        acc_scr[pr] = acc_scr[pr] * jnp.where(low, alphas[0], alphas[1]) + jnp.where(low, pvs[0], pvs[1])

    @pl.when(ki == nk - 1)
    def _():
        for pr in range(heads // 2):
            l0 = jnp.sum(l_scr[2 * pr], axis=-1, keepdims=True)
            l1 = jnp.sum(l_scr[2 * pr + 1], axis=-1, keepdims=True)
            o_ref[0, :, pr * lanes:(pr + 1) * lanes] = (acc_scr[pr] / jnp.where(low, l0, l1)).astype(o_ref.dtype)


def _flash(q, k, v, qcol, kcol, vcol, heads, dk, dv, scale, tq_pref, tk_pref):
    b, tq_all, _ = q.shape
    tk_all = k.shape[1]
    tq = _tile(tq_all, tq_pref)
    tk = _tile(tk_all, tk_pref)
    nk = tk_all // tk
    assert heads % 2 == 0 and 2 * dv == 128 and tk % 128 == 0
    kern = functools.partial(_flash_kernel, heads=heads, dk=dk, dv=dv, scale=scale, nk=nk)
    return pl.pallas_call(
        kern,
        grid=(b, tq_all // tq, nk),
        in_specs=[
            pl.BlockSpec((1, tq, heads * dk), lambda bi, i, j: (bi, i, qcol)),
            pl.BlockSpec((1, tk, heads * dk), lambda bi, i, j: (bi, j, kcol)),
            pl.BlockSpec((1, tk, heads * dv), lambda bi, i, j: (bi, j, vcol)),
        ],
        out_specs=pl.BlockSpec((1, tq, heads * dv), lambda bi, i, j: (bi, i, 0)),
        out_shape=jax.ShapeDtypeStruct((b, tq_all, heads * dv), BF16),
        scratch_shapes=[
            pltpu.VMEM((heads, tq, 2 * dv), F32),
            pltpu.VMEM((heads, tq, 2 * dv), F32),
            pltpu.VMEM((heads // 2, tq, 2 * dv), F32),
        ],
        compiler_params=_cparams("parallel", "parallel", "arbitrary"),
        name="flash",
    )(q, k, v)


def _mla_attn_kernel(q_ref, k_ref, v_ref, o_ref, m_scr, l_scr, acc_scr, *, heads, nk):
    ki = pl.program_id(2)
    w = MLA_DK_PAD
    tq = q_ref.shape[1]

    @pl.when(ki == 0)
    def _():
        m_scr[...] = jnp.full(m_scr.shape, -jnp.inf, F32)
        l_scr[...] = jnp.zeros(l_scr.shape, F32)
        acc_scr[...] = jnp.zeros(acc_scr.shape, F32)

    low = lax.broadcasted_iota(jnp.int32, (tq, w), 1) < MLA_DV
    for pr in range(heads // 2):
        vp = v_ref[0, :, pr * w:(pr + 1) * w]
        alphas, pvs = [], []
        for h in (2 * pr, 2 * pr + 1):
            sl = slice(h * w, (h + 1) * w)
            s = lax.dot_general(q_ref[0, :, sl], k_ref[0, :, sl], (((1,), (1,)), ((), ())),
                                preferred_element_type=F32)
            cols = [s[:, j * w:(j + 1) * w] for j in range(s.shape[1] // w)]
            m_prev = m_scr[h]
            m_new = jnp.maximum(m_prev, jnp.max(functools.reduce(jnp.maximum, cols), axis=-1, keepdims=True))
            alpha = jnp.exp2(m_prev - m_new)
            ps = [jnp.exp2(cj - m_new) for cj in cols]
            l_scr[h] = alpha * l_scr[h] + functools.reduce(jnp.add, ps)
            m_scr[h] = m_new
            p = jnp.concatenate([pj.astype(BF16) for pj in ps], axis=1)
            pvs.append(jnp.dot(p, vp, preferred_element_type=F32))
            alphas.append(alpha)
        acc_scr[pr] = acc_scr[pr] * jnp.where(low, alphas[0], alphas[1]) + jnp.where(low, pvs[0], pvs[1])

    @pl.when(ki == nk - 1)
    def _():
        for pr in range(heads // 2):
            l0 = jnp.sum(l_scr[2 * pr], axis=-1, keepdims=True)
            l1 = jnp.sum(l_scr[2 * pr + 1], axis=-1, keepdims=True)
            o_ref[0, :, pr * w:(pr + 1) * w] = (acc_scr[pr] / jnp.where(low, l0, l1)).astype(o_ref.dtype)


def _mla_attn(q, k, v, q_start, q_len, k_start, k_len, tq_pref, tk_pref):
    b, _, wq = q.shape
    heads = wq // MLA_DK_PAD
    tq = _tile(q_len, tq_pref, 8)
    tk = _tile(k_len, tk_pref, 128)
    assert q_start % tq == 0 and k_start % tk == 0
    q_off, k_off = q_start // tq, k_start // tk
    nk = k_len // tk
    kern = functools.partial(_mla_attn_kernel, heads=heads, nk=nk)
    return pl.pallas_call(
        kern,
        grid=(b, q_len // tq, nk),
        in_specs=[
            pl.BlockSpec((1, tq, wq), lambda bi, i, j: (bi, q_off + i, 0)),
            pl.BlockSpec((1, tk, wq), lambda bi, i, j: (bi, k_off + j, 0)),
            pl.BlockSpec((1, tk, heads * MLA_DV), lambda bi, i, j: (bi, k_off + j, 0)),
        ],
        out_specs=pl.BlockSpec((1, tq, heads * MLA_DV), lambda bi, i, j: (bi, i, 0)),
        out_shape=jax.ShapeDtypeStruct((b, q_len, heads * MLA_DV), BF16),
        scratch_shapes=[
            pltpu.VMEM((heads, tq, MLA_DK_PAD), F32),
            pltpu.VMEM((heads, tq, MLA_DK_PAD), F32),
            pltpu.VMEM((heads // 2, tq, MLA_DK_PAD), F32),
        ],
        compiler_params=_cparams("parallel", "parallel", "arbitrary"),
        name="mla_attn",
    )(q, k, v)


def _layer_norm(r, gain, bias):
    mu = jnp.mean(r, axis=-1, keepdims=True)
    rc = r - mu
    var = jnp.mean(rc * rc, axis=-1, keepdims=True)
    return rc * lax.rsqrt(var + EPS) * gain + bias


def _merge_kernel(yaf_ref, yab_ref, yb_ref, yc_ref, ga_ref, gb_ref, gc_ref, x_ref, g1_ref,
                  wa_ref, wb_ref, wc_ref, wo_ref, lng_ref, lnb_ref, o_ref):
    ya = (yaf_ref[0] + yab_ref[0]).astype(BF16)
    y = (jax.nn.sigmoid(ga_ref[0]) * jnp.dot(ya, wa_ref[...], preferred_element_type=F32)
         + jax.nn.sigmoid(gb_ref[0]) * jnp.dot(yb_ref[0].astype(BF16), wb_ref[...], preferred_element_type=F32)
         + jax.nn.sigmoid(gc_ref[0]) * jnp.dot(yc_ref[0].astype(BF16), wc_ref[...], preferred_element_type=F32))
    mix = jnp.dot(y.astype(BF16), wo_ref[...], preferred_element_type=F32)
    r = DEEPNORM_ALPHA * x_ref[0] + g1_ref[0] * mix
    o_ref[0] = _layer_norm(r, lng_ref[...], lnb_ref[...])


def _merge(yaf, yab, yb, yc, p, x, g1, wa, wb, wc, wo, lng, lnb):
    b, t, d = x.shape
    tm = _tile(t, 512)
    wbr = 512
    return pl.pallas_call(
        _merge_kernel,
        grid=(b, t // tm),
        in_specs=[
            pl.BlockSpec((1, tm, wbr), lambda bi, i: (bi, i, 0)),
            pl.BlockSpec((1, tm, wbr), lambda bi, i: (bi, i, 0)),
            pl.BlockSpec((1, tm, wbr), lambda bi, i: (bi, i, 0)),
            pl.BlockSpec((1, tm, wbr), lambda bi, i: (bi, i, 0)),
            pl.BlockSpec((1, tm, d), lambda bi, i: (bi, i, C_GATE_A // d)),
            pl.BlockSpec((1, tm, d), lambda bi, i: (bi, i, C_GATE_B // d)),
            pl.BlockSpec((1, tm, d), lambda bi, i: (bi, i, C_GATE_C // d)),
            pl.BlockSpec((1, tm, d), lambda bi, i: (bi, i, 0)),
            pl.BlockSpec((1, 1, d), lambda bi, i: (bi, 0, 0)),
            _const_spec((wbr, d)), _const_spec((wbr, d)), _const_spec((wbr, d)), _const_spec((d, d)),
            _const_spec((1, d)), _const_spec((1, d)),
        ],
        out_specs=pl.BlockSpec((1, tm, d), lambda bi, i: (bi, i, 0)),
        out_shape=jax.ShapeDtypeStruct((b, t, d), F32),
        compiler_params=_cparams("parallel", "parallel"),
        name="merge",
    )(yaf, yab, yb, yc, p, p, p, x, g1, wa, wb, wc, wo, lng, lnb)


def _ffn_kernel(x_ref, sc_ref, sh_ref, g2_ref, w1_ref, w2_ref, lng_ref, lnb_ref, o_ref, *, ff_chunk):
    x = x_ref[0]
    h = (x * (1.0 + sc_ref[0]) + sh_ref[0]).astype(BF16)
    acc = jnp.zeros(x.shape, F32)
    for j in range(D_FF // ff_chunk):
        u = jnp.dot(h, w1_ref[:, j * ff_chunk:(j + 1) * ff_chunk], preferred_element_type=F32)
        u = jnp.square(jnp.maximum(u, 0.0)).astype(BF16)
        acc = acc + jnp.dot(u, w2_ref[j * ff_chunk:(j + 1) * ff_chunk, :], preferred_element_type=F32)
    r = DEEPNORM_ALPHA * x + g2_ref[0] * acc
    o_ref[0] = _layer_norm(r, lng_ref[...], lnb_ref[...])


def _ffn(x, sc, sh, g2, w1, w2, lng, lnb):
    b, t, d = x.shape
    tm = _tile(t, 512)
    kern = functools.partial(_ffn_kernel, ff_chunk=1024)
    return pl.pallas_call(
        kern,
        grid=(b, t // tm),
        in_specs=[
            pl.BlockSpec((1, tm, d), lambda bi, i: (bi, i, 0)),
            pl.BlockSpec((1, 1, d), lambda bi, i: (bi, 0, 0)),
            pl.BlockSpec((1, 1, d), lambda bi, i: (bi, 0, 0)),
            pl.BlockSpec((1, 1, d), lambda bi, i: (bi, 0, 0)),
            _const_spec((d, D_FF)), _const_spec((D_FF, d)),
            _const_spec((1, d)), _const_spec((1, d)),
        ],
        out_specs=pl.BlockSpec((1, tm, d), lambda bi, i: (bi, i, 0)),
        out_shape=jax.ShapeDtypeStruct((b, t, d), F32),
        compiler_params=_cparams("parallel", "parallel"),
        name="ffn",
    )(x, sc, sh, g2, w1, w2, lng, lnb)


def _rope_tables(n_tok, rot_dim):
    t = jnp.arange(n_tok)
    row = (t // GRID_W).astype(F32)
    col = (t % GRID_W).astype(F32)
    n_freq = rot_dim // 4
    inv_freq = ROPE_BASE ** (-2.0 * jnp.arange(n_freq, dtype=F32) / (rot_dim // 2))
    ang = jnp.concatenate([row[:, None] * inv_freq, col[:, None] * inv_freq], axis=-1)
    return jnp.cos(ang), jnp.sin(ang)


def _ret_rope_tables(n_tok):
    cos, sin = _rope_tables(n_tok, RET_DK)
    cos_h = jnp.concatenate([cos, cos], axis=1)
    sin_h = jnp.concatenate([-sin, sin], axis=1)
    return jnp.tile(cos_h, (1, 2)), jnp.tile(sin_h, (1, 2))


def _mla_rope_tables(n_tok):
    cos, sin = _rope_tables(n_tok, MLA_ROPE)
    ones = jnp.ones((n_tok, MLA_NOPE), F32)
    zeros = jnp.zeros((n_tok, MLA_NOPE), F32)
    pad = jnp.zeros((n_tok, MLA_DK_PAD - MLA_NOPE - MLA_ROPE), F32)
    cos_h = jnp.concatenate([ones, cos, cos, pad], axis=1)
    sin_h = jnp.concatenate([zeros, -sin, sin, pad], axis=1)
    return cos_h, sin_h


def _pack_w_in(w):
    d = w.shape[0]
    return jnp.concatenate([w[:, :4096], w[:, 4128:7200], w[:, 4096:4128],
                            jnp.zeros((d, P_WIDTH - 7200), w.dtype)], axis=1).astype(BF16)


def _pack_mla_weights(w_qup, w_kvup):
    r = w_qup.shape[0]
    wq = w_qup.reshape(r, MLA_HEADS, MLA_NOPE + MLA_ROPE)
    half = MLA_ROPE // 2
    wqs = jnp.concatenate([jnp.zeros((r, MLA_HEADS, MLA_NOPE), wq.dtype), wq[:, :, MLA_NOPE + half:],
                           wq[:, :, MLA_NOPE:MLA_NOPE + half]], axis=2)
    pad_q = ((0, 0), (0, 0), (0, MLA_DK_PAD - MLA_NOPE - MLA_ROPE))
    wq = jnp.pad(wq, pad_q).reshape(r, MLA_HEADS * MLA_DK_PAD)
    wqs = jnp.pad(wqs, pad_q).reshape(r, MLA_HEADS * MLA_DK_PAD)
    wkv = w_kvup.reshape(r, MLA_HEADS, MLA_NOPE + MLA_DV)
    wk = jnp.pad(wkv[:, :, :MLA_NOPE], ((0, 0), (0, 0), (0, MLA_DK_PAD - MLA_NOPE))).reshape(r, MLA_HEADS * MLA_DK_PAD)
    wv = wkv[:, :, MLA_NOPE:].reshape(r, MLA_HEADS * MLA_DV)
    return wq.astype(BF16), wqs.astype(BF16), wk.astype(BF16), wv.astype(BF16)


def kernel(x, c, ctx, c_ctx, w_ada, b_ada, w_in, ret_log_decay, ret_gn_gain, na_rpb, mla_q_norm, mla_w_qup,
           mla_kv_norm, mla_w_kvup, w_branch_ret, w_branch_na, w_branch_mla, w_out, w_ff1, w_ff2, ln_gain, ln_bias):
    depth = w_ada.shape[0]
    b, t, d = x.shape
    lz = ctx.shape[1]
    rows = t // GRID_W

    cc = jnp.zeros((8, d), F32).at[:b].set(c).at[b].set(c_ctx)
    mod = _ada(cc, w_ada, b_ada)

    cos_r, sin_r = _ret_rope_tables(t)
    cos_m, sin_m = _mla_rope_tables(t)
    cos_m = jnp.concatenate([cos_m, jnp.ones((lz, MLA_DK_PAD), F32)], axis=0)
    sin_m = jnp.concatenate([sin_m, jnp.zeros((lz, MLA_DK_PAD), F32)], axis=0)
    cos_rz, sin_rz = cos_r[:lz], sin_r[:lz]
    na_bias = _na_bias_tables(na_rpb, rows)
    s_zero = jnp.zeros((b, 2, RET_HEADS * RET_DK, RET_DV), F32)

    z = ctx
    for l in range(depth):
        need_ctx = l < depth - 1
        mx = mod[l, :b].reshape(b, 6, 1, d)
        mz = jnp.broadcast_to(mod[l, b].reshape(1, 6, 1, d), (b, 6, 1, d))
        sh1x, sc1x, g1x, sh2x, sc2x, g2x = [mx[:, i] for i in range(6)]
        sh1z, sc1z, g1z, sh2z, sc2z, g2z = [mz[:, i] for i in range(6)]

        w_in_p = _pack_w_in(w_in[l])
        px = _inproj(x, sc1x, sh1x, w_in_p)
        pz = _inproj(z, sc1z, sh1z, w_in_p)

        lg = jnp.log1p(-jnp.exp(ret_log_decay[l].astype(F32))).reshape(2 * RET_HEADS)
        gn_gain = ret_gn_gain[l].reshape(1, RET_HEADS * RET_DV)
        yaf_z, yab_z, s_ctx = _retention(pz, lg, cos_rz, sin_rz, gn_gain, s_zero, use_rope=False)
        yaf_x, yab_x, _ = _retention(px, lg, cos_r, sin_r, gn_gain, s_ctx, use_rope=True)

        yb_x = _na(px, pz, na_bias, l)

        wq, wqs, wk, wv = _pack_mla_weights(mla_w_qup[l], mla_w_kvup[l])
        qn = mla_q_norm[l].reshape(1, MLA_RANK)
        kvn = mla_kv_norm[l].reshape(1, MLA_RANK)
        q_all, k_all, v_all = _mlaprep(px, pz, qn, kvn, wq, wqs, wk, wv, cos_m, sin_m)
        yc_x = _mla_attn(q_all, k_all, v_all, 0, t, 0, t + lz, 1024, 2816)

        wa = w_branch_ret[l].astype(BF16)
        wb = w_branch_na[l].astype(BF16)
        wc = w_branch_mla[l].astype(BF16)
        wo = w_out[l].astype(BF16)
        w1 = w_ff1[l].astype(BF16)
        w2 = w_ff2[l].astype(BF16)
        lng1, lnb1 = ln_gain[l, 0].reshape(1, d), ln_bias[l, 0].reshape(1, d)
        lng2, lnb2 = ln_gain[l, 1].reshape(1, d), ln_bias[l, 1].reshape(1, d)

        x1 = _merge(yaf_x, yab_x, yb_x, yc_x, px, x, g1x, wa, wb, wc, wo, lng1, lnb1)
        x = _ffn(x1, sc2x, sh2x, g2x, w1, w2, lng2, lnb2)

        if need_ctx:
            wna = NA_HEADS * NA_DH
            yb_z = _flash(pz, pz, pz, C_NA_Q // wna, C_NA_K // wna, C_NA_V // wna,
                          NA_HEADS, NA_DH, NA_DH, NA_DH ** -0.5 * LOG2_E, 256, 256)
            yc_z = _mla_attn(q_all, k_all, v_all, t, lz, t, lz, lz, lz)
            z1 = _merge(yaf_z, yab_z, yb_z, yc_z, pz, z, g1z, wa, wb, wc, wo, lng1, lnb1)
            z = _ffn(z1, sc2z, sh2z, g2z, w1, w2, lng2, lnb2)
    return x
```

```python
import functools

import numpy as np
import jax
import jax.numpy as jnp
from jax import lax
from jax.experimental import pallas as pl
from jax.experimental.pallas import tpu as pltpu

F32 = jnp.float32
BF16 = jnp.bfloat16

D_MODEL = 1024
GRID_W = 64
RET_HEADS = 4
RET_DK = 64
RET_DV = 128
RET_CHUNK = 128
NA_HEADS = 8
NA_DH = 64
NA_KH = 8
NA_KW = 16
MLA_HEADS = 8
MLA_RANK = 256
MLA_NOPE = 64
MLA_ROPE = 32
MLA_DV = 64
MLA_DK_PAD = 128
D_FF = 4 * D_MODEL
ROPE_BASE = 10000.0
EPS = 1e-5
DEPTH_FOR_NORM = 4
DEEPNORM_ALPHA = (2 * DEPTH_FOR_NORM) ** 0.25
MASK_VALUE = -1e30
LOG2_E = 1.4426950408889634

C_RET_Q, C_RET_K, C_RET_V, C_RET_GF, C_RET_GB = 0, 256, 512, 1024, 1536
C_NA_Q, C_NA_K, C_NA_V = 2048, 2560, 3072
C_MLA_Q, C_MLA_KV = 3584, 3840
C_GATE_A, C_GATE_B, C_GATE_C = 4096, 5120, 6144
C_MLA_KR = 7168
P_WIDTH = 7296
P_COL_TILE = 2432

NA_QROWS = 4
NA_KROWS = 12
NA_QB = NA_QROWS * GRID_W
NA_KB = NA_KROWS * GRID_W

VMEM_LIMIT = 56 * 1024 * 1024


def _cparams(*sem):
    return pltpu.CompilerParams(dimension_semantics=sem, vmem_limit_bytes=VMEM_LIMIT)


def _tile(n, pref, mult=1):
    t = min(n, pref) // mult * mult
    while n % t:
        t -= mult
    return t


def _const_spec(shape):
    nd = len(shape)
    return pl.BlockSpec(shape, lambda *_: (0,) * nd)


def _layer_spec(shape, layer):
    nd = len(shape)
    return pl.BlockSpec((1,) + tuple(shape), lambda *_: (layer,) + (0,) * nd)


def _ada_kernel(c_ref, w_ref, b_ref, o_ref):
    c = c_ref[...]
    a = c * jax.nn.sigmoid(c)
    o_ref[0] = jnp.dot(a.astype(BF16), w_ref[0].astype(BF16), preferred_element_type=F32) + b_ref[0]


def _ada(cc, w_ada, b_ada):
    depth, d, n = w_ada.shape
    tn = 1024
    return pl.pallas_call(
        _ada_kernel,
        grid=(depth, n // tn),
        in_specs=[
            pl.BlockSpec((8, d), lambda l, j: (0, 0)),
            pl.BlockSpec((1, d, tn), lambda l, j: (l, 0, j)),
            pl.BlockSpec((1, 1, tn), lambda l, j: (l, 0, j)),
        ],
        out_specs=pl.BlockSpec((1, 8, tn), lambda l, j: (l, 0, j)),
        out_shape=jax.ShapeDtypeStruct((depth, 8, n), F32),
        compiler_params=_cparams("parallel", "parallel"),
        name="ada",
    )(cc, w_ada, b_ada.reshape(depth, 1, n))


def _inproj_kernel(x_ref, sc_ref, sh_ref, w_ref, o_ref):
    h = x_ref[0] * (1.0 + sc_ref[0]) + sh_ref[0]
    o_ref[0] = jnp.dot(h.astype(BF16), w_ref[0], preferred_element_type=F32)


def _inproj(x, sc, sh, w, layer):
    b, t, d = x.shape
    tm = _tile(t, 512)
    tn = P_COL_TILE
    return pl.pallas_call(
        _inproj_kernel,
        grid=(P_WIDTH // tn, b, t // tm),
        in_specs=[
            pl.BlockSpec((1, tm, d), lambda j, bi, i: (bi, i, 0)),
            pl.BlockSpec((1, 1, d), lambda j, bi, i: (bi, 0, 0)),
            pl.BlockSpec((1, 1, d), lambda j, bi, i: (bi, 0, 0)),
            pl.BlockSpec((1, d, tn), lambda j, bi, i: (layer, 0, j)),
        ],
        out_specs=pl.BlockSpec((1, tm, tn), lambda j, bi, i: (bi, i, j)),
        out_shape=jax.ShapeDtypeStruct((b, t, P_WIDTH), F32),
        compiler_params=_cparams("parallel", "parallel", "parallel"),
        name="inproj",
    )(x, sc, sh, w)


def _swap_halves(x, half):
    n = x.shape[-1]
    lane = lax.broadcasted_iota(jnp.int32, x.shape, x.ndim - 1)
    first = (lane % (2 * half)) < half
    return jnp.where(first, pltpu.roll(x, n - half, x.ndim - 1), pltpu.roll(x, half, x.ndim - 1))


def _ret_kernel(lg_ref, qf_ref, qb_ref, kf_ref, kb_ref, vf_ref, vb_ref, gf_ref, gb_ref,
                cosf_ref, cosb_ref, sinf_ref, sinb_ref, gain_ref, s0_ref,
                yf_ref, yb_ref, sf_ref, s_scr, *, use_rope, n_chunks):
    c = pl.program_id(0)
    batch = qf_ref.shape[0]
    cc = RET_CHUNK

    @pl.when(c == 0)
    def _():
        s_scr[...] = s0_ref[...]

    gain = gain_ref[...]
    row = lax.broadcasted_iota(jnp.int32, (cc, cc), 0).astype(F32)
    col = lax.broadcasted_iota(jnp.int32, (cc, cc), 1).astype(F32)
    wqk = RET_HEADS * RET_DK
    pos = lax.broadcasted_iota(jnp.int32, (cc, wqk), 0).astype(F32)
    head_of_lane = lax.broadcasted_iota(jnp.int32, (cc, wqk), 1) // RET_DK
    dirs = (
        (qf_ref, kf_ref, vf_ref, gf_ref, cosf_ref, sinf_ref, yf_ref, row - col, pos + 1.0, cc - 1.0 - pos),
        (qb_ref, kb_ref, vb_ref, gb_ref, cosb_ref, sinb_ref, yb_ref, col - row, cc - pos, pos),
    )
    for d, (q_ref, k_ref, v_ref, g_ref, cos_ref, sin_ref, y_ref, diff, q_exp, k_exp) in enumerate(dirs):
        if use_rope:
            cos = jnp.concatenate([cos_ref[...], cos_ref[...]], axis=1)
            sin = jnp.concatenate([sin_ref[...], sin_ref[...]], axis=1)
        lgs = [lg_ref[d * RET_HEADS + h] for h in range(RET_HEADS)]
        lg_lanes = jnp.full((cc, wqk), lgs[RET_HEADS - 1], F32)
        for h in range(RET_HEADS - 2, -1, -1):
            lg_lanes = jnp.where(head_of_lane == h, lgs[h], lg_lanes)
        q_decay = jnp.exp(lg_lanes * q_exp)
        k_decay = jnp.exp(lg_lanes * k_exp)
        for b in range(batch):
            q = q_ref[b]
            k = k_ref[b] * (RET_DK ** -0.5)
            if use_rope:
                q = q * cos + _swap_halves(q, RET_DK // 2) * sin
                k = k * cos + _swap_halves(k, RET_DK // 2) * sin
            k_bf = k.astype(BF16)
            kt_decayed = (k * k_decay).T.astype(BF16)
            q_decayed = q * q_decay
            state_bf = s_scr[b, d].astype(BF16)
            for h in range(RET_HEADS):
                mine = head_of_lane == h
                sl = slice(h * RET_DV, (h + 1) * RET_DV)
                rows = slice(h * RET_DK, (h + 1) * RET_DK)
                decay = jnp.where(diff >= 0, jnp.exp(lgs[h] * jnp.maximum(diff, 0.0)), 0.0)
                vh = v_ref[b, :, sl].astype(BF16)
                scores = lax.dot_general(jnp.where(mine, q, 0.0).astype(BF16), k_bf, (((1,), (1,)), ((), ())),
                                         preferred_element_type=F32) * decay
                inner = jnp.dot(scores.astype(BF16), vh, preferred_element_type=F32)
                cross = jnp.dot(jnp.where(mine, q_decayed, 0.0).astype(BF16), state_bf, preferred_element_type=F32)
                o = inner + cross
                kv = jnp.dot(kt_decayed[rows], vh, preferred_element_type=F32)
                chunk_decay = jnp.exp(lgs[h] * jnp.full((RET_DK, RET_DV), float(cc), F32))
                s_scr[b, d, rows] = s_scr[b, d, rows] * chunk_decay + kv
                mu = jnp.mean(o, axis=-1, keepdims=True)
                oc = o - mu
                var = jnp.mean(oc * oc, axis=-1, keepdims=True)
                gate = g_ref[b, :, sl]
                y_ref[b, :, sl] = (gate * jax.nn.sigmoid(gate)) * (oc * lax.rsqrt(var + EPS) * gain[:, sl])

    @pl.when(c == n_chunks - 1)
    def _():
        sf_ref[...] = s_scr[...]


def _retention(p, lg, cos, sin, gain, s0, use_rope):
    b, t, _ = p.shape
    cc = RET_CHUNK
    n = t // cc
    wv = RET_HEADS * RET_DV

    def pspec(width, col, backward):
        if backward:
            return pl.BlockSpec((b, cc, width), lambda ci, lg_: (0, n - 1 - ci, col // width))
        return pl.BlockSpec((b, cc, width), lambda ci, lg_: (0, ci, col // width))

    def tspec(backward):
        if backward:
            return pl.BlockSpec((cc, 128), lambda ci, lg_: (n - 1 - ci, 0))
        return pl.BlockSpec((cc, 128), lambda ci, lg_: (ci, 0))

    state_spec = pl.BlockSpec((b, 2, RET_HEADS * RET_DK, RET_DV), lambda ci, lg_: (0, 0, 0, 0))
    kern = functools.partial(_ret_kernel, use_rope=use_rope, n_chunks=n)
    grid_spec = pltpu.PrefetchScalarGridSpec(
        num_scalar_prefetch=1,
        grid=(n,),
        in_specs=[
            pspec(256, C_RET_Q, False), pspec(256, C_RET_Q, True),
            pspec(256, C_RET_K, False), pspec(256, C_RET_K, True),
            pspec(wv, C_RET_V, False), pspec(wv, C_RET_V, True),
            pspec(wv, C_RET_GF, False), pspec(wv, C_RET_GB, True),
            tspec(False), tspec(True), tspec(False), tspec(True),
            pl.BlockSpec((1, wv), lambda ci, lg_: (0, 0)),
            state_spec,
        ],
        out_specs=[
            pl.BlockSpec((b, cc, wv), lambda ci, lg_: (0, ci, 0)),
            pl.BlockSpec((b, cc, wv), lambda ci, lg_: (0, n - 1 - ci, 0)),
            state_spec,
        ],
        scratch_shapes=[pltpu.VMEM((b, 2, RET_HEADS * RET_DK, RET_DV), F32)],
    )
    return pl.pallas_call(
        kern,
        grid_spec=grid_spec,
        out_shape=[
            jax.ShapeDtypeStruct((b, t, wv), F32),
            jax.ShapeDtypeStruct((b, t, wv), F32),
            jax.ShapeDtypeStruct((b, 2, RET_HEADS * RET_DK, RET_DV), F32),
        ],
        compiler_params=_cparams("arbitrary"),
        name="retention",
    )(lg, p, p, p, p, p, p, p, p, cos, cos, sin, sin, gain, s0)


def _na_tables(rows):
    groups = rows // NA_QROWS
    cols = np.arange(GRID_W)
    c0 = np.clip(cols - NA_KW // 2, 0, GRID_W - NA_KW)
    col_ok = (cols[None, :] >= c0[:, None]) & (cols[None, :] < c0[:, None] + NA_KW)
    dc = cols[None, :] - cols[:, None] + (NA_KW - 1)
    onehot = (dc[None] == np.arange(2 * NA_KW - 1)[:, None, None]) & col_ok[None]
    row_bias = np.full((3, NA_QROWS, NA_KROWS), 2 * NA_KH - 1, np.int64)
    for ti, g in enumerate((0, 1, groups - 1)):
        ws = int(np.clip(NA_QROWS * g - NA_KH // 2, 0, rows - NA_KROWS))
        for lr in range(NA_QROWS):
            r = NA_QROWS * g + lr
            r0 = int(np.clip(r - NA_KH // 2, 0, rows - NA_KH))
            for kr in range(NA_KROWS):
                if r0 <= ws + kr < r0 + NA_KH:
                    row_bias[ti, lr, kr] = ws + kr - r + (NA_KH - 1)
    return onehot, col_ok, row_bias


def _na_bias_tables(na_rpb, rows):
    onehot, col_ok, row_bias = _na_tables(rows)
    depth, heads = na_rpb.shape[:2]
    nd = 2 * NA_KH
    toep = jnp.einsum('lhdj,jck->lhdck', na_rpb, onehot.astype(np.float32), precision=lax.Precision.HIGHEST)
    toep = jnp.where(col_ok, toep, MASK_VALUE)
    masked = jnp.full(toep.shape[:2] + (1, GRID_W, GRID_W), MASK_VALUE, F32)
    toep = jnp.concatenate([toep, masked], axis=2)
    sides = jnp.concatenate([jnp.pad(toep, ((0, 0),) * 4 + ((0, GRID_W),)),
                             jnp.pad(toep, ((0, 0),) * 4 + ((GRID_W, 0),))], axis=2)
    pairs = row_bias.reshape(-1, 2)
    pick = ((pairs[:, :1] == np.arange(nd)[None, :]).astype(np.float32),
            (pairs[:, 1:] == np.arange(nd)[None, :]).astype(np.float32))
    pick = np.concatenate(pick, axis=1)
    tab = jnp.einsum('nd,lhdck->lhnck', pick, sides, precision=lax.Precision.HIGHEST)
    return tab.reshape(depth, heads, 3, NA_QROWS * NA_KROWS // 2, GRID_W, 2 * GRID_W)


def _na_kernel(q_ref, k0_ref, k1_ref, k2_ref, v0_ref, v1_ref, v2_ref, kz_ref, vz_ref, bias_ref, o_ref):
    lanes = 2 * NA_DH
    npair = NA_KROWS // 2
    low = lax.broadcasted_iota(jnp.int32, (q_ref.shape[1], lanes), 1) < NA_DH
    for pr in range(NA_HEADS // 2):
        sl = slice(pr * lanes, (pr + 1) * lanes)
        qp = q_ref[0, :, sl] * (NA_DH ** -0.5)
        qm = (jnp.where(low, qp, 0.0).astype(BF16), jnp.where(low, 0.0, qp).astype(BF16))
        ks = [r[0, :, sl].astype(BF16) for r in (k0_ref, k1_ref, k2_ref, kz_ref)]
        vs = [r[0, :, sl].astype(BF16) for r in (v0_ref, v1_ref, v2_ref, vz_ref)]
        outs = []
        for hh in range(2):
            h = 2 * pr + hh
            cols = []
            for j in range(4):
                sj = lax.dot_general(qm[hh], ks[j], (((1,), (1,)), ((), ())), preferred_element_type=F32)
                halves = [sj[:, i * lanes:(i + 1) * lanes] for i in range(sj.shape[1] // lanes)]
                if j < 3:
                    halves = [hv + jnp.concatenate([bias_ref[0, h, 0, lr * npair + j * len(halves) + i]
                                                    for lr in range(NA_QROWS)], axis=0)
                              for i, hv in enumerate(halves)]
                cols += halves
            m = jnp.max(functools.reduce(jnp.maximum, cols), axis=-1, keepdims=True)
            ps = [jnp.exp(cj - m) for cj in cols]
            l = jnp.sum(functools.reduce(jnp.add, ps), axis=-1, keepdims=True)
            o = None
            per = len(cols) // 4
            for j in range(4):
                pj = jnp.concatenate([pc.astype(BF16) for pc in ps[j * per:(j + 1) * per]], axis=1)
                oj = jnp.dot(pj, vs[j], preferred_element_type=F32)
                o = oj if o is None else o + oj
            outs.append(o / l)
        o_ref[0, :, sl] = jnp.where(low, outs[0], outs[1]).astype(o_ref.dtype)


def _na(px, pz, bias, layer):
    b, t, _ = px.shape
    lz = pz.shape[1]
    groups = t // NA_QB
    w = NA_HEADS * NA_DH
    kblk = NA_KB // 3
    assert lz == kblk and groups >= 3

    def kspec(col, off):
        return pl.BlockSpec((1, kblk, w), lambda bi, g: (bi, jnp.clip(g - 1, 0, groups - 3) + off, col // w))

    def tab(g):
        return jnp.where(g == 0, 0, jnp.where(g == groups - 1, 2, 1))

    return pl.pallas_call(
        _na_kernel,
        grid=(b, groups),
        in_specs=[
            pl.BlockSpec((1, NA_QB, w), lambda bi, g: (bi, g, C_NA_Q // w)),
            kspec(C_NA_K, 0), kspec(C_NA_K, 1), kspec(C_NA_K, 2),
            kspec(C_NA_V, 0), kspec(C_NA_V, 1), kspec(C_NA_V, 2),
            pl.BlockSpec((1, lz, w), lambda bi, g: (bi, 0, C_NA_K // w)),
            pl.BlockSpec((1, lz, w), lambda bi, g: (bi, 0, C_NA_V // w)),
            pl.BlockSpec((1, NA_HEADS, 1) + bias.shape[3:], lambda bi, g: (layer, 0, tab(g), 0, 0, 0)),
        ],
        out_specs=pl.BlockSpec((1, NA_QB, w), lambda bi, g: (bi, g, 0)),
        out_shape=jax.ShapeDtypeStruct((b, t, w), BF16),
        compiler_params=_cparams("parallel", "arbitrary"),
        name="na",
    )(px, px, px, px, px, px, px, pz, pz, bias)


def _rms(x, gain):
    return x * lax.rsqrt(jnp.mean(x * x, axis=-1, keepdims=True) + EPS) * gain


def _mlaprep_kernel(xq_ref, xkv_ref, xkr_ref, zq_ref, zkv_ref, zkr_ref, qn_ref, kvn_ref, wq_ref, wqs_ref, wk_ref,
                    wv_ref, cos_ref, sin_ref, q_ref, k_ref, v_ref, *, nx):
    latent = pl.program_id(1) < nx
    pq = jnp.where(latent, xq_ref[0], zq_ref[0])
    pkv = jnp.where(latent, xkv_ref[0], zkv_ref[0])
    pkr = jnp.where(latent, xkr_ref[0], zkr_ref[0])
    hq = _rms(pq, qn_ref[...]).astype(BF16)
    hkv = _rms(pkv, kvn_ref[...]).astype(BF16)
    q = jnp.dot(hq, wq_ref[0], preferred_element_type=F32)
    q_swapped = jnp.dot(hq, wqs_ref[0], preferred_element_type=F32)
    k = jnp.dot(hkv, wk_ref[0], preferred_element_type=F32)
    v = jnp.dot(hkv, wv_ref[0], preferred_element_type=F32)
    kr = pltpu.roll(pkr, MLA_NOPE, 1)
    k = k + jnp.concatenate([kr] * MLA_HEADS, axis=1)
    k_swapped = jnp.concatenate([_swap_halves(kr, MLA_ROPE // 2)] * MLA_HEADS, axis=1)
    cos = jnp.concatenate([cos_ref[...]] * MLA_HEADS, axis=1)
    sin = jnp.concatenate([sin_ref[...]] * MLA_HEADS, axis=1)
    q = q * cos + q_swapped * sin
    k = k * cos + k_swapped * sin
    q_ref[0] = (q * ((MLA_NOPE + MLA_ROPE) ** -0.5 * LOG2_E)).astype(BF16)
    k_ref[0] = k.astype(BF16)
    v_ref[0] = v.astype(BF16)


def _mlaprep(px, pz, qn, kvn, wq, wqs, wk, wv, cos, sin, layer):
    b, t, _ = px.shape
    lz = pz.shape[1]
    tm = lz
    nx = t // tm
    wqk = MLA_HEADS * MLA_DK_PAD
    wvv = MLA_HEADS * MLA_DV

    def xspec(width, col):
        return pl.BlockSpec((1, tm, width), lambda bi, i: (bi, jnp.minimum(i, nx - 1), col // width))

    def zspec(width, col):
        return pl.BlockSpec((1, tm, width), lambda bi, i: (bi, 0, col // width))

    out_spec = pl.BlockSpec((1, tm, wqk), lambda bi, i: (bi, i, 0))
    kern = functools.partial(_mlaprep_kernel, nx=nx)
    return pl.pallas_call(
        kern,
        grid=(b, nx + 1),
        in_specs=[
            xspec(MLA_RANK, C_MLA_Q), xspec(MLA_RANK, C_MLA_KV), xspec(128, C_MLA_KR),
            zspec(MLA_RANK, C_MLA_Q), zspec(MLA_RANK, C_MLA_KV), zspec(128, C_MLA_KR),
            _const_spec((1, MLA_RANK)), _const_spec((1, MLA_RANK)),
            _layer_spec((MLA_RANK, wqk), layer), _layer_spec((MLA_RANK, wqk), layer),
            _layer_spec((MLA_RANK, wqk), layer), _layer_spec((MLA_RANK, wvv), layer),
            pl.BlockSpec((tm, 128), lambda bi, i: (i, 0)),
            pl.BlockSpec((tm, 128), lambda bi, i: (i, 0)),
        ],
        out_specs=[out_spec, out_spec, pl.BlockSpec((1, tm, wvv), lambda bi, i: (bi, i, 0))],
        out_shape=[jax.ShapeDtypeStruct((b, t + lz, wqk), BF16)] * 2
        + [jax.ShapeDtypeStruct((b, t + lz, wvv), BF16)],
        compiler_params=_cparams("parallel", "parallel"),
        name="mlaprep",
    )(px, px, px, pz, pz, pz, qn, kvn, wq, wqs, wk, wv, cos, sin)


def _flash_kernel(q_ref, k_ref, v_ref, o_ref, m_scr, l_scr, acc_scr, *, heads, dk, dv, scale, nk):
    ki = pl.program_id(2)
    tq = q_ref.shape[1]
    tk = k_ref.shape[1]
    lanes = 2 * dv

    @pl.when(ki == 0)
    def _():
        m_scr[...] = jnp.full(m_scr.shape, -jnp.inf, F32)
        l_scr[...] = jnp.zeros(l_scr.shape, F32)
        acc_scr[...] = jnp.zeros(acc_scr.shape, F32)

    low = lax.broadcasted_iota(jnp.int32, (tq, lanes), 1) < dv
    for pr in range(heads // 2):
        vp = v_ref[0, :, pr * lanes:(pr + 1) * lanes].astype(BF16)
        alphas, pvs = [], []
        for h in (2 * pr, 2 * pr + 1):
            qh = q_ref[0, :, h * dk:(h + 1) * dk]
            if scale != 1.0:
                qh = qh * scale
            kh = k_ref[0, :, h * dk:(h + 1) * dk]
            s = lax.dot_general(qh.astype(BF16), kh.astype(BF16), (((1,), (1,)), ((), ())),
                                preferred_element_type=F32)
            cols = [s[:, j * lanes:(j + 1) * lanes] for j in range(tk // lanes)]
            m_prev = m_scr[h]
            m_tile = jnp.max(functools.reduce(jnp.maximum, cols), axis=-1, keepdims=True)
            m_new = jnp.maximum(m_prev, m_tile)
            alpha = jnp.exp2(m_prev - m_new)
            ps = [jnp.exp2(cj - m_new) for cj in cols]
            l_scr[h] = alpha * l_scr[h] + functools.reduce(jnp.add, ps)
            m_scr[h] = m_new
            p = jnp.concatenate([pj.astype(BF16) for pj in ps], axis=1)
            pvs.append(jnp.dot(p, vp, preferred_element_type=F32))
            alphas.append(alpha)
        acc_scr[pr] = acc_scr[pr] * jnp.where(low, alphas[0], alphas[1]) + jnp.where(low, pvs[0], pvs[1])

    @pl.when(ki == nk - 1)
    def _():
        for pr in range(heads // 2):
            l0 = jnp.sum(l_scr[2 * pr], axis=-1, keepdims=True)
            l1 = jnp.sum(l_scr[2 * pr + 1], axis=-1, keepdims=True)
            o_ref[0, :, pr * lanes:(pr + 1) * lanes] = (acc_scr[pr] / jnp.where(low, l0, l1)).astype(o_ref.dtype)


def _flash(q, k, v, qcol, kcol, vcol, heads, dk, dv, scale, tq_pref, tk_pref):
    b, tq_all, _ = q.shape
    tk_all = k.shape[1]
    tq = _tile(tq_all, tq_pref)
    tk = _tile(tk_all, tk_pref)
    nk = tk_all // tk
    assert heads % 2 == 0 and 2 * dv == 128 and tk % 128 == 0
    kern = functools.partial(_flash_kernel, heads=heads, dk=dk, dv=dv, scale=scale, nk=nk)
    return pl.pallas_call(
        kern,
        grid=(b, tq_all // tq, nk),
        in_specs=[
            pl.BlockSpec((1, tq, heads * dk), lambda bi, i, j: (bi, i, qcol)),
            pl.BlockSpec((1, tk, heads * dk), lambda bi, i, j: (bi, j, kcol)),
            pl.BlockSpec((1, tk, heads * dv), lambda bi, i, j: (bi, j, vcol)),
        ],
        out_specs=pl.BlockSpec((1, tq, heads * dv), lambda bi, i, j: (bi, i, 0)),
        out_shape=jax.ShapeDtypeStruct((b, tq_all, heads * dv), BF16),
        scratch_shapes=[
            pltpu.VMEM((heads, tq, 2 * dv), F32),
            pltpu.VMEM((heads, tq, 2 * dv), F32),
            pltpu.VMEM((heads // 2, tq, 2 * dv), F32),
        ],
        compiler_params=_cparams("parallel", "parallel", "arbitrary"),
        name="flash",
    )(q, k, v)


def _mla_attn_kernel(q_ref, k_ref, v_ref, o_ref, m_scr, l_scr, acc_scr, *, heads, nk):
    ki = pl.program_id(2)
    w = MLA_DK_PAD
    tq = q_ref.shape[1]

    @pl.when(ki == 0)
    def _():
        m_scr[...] = jnp.full(m_scr.shape, -jnp.inf, F32)
        l_scr[...] = jnp.zeros(l_scr.shape, F32)
        acc_scr[...] = jnp.zeros(acc_scr.shape, F32)

    low = lax.broadcasted_iota(jnp.int32, (tq, w), 1) < MLA_DV
    for pr in range(heads // 2):
        vp = v_ref[0, :, pr * w:(pr + 1) * w]
        alphas, pvs = [], []
        for h in (2 * pr, 2 * pr + 1):
            sl = slice(h * w, (h + 1) * w)
            s = lax.dot_general(q_ref[0, :, sl], k_ref[0, :, sl], (((1,), (1,)), ((), ())),
                                preferred_element_type=F32)
            cols = [s[:, j * w:(j + 1) * w] for j in range(s.shape[1] // w)]
            m_prev = m_scr[h]
            m_new = jnp.maximum(m_prev, jnp.max(functools.reduce(jnp.maximum, cols), axis=-1, keepdims=True))
            alpha = jnp.exp2(m_prev - m_new)
            ps = [jnp.exp2(cj - m_new) for cj in cols]
            l_scr[h] = alpha * l_scr[h] + functools.reduce(jnp.add, ps)
            m_scr[h] = m_new
            p = jnp.concatenate([pj.astype(BF16) for pj in ps], axis=1)
            pvs.append(jnp.dot(p, vp, preferred_element_type=F32))
            alphas.append(alpha)
        acc_scr[pr] = acc_scr[pr] * jnp.where(low, alphas[0], alphas[1]) + jnp.where(low, pvs[0], pvs[1])

    @pl.when(ki == nk - 1)
    def _():
        for pr in range(heads // 2):
            l0 = jnp.sum(l_scr[2 * pr], axis=-1, keepdims=True)
            l1 = jnp.sum(l_scr[2 * pr + 1], axis=-1, keepdims=True)
            o_ref[0, :, pr * w:(pr + 1) * w] = (acc_scr[pr] / jnp.where(low, l0, l1)).astype(o_ref.dtype)


def _mla_attn(q, k, v, q_start, q_len, k_start, k_len, tq_pref, tk_pref):
    b, _, wq = q.shape
    heads = wq // MLA_DK_PAD
    tq = _tile(q_len, tq_pref, 8)
    tk = _tile(k_len, tk_pref, 128)
    assert q_start % tq == 0 and k_start % tk == 0
    q_off, k_off = q_start // tq, k_start // tk
    nk = k_len // tk
    kern = functools.partial(_mla_attn_kernel, heads=heads, nk=nk)
    return pl.pallas_call(
        kern,
        grid=(b, q_len // tq, nk),
        in_specs=[
            pl.BlockSpec((1, tq, wq), lambda bi, i, j: (bi, q_off + i, 0)),
            pl.BlockSpec((1, tk, wq), lambda bi, i, j: (bi, k_off + j, 0)),
            pl.BlockSpec((1, tk, heads * MLA_DV), lambda bi, i, j: (bi, k_off + j, 0)),
        ],
        out_specs=pl.BlockSpec((1, tq, heads * MLA_DV), lambda bi, i, j: (bi, i, 0)),
        out_shape=jax.ShapeDtypeStruct((b, q_len, heads * MLA_DV), BF16),
        scratch_shapes=[
            pltpu.VMEM((heads, tq, MLA_DK_PAD), F32),
            pltpu.VMEM((heads, tq, MLA_DK_PAD), F32),
            pltpu.VMEM((heads // 2, tq, MLA_DK_PAD), F32),
        ],
        compiler_params=_cparams("parallel", "parallel", "arbitrary"),
        name="mla_attn",
    )(q, k, v)


def _layer_norm(r, gain, bias):
    mu = jnp.mean(r, axis=-1, keepdims=True)
    rc = r - mu
    var = jnp.mean(rc * rc, axis=-1, keepdims=True)
    return rc * lax.rsqrt(var + EPS) * gain + bias


def _merge_kernel(yaf_ref, yab_ref, yb_ref, yc_ref, ga_ref, gb_ref, gc_ref, x_ref, g1_ref,
                  wa_ref, wb_ref, wc_ref, wo_ref, lng_ref, lnb_ref, o_ref):
    ya = (yaf_ref[0] + yab_ref[0]).astype(BF16)
    y = (jax.nn.sigmoid(ga_ref[0]) * jnp.dot(ya, wa_ref[0], preferred_element_type=F32)
         + jax.nn.sigmoid(gb_ref[0]) * jnp.dot(yb_ref[0].astype(BF16), wb_ref[0], preferred_element_type=F32)
         + jax.nn.sigmoid(gc_ref[0]) * jnp.dot(yc_ref[0].astype(BF16), wc_ref[0], preferred_element_type=F32))
    mix = jnp.dot(y.astype(BF16), wo_ref[0], preferred_element_type=F32)
    r = DEEPNORM_ALPHA * x_ref[0] + g1_ref[0] * mix
    o_ref[0] = _layer_norm(r, lng_ref[...], lnb_ref[...])


def _merge(yaf, yab, yb, yc, p, x, g1, wa, wb, wc, wo, lng, lnb, layer):
    b, t, d = x.shape
    tm = _tile(t, 512)
    wbr = 512
    return pl.pallas_call(
        _merge_kernel,
        grid=(b, t // tm),
        in_specs=[
            pl.BlockSpec((1, tm, wbr), lambda bi, i: (bi, i, 0)),
            pl.BlockSpec((1, tm, wbr), lambda bi, i: (bi, i, 0)),
            pl.BlockSpec((1, tm, wbr), lambda bi, i: (bi, i, 0)),
            pl.BlockSpec((1, tm, wbr), lambda bi, i: (bi, i, 0)),
            pl.BlockSpec((1, tm, d), lambda bi, i: (bi, i, C_GATE_A // d)),
            pl.BlockSpec((1, tm, d), lambda bi, i: (bi, i, C_GATE_B // d)),
            pl.BlockSpec((1, tm, d), lambda bi, i: (bi, i, C_GATE_C // d)),
            pl.BlockSpec((1, tm, d), lambda bi, i: (bi, i, 0)),
            pl.BlockSpec((1, 1, d), lambda bi, i: (bi, 0, 0)),
            _layer_spec((wbr, d), layer), _layer_spec((wbr, d), layer), _layer_spec((wbr, d), layer),
            _layer_spec((d, d), layer),
            _const_spec((1, d)), _const_spec((1, d)),
        ],
        out_specs=pl.BlockSpec((1, tm, d), lambda bi, i: (bi, i, 0)),
        out_shape=jax.ShapeDtypeStruct((b, t, d), F32),
        compiler_params=_cparams("parallel", "parallel"),
        name="merge",
    )(yaf, yab, yb, yc, p, p, p, x, g1, wa, wb, wc, wo, lng, lnb)


def _ffn_kernel(x_ref, sc_ref, sh_ref, g2_ref, w1_ref, w2_ref, lng_ref, lnb_ref, o_ref, *, ff_chunk):
    x = x_ref[0]
    h = (x * (1.0 + sc_ref[0]) + sh_ref[0]).astype(BF16)
    acc = jnp.zeros(x.shape, F32)
    for j in range(D_FF // ff_chunk):
        u = jnp.dot(h, w1_ref[0, :, j * ff_chunk:(j + 1) * ff_chunk], preferred_element_type=F32)
        u = jnp.square(jnp.maximum(u, 0.0)).astype(BF16)
        acc = acc + jnp.dot(u, w2_ref[0, j * ff_chunk:(j + 1) * ff_chunk, :], preferred_element_type=F32)
    r = DEEPNORM_ALPHA * x + g2_ref[0] * acc
    o_ref[0] = _layer_norm(r, lng_ref[...], lnb_ref[...])


def _ffn(x, sc, sh, g2, w1, w2, lng, lnb, layer):
    b, t, d = x.shape
    tm = _tile(t, 512)
    kern = functools.partial(_ffn_kernel, ff_chunk=1024)
    return pl.pallas_call(
        kern,
        grid=(b, t // tm),
        in_specs=[
            pl.BlockSpec((1, tm, d), lambda bi, i: (bi, i, 0)),
            pl.BlockSpec((1, 1, d), lambda bi, i: (bi, 0, 0)),
            pl.BlockSpec((1, 1, d), lambda bi, i: (bi, 0, 0)),
            pl.BlockSpec((1, 1, d), lambda bi, i: (bi, 0, 0)),
            _layer_spec((d, D_FF), layer), _layer_spec((D_FF, d), layer),
            _const_spec((1, d)), _const_spec((1, d)),
        ],
        out_specs=pl.BlockSpec((1, tm, d), lambda bi, i: (bi, i, 0)),
        out_shape=jax.ShapeDtypeStruct((b, t, d), F32),
        compiler_params=_cparams("parallel", "parallel"),
        name="ffn",
    )(x, sc, sh, g2, w1, w2, lng, lnb)


def _rope_tables(n_tok, rot_dim):
    t = jnp.arange(n_tok)
    row = (t // GRID_W).astype(F32)
    col = (t % GRID_W).astype(F32)
    n_freq = rot_dim // 4
    inv_freq = ROPE_BASE ** (-2.0 * jnp.arange(n_freq, dtype=F32) / (rot_dim // 2))
    ang = jnp.concatenate([row[:, None] * inv_freq, col[:, None] * inv_freq], axis=-1)
    return jnp.cos(ang), jnp.sin(ang)


def _ret_rope_tables(n_tok):
    cos, sin = _rope_tables(n_tok, RET_DK)
    cos_h = jnp.concatenate([cos, cos], axis=1)
    sin_h = jnp.concatenate([-sin, sin], axis=1)
    return jnp.tile(cos_h, (1, 2)), jnp.tile(sin_h, (1, 2))


def _mla_rope_tables(n_tok):
    cos, sin = _rope_tables(n_tok, MLA_ROPE)
    ones = jnp.ones((n_tok, MLA_NOPE), F32)
    zeros = jnp.zeros((n_tok, MLA_NOPE), F32)
    pad = jnp.zeros((n_tok, MLA_DK_PAD - MLA_NOPE - MLA_ROPE), F32)
    cos_h = jnp.concatenate([ones, cos, cos, pad], axis=1)
    sin_h = jnp.concatenate([zeros, -sin, sin, pad], axis=1)
    return cos_h, sin_h


def _pack_w_in(w):
    pad = jnp.zeros(w.shape[:2] + (P_WIDTH - 7200,), w.dtype)
    return jnp.concatenate([w[..., :4096], w[..., 4128:7200], w[..., 4096:4128], pad], axis=-1).astype(BF16)


def _pack_mla_weights(w_qup, w_kvup):
    depth, r = w_qup.shape[:2]
    wq = w_qup.reshape(depth, r, MLA_HEADS, MLA_NOPE + MLA_ROPE)
    half = MLA_ROPE // 2
    wqs = jnp.concatenate([jnp.zeros((depth, r, MLA_HEADS, MLA_NOPE), wq.dtype), wq[..., MLA_NOPE + half:],
                           wq[..., MLA_NOPE:MLA_NOPE + half]], axis=-1)
    pad_q = ((0, 0), (0, 0), (0, 0), (0, MLA_DK_PAD - MLA_NOPE - MLA_ROPE))
    wq = jnp.pad(wq, pad_q).reshape(depth, r, MLA_HEADS * MLA_DK_PAD)
    wqs = jnp.pad(wqs, pad_q).reshape(depth, r, MLA_HEADS * MLA_DK_PAD)
    wkv = w_kvup.reshape(depth, r, MLA_HEADS, MLA_NOPE + MLA_DV)
    wk = jnp.pad(wkv[..., :MLA_NOPE], ((0, 0), (0, 0), (0, 0), (0, MLA_DK_PAD - MLA_NOPE)))
    wk = wk.reshape(depth, r, MLA_HEADS * MLA_DK_PAD)
    wv = wkv[..., MLA_NOPE:].reshape(depth, r, MLA_HEADS * MLA_DV)
    return wq.astype(BF16), wqs.astype(BF16), wk.astype(BF16), wv.astype(BF16)


def kernel(x, c, ctx, c_ctx, w_ada, b_ada, w_in, ret_log_decay, ret_gn_gain, na_rpb, mla_q_norm, mla_w_qup,
           mla_kv_norm, mla_w_kvup, w_branch_ret, w_branch_na, w_branch_mla, w_out, w_ff1, w_ff2, ln_gain, ln_bias):
    depth = w_ada.shape[0]
    b, t, d = x.shape
    lz = ctx.shape[1]
    rows = t // GRID_W

    cc = jnp.zeros((8, d), F32).at[:b].set(c).at[b].set(c_ctx)
    mod = _ada(cc, w_ada, b_ada)

    cos_r, sin_r = _ret_rope_tables(t)
    cos_m, sin_m = _mla_rope_tables(t)
    cos_m = jnp.concatenate([cos_m, jnp.ones((lz, MLA_DK_PAD), F32)], axis=0)
    sin_m = jnp.concatenate([sin_m, jnp.zeros((lz, MLA_DK_PAD), F32)], axis=0)
    cos_rz, sin_rz = cos_r[:lz], sin_r[:lz]
    na_bias = _na_bias_tables(na_rpb, rows)
    s_zero = jnp.zeros((b, 2, RET_HEADS * RET_DK, RET_DV), F32)

    w_in_p = _pack_w_in(w_in)
    wq, wqs, wk, wv = _pack_mla_weights(mla_w_qup, mla_w_kvup)
    wa = w_branch_ret.astype(BF16)
    wb = w_branch_na.astype(BF16)
    wc = w_branch_mla.astype(BF16)
    wo = w_out.astype(BF16)
    w1 = w_ff1.astype(BF16)
    w2 = w_ff2.astype(BF16)

    z = ctx
    for l in range(depth):
        need_ctx = l < depth - 1
        mx = mod[l, :b].reshape(b, 6, 1, d)
        mz = jnp.broadcast_to(mod[l, b].reshape(1, 6, 1, d), (b, 6, 1, d))
        sh1x, sc1x, g1x, sh2x, sc2x, g2x = [mx[:, i] for i in range(6)]
        sh1z, sc1z, g1z, sh2z, sc2z, g2z = [mz[:, i] for i in range(6)]

        px = _inproj(x, sc1x, sh1x, w_in_p, l)
        pz = _inproj(z, sc1z, sh1z, w_in_p, l)

        lg = jnp.log1p(-jnp.exp(ret_log_decay[l].astype(F32))).reshape(2 * RET_HEADS)
        gn_gain = ret_gn_gain[l].reshape(1, RET_HEADS * RET_DV)
        yaf_z, yab_z, s_ctx = _retention(pz, lg, cos_rz, sin_rz, gn_gain, s_zero, use_rope=False)
        yaf_x, yab_x, _ = _retention(px, lg, cos_r, sin_r, gn_gain, s_ctx, use_rope=True)

        yb_x = _na(px, pz, na_bias, l)

        qn = mla_q_norm[l].reshape(1, MLA_RANK)
        kvn = mla_kv_norm[l].reshape(1, MLA_RANK)
        q_all, k_all, v_all = _mlaprep(px, pz, qn, kvn, wq, wqs, wk, wv, cos_m, sin_m, l)
        yc_x = _mla_attn(q_all, k_all, v_all, 0, t, 0, t + lz, 1024, 2816)

        lng1, lnb1 = ln_gain[l, 0].reshape(1, d), ln_bias[l, 0].reshape(1, d)
        lng2, lnb2 = ln_gain[l, 1].reshape(1, d), ln_bias[l, 1].reshape(1, d)

        x1 = _merge(yaf_x, yab_x, yb_x, yc_x, px, x, g1x, wa, wb, wc, wo, lng1, lnb1, l)
        x = _ffn(x1, sc2x, sh2x, g2x, w1, w2, lng2, lnb2, l)

        if need_ctx:
            wna = NA_HEADS * NA_DH
            yb_z = _flash(pz, pz, pz, C_NA_Q // wna, C_NA_K // wna, C_NA_V // wna,
                          NA_HEADS, NA_DH, NA_DH, NA_DH ** -0.5 * LOG2_E, 256, 256)
            yc_z = _mla_attn(q_all, k_all, v_all, t, lz, t, lz, lz, lz)
            z1 = _merge(yaf_z, yab_z, yb_z, yc_z, pz, z, g1z, wa, wb, wc, wo, lng1, lnb1, l)
            z = _ffn(z1, sc2z, sh2z, g2z, w1, w2, lng2, lnb2, l)
    return x
```

```python
import functools

import numpy as np
import jax
import jax.numpy as jnp
from jax import lax
from jax.experimental import pallas as pl
from jax.experimental.pallas import tpu as pltpu

F32 = jnp.float32
BF16 = jnp.bfloat16

D_MODEL = 1024
GRID_W = 64
RET_HEADS = 4
RET_DK = 64
RET_DV = 128
RET_CHUNK = 128
NA_HEADS = 8
NA_DH = 64
NA_KH = 8
NA_KW = 16
MLA_HEADS = 8
MLA_RANK = 256
MLA_NOPE = 64
MLA_ROPE = 32
MLA_DV = 64
MLA_DK_PAD = 128
D_FF = 4 * D_MODEL
ROPE_BASE = 10000.0
EPS = 1e-5
DEPTH_FOR_NORM = 4
DEEPNORM_ALPHA = (2 * DEPTH_FOR_NORM) ** 0.25
MASK_VALUE = -1e30
LOG2_E = 1.4426950408889634

C_RET_Q, C_RET_K, C_RET_V, C_RET_GF, C_RET_GB = 0, 256, 512, 1024, 1536
C_NA_Q, C_NA_K, C_NA_V = 2048, 2560, 3072
C_MLA_Q, C_MLA_KV = 3584, 3840
C_GATE_A, C_GATE_B, C_GATE_C = 4096, 5120, 6144
C_MLA_KR = 7168
P_WIDTH = 7296
P_COL_TILE = 2432

NA_QROWS = 4
NA_KROWS = 12
NA_QB = NA_QROWS * GRID_W
NA_KB = NA_KROWS * GRID_W

VMEM_LIMIT = 56 * 1024 * 1024


def _cparams(*sem):
    return pltpu.CompilerParams(dimension_semantics=sem, vmem_limit_bytes=VMEM_LIMIT)


def _tile(n, pref, mult=1):
    t = min(n, pref) // mult * mult
    while n % t:
        t -= mult
    return t


def _const_spec(shape):
    nd = len(shape)
    return pl.BlockSpec(shape, lambda *_: (0,) * nd)


def _layer_spec(shape, layer):
    nd = len(shape)
    return pl.BlockSpec((1,) + tuple(shape), lambda *_: (layer,) + (0,) * nd)


def _ada_kernel(c_ref, w_ref, b_ref, o_ref):
    c = c_ref[...]
    a = c * jax.nn.sigmoid(c)
    o_ref[0] = jnp.dot(a.astype(BF16), w_ref[0].astype(BF16), preferred_element_type=F32) + b_ref[0]


def _ada(cc, w_ada, b_ada):
    depth, d, n = w_ada.shape
    tn = _tile(n, 3072, 128)
    return pl.pallas_call(
        _ada_kernel,
        grid=(depth, n // tn),
        in_specs=[
            pl.BlockSpec((8, d), lambda l, j: (0, 0)),
            pl.BlockSpec((1, d, tn), lambda l, j: (l, 0, j)),
            pl.BlockSpec((1, 1, tn), lambda l, j: (l, 0, j)),
        ],
        out_specs=pl.BlockSpec((1, 8, tn), lambda l, j: (l, 0, j)),
        out_shape=jax.ShapeDtypeStruct((depth, 8, n), F32),
        compiler_params=_cparams("parallel", "parallel"),
        name="ada",
    )(cc, w_ada, b_ada.reshape(depth, 1, n))


def _inproj_kernel(x_ref, sc_ref, sh_ref, w_ref, o_ref):
    h = x_ref[0] * (1.0 + sc_ref[0]) + sh_ref[0]
    o_ref[0] = jnp.dot(h.astype(BF16), w_ref[0], preferred_element_type=F32)


def _inproj(x, sc, sh, w, layer):
    b, t, d = x.shape
    tm = _tile(t, 512)
    tn = P_COL_TILE
    return pl.pallas_call(
        _inproj_kernel,
        grid=(P_WIDTH // tn, b, t // tm),
        in_specs=[
            pl.BlockSpec((1, tm, d), lambda j, bi, i: (bi, i, 0)),
            pl.BlockSpec((1, 1, d), lambda j, bi, i: (bi, 0, 0)),
            pl.BlockSpec((1, 1, d), lambda j, bi, i: (bi, 0, 0)),
            pl.BlockSpec((1, d, tn), lambda j, bi, i: (layer, 0, j)),
        ],
        out_specs=pl.BlockSpec((1, tm, tn), lambda j, bi, i: (bi, i, j)),
        out_shape=jax.ShapeDtypeStruct((b, t, P_WIDTH), F32),
        compiler_params=_cparams("parallel", "parallel", "parallel"),
        name="inproj",
    )(x, sc, sh, w)


def _swap_halves(x, half):
    n = x.shape[-1]
    lane = lax.broadcasted_iota(jnp.int32, x.shape, x.ndim - 1)
    first = (lane % (2 * half)) < half
    return jnp.where(first, pltpu.roll(x, n - half, x.ndim - 1), pltpu.roll(x, half, x.ndim - 1))


def _ret_kernel(lg_ref, qf_ref, qb_ref, kf_ref, kb_ref, vf_ref, vb_ref, gf_ref, gb_ref,
                cosf_ref, cosb_ref, sinf_ref, sinb_ref, gain_ref, s0_ref,
                yf_ref, yb_ref, sf_ref, s_scr, *, use_rope, n_chunks):
    c = pl.program_id(0)
    batch = qf_ref.shape[0]
    cc = RET_CHUNK

    @pl.when(c == 0)
    def _():
        s_scr[...] = s0_ref[...]

    gain = gain_ref[...]
    row = lax.broadcasted_iota(jnp.int32, (cc, cc), 0).astype(F32)
    col = lax.broadcasted_iota(jnp.int32, (cc, cc), 1).astype(F32)
    wqk = RET_HEADS * RET_DK
    pos = lax.broadcasted_iota(jnp.int32, (cc, wqk), 0).astype(F32)
    head_of_lane = lax.broadcasted_iota(jnp.int32, (cc, wqk), 1) // RET_DK
    dirs = (
        (qf_ref, kf_ref, vf_ref, gf_ref, cosf_ref, sinf_ref, yf_ref, row - col, pos + 1.0, cc - 1.0 - pos),
        (qb_ref, kb_ref, vb_ref, gb_ref, cosb_ref, sinb_ref, yb_ref, col - row, cc - pos, pos),
    )
    for d, (q_ref, k_ref, v_ref, g_ref, cos_ref, sin_ref, y_ref, diff, q_exp, k_exp) in enumerate(dirs):
        if use_rope:
            cos = jnp.concatenate([cos_ref[...], cos_ref[...]], axis=1)
            sin = jnp.concatenate([sin_ref[...], sin_ref[...]], axis=1)
        lgs = [lg_ref[d * RET_HEADS + h] for h in range(RET_HEADS)]
        lg_lanes = jnp.full((cc, wqk), lgs[RET_HEADS - 1], F32)
        for h in range(RET_HEADS - 2, -1, -1):
            lg_lanes = jnp.where(head_of_lane == h, lgs[h], lg_lanes)
        q_decay = jnp.exp(lg_lanes * q_exp)
        k_decay = jnp.exp(lg_lanes * k_exp)
        for b in range(batch):
            q = q_ref[b]
            k = k_ref[b] * (RET_DK ** -0.5)
            if use_rope:
                q = q * cos + _swap_halves(q, RET_DK // 2) * sin
                k = k * cos + _swap_halves(k, RET_DK // 2) * sin
            k_bf = k.astype(BF16)
            kt_decayed = (k * k_decay).T.astype(BF16)
            q_decayed = q * q_decay
            state_bf = s_scr[b, d].astype(BF16)
            for h in range(RET_HEADS):
                mine = head_of_lane == h
                sl = slice(h * RET_DV, (h + 1) * RET_DV)
                rows = slice(h * RET_DK, (h + 1) * RET_DK)
                decay = jnp.where(diff >= 0, jnp.exp(lgs[h] * jnp.maximum(diff, 0.0)), 0.0)
                vh = v_ref[b, :, sl].astype(BF16)
                scores = lax.dot_general(jnp.where(mine, q, 0.0).astype(BF16), k_bf, (((1,), (1,)), ((), ())),
                                         preferred_element_type=F32) * decay
                inner = jnp.dot(scores.astype(BF16), vh, preferred_element_type=F32)
                cross = jnp.dot(jnp.where(mine, q_decayed, 0.0).astype(BF16), state_bf, preferred_element_type=F32)
                o = inner + cross
                kv = jnp.dot(kt_decayed[rows], vh, preferred_element_type=F32)
                chunk_decay = jnp.exp(lgs[h] * jnp.full((RET_DK, RET_DV), float(cc), F32))
                s_scr[b, d, rows] = s_scr[b, d, rows] * chunk_decay + kv
                mu = jnp.mean(o, axis=-1, keepdims=True)
                oc = o - mu
                var = jnp.mean(oc * oc, axis=-1, keepdims=True)
                gate = g_ref[b, :, sl]
                y_ref[b, :, sl] = (gate * jax.nn.sigmoid(gate)) * (oc * lax.rsqrt(var + EPS) * gain[:, sl])

    @pl.when(c == n_chunks - 1)
    def _():
        sf_ref[...] = s_scr[...]


def _retention(p, lg, cos, sin, gain, s0, use_rope):
    b, t, _ = p.shape
    cc = RET_CHUNK
    n = t // cc
    wv = RET_HEADS * RET_DV

    def pspec(width, col, backward):
        if backward:
            return pl.BlockSpec((b, cc, width), lambda ci, lg_: (0, n - 1 - ci, col // width))
        return pl.BlockSpec((b, cc, width), lambda ci, lg_: (0, ci, col // width))

    def tspec(backward):
        if backward:
            return pl.BlockSpec((cc, 128), lambda ci, lg_: (n - 1 - ci, 0))
        return pl.BlockSpec((cc, 128), lambda ci, lg_: (ci, 0))

    state_spec = pl.BlockSpec((b, 2, RET_HEADS * RET_DK, RET_DV), lambda ci, lg_: (0, 0, 0, 0))
    kern = functools.partial(_ret_kernel, use_rope=use_rope, n_chunks=n)
    grid_spec = pltpu.PrefetchScalarGridSpec(
        num_scalar_prefetch=1,
        grid=(n,),
        in_specs=[
            pspec(256, C_RET_Q, False), pspec(256, C_RET_Q, True),
            pspec(256, C_RET_K, False), pspec(256, C_RET_K, True),
            pspec(wv, C_RET_V, False), pspec(wv, C_RET_V, True),
            pspec(wv, C_RET_GF, False), pspec(wv, C_RET_GB, True),
            tspec(False), tspec(True), tspec(False), tspec(True),
            pl.BlockSpec((1, wv), lambda ci, lg_: (0, 0)),
            state_spec,
        ],
        out_specs=[
            pl.BlockSpec((b, cc, wv), lambda ci, lg_: (0, ci, 0)),
            pl.BlockSpec((b, cc, wv), lambda ci, lg_: (0, n - 1 - ci, 0)),
            state_spec,
        ],
        scratch_shapes=[pltpu.VMEM((b, 2, RET_HEADS * RET_DK, RET_DV), F32)],
    )
    return pl.pallas_call(
        kern,
        grid_spec=grid_spec,
        out_shape=[
            jax.ShapeDtypeStruct((b, t, wv), F32),
            jax.ShapeDtypeStruct((b, t, wv), F32),
            jax.ShapeDtypeStruct((b, 2, RET_HEADS * RET_DK, RET_DV), F32),
        ],
        compiler_params=_cparams("arbitrary"),
        name="retention",
    )(lg, p, p, p, p, p, p, p, p, cos, cos, sin, sin, gain, s0)


def _na_tables(rows):
    groups = rows // NA_QROWS
    cols = np.arange(GRID_W)
    c0 = np.clip(cols - NA_KW // 2, 0, GRID_W - NA_KW)
    col_ok = (cols[None, :] >= c0[:, None]) & (cols[None, :] < c0[:, None] + NA_KW)
    dc = cols[None, :] - cols[:, None] + (NA_KW - 1)
    onehot = (dc[None] == np.arange(2 * NA_KW - 1)[:, None, None]) & col_ok[None]
    row_bias = np.full((3, NA_QROWS, NA_KROWS), 2 * NA_KH - 1, np.int64)
    for ti, g in enumerate((0, 1, groups - 1)):
        ws = int(np.clip(NA_QROWS * g - NA_KH // 2, 0, rows - NA_KROWS))
        for lr in range(NA_QROWS):
            r = NA_QROWS * g + lr
            r0 = int(np.clip(r - NA_KH // 2, 0, rows - NA_KH))
            for kr in range(NA_KROWS):
                if r0 <= ws + kr < r0 + NA_KH:
                    row_bias[ti, lr, kr] = ws + kr - r + (NA_KH - 1)
    return onehot, col_ok, row_bias


def _na_bias_tables(na_rpb, rows):
    onehot, col_ok, row_bias = _na_tables(rows)
    depth, heads = na_rpb.shape[:2]
    nd = 2 * NA_KH
    toep = jnp.einsum('lhdj,jck->lhdck', na_rpb, onehot.astype(np.float32), precision=lax.Precision.HIGHEST)
    toep = jnp.where(col_ok, toep * LOG2_E, MASK_VALUE)
    masked = jnp.full(toep.shape[:2] + (1, GRID_W, GRID_W), MASK_VALUE, F32)
    toep = jnp.concatenate([toep, masked], axis=2)
    sides = jnp.concatenate([jnp.pad(toep, ((0, 0),) * 4 + ((0, GRID_W),)),
                             jnp.pad(toep, ((0, 0),) * 4 + ((GRID_W, 0),))], axis=2)
    pairs = row_bias.reshape(-1, 2)
    pick = ((pairs[:, :1] == np.arange(nd)[None, :]).astype(np.float32),
            (pairs[:, 1:] == np.arange(nd)[None, :]).astype(np.float32))
    pick = np.concatenate(pick, axis=1)
    tab = jnp.einsum('nd,lhdck->lhnck', pick, sides, precision=lax.Precision.HIGHEST)
    return tab.reshape(depth, heads, 3, NA_QROWS * NA_KROWS // 2, GRID_W, 2 * GRID_W)


def _na_kernel(q_ref, k0_ref, k1_ref, k2_ref, v0_ref, v1_ref, v2_ref, kz_ref, vz_ref, bias_ref, o_ref):
    lanes = 2 * NA_DH
    npair = NA_KROWS // 2
    low = lax.broadcasted_iota(jnp.int32, (q_ref.shape[1], lanes), 1) < NA_DH
    for pr in range(NA_HEADS // 2):
        sl = slice(pr * lanes, (pr + 1) * lanes)
        qp = q_ref[0, :, sl] * (NA_DH ** -0.5 * LOG2_E)
        qm = (jnp.where(low, qp, 0.0).astype(BF16), jnp.where(low, 0.0, qp).astype(BF16))
        ks = [r[0, :, sl].astype(BF16) for r in (k0_ref, k1_ref, k2_ref, kz_ref)]
        vs = [r[0, :, sl].astype(BF16) for r in (v0_ref, v1_ref, v2_ref, vz_ref)]
        outs = []
        for hh in range(2):
            h = 2 * pr + hh
            cols = []
            for j in range(4):
                sj = lax.dot_general(qm[hh], ks[j], (((1,), (1,)), ((), ())), preferred_element_type=F32)
                halves = [sj[:, i * lanes:(i + 1) * lanes] for i in range(sj.shape[1] // lanes)]
                if j < 3:
                    halves = [hv + jnp.concatenate([bias_ref[0, h, 0, lr * npair + j * len(halves) + i]
                                                    for lr in range(NA_QROWS)], axis=0)
                              for i, hv in enumerate(halves)]
                cols += halves
            m = jnp.max(functools.reduce(jnp.maximum, cols), axis=-1, keepdims=True)
            ps = [jnp.exp2(cj - m) for cj in cols]
            l = jnp.sum(functools.reduce(jnp.add, ps), axis=-1, keepdims=True)
            o = None
            per = len(cols) // 4
            for j in range(4):
                pj = jnp.concatenate([pc.astype(BF16) for pc in ps[j * per:(j + 1) * per]], axis=1)
                oj = jnp.dot(pj, vs[j], preferred_element_type=F32)
                o = oj if o is None else o + oj
            outs.append(o / l)
        o_ref[0, :, sl] = jnp.where(low, outs[0], outs[1]).astype(o_ref.dtype)


def _na(px, pz, bias, layer):
    b, t, _ = px.shape
    lz = pz.shape[1]
    groups = t // NA_QB
    w = NA_HEADS * NA_DH
    kblk = NA_KB // 3
    assert lz == kblk and groups >= 3

    def kspec(col, off):
        return pl.BlockSpec((1, kblk, w), lambda bi, g: (bi, jnp.clip(g - 1, 0, groups - 3) + off, col // w))

    def tab(g):
        return jnp.where(g == 0, 0, jnp.where(g == groups - 1, 2, 1))

    return pl.pallas_call(
        _na_kernel,
        grid=(b, groups),
        in_specs=[
            pl.BlockSpec((1, NA_QB, w), lambda bi, g: (bi, g, C_NA_Q // w)),
            kspec(C_NA_K, 0), kspec(C_NA_K, 1), kspec(C_NA_K, 2),
            kspec(C_NA_V, 0), kspec(C_NA_V, 1), kspec(C_NA_V, 2),
            pl.BlockSpec((1, lz, w), lambda bi, g: (bi, 0, C_NA_K // w)),
            pl.BlockSpec((1, lz, w), lambda bi, g: (bi, 0, C_NA_V // w)),
            pl.BlockSpec((1, NA_HEADS, 1) + bias.shape[3:], lambda bi, g: (layer, 0, tab(g), 0, 0, 0)),
        ],
        out_specs=pl.BlockSpec((1, NA_QB, w), lambda bi, g: (bi, g, 0)),
        out_shape=jax.ShapeDtypeStruct((b, t, w), BF16),
        compiler_params=_cparams("parallel", "arbitrary"),
        name="na",
    )(px, px, px, px, px, px, px, pz, pz, bias)


def _rms(x, gain):
    return x * lax.rsqrt(jnp.mean(x * x, axis=-1, keepdims=True) + EPS) * gain


def _mlaprep_kernel(xq_ref, xkv_ref, xkr_ref, zq_ref, zkv_ref, zkr_ref, qn_ref, kvn_ref, wq_ref, wqs_ref, wk_ref,
                    wv_ref, cos_ref, sin_ref, q_ref, k_ref, v_ref, *, nx):
    latent = pl.program_id(1) < nx
    pq = jnp.where(latent, xq_ref[0], zq_ref[0])
    pkv = jnp.where(latent, xkv_ref[0], zkv_ref[0])
    pkr = jnp.where(latent, xkr_ref[0], zkr_ref[0])
    hq = _rms(pq, qn_ref[...]).astype(BF16)
    hkv = _rms(pkv, kvn_ref[...]).astype(BF16)
    q = jnp.dot(hq, wq_ref[0], preferred_element_type=F32)
    q_swapped = jnp.dot(hq, wqs_ref[0], preferred_element_type=F32)
    k = jnp.dot(hkv, wk_ref[0], preferred_element_type=F32)
    v = jnp.dot(hkv, wv_ref[0], preferred_element_type=F32)
    kr = pltpu.roll(pkr, MLA_NOPE, 1)
    k = k + jnp.concatenate([kr] * MLA_HEADS, axis=1)
    k_swapped = jnp.concatenate([_swap_halves(kr, MLA_ROPE // 2)] * MLA_HEADS, axis=1)
    cos = jnp.concatenate([cos_ref[...]] * MLA_HEADS, axis=1)
    sin = jnp.concatenate([sin_ref[...]] * MLA_HEADS, axis=1)
    q = q * cos + q_swapped * sin
    k = k * cos + k_swapped * sin
    q_ref[0] = (q * ((MLA_NOPE + MLA_ROPE) ** -0.5 * LOG2_E)).astype(BF16)
    k_ref[0] = k.astype(BF16)
    v_ref[0] = v.astype(BF16)


def _mlaprep(px, pz, qn, kvn, wq, wqs, wk, wv, cos, sin, layer):
    b, t, _ = px.shape
    lz = pz.shape[1]
    tm = lz
    nx = t // tm
    wqk = MLA_HEADS * MLA_DK_PAD
    wvv = MLA_HEADS * MLA_DV

    def xspec(width, col):
        return pl.BlockSpec((1, tm, width), lambda bi, i: (bi, jnp.minimum(i, nx - 1), col // width))

    def zspec(width, col):
        return pl.BlockSpec((1, tm, width), lambda bi, i: (bi, 0, col // width))

    out_spec = pl.BlockSpec((1, tm, wqk), lambda bi, i: (bi, i, 0))
    kern = functools.partial(_mlaprep_kernel, nx=nx)
    return pl.pallas_call(
        kern,
        grid=(b, nx + 1),
        in_specs=[
            xspec(MLA_RANK, C_MLA_Q), xspec(MLA_RANK, C_MLA_KV), xspec(128, C_MLA_KR),
            zspec(MLA_RANK, C_MLA_Q), zspec(MLA_RANK, C_MLA_KV), zspec(128, C_MLA_KR),
            _const_spec((1, MLA_RANK)), _const_spec((1, MLA_RANK)),
            _layer_spec((MLA_RANK, wqk), layer), _layer_spec((MLA_RANK, wqk), layer),
            _layer_spec((MLA_RANK, wqk), layer), _layer_spec((MLA_RANK, wvv), layer),
            pl.BlockSpec((tm, 128), lambda bi, i: (i, 0)),
            pl.BlockSpec((tm, 128), lambda bi, i: (i, 0)),
        ],
        out_specs=[out_spec, out_spec, pl.BlockSpec((1, tm, wvv), lambda bi, i: (bi, i, 0))],
        out_shape=[jax.ShapeDtypeStruct((b, t + lz, wqk), BF16)] * 2
        + [jax.ShapeDtypeStruct((b, t + lz, wvv), BF16)],
        compiler_params=_cparams("parallel", "parallel"),
        name="mlaprep",
    )(px, px, px, pz, pz, pz, qn, kvn, wq, wqs, wk, wv, cos, sin)


def _flash_kernel(q_ref, k_ref, v_ref, o_ref, m_scr, l_scr, acc_scr, *, heads, dk, dv, scale, nk):
    ki = pl.program_id(2)
    tq = q_ref.shape[1]
    tk = k_ref.shape[1]
    lanes = 2 * dv

    @pl.when(ki == 0)
    def _():
        m_scr[...] = jnp.full(m_scr.shape, -jnp.inf, F32)
        l_scr[...] = jnp.zeros(l_scr.shape, F32)
        acc_scr[...] = jnp.zeros(acc_scr.shape, F32)

    low = lax.broadcasted_iota(jnp.int32, (tq, lanes), 1) < dv
    for pr in range(heads // 2):
        vp = v_ref[0, :, pr * lanes:(pr + 1) * lanes].astype(BF16)
        alphas, pvs = [], []
        for h in (2 * pr, 2 * pr + 1):
            qh = q_ref[0, :, h * dk:(h + 1) * dk]
            if scale != 1.0:
                qh = qh * scale
            kh = k_ref[0, :, h * dk:(h + 1) * dk]
            s = lax.dot_general(qh.astype(BF16), kh.astype(BF16), (((1,), (1,)), ((), ())),
                                preferred_element_type=F32)
            cols = [s[:, j * lanes:(j + 1) * lanes] for j in range(tk // lanes)]
            m_prev = m_scr[h]
            m_tile = jnp.max(functools.reduce(jnp.maximum, cols), axis=-1, keepdims=True)
            m_new = jnp.maximum(m_prev, m_tile)
            alpha = jnp.exp2(m_prev - m_new)
            ps = [jnp.exp2(cj - m_new) for cj in cols]
            l_scr[h] = alpha * l_scr[h] + functools.reduce(jnp.add, ps)
            m_scr[h] = m_new
            p = jnp.concatenate([pj.astype(BF16) for pj in ps], axis=1)
            pvs.append(jnp.dot(p, vp, preferred_element_type=F32))
            alphas.append(alpha)
        acc_scr[pr] = acc_scr[pr] * jnp.where(low, alphas[0], alphas[1]) + jnp.where(low, pvs[0], pvs[1])

    @pl.when(ki == nk - 1)
    def _():
        for pr in range(heads // 2):
            l0 = jnp.sum(l_scr[2 * pr], axis=-1, keepdims=True)
            l1 = jnp.sum(l_scr[2 * pr + 1], axis=-1, keepdims=True)
            o_ref[0, :, pr * lanes:(pr + 1) * lanes] = (acc_scr[pr] / jnp.where(low, l0, l1)).astype(o_ref.dtype)


def _flash(q, k, v, qcol, kcol, vcol, heads, dk, dv, scale, tq_pref, tk_pref):
    b, tq_all, _ = q.shape
    tk_all = k.shape[1]
    tq = _tile(tq_all, tq_pref)
    tk = _tile(tk_all, tk_pref)
    nk = tk_all // tk
    assert heads % 2 == 0 and 2 * dv == 128 and tk % 128 == 0
    kern = functools.partial(_flash_kernel, heads=heads, dk=dk, dv=dv, scale=scale, nk=nk)
    return pl.pallas_call(
        kern,
        grid=(b, tq_all // tq, nk),
        in_specs=[
            pl.BlockSpec((1, tq, heads * dk), lambda bi, i, j: (bi, i, qcol)),
            pl.BlockSpec((1, tk, heads * dk), lambda bi, i, j: (bi, j, kcol)),
            pl.BlockSpec((1, tk, heads * dv), lambda bi, i, j: (bi, j, vcol)),
        ],
        out_specs=pl.BlockSpec((1, tq, heads * dv), lambda bi, i, j: (bi, i, 0)),
        out_shape=jax.ShapeDtypeStruct((b, tq_all, heads * dv), BF16),
        scratch_shapes=[
            pltpu.VMEM((heads, tq, 2 * dv), F32),
            pltpu.VMEM((heads, tq, 2 * dv), F32),
            pltpu.VMEM((heads // 2, tq, 2 * dv), F32),
        ],
        compiler_params=_cparams("parallel", "parallel", "arbitrary"),
        name="flash",
    )(q, k, v)


def _mla_attn_kernel(q_ref, k_ref, v_ref, o_ref, m_scr, l_scr, acc_scr, *, heads, nk):
    ki = pl.program_id(2)
    w = MLA_DK_PAD
    tq = q_ref.shape[1]

    @pl.when(ki == 0)
    def _():
        m_scr[...] = jnp.full(m_scr.shape, -jnp.inf, F32)
        l_scr[...] = jnp.zeros(l_scr.shape, F32)
        acc_scr[...] = jnp.zeros(acc_scr.shape, F32)

    low = lax.broadcasted_iota(jnp.int32, (tq, w), 1) < MLA_DV
    for pr in range(heads // 2):
        vp = v_ref[0, :, pr * w:(pr + 1) * w]
        alphas, pvs = [], []
        for h in (2 * pr, 2 * pr + 1):
            sl = slice(h * w, (h + 1) * w)
            s = lax.dot_general(q_ref[0, :, sl], k_ref[0, :, sl], (((1,), (1,)), ((), ())),
                                preferred_element_type=F32)
            cols = [s[:, j * w:(j + 1) * w] for j in range(s.shape[1] // w)]
            m_prev = m_scr[h]
            m_new = jnp.maximum(m_prev, jnp.max(functools.reduce(jnp.maximum, cols), axis=-1, keepdims=True))
            alpha = jnp.exp2(m_prev - m_new)
            ps = [jnp.exp2(cj - m_new) for cj in cols]
            l_scr[h] = alpha * l_scr[h] + functools.reduce(jnp.add, ps)
            m_scr[h] = m_new
            p = jnp.concatenate([pj.astype(BF16) for pj in ps], axis=1)
            pvs.append(jnp.dot(p, vp, preferred_element_type=F32))
            alphas.append(alpha)
        acc_scr[pr] = acc_scr[pr] * jnp.where(low, alphas[0], alphas[1]) + jnp.where(low, pvs[0], pvs[1])

    @pl.when(ki == nk - 1)
    def _():
        for pr in range(heads // 2):
            l0 = jnp.sum(l_scr[2 * pr], axis=-1, keepdims=True)
            l1 = jnp.sum(l_scr[2 * pr + 1], axis=-1, keepdims=True)
            o_ref[0, :, pr * w:(pr + 1) * w] = (acc_scr[pr] / jnp.where(low, l0, l1)).astype(o_ref.dtype)


def _mla_attn(q, k, v, q_start, q_len, k_start, k_len, tq_pref, tk_pref):
    b, _, wq = q.shape
    heads = wq // MLA_DK_PAD
    tq = _tile(q_len, tq_pref, 8)
    tk = _tile(k_len, tk_pref, 128)
    assert q_start % tq == 0 and k_start % tk == 0
    q_off, k_off = q_start // tq, k_start // tk
    nk = k_len // tk
    kern = functools.partial(_mla_attn_kernel, heads=heads, nk=nk)
    return pl.pallas_call(
        kern,
        grid=(b, q_len // tq, nk),
        in_specs=[
            pl.BlockSpec((1, tq, wq), lambda bi, i, j: (bi, q_off + i, 0)),
            pl.BlockSpec((1, tk, wq), lambda bi, i, j: (bi, k_off + j, 0)),
            pl.BlockSpec((1, tk, heads * MLA_DV), lambda bi, i, j: (bi, k_off + j, 0)),
        ],
        out_specs=pl.BlockSpec((1, tq, heads * MLA_DV), lambda bi, i, j: (bi, i, 0)),
        out_shape=jax.ShapeDtypeStruct((b, q_len, heads * MLA_DV), BF16),
        scratch_shapes=[
            pltpu.VMEM((heads, tq, MLA_DK_PAD), F32),
            pltpu.VMEM((heads, tq, MLA_DK_PAD), F32),
            pltpu.VMEM((heads // 2, tq, MLA_DK_PAD), F32),
        ],
        compiler_params=_cparams("parallel", "parallel", "arbitrary"),
        name="mla_attn",
    )(q, k, v)


def _layer_norm(r, gain, bias):
    mu = jnp.mean(r, axis=-1, keepdims=True)
    rc = r - mu
    var = jnp.mean(rc * rc, axis=-1, keepdims=True)
    return rc * lax.rsqrt(var + EPS) * gain + bias


def _post_kernel(yaf_ref, yab_ref, yb_ref, yc_ref, ga_ref, gb_ref, gc_ref, x_ref, g1_ref, sc_ref, sh_ref, g2_ref,
                 wa_ref, wb_ref, wc_ref, wo_ref, w1_ref, w2_ref, lng_ref, lnb_ref, o_ref, *, ff_chunk):
    ya = (yaf_ref[0] + yab_ref[0]).astype(BF16)
    y = (jax.nn.sigmoid(ga_ref[0]) * jnp.dot(ya, wa_ref[0], preferred_element_type=F32)
         + jax.nn.sigmoid(gb_ref[0]) * jnp.dot(yb_ref[0].astype(BF16), wb_ref[0], preferred_element_type=F32)
         + jax.nn.sigmoid(gc_ref[0]) * jnp.dot(yc_ref[0].astype(BF16), wc_ref[0], preferred_element_type=F32))
    mix = jnp.dot(y.astype(BF16), wo_ref[0], preferred_element_type=F32)
    x1 = _layer_norm(DEEPNORM_ALPHA * x_ref[0] + g1_ref[0] * mix, lng_ref[0:1], lnb_ref[0:1])
    h = (x1 * (1.0 + sc_ref[0]) + sh_ref[0]).astype(BF16)
    acc = jnp.zeros(x1.shape, F32)
    for j in range(D_FF // ff_chunk):
        u = jnp.dot(h, w1_ref[0, :, j * ff_chunk:(j + 1) * ff_chunk], preferred_element_type=F32)
        u = jnp.square(jnp.maximum(u, 0.0)).astype(BF16)
        acc = acc + jnp.dot(u, w2_ref[0, j * ff_chunk:(j + 1) * ff_chunk, :], preferred_element_type=F32)
    o_ref[0] = _layer_norm(DEEPNORM_ALPHA * x1 + g2_ref[0] * acc, lng_ref[1:2], lnb_ref[1:2])


def _post(yaf, yab, yb, yc, p, x, g1, sc2, sh2, g2, wa, wb, wc, wo, w1, w2, lng, lnb, layer):
    b, t, d = x.shape
    tm = _tile(t, 512)
    wbr = 512
    row = lambda width, col: pl.BlockSpec((1, tm, width), lambda bi, i: (bi, i, col // width))
    mod = pl.BlockSpec((1, 1, d), lambda bi, i: (bi, 0, 0))
    kern = functools.partial(_post_kernel, ff_chunk=1024)
    return pl.pallas_call(
        kern,
        grid=(b, t // tm),
        in_specs=[
            row(wbr, 0), row(wbr, 0), row(wbr, 0), row(wbr, 0),
            row(d, C_GATE_A), row(d, C_GATE_B), row(d, C_GATE_C), row(d, 0),
            mod, mod, mod, mod,
            _layer_spec((wbr, d), layer), _layer_spec((wbr, d), layer), _layer_spec((wbr, d), layer),
            _layer_spec((d, d), layer), _layer_spec((d, D_FF), layer), _layer_spec((D_FF, d), layer),
            _const_spec((2, d)), _const_spec((2, d)),
        ],
        out_specs=row(d, 0),
        out_shape=jax.ShapeDtypeStruct((b, t, d), F32),
        compiler_params=_cparams("parallel", "parallel"),
        name="post",
    )(yaf, yab, yb, yc, p, p, p, x, g1, sc2, sh2, g2, wa, wb, wc, wo, w1, w2, lng, lnb)


def _rope_tables(n_tok, rot_dim):
    t = jnp.arange(n_tok)
    row = (t // GRID_W).astype(F32)
    col = (t % GRID_W).astype(F32)
    n_freq = rot_dim // 4
    inv_freq = ROPE_BASE ** (-2.0 * jnp.arange(n_freq, dtype=F32) / (rot_dim // 2))
    ang = jnp.concatenate([row[:, None] * inv_freq, col[:, None] * inv_freq], axis=-1)
    return jnp.cos(ang), jnp.sin(ang)


def _ret_rope_tables(n_tok):
    cos, sin = _rope_tables(n_tok, RET_DK)
    cos_h = jnp.concatenate([cos, cos], axis=1)
    sin_h = jnp.concatenate([-sin, sin], axis=1)
    return jnp.tile(cos_h, (1, 2)), jnp.tile(sin_h, (1, 2))


def _mla_rope_tables(n_tok):
    cos, sin = _rope_tables(n_tok, MLA_ROPE)
    ones = jnp.ones((n_tok, MLA_NOPE), F32)
    zeros = jnp.zeros((n_tok, MLA_NOPE), F32)
    pad = jnp.zeros((n_tok, MLA_DK_PAD - MLA_NOPE - MLA_ROPE), F32)
    cos_h = jnp.concatenate([ones, cos, cos, pad], axis=1)
    sin_h = jnp.concatenate([zeros, -sin, sin, pad], axis=1)
    return cos_h, sin_h


def _pack_w_in(w):
    pad = jnp.zeros(w.shape[:2] + (P_WIDTH - 7200,), w.dtype)
    return jnp.concatenate([w[..., :4096], w[..., 4128:7200], w[..., 4096:4128], pad], axis=-1).astype(BF16)


def _pack_mla_weights(w_qup, w_kvup):
    depth, r = w_qup.shape[:2]
    wq = w_qup.reshape(depth, r, MLA_HEADS, MLA_NOPE + MLA_ROPE)
    half = MLA_ROPE // 2
    wqs = jnp.concatenate([jnp.zeros((depth, r, MLA_HEADS, MLA_NOPE), wq.dtype), wq[..., MLA_NOPE + half:],
                           wq[..., MLA_NOPE:MLA_NOPE + half]], axis=-1)
    pad_q = ((0, 0), (0, 0), (0, 0), (0, MLA_DK_PAD - MLA_NOPE - MLA_ROPE))
    wq = jnp.pad(wq, pad_q).reshape(depth, r, MLA_HEADS * MLA_DK_PAD)
    wqs = jnp.pad(wqs, pad_q).reshape(depth, r, MLA_HEADS * MLA_DK_PAD)
    wkv = w_kvup.reshape(depth, r, MLA_HEADS, MLA_NOPE + MLA_DV)
    wk = jnp.pad(wkv[..., :MLA_NOPE], ((0, 0), (0, 0), (0, 0), (0, MLA_DK_PAD - MLA_NOPE)))
    wk = wk.reshape(depth, r, MLA_HEADS * MLA_DK_PAD)
    wv = wkv[..., MLA_NOPE:].reshape(depth, r, MLA_HEADS * MLA_DV)
    return wq.astype(BF16), wqs.astype(BF16), wk.astype(BF16), wv.astype(BF16)


def kernel(x, c, ctx, c_ctx, w_ada, b_ada, w_in, ret_log_decay, ret_gn_gain, na_rpb, mla_q_norm, mla_w_qup,
           mla_kv_norm, mla_w_kvup, w_branch_ret, w_branch_na, w_branch_mla, w_out, w_ff1, w_ff2, ln_gain, ln_bias):
    depth = w_ada.shape[0]
    b, t, d = x.shape
    lz = ctx.shape[1]
    rows = t // GRID_W

    cc = jnp.zeros((8, d), F32).at[:b].set(c).at[b].set(c_ctx)
    mod = _ada(cc, w_ada, b_ada)

    cos_r, sin_r = _ret_rope_tables(t)
    cos_m, sin_m = _mla_rope_tables(t)
    cos_m = jnp.concatenate([cos_m, jnp.ones((lz, MLA_DK_PAD), F32)], axis=0)
    sin_m = jnp.concatenate([sin_m, jnp.zeros((lz, MLA_DK_PAD), F32)], axis=0)
    cos_rz, sin_rz = cos_r[:lz], sin_r[:lz]
    na_bias = _na_bias_tables(na_rpb, rows)
    s_zero = jnp.zeros((b, 2, RET_HEADS * RET_DK, RET_DV), F32)

    w_in_p = _pack_w_in(w_in)
    wq, wqs, wk, wv = _pack_mla_weights(mla_w_qup, mla_w_kvup)
    wa = w_branch_ret.astype(BF16)
    wb = w_branch_na.astype(BF16)
    wc = w_branch_mla.astype(BF16)
    wo = w_out.astype(BF16)
    w1 = w_ff1.astype(BF16)
    w2 = w_ff2.astype(BF16)

    z = ctx
    for l in range(depth):
        need_ctx = l < depth - 1
        mx = mod[l, :b].reshape(b, 6, 1, d)
        mz = jnp.broadcast_to(mod[l, b].reshape(1, 6, 1, d), (b, 6, 1, d))
        sh1x, sc1x, g1x, sh2x, sc2x, g2x = [mx[:, i] for i in range(6)]
        sh1z, sc1z, g1z, sh2z, sc2z, g2z = [mz[:, i] for i in range(6)]

        px = _inproj(x, sc1x, sh1x, w_in_p, l)
        pz = _inproj(z, sc1z, sh1z, w_in_p, l)

        lg = jnp.log1p(-jnp.exp(ret_log_decay[l].astype(F32))).reshape(2 * RET_HEADS)
        gn_gain = ret_gn_gain[l].reshape(1, RET_HEADS * RET_DV)
        yaf_z, yab_z, s_ctx = _retention(pz, lg, cos_rz, sin_rz, gn_gain, s_zero, use_rope=False)
        yaf_x, yab_x, _ = _retention(px, lg, cos_r, sin_r, gn_gain, s_ctx, use_rope=True)

        yb_x = _na(px, pz, na_bias, l)

        qn = mla_q_norm[l].reshape(1, MLA_RANK)
        kvn = mla_kv_norm[l].reshape(1, MLA_RANK)
        q_all, k_all, v_all = _mlaprep(px, pz, qn, kvn, wq, wqs, wk, wv, cos_m, sin_m, l)
        yc_x = _mla_attn(q_all, k_all, v_all, 0, t, 0, t + lz, 1024, 2816)

        x = _post(yaf_x, yab_x, yb_x, yc_x, px, x, g1x, sc2x, sh2x, g2x, wa, wb, wc, wo, w1, w2,
                  ln_gain[l], ln_bias[l], l)

        if need_ctx:
            wna = NA_HEADS * NA_DH
            yb_z = _flash(pz, pz, pz, C_NA_Q // wna, C_NA_K // wna, C_NA_V // wna,
                          NA_HEADS, NA_DH, NA_DH, NA_DH ** -0.5 * LOG2_E, 256, 256)
            yc_z = _mla_attn(q_all, k_all, v_all, t, lz, t, lz, lz, lz)
            z = _post(yaf_z, yab_z, yb_z, yc_z, pz, z, g1z, sc2z, sh2z, g2z, wa, wb, wc, wo, w1, w2,
                      ln_gain[l], ln_bias[l], l)
    return x
```

```python
import functools

import numpy as np
import jax
import jax.numpy as jnp
from jax import lax
from jax.experimental import pallas as pl
from jax.experimental.pallas import tpu as pltpu

F32 = jnp.float32
BF16 = jnp.bfloat16

D_MODEL = 1024
GRID_W = 64
RET_HEADS = 4
RET_DK = 64
RET_DV = 128
RET_CHUNK = 256
NA_HEADS = 8
NA_DH = 64
NA_KH = 8
NA_KW = 16
MLA_HEADS = 8
MLA_RANK = 256
MLA_NOPE = 64
MLA_ROPE = 32
MLA_DV = 64
MLA_DK_PAD = 128
D_FF = 4 * D_MODEL
ROPE_BASE = 10000.0
EPS = 1e-5
DEPTH_FOR_NORM = 4
DEEPNORM_ALPHA = (2 * DEPTH_FOR_NORM) ** 0.25
MASK_VALUE = -1e30
LOG2_E = 1.4426950408889634

C_RET_Q, C_RET_K, C_RET_V, C_RET_GF, C_RET_GB = 0, 256, 512, 1024, 1536
C_NA_Q, C_NA_K, C_NA_V = 2048, 2560, 3072
C_MLA_Q, C_MLA_KV = 3584, 3840
C_GATE_A, C_GATE_B, C_GATE_C = 4096, 5120, 6144
C_MLA_KR = 7168
P_WIDTH = 7296
P_COL_TILE = 2432

NA_QROWS = 4
NA_KROWS = 12
NA_QB = NA_QROWS * GRID_W
NA_KB = NA_KROWS * GRID_W

VMEM_LIMIT = 56 * 1024 * 1024


def _cparams(*sem):
    return pltpu.CompilerParams(dimension_semantics=sem, vmem_limit_bytes=VMEM_LIMIT)


def _tile(n, pref, mult=1):
    t = min(n, pref) // mult * mult
    while n % t:
        t -= mult
    return t


def _const_spec(shape):
    nd = len(shape)
    return pl.BlockSpec(shape, lambda *_: (0,) * nd)


def _layer_spec(shape, layer):
    nd = len(shape)
    return pl.BlockSpec((1,) + tuple(shape), lambda *_: (layer,) + (0,) * nd)


def _ada_kernel(c_ref, w_ref, b_ref, o_ref):
    c = c_ref[...]
    a = c * jax.nn.sigmoid(c)
    o_ref[0] = jnp.dot(a.astype(BF16), w_ref[0].astype(BF16), preferred_element_type=F32) + b_ref[0]


def _ada(cc, w_ada, b_ada):
    depth, d, n = w_ada.shape
    tn = _tile(n, 3072, 128)
    return pl.pallas_call(
        _ada_kernel,
        grid=(depth, n // tn),
        in_specs=[
            pl.BlockSpec((8, d), lambda l, j: (0, 0)),
            pl.BlockSpec((1, d, tn), lambda l, j: (l, 0, j)),
            pl.BlockSpec((1, 1, tn), lambda l, j: (l, 0, j)),
        ],
        out_specs=pl.BlockSpec((1, 8, tn), lambda l, j: (l, 0, j)),
        out_shape=jax.ShapeDtypeStruct((depth, 8, n), F32),
        compiler_params=_cparams("parallel", "parallel"),
        name="ada",
    )(cc, w_ada, b_ada.reshape(depth, 1, n))


def _inproj_kernel(x_ref, sc_ref, sh_ref, w_ref, o_ref):
    h = x_ref[0] * (1.0 + sc_ref[0]) + sh_ref[0]
    o_ref[0] = jnp.dot(h.astype(BF16), w_ref[0], preferred_element_type=F32)


def _inproj(x, sc, sh, w, layer):
    b, t, d = x.shape
    tm = _tile(t, 512)
    tn = P_COL_TILE
    return pl.pallas_call(
        _inproj_kernel,
        grid=(P_WIDTH // tn, b, t // tm),
        in_specs=[
            pl.BlockSpec((1, tm, d), lambda j, bi, i: (bi, i, 0)),
            pl.BlockSpec((1, 1, d), lambda j, bi, i: (bi, 0, 0)),
            pl.BlockSpec((1, 1, d), lambda j, bi, i: (bi, 0, 0)),
            pl.BlockSpec((1, d, tn), lambda j, bi, i: (layer, 0, j)),
        ],
        out_specs=pl.BlockSpec((1, tm, tn), lambda j, bi, i: (bi, i, j)),
        out_shape=jax.ShapeDtypeStruct((b, t, P_WIDTH), F32),
        compiler_params=_cparams("parallel", "parallel", "parallel"),
        name="inproj",
    )(x, sc, sh, w)


def _swap_halves(x, half):
    n = x.shape[-1]
    lane = lax.broadcasted_iota(jnp.int32, x.shape, x.ndim - 1)
    first = (lane % (2 * half)) < half
    return jnp.where(first, pltpu.roll(x, n - half, x.ndim - 1), pltpu.roll(x, half, x.ndim - 1))


def _ret_kernel(lg_ref, qf_ref, qb_ref, kf_ref, kb_ref, vf_ref, vb_ref, gf_ref, gb_ref,
                cosf_ref, cosb_ref, sinf_ref, sinb_ref, gain_ref, s0_ref,
                yf_ref, yb_ref, sf_ref, s_scr, *, use_rope, n_chunks):
    c = pl.program_id(0)
    batch = qf_ref.shape[0]
    cc = RET_CHUNK

    @pl.when(c == 0)
    def _():
        s_scr[...] = s0_ref[...]

    gain = gain_ref[...]
    row = lax.broadcasted_iota(jnp.int32, (cc, cc), 0).astype(F32)
    col = lax.broadcasted_iota(jnp.int32, (cc, cc), 1).astype(F32)
    wqk = RET_HEADS * RET_DK
    pos = lax.broadcasted_iota(jnp.int32, (cc, wqk), 0).astype(F32)
    head_of_lane = lax.broadcasted_iota(jnp.int32, (cc, wqk), 1) // RET_DK
    dirs = (
        (qf_ref, kf_ref, vf_ref, gf_ref, cosf_ref, sinf_ref, yf_ref, row - col, pos + 1.0, cc - 1.0 - pos),
        (qb_ref, kb_ref, vb_ref, gb_ref, cosb_ref, sinb_ref, yb_ref, col - row, cc - pos, pos),
    )
    for d, (q_ref, k_ref, v_ref, g_ref, cos_ref, sin_ref, y_ref, diff, q_exp, k_exp) in enumerate(dirs):
        if use_rope:
            cos = jnp.concatenate([cos_ref[...], cos_ref[...]], axis=1)
            sin = jnp.concatenate([sin_ref[...], sin_ref[...]], axis=1)
        lgs = [lg_ref[d * RET_HEADS + h] for h in range(RET_HEADS)]
        lg_lanes = jnp.full((cc, wqk), lgs[RET_HEADS - 1], F32)
        for h in range(RET_HEADS - 2, -1, -1):
            lg_lanes = jnp.where(head_of_lane == h, lgs[h], lg_lanes)
        q_decay = jnp.exp(lg_lanes * q_exp)
        k_decay = jnp.exp(lg_lanes * k_exp)
        for b in range(batch):
            q = q_ref[b]
            k = k_ref[b] * (RET_DK ** -0.5)
            if use_rope:
                q = q * cos + _swap_halves(q, RET_DK // 2) * sin
                k = k * cos + _swap_halves(k, RET_DK // 2) * sin
            k_bf = k.astype(BF16)
            kt_decayed = (k * k_decay).T.astype(BF16)
            q_decayed = q * q_decay
            state_bf = s_scr[b, d].astype(BF16)
            for h in range(RET_HEADS):
                mine = head_of_lane == h
                sl = slice(h * RET_DV, (h + 1) * RET_DV)
                rows = slice(h * RET_DK, (h + 1) * RET_DK)
                decay = jnp.where(diff >= 0, jnp.exp(lgs[h] * jnp.maximum(diff, 0.0)), 0.0)
                vh = v_ref[b, :, sl].astype(BF16)
                scores = lax.dot_general(jnp.where(mine, q, 0.0).astype(BF16), k_bf, (((1,), (1,)), ((), ())),
                                         preferred_element_type=F32) * decay
                inner = jnp.dot(scores.astype(BF16), vh, preferred_element_type=F32)
                cross = jnp.dot(jnp.where(mine, q_decayed, 0.0).astype(BF16), state_bf, preferred_element_type=F32)
                o = inner + cross
                kv = jnp.dot(kt_decayed[rows], vh, preferred_element_type=F32)
                chunk_decay = jnp.exp(lgs[h] * jnp.full((RET_DK, RET_DV), float(cc), F32))
                s_scr[b, d, rows] = s_scr[b, d, rows] * chunk_decay + kv
                mu = jnp.mean(o, axis=-1, keepdims=True)
                oc = o - mu
                var = jnp.mean(oc * oc, axis=-1, keepdims=True)
                gate = g_ref[b, :, sl]
                y_ref[b, :, sl] = (gate * jax.nn.sigmoid(gate)) * (oc * lax.rsqrt(var + EPS) * gain[:, sl])

    @pl.when(c == n_chunks - 1)
    def _():
        sf_ref[...] = s_scr[...]


def _retention(p, lg, cos, sin, gain, s0, use_rope):
    b, t, _ = p.shape
    cc = RET_CHUNK
    n = t // cc
    wv = RET_HEADS * RET_DV

    def pspec(width, col, backward):
        if backward:
            return pl.BlockSpec((b, cc, width), lambda ci, lg_: (0, n - 1 - ci, col // width))
        return pl.BlockSpec((b, cc, width), lambda ci, lg_: (0, ci, col // width))

    def tspec(backward):
        if backward:
            return pl.BlockSpec((cc, 128), lambda ci, lg_: (n - 1 - ci, 0))
        return pl.BlockSpec((cc, 128), lambda ci, lg_: (ci, 0))

    state_spec = pl.BlockSpec((b, 2, RET_HEADS * RET_DK, RET_DV), lambda ci, lg_: (0, 0, 0, 0))
    kern = functools.partial(_ret_kernel, use_rope=use_rope, n_chunks=n)
    grid_spec = pltpu.PrefetchScalarGridSpec(
        num_scalar_prefetch=1,
        grid=(n,),
        in_specs=[
            pspec(256, C_RET_Q, False), pspec(256, C_RET_Q, True),
            pspec(256, C_RET_K, False), pspec(256, C_RET_K, True),
            pspec(wv, C_RET_V, False), pspec(wv, C_RET_V, True),
            pspec(wv, C_RET_GF, False), pspec(wv, C_RET_GB, True),
            tspec(False), tspec(True), tspec(False), tspec(True),
            pl.BlockSpec((1, wv), lambda ci, lg_: (0, 0)),
            state_spec,
        ],
        out_specs=[
            pl.BlockSpec((b, cc, wv), lambda ci, lg_: (0, ci, 0)),
            pl.BlockSpec((b, cc, wv), lambda ci, lg_: (0, n - 1 - ci, 0)),
            state_spec,
        ],
        scratch_shapes=[pltpu.VMEM((b, 2, RET_HEADS * RET_DK, RET_DV), F32)],
    )
    return pl.pallas_call(
        kern,
        grid_spec=grid_spec,
        out_shape=[
            jax.ShapeDtypeStruct((b, t, wv), F32),
            jax.ShapeDtypeStruct((b, t, wv), F32),
            jax.ShapeDtypeStruct((b, 2, RET_HEADS * RET_DK, RET_DV), F32),
        ],
        compiler_params=_cparams("arbitrary"),
        name="retention",
    )(lg, p, p, p, p, p, p, p, p, cos, cos, sin, sin, gain, s0)


def _na_tables(rows):
    groups = rows // NA_QROWS
    cols = np.arange(GRID_W)
    c0 = np.clip(cols - NA_KW // 2, 0, GRID_W - NA_KW)
    col_ok = (cols[None, :] >= c0[:, None]) & (cols[None, :] < c0[:, None] + NA_KW)
    dc = cols[None, :] - cols[:, None] + (NA_KW - 1)
    onehot = (dc[None] == np.arange(2 * NA_KW - 1)[:, None, None]) & col_ok[None]
    row_bias = np.full((3, NA_QROWS, NA_KROWS), 2 * NA_KH - 1, np.int64)
    for ti, g in enumerate((0, 1, groups - 1)):
        ws = int(np.clip(NA_QROWS * g - NA_KH // 2, 0, rows - NA_KROWS))
        for lr in range(NA_QROWS):
            r = NA_QROWS * g + lr
            r0 = int(np.clip(r - NA_KH // 2, 0, rows - NA_KH))
            for kr in range(NA_KROWS):
                if r0 <= ws + kr < r0 + NA_KH:
                    row_bias[ti, lr, kr] = ws + kr - r + (NA_KH - 1)
    return onehot, col_ok, row_bias


def _na_bias_tables(na_rpb, rows):
    onehot, col_ok, row_bias = _na_tables(rows)
    depth, heads = na_rpb.shape[:2]
    nd = 2 * NA_KH
    toep = jnp.einsum('lhdj,jck->lhdck', na_rpb, onehot.astype(np.float32), precision=lax.Precision.HIGHEST)
    toep = jnp.where(col_ok, toep * LOG2_E, MASK_VALUE)
    masked = jnp.full(toep.shape[:2] + (1, GRID_W, GRID_W), MASK_VALUE, F32)
    toep = jnp.concatenate([toep, masked], axis=2)
    sides = jnp.concatenate([jnp.pad(toep, ((0, 0),) * 4 + ((0, GRID_W),)),
                             jnp.pad(toep, ((0, 0),) * 4 + ((GRID_W, 0),))], axis=2)
    pairs = row_bias.reshape(-1, 2)
    pick = ((pairs[:, :1] == np.arange(nd)[None, :]).astype(np.float32),
            (pairs[:, 1:] == np.arange(nd)[None, :]).astype(np.float32))
    pick = np.concatenate(pick, axis=1)
    tab = jnp.einsum('nd,lhdck->lhnck', pick, sides, precision=lax.Precision.HIGHEST)
    return tab.reshape(depth, heads, 3, NA_QROWS * NA_KROWS // 2, GRID_W, 2 * GRID_W)


def _na_kernel(q_ref, k0_ref, k1_ref, k2_ref, v0_ref, v1_ref, v2_ref, kz_ref, vz_ref, bias_ref, o_ref):
    lanes = 2 * NA_DH
    npair = NA_KROWS // 2
    low = lax.broadcasted_iota(jnp.int32, (q_ref.shape[1], lanes), 1) < NA_DH
    for pr in range(NA_HEADS // 2):
        sl = slice(pr * lanes, (pr + 1) * lanes)
        qp = q_ref[0, :, sl] * (NA_DH ** -0.5 * LOG2_E)
        qm = (jnp.where(low, qp, 0.0).astype(BF16), jnp.where(low, 0.0, qp).astype(BF16))
        ks = [r[0, :, sl].astype(BF16) for r in (k0_ref, k1_ref, k2_ref, kz_ref)]
        vs = [r[0, :, sl].astype(BF16) for r in (v0_ref, v1_ref, v2_ref, vz_ref)]
        outs = []
        for hh in range(2):
            h = 2 * pr + hh
            cols = []
            for j in range(4):
                sj = lax.dot_general(qm[hh], ks[j], (((1,), (1,)), ((), ())), preferred_element_type=F32)
                halves = [sj[:, i * lanes:(i + 1) * lanes] for i in range(sj.shape[1] // lanes)]
                if j < 3:
                    halves = [hv + jnp.concatenate([bias_ref[0, h, 0, lr * npair + j * len(halves) + i]
                                                    for lr in range(NA_QROWS)], axis=0)
                              for i, hv in enumerate(halves)]
                cols += halves
            m = jnp.max(functools.reduce(jnp.maximum, cols), axis=-1, keepdims=True)
            ps = [jnp.exp2(cj - m) for cj in cols]
            l = jnp.sum(functools.reduce(jnp.add, ps), axis=-1, keepdims=True)
            o = None
            per = len(cols) // 4
            for j in range(4):
                pj = jnp.concatenate([pc.astype(BF16) for pc in ps[j * per:(j + 1) * per]], axis=1)
                oj = jnp.dot(pj, vs[j], preferred_element_type=F32)
                o = oj if o is None else o + oj
            outs.append(o / l)
        o_ref[0, :, sl] = jnp.where(low, outs[0], outs[1]).astype(o_ref.dtype)


def _na(px, pz, bias, layer):
    b, t, _ = px.shape
    lz = pz.shape[1]
    groups = t // NA_QB
    w = NA_HEADS * NA_DH
    kblk = NA_KB // 3
    assert lz == kblk and groups >= 3

    def kspec(col, off):
        return pl.BlockSpec((1, kblk, w), lambda bi, g: (bi, jnp.clip(g - 1, 0, groups - 3) + off, col // w))

    def tab(g):
        return jnp.where(g == 0, 0, jnp.where(g == groups - 1, 2, 1))

    return pl.pallas_call(
        _na_kernel,
        grid=(b, groups),
        in_specs=[
            pl.BlockSpec((1, NA_QB, w), lambda bi, g: (bi, g, C_NA_Q // w)),
            kspec(C_NA_K, 0), kspec(C_NA_K, 1), kspec(C_NA_K, 2),
            kspec(C_NA_V, 0), kspec(C_NA_V, 1), kspec(C_NA_V, 2),
            pl.BlockSpec((1, lz, w), lambda bi, g: (bi, 0, C_NA_K // w)),
            pl.BlockSpec((1, lz, w), lambda bi, g: (bi, 0, C_NA_V // w)),
            pl.BlockSpec((1, NA_HEADS, 1) + bias.shape[3:], lambda bi, g: (layer, 0, tab(g), 0, 0, 0)),
        ],
        out_specs=pl.BlockSpec((1, NA_QB, w), lambda bi, g: (bi, g, 0)),
        out_shape=jax.ShapeDtypeStruct((b, t, w), BF16),
        compiler_params=_cparams("parallel", "arbitrary"),
        name="na",
    )(px, px, px, px, px, px, px, pz, pz, bias)


def _rms(x, gain):
    return x * lax.rsqrt(jnp.mean(x * x, axis=-1, keepdims=True) + EPS) * gain


def _mlaprep_kernel(xq_ref, xkv_ref, xkr_ref, zq_ref, zkv_ref, zkr_ref, qn_ref, kvn_ref, wq_ref, wqs_ref, wk_ref,
                    wv_ref, cos_ref, sin_ref, q_ref, k_ref, v_ref, *, nx):
    latent = pl.program_id(1) < nx
    pq = jnp.where(latent, xq_ref[0], zq_ref[0])
    pkv = jnp.where(latent, xkv_ref[0], zkv_ref[0])
    pkr = jnp.where(latent, xkr_ref[0], zkr_ref[0])
    hq = _rms(pq, qn_ref[...]).astype(BF16)
    hkv = _rms(pkv, kvn_ref[...]).astype(BF16)
    q = jnp.dot(hq, wq_ref[0], preferred_element_type=F32)
    q_swapped = jnp.dot(hq, wqs_ref[0], preferred_element_type=F32)
    k = jnp.dot(hkv, wk_ref[0], preferred_element_type=F32)
    v = jnp.dot(hkv, wv_ref[0], preferred_element_type=F32)
    kr = pltpu.roll(pkr, MLA_NOPE, 1)
    k = k + jnp.concatenate([kr] * MLA_HEADS, axis=1)
    k_swapped = jnp.concatenate([_swap_halves(kr, MLA_ROPE // 2)] * MLA_HEADS, axis=1)
    cos = jnp.concatenate([cos_ref[...]] * MLA_HEADS, axis=1)
    sin = jnp.concatenate([sin_ref[...]] * MLA_HEADS, axis=1)
    q = q * cos + q_swapped * sin
    k = k * cos + k_swapped * sin
    q_ref[0] = (q * ((MLA_NOPE + MLA_ROPE) ** -0.5 * LOG2_E)).astype(BF16)
    k_ref[0] = k.astype(BF16)
    v_ref[0] = v.astype(BF16)


def _mlaprep(px, pz, qn, kvn, wq, wqs, wk, wv, cos, sin, layer):
    b, t, _ = px.shape
    lz = pz.shape[1]
    tm = lz
    nx = t // tm
    wqk = MLA_HEADS * MLA_DK_PAD
    wvv = MLA_HEADS * MLA_DV

    def xspec(width, col):
        return pl.BlockSpec((1, tm, width), lambda bi, i: (bi, jnp.minimum(i, nx - 1), col // width))

    def zspec(width, col):
        return pl.BlockSpec((1, tm, width), lambda bi, i: (bi, 0, col // width))

    out_spec = pl.BlockSpec((1, tm, wqk), lambda bi, i: (bi, i, 0))
    kern = functools.partial(_mlaprep_kernel, nx=nx)
    return pl.pallas_call(
        kern,
        grid=(b, nx + 1),
        in_specs=[
            xspec(MLA_RANK, C_MLA_Q), xspec(MLA_RANK, C_MLA_KV), xspec(128, C_MLA_KR),
            zspec(MLA_RANK, C_MLA_Q), zspec(MLA_RANK, C_MLA_KV), zspec(128, C_MLA_KR),
            _const_spec((1, MLA_RANK)), _const_spec((1, MLA_RANK)),
            _layer_spec((MLA_RANK, wqk), layer), _layer_spec((MLA_RANK, wqk), layer),
            _layer_spec((MLA_RANK, wqk), layer), _layer_spec((MLA_RANK, wvv), layer),
            pl.BlockSpec((tm, 128), lambda bi, i: (i, 0)),
            pl.BlockSpec((tm, 128), lambda bi, i: (i, 0)),
        ],
        out_specs=[out_spec, out_spec, pl.BlockSpec((1, tm, wvv), lambda bi, i: (bi, i, 0))],
        out_shape=[jax.ShapeDtypeStruct((b, t + lz, wqk), BF16)] * 2
        + [jax.ShapeDtypeStruct((b, t + lz, wvv), BF16)],
        compiler_params=_cparams("parallel", "parallel"),
        name="mlaprep",
    )(px, px, px, pz, pz, pz, qn, kvn, wq, wqs, wk, wv, cos, sin)


def _flash_kernel(q_ref, k_ref, v_ref, o_ref, m_scr, l_scr, acc_scr, *, heads, dk, dv, scale, nk):
    ki = pl.program_id(2)
    tq = q_ref.shape[1]
    tk = k_ref.shape[1]
    lanes = 2 * dv

    @pl.when(ki == 0)
    def _():
        m_scr[...] = jnp.full(m_scr.shape, -jnp.inf, F32)
        l_scr[...] = jnp.zeros(l_scr.shape, F32)
        acc_scr[...] = jnp.zeros(acc_scr.shape, F32)

    low = lax.broadcasted_iota(jnp.int32, (tq, lanes), 1) < dv
    for pr in range(heads // 2):
        vp = v_ref[0, :, pr * lanes:(pr + 1) * lanes].astype(BF16)
        alphas, pvs = [], []
        for h in (2 * pr, 2 * pr + 1):
            qh = q_ref[0, :, h * dk:(h + 1) * dk]
            if scale != 1.0:
                qh = qh * scale
            kh = k_ref[0, :, h * dk:(h + 1) * dk]
            s = lax.dot_general(qh.astype(BF16), kh.astype(BF16), (((1,), (1,)), ((), ())),
                                preferred_element_type=F32)
            cols = [s[:, j * lanes:(j + 1) * lanes] for j in range(tk // lanes)]
            m_prev = m_scr[h]
            m_tile = jnp.max(functools.reduce(jnp.maximum, cols), axis=-1, keepdims=True)
            m_new = jnp.maximum(m_prev, m_tile)
            alpha = jnp.exp2(m_prev - m_new)
            ps = [jnp.exp2(cj - m_new) for cj in cols]
            l_scr[h] = alpha * l_scr[h] + functools.reduce(jnp.add, ps)
            m_scr[h] = m_new
            p = jnp.concatenate([pj.astype(BF16) for pj in ps], axis=1)
            pvs.append(jnp.dot(p, vp, preferred_element_type=F32))
            alphas.append(alpha)
        acc_scr[pr] = acc_scr[pr] * jnp.where(low, alphas[0], alphas[1]) + jnp.where(low, pvs[0], pvs[1])

    @pl.when(ki == nk - 1)
    def _():
        for pr in range(heads // 2):
            l0 = jnp.sum(l_scr[2 * pr], axis=-1, keepdims=True)
            l1 = jnp.sum(l_scr[2 * pr + 1], axis=-1, keepdims=True)
            o_ref[0, :, pr * lanes:(pr + 1) * lanes] = (acc_scr[pr] / jnp.where(low, l0, l1)).astype(o_ref.dtype)


def _flash(q, k, v, qcol, kcol, vcol, heads, dk, dv, scale, tq_pref, tk_pref):
    b, tq_all, _ = q.shape
    tk_all = k.shape[1]
    tq = _tile(tq_all, tq_pref)
    tk = _tile(tk_all, tk_pref)
    nk = tk_all // tk
    assert heads % 2 == 0 and 2 * dv == 128 and tk % 128 == 0
    kern = functools.partial(_flash_kernel, heads=heads, dk=dk, dv=dv, scale=scale, nk=nk)
    return pl.pallas_call(
        kern,
        grid=(b, tq_all // tq, nk),
        in_specs=[
            pl.BlockSpec((1, tq, heads * dk), lambda bi, i, j: (bi, i, qcol)),
            pl.BlockSpec((1, tk, heads * dk), lambda bi, i, j: (bi, j, kcol)),
            pl.BlockSpec((1, tk, heads * dv), lambda bi, i, j: (bi, j, vcol)),
        ],
        out_specs=pl.BlockSpec((1, tq, heads * dv), lambda bi, i, j: (bi, i, 0)),
        out_shape=jax.ShapeDtypeStruct((b, tq_all, heads * dv), BF16),
        scratch_shapes=[
            pltpu.VMEM((heads, tq, 2 * dv), F32),
            pltpu.VMEM((heads, tq, 2 * dv), F32),
            pltpu.VMEM((heads // 2, tq, 2 * dv), F32),
        ],
        compiler_params=_cparams("parallel", "parallel", "arbitrary"),
        name="flash",
    )(q, k, v)


def _mla_attn_kernel(q_ref, k_ref, v_ref, o_ref, m_scr, l_scr, acc_scr, *, heads, nk):
    ki = pl.program_id(2)
    w = MLA_DK_PAD
    tq = q_ref.shape[1]

    @pl.when(ki == 0)
    def _():
        m_scr[...] = jnp.full(m_scr.shape, -jnp.inf, F32)
        l_scr[...] = jnp.zeros(l_scr.shape, F32)
        acc_scr[...] = jnp.zeros(acc_scr.shape, F32)

    low = lax.broadcasted_iota(jnp.int32, (tq, w), 1) < MLA_DV
    for pr in range(heads // 2):
        vp = v_ref[0, :, pr * w:(pr + 1) * w]
        alphas, pvs = [], []
        for h in (2 * pr, 2 * pr + 1):
            sl = slice(h * w, (h + 1) * w)
            s = lax.dot_general(q_ref[0, :, sl], k_ref[0, :, sl], (((1,), (1,)), ((), ())),
                                preferred_element_type=F32)
            cols = [s[:, j * w:(j + 1) * w] for j in range(s.shape[1] // w)]
            m_prev = m_scr[h]
            m_new = jnp.maximum(m_prev, jnp.max(functools.reduce(jnp.maximum, cols), axis=-1, keepdims=True))
            alpha = jnp.exp2(m_prev - m_new)
            ps = [jnp.exp2(cj - m_new) for cj in cols]
            l_scr[h] = alpha * l_scr[h] + functools.reduce(jnp.add, ps)
            m_scr[h] = m_new
            p = jnp.concatenate([pj.astype(BF16) for pj in ps], axis=1)
            pvs.append(jnp.dot(p, vp, preferred_element_type=F32))
            alphas.append(alpha)
        acc_scr[pr] = acc_scr[pr] * jnp.where(low, alphas[0], alphas[1]) + jnp.where(low, pvs[0], pvs[1])

    @pl.when(ki == nk - 1)
    def _():
        for pr in range(heads // 2):
            l0 = jnp.sum(l_scr[2 * pr], axis=-1, keepdims=True)
            l1 = jnp.sum(l_scr[2 * pr + 1], axis=-1, keepdims=True)
            o_ref[0, :, pr * w:(pr + 1) * w] = (acc_scr[pr] / jnp.where(low, l0, l1)).astype(o_ref.dtype)


def _mla_attn(q, k, v, q_start, q_len, k_start, k_len, tq_pref, tk_pref):
    b, _, wq = q.shape
    heads = wq // MLA_DK_PAD
    tq = _tile(q_len, tq_pref, 8)
    tk = _tile(k_len, tk_pref, 128)
    assert q_start % tq == 0 and k_start % tk == 0
    q_off, k_off = q_start // tq, k_start // tk
    nk = k_len // tk
    kern = functools.partial(_mla_attn_kernel, heads=heads, nk=nk)
    return pl.pallas_call(
        kern,
        grid=(b, q_len // tq, nk),
        in_specs=[
            pl.BlockSpec((1, tq, wq), lambda bi, i, j: (bi, q_off + i, 0)),
            pl.BlockSpec((1, tk, wq), lambda bi, i, j: (bi, k_off + j, 0)),
            pl.BlockSpec((1, tk, heads * MLA_DV), lambda bi, i, j: (bi, k_off + j, 0)),
        ],
        out_specs=pl.BlockSpec((1, tq, heads * MLA_DV), lambda bi, i, j: (bi, i, 0)),
        out_shape=jax.ShapeDtypeStruct((b, q_len, heads * MLA_DV), BF16),
        scratch_shapes=[
            pltpu.VMEM((heads, tq, MLA_DK_PAD), F32),
            pltpu.VMEM((heads, tq, MLA_DK_PAD), F32),
            pltpu.VMEM((heads // 2, tq, MLA_DK_PAD), F32),
        ],
        compiler_params=_cparams("parallel", "parallel", "arbitrary"),
        name="mla_attn",
    )(q, k, v)


def _layer_norm(r, gain, bias):
    mu = jnp.mean(r, axis=-1, keepdims=True)
    rc = r - mu
    var = jnp.mean(rc * rc, axis=-1, keepdims=True)
    return rc * lax.rsqrt(var + EPS) * gain + bias


def _post_kernel(yaf_ref, yab_ref, yb_ref, yc_ref, ga_ref, gb_ref, gc_ref, x_ref, g1_ref, sc_ref, sh_ref, g2_ref,
                 wa_ref, wb_ref, wc_ref, wo_ref, w1_ref, w2_ref, lng_ref, lnb_ref, o_ref, *, ff_chunk):
    ya = (yaf_ref[0] + yab_ref[0]).astype(BF16)
    y = (jax.nn.sigmoid(ga_ref[0]) * jnp.dot(ya, wa_ref[0], preferred_element_type=F32)
         + jax.nn.sigmoid(gb_ref[0]) * jnp.dot(yb_ref[0].astype(BF16), wb_ref[0], preferred_element_type=F32)
         + jax.nn.sigmoid(gc_ref[0]) * jnp.dot(yc_ref[0].astype(BF16), wc_ref[0], preferred_element_type=F32))
    mix = jnp.dot(y.astype(BF16), wo_ref[0], preferred_element_type=F32)
    x1 = _layer_norm(DEEPNORM_ALPHA * x_ref[0] + g1_ref[0] * mix, lng_ref[0:1], lnb_ref[0:1])
    h = (x1 * (1.0 + sc_ref[0]) + sh_ref[0]).astype(BF16)
    acc = jnp.zeros(x1.shape, F32)
    for j in range(D_FF // ff_chunk):
        u = jnp.dot(h, w1_ref[0, :, j * ff_chunk:(j + 1) * ff_chunk], preferred_element_type=F32)
        u = jnp.square(jnp.maximum(u, 0.0)).astype(BF16)
        acc = acc + jnp.dot(u, w2_ref[0, j * ff_chunk:(j + 1) * ff_chunk, :], preferred_element_type=F32)
    o_ref[0] = _layer_norm(DEEPNORM_ALPHA * x1 + g2_ref[0] * acc, lng_ref[1:2], lnb_ref[1:2])


def _post(yaf, yab, yb, yc, p, x, g1, sc2, sh2, g2, wa, wb, wc, wo, w1, w2, lng, lnb, layer):
    b, t, d = x.shape
    tm = _tile(t, 512)
    wbr = 512
    row = lambda width, col: pl.BlockSpec((1, tm, width), lambda bi, i: (bi, i, col // width))
    mod = pl.BlockSpec((1, 1, d), lambda bi, i: (bi, 0, 0))
    kern = functools.partial(_post_kernel, ff_chunk=1024)
    return pl.pallas_call(
        kern,
        grid=(b, t // tm),
        in_specs=[
            row(wbr, 0), row(wbr, 0), row(wbr, 0), row(wbr, 0),
            row(d, C_GATE_A), row(d, C_GATE_B), row(d, C_GATE_C), row(d, 0),
            mod, mod, mod, mod,
            _layer_spec((wbr, d), layer), _layer_spec((wbr, d), layer), _layer_spec((wbr, d), layer),
            _layer_spec((d, d), layer), _layer_spec((d, D_FF), layer), _layer_spec((D_FF, d), layer),
            _const_spec((2, d)), _const_spec((2, d)),
        ],
        out_specs=row(d, 0),
        out_shape=jax.ShapeDtypeStruct((b, t, d), F32),
        compiler_params=_cparams("parallel", "parallel"),
        name="post",
    )(yaf, yab, yb, yc, p, p, p, x, g1, sc2, sh2, g2, wa, wb, wc, wo, w1, w2, lng, lnb)


def _rope_tables(n_tok, rot_dim):
    t = jnp.arange(n_tok)
    row = (t // GRID_W).astype(F32)
    col = (t % GRID_W).astype(F32)
    n_freq = rot_dim // 4
    inv_freq = ROPE_BASE ** (-2.0 * jnp.arange(n_freq, dtype=F32) / (rot_dim // 2))
    ang = jnp.concatenate([row[:, None] * inv_freq, col[:, None] * inv_freq], axis=-1)
    return jnp.cos(ang), jnp.sin(ang)


def _ret_rope_tables(n_tok):
    cos, sin = _rope_tables(n_tok, RET_DK)
    cos_h = jnp.concatenate([cos, cos], axis=1)
    sin_h = jnp.concatenate([-sin, sin], axis=1)
    return jnp.tile(cos_h, (1, 2)), jnp.tile(sin_h, (1, 2))


def _mla_rope_tables(n_tok):
    cos, sin = _rope_tables(n_tok, MLA_ROPE)
    ones = jnp.ones((n_tok, MLA_NOPE), F32)
    zeros = jnp.zeros((n_tok, MLA_NOPE), F32)
    pad = jnp.zeros((n_tok, MLA_DK_PAD - MLA_NOPE - MLA_ROPE), F32)
    cos_h = jnp.concatenate([ones, cos, cos, pad], axis=1)
    sin_h = jnp.concatenate([zeros, -sin, sin, pad], axis=1)
    return cos_h, sin_h


def _pack_w_in(w):
    w = w.astype(BF16)
    pad = jnp.zeros(w.shape[:2] + (P_WIDTH - 7200,), BF16)
    return jnp.concatenate([w[..., :4096], w[..., 4128:7200], w[..., 4096:4128], pad], axis=-1)


def _pack_mla_weights(w_qup, w_kvup):
    depth, r = w_qup.shape[:2]
    wq = w_qup.reshape(depth, r, MLA_HEADS, MLA_NOPE + MLA_ROPE)
    half = MLA_ROPE // 2
    wqs = jnp.concatenate([jnp.zeros((depth, r, MLA_HEADS, MLA_NOPE), wq.dtype), wq[..., MLA_NOPE + half:],
                           wq[..., MLA_NOPE:MLA_NOPE + half]], axis=-1)
    pad_q = ((0, 0), (0, 0), (0, 0), (0, MLA_DK_PAD - MLA_NOPE - MLA_ROPE))
    wq = jnp.pad(wq, pad_q).reshape(depth, r, MLA_HEADS * MLA_DK_PAD)
    wqs = jnp.pad(wqs, pad_q).reshape(depth, r, MLA_HEADS * MLA_DK_PAD)
    wkv = w_kvup.reshape(depth, r, MLA_HEADS, MLA_NOPE + MLA_DV)
    wk = jnp.pad(wkv[..., :MLA_NOPE], ((0, 0), (0, 0), (0, 0), (0, MLA_DK_PAD - MLA_NOPE)))
    wk = wk.reshape(depth, r, MLA_HEADS * MLA_DK_PAD)
    wv = wkv[..., MLA_NOPE:].reshape(depth, r, MLA_HEADS * MLA_DV)
    return wq.astype(BF16), wqs.astype(BF16), wk.astype(BF16), wv.astype(BF16)


def kernel(x, c, ctx, c_ctx, w_ada, b_ada, w_in, ret_log_decay, ret_gn_gain, na_rpb, mla_q_norm, mla_w_qup,
           mla_kv_norm, mla_w_kvup, w_branch_ret, w_branch_na, w_branch_mla, w_out, w_ff1, w_ff2, ln_gain, ln_bias):
    depth = w_ada.shape[0]
    b, t, d = x.shape
    lz = ctx.shape[1]
    rows = t // GRID_W

    cc = jnp.zeros((8, d), F32).at[:b].set(c).at[b].set(c_ctx)
    mod = _ada(cc, w_ada, b_ada)

    cos_r, sin_r = _ret_rope_tables(t)
    cos_m, sin_m = _mla_rope_tables(t)
    cos_m = jnp.concatenate([cos_m, jnp.ones((lz, MLA_DK_PAD), F32)], axis=0)
    sin_m = jnp.concatenate([sin_m, jnp.zeros((lz, MLA_DK_PAD), F32)], axis=0)
    cos_rz, sin_rz = cos_r[:lz], sin_r[:lz]
    na_bias = _na_bias_tables(na_rpb, rows)
    s_zero = jnp.zeros((b, 2, RET_HEADS * RET_DK, RET_DV), F32)

    w_in_p = _pack_w_in(w_in)
    wq, wqs, wk, wv = _pack_mla_weights(mla_w_qup, mla_w_kvup)
    wa = w_branch_ret.astype(BF16)
    wb = w_branch_na.astype(BF16)
    wc = w_branch_mla.astype(BF16)
    wo = w_out.astype(BF16)
    w1 = w_ff1.astype(BF16)
    w2 = w_ff2.astype(BF16)

    z = ctx
    for l in range(depth):
        need_ctx = l < depth - 1
        mx = mod[l, :b].reshape(b, 6, 1, d)
        mz = jnp.broadcast_to(mod[l, b].reshape(1, 6, 1, d), (b, 6, 1, d))
        sh1x, sc1x, g1x, sh2x, sc2x, g2x = [mx[:, i] for i in range(6)]
        sh1z, sc1z, g1z, sh2z, sc2z, g2z = [mz[:, i] for i in range(6)]

        px = _inproj(x, sc1x, sh1x, w_in_p, l)
        pz = _inproj(z, sc1z, sh1z, w_in_p, l)

        lg = jnp.log1p(-jnp.exp(ret_log_decay[l].astype(F32))).reshape(2 * RET_HEADS)
        gn_gain = ret_gn_gain[l].reshape(1, RET_HEADS * RET_DV)
        yaf_z, yab_z, s_ctx = _retention(pz, lg, cos_rz, sin_rz, gn_gain, s_zero, use_rope=False)
        yaf_x, yab_x, _ = _retention(px, lg, cos_r, sin_r, gn_gain, s_ctx, use_rope=True)

        yb_x = _na(px, pz, na_bias, l)

        qn = mla_q_norm[l].reshape(1, MLA_RANK)
        kvn = mla_kv_norm[l].reshape(1, MLA_RANK)
        q_all, k_all, v_all = _mlaprep(px, pz, qn, kvn, wq, wqs, wk, wv, cos_m, sin_m, l)
        yc_x = _mla_attn(q_all, k_all, v_all, 0, t, 0, t + lz, 1024, 2816)

        x = _post(yaf_x, yab_x, yb_x, yc_x, px, x, g1x, sc2x, sh2x, g2x, wa, wb, wc, wo, w1, w2,
                  ln_gain[l], ln_bias[l], l)

        if need_ctx:
            wna = NA_HEADS * NA_DH
            yb_z = _flash(pz, pz, pz, C_NA_Q // wna, C_NA_K // wna, C_NA_V // wna,
                          NA_HEADS, NA_DH, NA_DH, NA_DH ** -0.5 * LOG2_E, 256, 256)
            yc_z = _mla_attn(q_all, k_all, v_all, t, lz, t, lz, lz, lz)
            z = _post(yaf_z, yab_z, yb_z, yc_z, pz, z, g1z, sc2z, sh2z, g2z, wa, wb, wc, wo, w1, w2,
                      ln_gain[l], ln_bias[l], l)
    return x
```

```python
import functools

import numpy as np
import jax
import jax.numpy as jnp
from jax import lax
from jax.experimental import pallas as pl
from jax.experimental.pallas import tpu as pltpu

F32 = jnp.float32
BF16 = jnp.bfloat16

D_MODEL = 1024
GRID_W = 64
RET_HEADS = 4
RET_DK = 64
RET_DV = 128
RET_CHUNK = 512
NA_HEADS = 8
NA_DH = 64
NA_KH = 8
NA_KW = 16
MLA_HEADS = 8
MLA_RANK = 256
MLA_NOPE = 64
MLA_ROPE = 32
MLA_DV = 64
MLA_DK_PAD = 128
D_FF = 4 * D_MODEL
ROPE_BASE = 10000.0
EPS = 1e-5
DEPTH_FOR_NORM = 4
DEEPNORM_ALPHA = (2 * DEPTH_FOR_NORM) ** 0.25
MASK_VALUE = -1e30
LOG2_E = 1.4426950408889634

C_RET_Q, C_RET_K, C_RET_V, C_RET_GF, C_RET_GB = 0, 256, 512, 1024, 1536
C_NA_Q, C_NA_K, C_NA_V = 2048, 2560, 3072
C_MLA_Q, C_MLA_KV = 3584, 3840
C_GATE_A, C_GATE_B, C_GATE_C = 4096, 5120, 6144
C_MLA_KR = 7168
P_WIDTH = 7296
P_COL_TILE = 2432

NA_QROWS = 4
NA_KROWS = 12
NA_QB = NA_QROWS * GRID_W
NA_KB = NA_KROWS * GRID_W

VMEM_LIMIT = 56 * 1024 * 1024


def _cparams(*sem):
    return pltpu.CompilerParams(dimension_semantics=sem, vmem_limit_bytes=VMEM_LIMIT)


def _tile(n, pref, mult=1):
    t = min(n, pref) // mult * mult
    while n % t:
        t -= mult
    return t


def _const_spec(shape):
    nd = len(shape)
    return pl.BlockSpec(shape, lambda *_: (0,) * nd)


def _layer_spec(shape, layer):
    nd = len(shape)
    return pl.BlockSpec((1,) + tuple(shape), lambda *_: (layer,) + (0,) * nd)


def _ada_kernel(c_ref, w_ref, b_ref, o_ref):
    c = c_ref[...]
    a = c * jax.nn.sigmoid(c)
    o_ref[0] = jnp.dot(a.astype(BF16), w_ref[0].astype(BF16), preferred_element_type=F32) + b_ref[0]


def _ada(cc, w_ada, b_ada):
    depth, d, n = w_ada.shape
    tn = _tile(n, 3072, 128)
    return pl.pallas_call(
        _ada_kernel,
        grid=(depth, n // tn),
        in_specs=[
            pl.BlockSpec((8, d), lambda l, j: (0, 0)),
            pl.BlockSpec((1, d, tn), lambda l, j: (l, 0, j)),
            pl.BlockSpec((1, 1, tn), lambda l, j: (l, 0, j)),
        ],
        out_specs=pl.BlockSpec((1, 8, tn), lambda l, j: (l, 0, j)),
        out_shape=jax.ShapeDtypeStruct((depth, 8, n), F32),
        compiler_params=_cparams("parallel", "parallel"),
        name="ada",
    )(cc, w_ada, b_ada.reshape(depth, 1, n))


def _inproj_kernel(x_ref, sc_ref, sh_ref, w_ref, o_ref):
    h = x_ref[0] * (1.0 + sc_ref[0]) + sh_ref[0]
    o_ref[0] = jnp.dot(h.astype(BF16), w_ref[0], preferred_element_type=F32)


def _inproj(x, sc, sh, w, layer):
    b, t, d = x.shape
    tm = _tile(t, 512)
    tn = P_COL_TILE
    return pl.pallas_call(
        _inproj_kernel,
        grid=(P_WIDTH // tn, b, t // tm),
        in_specs=[
            pl.BlockSpec((1, tm, d), lambda j, bi, i: (bi, i, 0)),
            pl.BlockSpec((1, 1, d), lambda j, bi, i: (bi, 0, 0)),
            pl.BlockSpec((1, 1, d), lambda j, bi, i: (bi, 0, 0)),
            pl.BlockSpec((1, d, tn), lambda j, bi, i: (layer, 0, j)),
        ],
        out_specs=pl.BlockSpec((1, tm, tn), lambda j, bi, i: (bi, i, j)),
        out_shape=jax.ShapeDtypeStruct((b, t, P_WIDTH), F32),
        compiler_params=_cparams("parallel", "parallel", "parallel"),
        name="inproj",
    )(x, sc, sh, w)


def _swap_halves(x, half):
    n = x.shape[-1]
    lane = lax.broadcasted_iota(jnp.int32, x.shape, x.ndim - 1)
    first = (lane % (2 * half)) < half
    return jnp.where(first, pltpu.roll(x, n - half, x.ndim - 1), pltpu.roll(x, half, x.ndim - 1))


def _ret_kernel(lg_ref, qf_ref, qb_ref, kf_ref, kb_ref, vf_ref, vb_ref, gf_ref, gb_ref,
                cosf_ref, cosb_ref, sinf_ref, sinb_ref, gain_ref, s0_ref,
                yf_ref, yb_ref, sf_ref, s_scr, *, use_rope, n_chunks):
    c = pl.program_id(0)
    batch, cc = qf_ref.shape[:2]

    @pl.when(c == 0)
    def _():
        s_scr[...] = s0_ref[...]

    gain = gain_ref[...]
    row = lax.broadcasted_iota(jnp.int32, (cc, cc), 0).astype(F32)
    col = lax.broadcasted_iota(jnp.int32, (cc, cc), 1).astype(F32)
    wqk = RET_HEADS * RET_DK
    pos = lax.broadcasted_iota(jnp.int32, (cc, wqk), 0).astype(F32)
    head_of_lane = lax.broadcasted_iota(jnp.int32, (cc, wqk), 1) // RET_DK
    dirs = (
        (qf_ref, kf_ref, vf_ref, gf_ref, cosf_ref, sinf_ref, yf_ref, row - col, pos + 1.0, cc - 1.0 - pos),
        (qb_ref, kb_ref, vb_ref, gb_ref, cosb_ref, sinb_ref, yb_ref, col - row, cc - pos, pos),
    )
    for d, (q_ref, k_ref, v_ref, g_ref, cos_ref, sin_ref, y_ref, diff, q_exp, k_exp) in enumerate(dirs):
        if use_rope:
            cos = jnp.concatenate([cos_ref[...], cos_ref[...]], axis=1)
            sin = jnp.concatenate([sin_ref[...], sin_ref[...]], axis=1)
        lgs = [lg_ref[d * RET_HEADS + h] for h in range(RET_HEADS)]
        lg_lanes = jnp.full((cc, wqk), lgs[RET_HEADS - 1], F32)
        for h in range(RET_HEADS - 2, -1, -1):
            lg_lanes = jnp.where(head_of_lane == h, lgs[h], lg_lanes)
        q_decay = jnp.exp(lg_lanes * q_exp)
        k_decay = jnp.exp(lg_lanes * k_exp)
        for b in range(batch):
            q = q_ref[b]
            k = k_ref[b] * (RET_DK ** -0.5)
            if use_rope:
                q = q * cos + _swap_halves(q, RET_DK // 2) * sin
                k = k * cos + _swap_halves(k, RET_DK // 2) * sin
            k_bf = k.astype(BF16)
            kt_decayed = (k * k_decay).T.astype(BF16)
            q_decayed = q * q_decay
            state_bf = s_scr[b, d].astype(BF16)
            for h in range(RET_HEADS):
                mine = head_of_lane == h
                sl = slice(h * RET_DV, (h + 1) * RET_DV)
                rows = slice(h * RET_DK, (h + 1) * RET_DK)
                decay = jnp.where(diff >= 0, jnp.exp(lgs[h] * jnp.maximum(diff, 0.0)), 0.0)
                vh = v_ref[b, :, sl].astype(BF16)
                scores = lax.dot_general(jnp.where(mine, q, 0.0).astype(BF16), k_bf, (((1,), (1,)), ((), ())),
                                         preferred_element_type=F32) * decay
                inner = jnp.dot(scores.astype(BF16), vh, preferred_element_type=F32)
                cross = jnp.dot(jnp.where(mine, q_decayed, 0.0).astype(BF16), state_bf, preferred_element_type=F32)
                o = inner + cross
                kv = jnp.dot(kt_decayed[rows], vh, preferred_element_type=F32)
                chunk_decay = jnp.exp(lgs[h] * jnp.full((RET_DK, RET_DV), float(cc), F32))
                s_scr[b, d, rows] = s_scr[b, d, rows] * chunk_decay + kv
                mu = jnp.mean(o, axis=-1, keepdims=True)
                oc = o - mu
                var = jnp.mean(oc * oc, axis=-1, keepdims=True)
                gate = g_ref[b, :, sl]
                y_ref[b, :, sl] = (gate * jax.nn.sigmoid(gate)) * (oc * lax.rsqrt(var + EPS) * gain[:, sl])

    @pl.when(c == n_chunks - 1)
    def _():
        sf_ref[...] = s_scr[...]


def _retention(p, lg, cos, sin, gain, s0, use_rope):
    b, t, _ = p.shape
    cc = _tile(t, RET_CHUNK, 8)
    n = t // cc
    wv = RET_HEADS * RET_DV

    def pspec(width, col, backward):
        if backward:
            return pl.BlockSpec((b, cc, width), lambda ci, lg_: (0, n - 1 - ci, col // width))
        return pl.BlockSpec((b, cc, width), lambda ci, lg_: (0, ci, col // width))

    def tspec(backward):
        if backward:
            return pl.BlockSpec((cc, 128), lambda ci, lg_: (n - 1 - ci, 0))
        return pl.BlockSpec((cc, 128), lambda ci, lg_: (ci, 0))

    state_spec = pl.BlockSpec((b, 2, RET_HEADS * RET_DK, RET_DV), lambda ci, lg_: (0, 0, 0, 0))
    kern = functools.partial(_ret_kernel, use_rope=use_rope, n_chunks=n)
    grid_spec = pltpu.PrefetchScalarGridSpec(
        num_scalar_prefetch=1,
        grid=(n,),
        in_specs=[
            pspec(256, C_RET_Q, False), pspec(256, C_RET_Q, True),
            pspec(256, C_RET_K, False), pspec(256, C_RET_K, True),
            pspec(wv, C_RET_V, False), pspec(wv, C_RET_V, True),
            pspec(wv, C_RET_GF, False), pspec(wv, C_RET_GB, True),
            tspec(False), tspec(True), tspec(False), tspec(True),
            pl.BlockSpec((1, wv), lambda ci, lg_: (0, 0)),
            state_spec,
        ],
        out_specs=[
            pl.BlockSpec((b, cc, wv), lambda ci, lg_: (0, ci, 0)),
            pl.BlockSpec((b, cc, wv), lambda ci, lg_: (0, n - 1 - ci, 0)),
            state_spec,
        ],
        scratch_shapes=[pltpu.VMEM((b, 2, RET_HEADS * RET_DK, RET_DV), F32)],
    )
    return pl.pallas_call(
        kern,
        grid_spec=grid_spec,
        out_shape=[
            jax.ShapeDtypeStruct((b, t, wv), F32),
            jax.ShapeDtypeStruct((b, t, wv), F32),
            jax.ShapeDtypeStruct((b, 2, RET_HEADS * RET_DK, RET_DV), F32),
        ],
        compiler_params=_cparams("arbitrary"),
        name="retention",
    )(lg, p, p, p, p, p, p, p, p, cos, cos, sin, sin, gain, s0)


def _na_tables(rows):
    groups = rows // NA_QROWS
    cols = np.arange(GRID_W)
    c0 = np.clip(cols - NA_KW // 2, 0, GRID_W - NA_KW)
    col_ok = (cols[None, :] >= c0[:, None]) & (cols[None, :] < c0[:, None] + NA_KW)
    dc = cols[None, :] - cols[:, None] + (NA_KW - 1)
    onehot = (dc[None] == np.arange(2 * NA_KW - 1)[:, None, None]) & col_ok[None]
    row_bias = np.full((3, NA_QROWS, NA_KROWS), 2 * NA_KH - 1, np.int64)
    for ti, g in enumerate((0, 1, groups - 1)):
        ws = int(np.clip(NA_QROWS * g - NA_KH // 2, 0, rows - NA_KROWS))
        for lr in range(NA_QROWS):
            r = NA_QROWS * g + lr
            r0 = int(np.clip(r - NA_KH // 2, 0, rows - NA_KH))
            for kr in range(NA_KROWS):
                if r0 <= ws + kr < r0 + NA_KH:
                    row_bias[ti, lr, kr] = ws + kr - r + (NA_KH - 1)
    return onehot, col_ok, row_bias


def _na_bias_tables(na_rpb, rows):
    onehot, col_ok, row_bias = _na_tables(rows)
    depth, heads = na_rpb.shape[:2]
    nd = 2 * NA_KH
    toep = jnp.einsum('lhdj,jck->lhdck', na_rpb, onehot.astype(np.float32), precision=lax.Precision.HIGHEST)
    toep = jnp.where(col_ok, toep * LOG2_E, MASK_VALUE)
    masked = jnp.full(toep.shape[:2] + (1, GRID_W, GRID_W), MASK_VALUE, F32)
    toep = jnp.concatenate([toep, masked], axis=2)
    sides = jnp.concatenate([jnp.pad(toep, ((0, 0),) * 4 + ((0, GRID_W),)),
                             jnp.pad(toep, ((0, 0),) * 4 + ((GRID_W, 0),))], axis=2)
    pairs = row_bias.reshape(-1, 2)
    pick = ((pairs[:, :1] == np.arange(nd)[None, :]).astype(np.float32),
            (pairs[:, 1:] == np.arange(nd)[None, :]).astype(np.float32))
    pick = np.concatenate(pick, axis=1)
    tab = jnp.einsum('nd,lhdck->lhnck', pick, sides, precision=lax.Precision.HIGHEST)
    return tab.reshape(depth, heads, 3, NA_QROWS * NA_KROWS // 2, GRID_W, 2 * GRID_W)


def _na_kernel(q_ref, k0_ref, k1_ref, k2_ref, v0_ref, v1_ref, v2_ref, kz_ref, vz_ref, bias_ref, o_ref):
    lanes = 2 * NA_DH
    npair = NA_KROWS // 2
    low = lax.broadcasted_iota(jnp.int32, (q_ref.shape[1], lanes), 1) < NA_DH
    for pr in range(NA_HEADS // 2):
        sl = slice(pr * lanes, (pr + 1) * lanes)
        qp = q_ref[0, :, sl] * (NA_DH ** -0.5 * LOG2_E)
        qm = (jnp.where(low, qp, 0.0).astype(BF16), jnp.where(low, 0.0, qp).astype(BF16))
        ks = [r[0, :, sl].astype(BF16) for r in (k0_ref, k1_ref, k2_ref, kz_ref)]
        vs = [r[0, :, sl].astype(BF16) for r in (v0_ref, v1_ref, v2_ref, vz_ref)]
        outs = []
        for hh in range(2):
            h = 2 * pr + hh
            cols = []
            for j in range(4):
                sj = lax.dot_general(qm[hh], ks[j], (((1,), (1,)), ((), ())), preferred_element_type=F32)
                halves = [sj[:, i * lanes:(i + 1) * lanes] for i in range(sj.shape[1] // lanes)]
                if j < 3:
                    halves = [hv + jnp.concatenate([bias_ref[0, h, 0, lr * npair + j * len(halves) + i]
                                                    for lr in range(NA_QROWS)], axis=0)
                              for i, hv in enumerate(halves)]
                cols += halves
            m = jnp.max(functools.reduce(jnp.maximum, cols), axis=-1, keepdims=True)
            ps = [jnp.exp2(cj - m) for cj in cols]
            l = jnp.sum(functools.reduce(jnp.add, ps), axis=-1, keepdims=True)
            o = None
            per = len(cols) // 4
            for j in range(4):
                pj = jnp.concatenate([pc.astype(BF16) for pc in ps[j * per:(j + 1) * per]], axis=1)
                oj = jnp.dot(pj, vs[j], preferred_element_type=F32)
                o = oj if o is None else o + oj
            outs.append(o / l)
        o_ref[0, :, sl] = jnp.where(low, outs[0], outs[1]).astype(o_ref.dtype)


def _na(px, pz, bias, layer):
    b, t, _ = px.shape
    lz = pz.shape[1]
    groups = t // NA_QB
    w = NA_HEADS * NA_DH
    kblk = NA_KB // 3
    assert lz == kblk and groups >= 3

    def kspec(col, off):
        return pl.BlockSpec((1, kblk, w), lambda bi, g: (bi, jnp.clip(g - 1, 0, groups - 3) + off, col // w))

    def tab(g):
        return jnp.where(g == 0, 0, jnp.where(g == groups - 1, 2, 1))

    return pl.pallas_call(
        _na_kernel,
        grid=(b, groups),
        in_specs=[
            pl.BlockSpec((1, NA_QB, w), lambda bi, g: (bi, g, C_NA_Q // w)),
            kspec(C_NA_K, 0), kspec(C_NA_K, 1), kspec(C_NA_K, 2),
            kspec(C_NA_V, 0), kspec(C_NA_V, 1), kspec(C_NA_V, 2),
            pl.BlockSpec((1, lz, w), lambda bi, g: (bi, 0, C_NA_K // w)),
            pl.BlockSpec((1, lz, w), lambda bi, g: (bi, 0, C_NA_V // w)),
            pl.BlockSpec((1, NA_HEADS, 1) + bias.shape[3:], lambda bi, g: (layer, 0, tab(g), 0, 0, 0)),
        ],
        out_specs=pl.BlockSpec((1, NA_QB, w), lambda bi, g: (bi, g, 0)),
        out_shape=jax.ShapeDtypeStruct((b, t, w), BF16),
        compiler_params=_cparams("parallel", "arbitrary"),
        name="na",
    )(px, px, px, px, px, px, px, pz, pz, bias)


def _rms(x, gain):
    return x * lax.rsqrt(jnp.mean(x * x, axis=-1, keepdims=True) + EPS) * gain


def _mlaprep_kernel(xq_ref, xkv_ref, xkr_ref, zq_ref, zkv_ref, zkr_ref, qn_ref, kvn_ref, wq_ref, wqs_ref, wk_ref,
                    wv_ref, cos_ref, sin_ref, q_ref, k_ref, v_ref, *, nx):
    latent = pl.program_id(1) < nx
    pq = jnp.where(latent, xq_ref[0], zq_ref[0])
    pkv = jnp.where(latent, xkv_ref[0], zkv_ref[0])
    pkr = jnp.where(latent, xkr_ref[0], zkr_ref[0])
    hq = _rms(pq, qn_ref[...]).astype(BF16)
    hkv = _rms(pkv, kvn_ref[...]).astype(BF16)
    q = jnp.dot(hq, wq_ref[0], preferred_element_type=F32)
    q_swapped = jnp.dot(hq, wqs_ref[0], preferred_element_type=F32)
    k = jnp.dot(hkv, wk_ref[0], preferred_element_type=F32)
    v = jnp.dot(hkv, wv_ref[0], preferred_element_type=F32)
    kr = pltpu.roll(pkr, MLA_NOPE, 1)
    k = k + jnp.concatenate([kr] * MLA_HEADS, axis=1)
    k_swapped = jnp.concatenate([_swap_halves(kr, MLA_ROPE // 2)] * MLA_HEADS, axis=1)
    cos = jnp.concatenate([cos_ref[...]] * MLA_HEADS, axis=1)
    sin = jnp.concatenate([sin_ref[...]] * MLA_HEADS, axis=1)
    q = q * cos + q_swapped * sin
    k = k * cos + k_swapped * sin
    q_ref[0] = (q * ((MLA_NOPE + MLA_ROPE) ** -0.5 * LOG2_E)).astype(BF16)
    k_ref[0] = k.astype(BF16)
    v_ref[0] = v.astype(BF16)


def _mlaprep(px, pz, qn, kvn, wq, wqs, wk, wv, cos, sin, layer):
    b, t, _ = px.shape
    lz = pz.shape[1]
    tm = lz
    nx = t // tm
    wqk = MLA_HEADS * MLA_DK_PAD
    wvv = MLA_HEADS * MLA_DV

    def xspec(width, col):
        return pl.BlockSpec((1, tm, width), lambda bi, i: (bi, jnp.minimum(i, nx - 1), col // width))

    def zspec(width, col):
        return pl.BlockSpec((1, tm, width), lambda bi, i: (bi, 0, col // width))

    out_spec = pl.BlockSpec((1, tm, wqk), lambda bi, i: (bi, i, 0))
    kern = functools.partial(_mlaprep_kernel, nx=nx)
    return pl.pallas_call(
        kern,
        grid=(b, nx + 1),
        in_specs=[
            xspec(MLA_RANK, C_MLA_Q), xspec(MLA_RANK, C_MLA_KV), xspec(128, C_MLA_KR),
            zspec(MLA_RANK, C_MLA_Q), zspec(MLA_RANK, C_MLA_KV), zspec(128, C_MLA_KR),
            _const_spec((1, MLA_RANK)), _const_spec((1, MLA_RANK)),
            _layer_spec((MLA_RANK, wqk), layer), _layer_spec((MLA_RANK, wqk), layer),
            _layer_spec((MLA_RANK, wqk), layer), _layer_spec((MLA_RANK, wvv), layer),
            pl.BlockSpec((tm, 128), lambda bi, i: (i, 0)),
            pl.BlockSpec((tm, 128), lambda bi, i: (i, 0)),
        ],
        out_specs=[out_spec, out_spec, pl.BlockSpec((1, tm, wvv), lambda bi, i: (bi, i, 0))],
        out_shape=[jax.ShapeDtypeStruct((b, t + lz, wqk), BF16)] * 2
        + [jax.ShapeDtypeStruct((b, t + lz, wvv), BF16)],
        compiler_params=_cparams("parallel", "parallel"),
        name="mlaprep",
    )(px, px, px, pz, pz, pz, qn, kvn, wq, wqs, wk, wv, cos, sin)


def _flash_kernel(q_ref, k_ref, v_ref, o_ref, m_scr, l_scr, acc_scr, *, heads, dk, dv, scale, nk):
    ki = pl.program_id(2)
    tq = q_ref.shape[1]
    tk = k_ref.shape[1]
    lanes = 2 * dv

    @pl.when(ki == 0)
    def _():
        m_scr[...] = jnp.full(m_scr.shape, -jnp.inf, F32)
        l_scr[...] = jnp.zeros(l_scr.shape, F32)
        acc_scr[...] = jnp.zeros(acc_scr.shape, F32)

    low = lax.broadcasted_iota(jnp.int32, (tq, lanes), 1) < dv
    for pr in range(heads // 2):
        vp = v_ref[0, :, pr * lanes:(pr + 1) * lanes].astype(BF16)
        alphas, pvs = [], []
        for h in (2 * pr, 2 * pr + 1):
            qh = q_ref[0, :, h * dk:(h + 1) * dk]
            if scale != 1.0:
                qh = qh * scale
            kh = k_ref[0, :, h * dk:(h + 1) * dk]
            s = lax.dot_general(qh.astype(BF16), kh.astype(BF16), (((1,), (1,)), ((), ())),
                                preferred_element_type=F32)
            cols = [s[:, j * lanes:(j + 1) * lanes] for j in range(tk // lanes)]
            m_prev = m_scr[h]
            m_tile = jnp.max(functools.reduce(jnp.maximum, cols), axis=-1, keepdims=True)
            m_new = jnp.maximum(m_prev, m_tile)
            alpha = jnp.exp2(m_prev - m_new)
            ps = [jnp.exp2(cj - m_new) for cj in cols]
            l_scr[h] = alpha * l_scr[h] + functools.reduce(jnp.add, ps)
            m_scr[h] = m_new
            p = jnp.concatenate([pj.astype(BF16) for pj in ps], axis=1)
            pvs.append(jnp.dot(p, vp, preferred_element_type=F32))
            alphas.append(alpha)
        acc_scr[pr] = acc_scr[pr] * jnp.where(low, alphas[0], alphas[1]) + jnp.where(low, pvs[0], pvs[1])

    @pl.when(ki == nk - 1)
    def _():
        for pr in range(heads // 2):
            l0 = jnp.sum(l_scr[2 * pr], axis=-1, keepdims=True)
            l1 = jnp.sum(l_scr[2 * pr + 1], axis=-1, keepdims=True)
            o_ref[0, :, pr * lanes:(pr + 1) * lanes] = (acc_scr[pr] / jnp.where(low, l0, l1)).astype(o_ref.dtype)


def _flash(q, k, v, qcol, kcol, vcol, heads, dk, dv, scale, tq_pref, tk_pref):
    b, tq_all, _ = q.shape
    tk_all = k.shape[1]
    tq = _tile(tq_all, tq_pref)
    tk = _tile(tk_all, tk_pref)
    nk = tk_all // tk
    assert heads % 2 == 0 and 2 * dv == 128 and tk % 128 == 0
    kern = functools.partial(_flash_kernel, heads=heads, dk=dk, dv=dv, scale=scale, nk=nk)
    return pl.pallas_call(
        kern,
        grid=(b, tq_all // tq, nk),
        in_specs=[
            pl.BlockSpec((1, tq, heads * dk), lambda bi, i, j: (bi, i, qcol)),
            pl.BlockSpec((1, tk, heads * dk), lambda bi, i, j: (bi, j, kcol)),
            pl.BlockSpec((1, tk, heads * dv), lambda bi, i, j: (bi, j, vcol)),
        ],
        out_specs=pl.BlockSpec((1, tq, heads * dv), lambda bi, i, j: (bi, i, 0)),
        out_shape=jax.ShapeDtypeStruct((b, tq_all, heads * dv), BF16),
        scratch_shapes=[
            pltpu.VMEM((heads, tq, 2 * dv), F32),
            pltpu.VMEM((heads, tq, 2 * dv), F32),
            pltpu.VMEM((heads // 2, tq, 2 * dv), F32),
        ],
        compiler_params=_cparams("parallel", "parallel", "arbitrary"),
        name="flash",
    )(q, k, v)


def _mla_attn_kernel(q_ref, k_ref, v_ref, o_ref, m_scr, l_scr, acc_scr, *, heads, nk):
    ki = pl.program_id(2)
    w = MLA_DK_PAD
    tq = q_ref.shape[1]

    @pl.when(ki == 0)
    def _():
        m_scr[...] = jnp.full(m_scr.shape, -jnp.inf, F32)
        l_scr[...] = jnp.zeros(l_scr.shape, F32)
        acc_scr[...] = jnp.zeros(acc_scr.shape, F32)

    low = lax.broadcasted_iota(jnp.int32, (tq, w), 1) < MLA_DV
    for pr in range(heads // 2):
        vp = v_ref[0, :, pr * w:(pr + 1) * w]
        alphas, pvs = [], []
        for h in (2 * pr, 2 * pr + 1):
            sl = slice(h * w, (h + 1) * w)
            s = lax.dot_general(q_ref[0, :, sl], k_ref[0, :, sl], (((1,), (1,)), ((), ())),
                                preferred_element_type=F32)
            cols = [s[:, j * w:(j + 1) * w] for j in range(s.shape[1] // w)]
            m_prev = m_scr[h]
            m_new = jnp.maximum(m_prev, jnp.max(functools.reduce(jnp.maximum, cols), axis=-1, keepdims=True))
            alpha = jnp.exp2(m_prev - m_new)
            ps = [jnp.exp2(cj - m_new) for cj in cols]
            l_scr[h] = alpha * l_scr[h] + functools.reduce(jnp.add, ps)
            m_scr[h] = m_new
            p = jnp.concatenate([pj.astype(BF16) for pj in ps], axis=1)
            pvs.append(jnp.dot(p, vp, preferred_element_type=F32))
            alphas.append(alpha)
        acc_scr[pr] = acc_scr[pr] * jnp.where(low, alphas[0], alphas[1]) + jnp.where(low, pvs[0], pvs[1])

    @pl.when(ki == nk - 1)
    def _():
        for pr in range(heads // 2):
            l0 = jnp.sum(l_scr[2 * pr], axis=-1, keepdims=True)
            l1 = jnp.sum(l_scr[2 * pr + 1], axis=-1, keepdims=True)
            o_ref[0, :, pr * w:(pr + 1) * w] = (acc_scr[pr] / jnp.where(low, l0, l1)).astype(o_ref.dtype)


def _mla_attn(q, k, v, q_start, q_len, k_start, k_len, tq_pref, tk_pref):
    b, _, wq = q.shape
    heads = wq // MLA_DK_PAD
    tq = _tile(q_len, tq_pref, 8)
    tk = _tile(k_len, tk_pref, 128)
    assert q_start % tq == 0 and k_start % tk == 0
    q_off, k_off = q_start // tq, k_start // tk
    nk = k_len // tk
    kern = functools.partial(_mla_attn_kernel, heads=heads, nk=nk)
    return pl.pallas_call(
        kern,
        grid=(b, q_len // tq, nk),
        in_specs=[
            pl.BlockSpec((1, tq, wq), lambda bi, i, j: (bi, q_off + i, 0)),
            pl.BlockSpec((1, tk, wq), lambda bi, i, j: (bi, k_off + j, 0)),
            pl.BlockSpec((1, tk, heads * MLA_DV), lambda bi, i, j: (bi, k_off + j, 0)),
        ],
        out_specs=pl.BlockSpec((1, tq, heads * MLA_DV), lambda bi, i, j: (bi, i, 0)),
        out_shape=jax.ShapeDtypeStruct((b, q_len, heads * MLA_DV), BF16),
        scratch_shapes=[
            pltpu.VMEM((heads, tq, MLA_DK_PAD), F32),
            pltpu.VMEM((heads, tq, MLA_DK_PAD), F32),
            pltpu.VMEM((heads // 2, tq, MLA_DK_PAD), F32),
        ],
        compiler_params=_cparams("parallel", "parallel", "arbitrary"),
        name="mla_attn",
    )(q, k, v)


def _layer_norm(r, gain, bias):
    mu = jnp.mean(r, axis=-1, keepdims=True)
    rc = r - mu
    var = jnp.mean(rc * rc, axis=-1, keepdims=True)
    return rc * lax.rsqrt(var + EPS) * gain + bias


def _post_kernel(yaf_ref, yab_ref, yb_ref, yc_ref, ga_ref, gb_ref, gc_ref, x_ref, g1_ref, sc_ref, sh_ref, g2_ref,
                 wa_ref, wb_ref, wc_ref, wo_ref, w1_ref, w2_ref, lng_ref, lnb_ref, o_ref, *, ff_chunk):
    ya = (yaf_ref[0] + yab_ref[0]).astype(BF16)
    y = (jax.nn.sigmoid(ga_ref[0]) * jnp.dot(ya, wa_ref[0], preferred_element_type=F32)
         + jax.nn.sigmoid(gb_ref[0]) * jnp.dot(yb_ref[0].astype(BF16), wb_ref[0], preferred_element_type=F32)
         + jax.nn.sigmoid(gc_ref[0]) * jnp.dot(yc_ref[0].astype(BF16), wc_ref[0], preferred_element_type=F32))
    mix = jnp.dot(y.astype(BF16), wo_ref[0], preferred_element_type=F32)
    x1 = _layer_norm(DEEPNORM_ALPHA * x_ref[0] + g1_ref[0] * mix, lng_ref[0:1], lnb_ref[0:1])
    h = (x1 * (1.0 + sc_ref[0]) + sh_ref[0]).astype(BF16)
    acc = jnp.zeros(x1.shape, F32)
    for j in range(D_FF // ff_chunk):
        u = jnp.dot(h, w1_ref[0, :, j * ff_chunk:(j + 1) * ff_chunk], preferred_element_type=F32)
        u = jnp.square(jnp.maximum(u, 0.0)).astype(BF16)
        acc = acc + jnp.dot(u, w2_ref[0, j * ff_chunk:(j + 1) * ff_chunk, :], preferred_element_type=F32)
    o_ref[0] = _layer_norm(DEEPNORM_ALPHA * x1 + g2_ref[0] * acc, lng_ref[1:2], lnb_ref[1:2])


def _post(yaf, yab, yb, yc, p, x, g1, sc2, sh2, g2, wa, wb, wc, wo, w1, w2, lng, lnb, layer):
    b, t, d = x.shape
    tm = _tile(t, 512)
    wbr = 512
    row = lambda width, col: pl.BlockSpec((1, tm, width), lambda bi, i: (bi, i, col // width))
    mod = pl.BlockSpec((1, 1, d), lambda bi, i: (bi, 0, 0))
    kern = functools.partial(_post_kernel, ff_chunk=1024)
    return pl.pallas_call(
        kern,
        grid=(b, t // tm),
        in_specs=[
            row(wbr, 0), row(wbr, 0), row(wbr, 0), row(wbr, 0),
            row(d, C_GATE_A), row(d, C_GATE_B), row(d, C_GATE_C), row(d, 0),
            mod, mod, mod, mod,
            _layer_spec((wbr, d), layer), _layer_spec((wbr, d), layer), _layer_spec((wbr, d), layer),
            _layer_spec((d, d), layer), _layer_spec((d, D_FF), layer), _layer_spec((D_FF, d), layer),
            _const_spec((2, d)), _const_spec((2, d)),
        ],
        out_specs=row(d, 0),
        out_shape=jax.ShapeDtypeStruct((b, t, d), F32),
        compiler_params=_cparams("parallel", "parallel"),
        name="post",
    )(yaf, yab, yb, yc, p, p, p, x, g1, sc2, sh2, g2, wa, wb, wc, wo, w1, w2, lng, lnb)


def _rope_tables(n_tok, rot_dim):
    t = jnp.arange(n_tok)
    row = (t // GRID_W).astype(F32)
    col = (t % GRID_W).astype(F32)
    n_freq = rot_dim // 4
    inv_freq = ROPE_BASE ** (-2.0 * jnp.arange(n_freq, dtype=F32) / (rot_dim // 2))
    ang = jnp.concatenate([row[:, None] * inv_freq, col[:, None] * inv_freq], axis=-1)
    return jnp.cos(ang), jnp.sin(ang)


def _ret_rope_tables(n_tok):
    cos, sin = _rope_tables(n_tok, RET_DK)
    cos_h = jnp.concatenate([cos, cos], axis=1)
    sin_h = jnp.concatenate([-sin, sin], axis=1)
    return jnp.tile(cos_h, (1, 2)), jnp.tile(sin_h, (1, 2))


def _mla_rope_tables(n_tok):
    cos, sin = _rope_tables(n_tok, MLA_ROPE)
    ones = jnp.ones((n_tok, MLA_NOPE), F32)
    zeros = jnp.zeros((n_tok, MLA_NOPE), F32)
    pad = jnp.zeros((n_tok, MLA_DK_PAD - MLA_NOPE - MLA_ROPE), F32)
    cos_h = jnp.concatenate([ones, cos, cos, pad], axis=1)
    sin_h = jnp.concatenate([zeros, -sin, sin, pad], axis=1)
    return cos_h, sin_h


def _pack_w_in(w):
    w = w.astype(BF16)
    pad = jnp.zeros(w.shape[:2] + (P_WIDTH - 7200,), BF16)
    return jnp.concatenate([w[..., :4096], w[..., 4128:7200], w[..., 4096:4128], pad], axis=-1)


def _pack_mla_weights(w_qup, w_kvup):
    depth, r = w_qup.shape[:2]
    wq = w_qup.reshape(depth, r, MLA_HEADS, MLA_NOPE + MLA_ROPE)
    half = MLA_ROPE // 2
    wqs = jnp.concatenate([jnp.zeros((depth, r, MLA_HEADS, MLA_NOPE), wq.dtype), wq[..., MLA_NOPE + half:],
                           wq[..., MLA_NOPE:MLA_NOPE + half]], axis=-1)
    pad_q = ((0, 0), (0, 0), (0, 0), (0, MLA_DK_PAD - MLA_NOPE - MLA_ROPE))
    wq = jnp.pad(wq, pad_q).reshape(depth, r, MLA_HEADS * MLA_DK_PAD)
    wqs = jnp.pad(wqs, pad_q).reshape(depth, r, MLA_HEADS * MLA_DK_PAD)
    wkv = w_kvup.reshape(depth, r, MLA_HEADS, MLA_NOPE + MLA_DV)
    wk = jnp.pad(wkv[..., :MLA_NOPE], ((0, 0), (0, 0), (0, 0), (0, MLA_DK_PAD - MLA_NOPE)))
    wk = wk.reshape(depth, r, MLA_HEADS * MLA_DK_PAD)
    wv = wkv[..., MLA_NOPE:].reshape(depth, r, MLA_HEADS * MLA_DV)
    return wq.astype(BF16), wqs.astype(BF16), wk.astype(BF16), wv.astype(BF16)


def kernel(x, c, ctx, c_ctx, w_ada, b_ada, w_in, ret_log_decay, ret_gn_gain, na_rpb, mla_q_norm, mla_w_qup,
           mla_kv_norm, mla_w_kvup, w_branch_ret, w_branch_na, w_branch_mla, w_out, w_ff1, w_ff2, ln_gain, ln_bias):
    depth = w_ada.shape[0]
    b, t, d = x.shape
    lz = ctx.shape[1]
    rows = t // GRID_W

    cc = jnp.zeros((8, d), F32).at[:b].set(c).at[b].set(c_ctx)
    mod = _ada(cc, w_ada, b_ada)

    cos_r, sin_r = _ret_rope_tables(t)
    cos_m, sin_m = _mla_rope_tables(t)
    cos_m = jnp.concatenate([cos_m, jnp.ones((lz, MLA_DK_PAD), F32)], axis=0)
    sin_m = jnp.concatenate([sin_m, jnp.zeros((lz, MLA_DK_PAD), F32)], axis=0)
    cos_rz, sin_rz = cos_r[:lz], sin_r[:lz]
    na_bias = _na_bias_tables(na_rpb, rows)
    s_zero = jnp.zeros((b, 2, RET_HEADS * RET_DK, RET_DV), F32)

    w_in_p = _pack_w_in(w_in)
    wq, wqs, wk, wv = _pack_mla_weights(mla_w_qup, mla_w_kvup)
    wa = w_branch_ret.astype(BF16)
    wb = w_branch_na.astype(BF16)
    wc = w_branch_mla.astype(BF16)
    wo = w_out.astype(BF16)
    w1 = w_ff1.astype(BF16)
    w2 = w_ff2.astype(BF16)

    z = ctx
    for l in range(depth):
        need_ctx = l < depth - 1
        mx = mod[l, :b].reshape(b, 6, 1, d)
        mz = jnp.broadcast_to(mod[l, b].reshape(1, 6, 1, d), (b, 6, 1, d))
        sh1x, sc1x, g1x, sh2x, sc2x, g2x = [mx[:, i] for i in range(6)]
        sh1z, sc1z, g1z, sh2z, sc2z, g2z = [mz[:, i] for i in range(6)]

        px = _inproj(x, sc1x, sh1x, w_in_p, l)
        pz = _inproj(z, sc1z, sh1z, w_in_p, l)

        lg = jnp.log1p(-jnp.exp(ret_log_decay[l].astype(F32))).reshape(2 * RET_HEADS)
        gn_gain = ret_gn_gain[l].reshape(1, RET_HEADS * RET_DV)
        yaf_z, yab_z, s_ctx = _retention(pz, lg, cos_rz, sin_rz, gn_gain, s_zero, use_rope=False)
        yaf_x, yab_x, _ = _retention(px, lg, cos_r, sin_r, gn_gain, s_ctx, use_rope=True)

        yb_x = _na(px, pz, na_bias, l)

        qn = mla_q_norm[l].reshape(1, MLA_RANK)
        kvn = mla_kv_norm[l].reshape(1, MLA_RANK)
        q_all, k_all, v_all = _mlaprep(px, pz, qn, kvn, wq, wqs, wk, wv, cos_m, sin_m, l)
        yc_x = _mla_attn(q_all, k_all, v_all, 0, t, 0, t + lz, 1024, 2816)

        x = _post(yaf_x, yab_x, yb_x, yc_x, px, x, g1x, sc2x, sh2x, g2x, wa, wb, wc, wo, w1, w2,
                  ln_gain[l], ln_bias[l], l)

        if need_ctx:
            wna = NA_HEADS * NA_DH
            yb_z = _flash(pz, pz, pz, C_NA_Q // wna, C_NA_K // wna, C_NA_V // wna,
                          NA_HEADS, NA_DH, NA_DH, NA_DH ** -0.5 * LOG2_E, 256, 256)
            yc_z = _mla_attn(q_all, k_all, v_all, t, lz, t, lz, lz, lz)
            z = _post(yaf_z, yab_z, yb_z, yc_z, pz, z, g1z, sc2z, sh2z, g2z, wa, wb, wc, wo, w1, w2,
                      ln_gain[l], ln_bias[l], l)
    return x
```

```python
import functools

import numpy as np
import jax
import jax.numpy as jnp
from jax import lax
from jax.experimental import pallas as pl
from jax.experimental.pallas import tpu as pltpu

F32 = jnp.float32
BF16 = jnp.bfloat16

D_MODEL = 1024
GRID_W = 64
RET_HEADS = 4
RET_DK = 64
RET_DV = 128
RET_CHUNK = 256
NA_HEADS = 8
NA_DH = 64
NA_KH = 8
NA_KW = 16
MLA_HEADS = 8
MLA_RANK = 256
MLA_NOPE = 64
MLA_ROPE = 32
MLA_DV = 64
MLA_DK_PAD = 128
D_FF = 4 * D_MODEL
ROPE_BASE = 10000.0
EPS = 1e-5
DEPTH_FOR_NORM = 4
DEEPNORM_ALPHA = (2 * DEPTH_FOR_NORM) ** 0.25
MASK_VALUE = -1e30
LOG2_E = 1.4426950408889634

C_RET_Q, C_RET_K, C_RET_V, C_RET_GF, C_RET_GB = 0, 256, 512, 1024, 1536
C_NA_Q, C_NA_K, C_NA_V = 2048, 2560, 3072
C_MLA_Q, C_MLA_KV = 3584, 3840
C_GATE_A, C_GATE_B, C_GATE_C = 4096, 5120, 6144
C_MLA_KR = 7168
P_WIDTH = 7296
P_COL_TILE = 2432

NA_QROWS = 4
NA_KROWS = 12
NA_QB = NA_QROWS * GRID_W
NA_KB = NA_KROWS * GRID_W

VMEM_LIMIT = 56 * 1024 * 1024


def _cparams(*sem):
    return pltpu.CompilerParams(dimension_semantics=sem, vmem_limit_bytes=VMEM_LIMIT)


def _tile(n, pref, mult=1):
    t = min(n, pref) // mult * mult
    while n % t:
        t -= mult
    return t


def _const_spec(shape):
    nd = len(shape)
    return pl.BlockSpec(shape, lambda *_: (0,) * nd)


MOD_SH1, MOD_SC1, MOD_G1, MOD_SH2, MOD_SC2, MOD_G2 = range(6)


def _mod_spec(d, layer, which, ctx_row, batch_axis):
    def index(*g):
        return (layer, g[batch_axis] if ctx_row is None else ctx_row, which, 0, 0)
    return pl.BlockSpec((1, 1, 1, 1, d), index)


def _layer_spec(shape, layer):
    nd = len(shape)
    return pl.BlockSpec((1,) + tuple(shape), lambda *_: (layer,) + (0,) * nd)


def _ada_kernel(c_ref, w_ref, b_ref, o_ref):
    c = c_ref[...]
    a = c * jax.nn.sigmoid(c)
    o_ref[0] = jnp.dot(a.astype(BF16), w_ref[0].astype(BF16), preferred_element_type=F32) + b_ref[0]


def _ada(cc, w_ada, b_ada):
    depth, d, n = w_ada.shape
    tn = _tile(n, 3072, 128)
    return pl.pallas_call(
        _ada_kernel,
        grid=(depth, n // tn),
        in_specs=[
            pl.BlockSpec((8, d), lambda l, j: (0, 0)),
            pl.BlockSpec((1, d, tn), lambda l, j: (l, 0, j)),
            pl.BlockSpec((1, 1, tn), lambda l, j: (l, 0, j)),
        ],
        out_specs=pl.BlockSpec((1, 8, tn), lambda l, j: (l, 0, j)),
        out_shape=jax.ShapeDtypeStruct((depth, 8, n), F32),
        compiler_params=_cparams("parallel", "parallel"),
        name="ada",
    )(cc, w_ada, b_ada.reshape(depth, 1, n))


def _inproj_kernel(x_ref, sc_ref, sh_ref, w_ref, o_ref):
    h = x_ref[0] * (1.0 + sc_ref[0, 0, 0]) + sh_ref[0, 0, 0]
    o_ref[0] = jnp.dot(h.astype(BF16), w_ref[0], preferred_element_type=F32)


def _inproj(x, mod, ctx_row, w, layer):
    b, t, d = x.shape
    tm = _tile(t, 512)
    tn = P_COL_TILE
    return pl.pallas_call(
        _inproj_kernel,
        grid=(P_WIDTH // tn, b, t // tm),
        in_specs=[
            pl.BlockSpec((1, tm, d), lambda j, bi, i: (bi, i, 0)),
            _mod_spec(d, layer, MOD_SC1, ctx_row, 1),
            _mod_spec(d, layer, MOD_SH1, ctx_row, 1),
            pl.BlockSpec((1, d, tn), lambda j, bi, i: (layer, 0, j)),
        ],
        out_specs=pl.BlockSpec((1, tm, tn), lambda j, bi, i: (bi, i, j)),
        out_shape=jax.ShapeDtypeStruct((b, t, P_WIDTH), F32),
        compiler_params=_cparams("parallel", "parallel", "parallel"),
        name="inproj",
    )(x, mod, mod, w)


def _swap_halves(x, half):
    n = x.shape[-1]
    lane = lax.broadcasted_iota(jnp.int32, x.shape, x.ndim - 1)
    first = (lane % (2 * half)) < half
    return jnp.where(first, pltpu.roll(x, n - half, x.ndim - 1), pltpu.roll(x, half, x.ndim - 1))


def _ret_kernel(lg_ref, qf_ref, qb_ref, kf_ref, kb_ref, vf_ref, vb_ref, gf_ref, gb_ref,
                cosf_ref, cosb_ref, sinf_ref, sinb_ref, gain_ref, s0_ref,
                yf_ref, yb_ref, sf_ref, s_scr, *, use_rope, n_chunks, layer):
    c = pl.program_id(0)
    batch, cc = qf_ref.shape[:2]
    lg_base = layer * 2 * RET_HEADS

    @pl.when(c == 0)
    def _():
        s_scr[...] = s0_ref[...]

    gain = gain_ref[0]
    row = lax.broadcasted_iota(jnp.int32, (cc, cc), 0).astype(F32)
    col = lax.broadcasted_iota(jnp.int32, (cc, cc), 1).astype(F32)
    wqk = RET_HEADS * RET_DK
    pos = lax.broadcasted_iota(jnp.int32, (cc, wqk), 0).astype(F32)
    head_of_lane = lax.broadcasted_iota(jnp.int32, (cc, wqk), 1) // RET_DK
    dirs = (
        (qf_ref, kf_ref, vf_ref, gf_ref, cosf_ref, sinf_ref, yf_ref, row - col, pos + 1.0, cc - 1.0 - pos),
        (qb_ref, kb_ref, vb_ref, gb_ref, cosb_ref, sinb_ref, yb_ref, col - row, cc - pos, pos),
    )
    for d, (q_ref, k_ref, v_ref, g_ref, cos_ref, sin_ref, y_ref, diff, q_exp, k_exp) in enumerate(dirs):
        if use_rope:
            cos = jnp.concatenate([cos_ref[...], cos_ref[...]], axis=1)
            sin = jnp.concatenate([sin_ref[...], sin_ref[...]], axis=1)
        lgs = [lg_ref[lg_base + d * RET_HEADS + h] for h in range(RET_HEADS)]
        lg_lanes = jnp.full((cc, wqk), lgs[RET_HEADS - 1], F32)
        for h in range(RET_HEADS - 2, -1, -1):
            lg_lanes = jnp.where(head_of_lane == h, lgs[h], lg_lanes)
        q_decay = jnp.exp(lg_lanes * q_exp)
        k_decay = jnp.exp(lg_lanes * k_exp)
        for b in range(batch):
            q = q_ref[b]
            k = k_ref[b] * (RET_DK ** -0.5)
            if use_rope:
                q = q * cos + _swap_halves(q, RET_DK // 2) * sin
                k = k * cos + _swap_halves(k, RET_DK // 2) * sin
            k_bf = k.astype(BF16)
            kt_decayed = (k * k_decay).T.astype(BF16)
            q_decayed = q * q_decay
            state_bf = s_scr[b, d].astype(BF16)
            for h in range(RET_HEADS):
                mine = head_of_lane == h
                sl = slice(h * RET_DV, (h + 1) * RET_DV)
                rows = slice(h * RET_DK, (h + 1) * RET_DK)
                decay = jnp.where(diff >= 0, jnp.exp(lgs[h] * jnp.maximum(diff, 0.0)), 0.0)
                vh = v_ref[b, :, sl].astype(BF16)
                scores = lax.dot_general(jnp.where(mine, q, 0.0).astype(BF16), k_bf, (((1,), (1,)), ((), ())),
                                         preferred_element_type=F32) * decay
                inner = jnp.dot(scores.astype(BF16), vh, preferred_element_type=F32)
                cross = jnp.dot(jnp.where(mine, q_decayed, 0.0).astype(BF16), state_bf, preferred_element_type=F32)
                o = inner + cross
                kv = jnp.dot(kt_decayed[rows], vh, preferred_element_type=F32)
                chunk_decay = jnp.exp(lgs[h] * jnp.full((RET_DK, RET_DV), float(cc), F32))
                s_scr[b, d, rows] = s_scr[b, d, rows] * chunk_decay + kv
                mu = jnp.mean(o, axis=-1, keepdims=True)
                oc = o - mu
                var = jnp.mean(oc * oc, axis=-1, keepdims=True)
                gate = g_ref[b, :, sl]
                y_ref[b, :, sl] = (gate * jax.nn.sigmoid(gate)) * (oc * lax.rsqrt(var + EPS) * gain[:, sl])

    @pl.when(c == n_chunks - 1)
    def _():
        sf_ref[...] = s_scr[...]


def _retention(p, lg, cos, sin, gain, s0, use_rope, layer):
    b, t, _ = p.shape
    cc = _tile(t, RET_CHUNK, 8)
    n = t // cc
    wv = RET_HEADS * RET_DV

    def pspec(width, col, backward):
        if backward:
            return pl.BlockSpec((b, cc, width), lambda ci, lg_: (0, n - 1 - ci, col // width))
        return pl.BlockSpec((b, cc, width), lambda ci, lg_: (0, ci, col // width))

    def tspec(backward):
        if backward:
            return pl.BlockSpec((cc, 128), lambda ci, lg_: (n - 1 - ci, 0))
        return pl.BlockSpec((cc, 128), lambda ci, lg_: (ci, 0))

    state_spec = pl.BlockSpec((b, 2, RET_HEADS * RET_DK, RET_DV), lambda ci, lg_: (0, 0, 0, 0))
    kern = functools.partial(_ret_kernel, use_rope=use_rope, n_chunks=n, layer=layer)
    grid_spec = pltpu.PrefetchScalarGridSpec(
        num_scalar_prefetch=1,
        grid=(n,),
        in_specs=[
            pspec(256, C_RET_Q, False), pspec(256, C_RET_Q, True),
            pspec(256, C_RET_K, False), pspec(256, C_RET_K, True),
            pspec(wv, C_RET_V, False), pspec(wv, C_RET_V, True),
            pspec(wv, C_RET_GF, False), pspec(wv, C_RET_GB, True),
            tspec(False), tspec(True), tspec(False), tspec(True),
            pl.BlockSpec((1, 1, wv), lambda ci, lg_: (layer, 0, 0)),
            state_spec,
        ],
        out_specs=[
            pl.BlockSpec((b, cc, wv), lambda ci, lg_: (0, ci, 0)),
            pl.BlockSpec((b, cc, wv), lambda ci, lg_: (0, n - 1 - ci, 0)),
            state_spec,
        ],
        scratch_shapes=[pltpu.VMEM((b, 2, RET_HEADS * RET_DK, RET_DV), F32)],
    )
    return pl.pallas_call(
        kern,
        grid_spec=grid_spec,
        out_shape=[
            jax.ShapeDtypeStruct((b, t, wv), F32),
            jax.ShapeDtypeStruct((b, t, wv), F32),
            jax.ShapeDtypeStruct((b, 2, RET_HEADS * RET_DK, RET_DV), F32),
        ],
        compiler_params=_cparams("arbitrary"),
        name="retention",
    )(lg, p, p, p, p, p, p, p, p, cos, cos, sin, sin, gain, s0)


def _na_tables(rows):
    groups = rows // NA_QROWS
    cols = np.arange(GRID_W)
    c0 = np.clip(cols - NA_KW // 2, 0, GRID_W - NA_KW)
    col_ok = (cols[None, :] >= c0[:, None]) & (cols[None, :] < c0[:, None] + NA_KW)
    dc = cols[None, :] - cols[:, None] + (NA_KW - 1)
    onehot = (dc[None] == np.arange(2 * NA_KW - 1)[:, None, None]) & col_ok[None]
    row_bias = np.full((3, NA_QROWS, NA_KROWS), 2 * NA_KH - 1, np.int64)
    for ti, g in enumerate((0, 1, groups - 1)):
        ws = int(np.clip(NA_QROWS * g - NA_KH // 2, 0, rows - NA_KROWS))
        for lr in range(NA_QROWS):
            r = NA_QROWS * g + lr
            r0 = int(np.clip(r - NA_KH // 2, 0, rows - NA_KH))
            for kr in range(NA_KROWS):
                if r0 <= ws + kr < r0 + NA_KH:
                    row_bias[ti, lr, kr] = ws + kr - r + (NA_KH - 1)
    return onehot, col_ok, row_bias


def _na_bias_tables(na_rpb, rows):
    onehot, col_ok, row_bias = _na_tables(rows)
    depth, heads = na_rpb.shape[:2]
    nd = 2 * NA_KH
    toep = jnp.einsum('lhdj,jck->lhdck', na_rpb, onehot.astype(np.float32), precision=lax.Precision.HIGHEST)
    toep = jnp.where(col_ok, toep * LOG2_E, MASK_VALUE)
    masked = jnp.full(toep.shape[:2] + (1, GRID_W, GRID_W), MASK_VALUE, F32)
    toep = jnp.concatenate([toep, masked], axis=2)
    sides = jnp.concatenate([jnp.pad(toep, ((0, 0),) * 4 + ((0, GRID_W),)),
                             jnp.pad(toep, ((0, 0),) * 4 + ((GRID_W, 0),))], axis=2)
    pairs = row_bias.reshape(-1, 2)
    pick = ((pairs[:, :1] == np.arange(nd)[None, :]).astype(np.float32),
            (pairs[:, 1:] == np.arange(nd)[None, :]).astype(np.float32))
    pick = np.concatenate(pick, axis=1)
    tab = jnp.einsum('nd,lhdck->lhnck', pick, sides, precision=lax.Precision.HIGHEST)
    return tab.reshape(depth, heads, 3, NA_QROWS * NA_KROWS // 2, GRID_W, 2 * GRID_W)


def _na_kernel(q_ref, k0_ref, k1_ref, k2_ref, v0_ref, v1_ref, v2_ref, kz_ref, vz_ref, bias_ref, o_ref):
    lanes = 2 * NA_DH
    npair = NA_KROWS // 2
    low = lax.broadcasted_iota(jnp.int32, (q_ref.shape[1], lanes), 1) < NA_DH
    for pr in range(NA_HEADS // 2):
        sl = slice(pr * lanes, (pr + 1) * lanes)
        qp = q_ref[0, :, sl] * (NA_DH ** -0.5 * LOG2_E)
        qm = (jnp.where(low, qp, 0.0).astype(BF16), jnp.where(low, 0.0, qp).astype(BF16))
        ks = [r[0, :, sl].astype(BF16) for r in (k0_ref, k1_ref, k2_ref, kz_ref)]
        vs = [r[0, :, sl].astype(BF16) for r in (v0_ref, v1_ref, v2_ref, vz_ref)]
        outs = []
        for hh in range(2):
            h = 2 * pr + hh
            cols = []
            for j in range(4):
                sj = lax.dot_general(qm[hh], ks[j], (((1,), (1,)), ((), ())), preferred_element_type=F32)
                halves = [sj[:, i * lanes:(i + 1) * lanes] for i in range(sj.shape[1] // lanes)]
                if j < 3:
                    halves = [hv + jnp.concatenate([bias_ref[0, h, 0, lr * npair + j * len(halves) + i]
                                                    for lr in range(NA_QROWS)], axis=0)
                              for i, hv in enumerate(halves)]
                cols += halves
            m = jnp.max(functools.reduce(jnp.maximum, cols), axis=-1, keepdims=True)
            ps = [jnp.exp2(cj - m) for cj in cols]
            l = jnp.sum(functools.reduce(jnp.add, ps), axis=-1, keepdims=True)
            o = None
            per = len(cols) // 4
            for j in range(4):
                pj = jnp.concatenate([pc.astype(BF16) for pc in ps[j * per:(j + 1) * per]], axis=1)
                oj = jnp.dot(pj, vs[j], preferred_element_type=F32)
                o = oj if o is None else o + oj
            outs.append(o / l)
        o_ref[0, :, sl] = jnp.where(low, outs[0], outs[1]).astype(o_ref.dtype)


def _na(px, pz, bias, layer):
    b, t, _ = px.shape
    lz = pz.shape[1]
    groups = t // NA_QB
    w = NA_HEADS * NA_DH
    kblk = NA_KB // 3
    assert lz == kblk and groups >= 3

    def kspec(col, off):
        return pl.BlockSpec((1, kblk, w), lambda bi, g: (bi, jnp.clip(g - 1, 0, groups - 3) + off, col // w))

    def tab(g):
        return jnp.where(g == 0, 0, jnp.where(g == groups - 1, 2, 1))

    return pl.pallas_call(
        _na_kernel,
        grid=(b, groups),
        in_specs=[
            pl.BlockSpec((1, NA_QB, w), lambda bi, g: (bi, g, C_NA_Q // w)),
            kspec(C_NA_K, 0), kspec(C_NA_K, 1), kspec(C_NA_K, 2),
            kspec(C_NA_V, 0), kspec(C_NA_V, 1), kspec(C_NA_V, 2),
            pl.BlockSpec((1, lz, w), lambda bi, g: (bi, 0, C_NA_K // w)),
            pl.BlockSpec((1, lz, w), lambda bi, g: (bi, 0, C_NA_V // w)),
            pl.BlockSpec((1, NA_HEADS, 1) + bias.shape[3:], lambda bi, g: (layer, 0, tab(g), 0, 0, 0)),
        ],
        out_specs=pl.BlockSpec((1, NA_QB, w), lambda bi, g: (bi, g, 0)),
        out_shape=jax.ShapeDtypeStruct((b, t, w), BF16),
        compiler_params=_cparams("parallel", "arbitrary"),
        name="na",
    )(px, px, px, px, px, px, px, pz, pz, bias)


def _rms(x, gain):
    return x * lax.rsqrt(jnp.mean(x * x, axis=-1, keepdims=True) + EPS) * gain


def _mlaprep_kernel(xq_ref, xkv_ref, xkr_ref, zq_ref, zkv_ref, zkr_ref, qn_ref, kvn_ref, wq_ref, wqs_ref, wk_ref,
                    wv_ref, cos_ref, sin_ref, q_ref, k_ref, v_ref, *, nx):
    latent = pl.program_id(1) < nx
    pq = jnp.where(latent, xq_ref[0], zq_ref[0])
    pkv = jnp.where(latent, xkv_ref[0], zkv_ref[0])
    pkr = jnp.where(latent, xkr_ref[0], zkr_ref[0])
    hq = _rms(pq, qn_ref[0]).astype(BF16)
    hkv = _rms(pkv, kvn_ref[0]).astype(BF16)
    q = jnp.dot(hq, wq_ref[0], preferred_element_type=F32)
    q_swapped = jnp.dot(hq, wqs_ref[0], preferred_element_type=F32)
    k = jnp.dot(hkv, wk_ref[0], preferred_element_type=F32)
    v = jnp.dot(hkv, wv_ref[0], preferred_element_type=F32)
    kr = pltpu.roll(pkr, MLA_NOPE, 1)
    k = k + jnp.concatenate([kr] * MLA_HEADS, axis=1)
    k_swapped = jnp.concatenate([_swap_halves(kr, MLA_ROPE // 2)] * MLA_HEADS, axis=1)
    cos = jnp.concatenate([cos_ref[...]] * MLA_HEADS, axis=1)
    sin = jnp.concatenate([sin_ref[...]] * MLA_HEADS, axis=1)
    q = q * cos + q_swapped * sin
    k = k * cos + k_swapped * sin
    q_ref[0] = (q * ((MLA_NOPE + MLA_ROPE) ** -0.5 * LOG2_E)).astype(BF16)
    k_ref[0] = k.astype(BF16)
    v_ref[0] = v.astype(BF16)


def _mlaprep(px, pz, qn, kvn, wq, wqs, wk, wv, cos, sin, layer):
    b, t, _ = px.shape
    lz = pz.shape[1]
    tm = lz
    nx = t // tm
    wqk = MLA_HEADS * MLA_DK_PAD
    wvv = MLA_HEADS * MLA_DV

    def xspec(width, col):
        return pl.BlockSpec((1, tm, width), lambda bi, i: (bi, jnp.minimum(i, nx - 1), col // width))

    def zspec(width, col):
        return pl.BlockSpec((1, tm, width), lambda bi, i: (bi, 0, col // width))

    out_spec = pl.BlockSpec((1, tm, wqk), lambda bi, i: (bi, i, 0))
    kern = functools.partial(_mlaprep_kernel, nx=nx)
    return pl.pallas_call(
        kern,
        grid=(b, nx + 1),
        in_specs=[
            xspec(MLA_RANK, C_MLA_Q), xspec(MLA_RANK, C_MLA_KV), xspec(128, C_MLA_KR),
            zspec(MLA_RANK, C_MLA_Q), zspec(MLA_RANK, C_MLA_KV), zspec(128, C_MLA_KR),
            _layer_spec((1, MLA_RANK), layer), _layer_spec((1, MLA_RANK), layer),
            _layer_spec((MLA_RANK, wqk), layer), _layer_spec((MLA_RANK, wqk), layer),
            _layer_spec((MLA_RANK, wqk), layer), _layer_spec((MLA_RANK, wvv), layer),
            pl.BlockSpec((tm, 128), lambda bi, i: (i, 0)),
            pl.BlockSpec((tm, 128), lambda bi, i: (i, 0)),
        ],
        out_specs=[out_spec, out_spec, pl.BlockSpec((1, tm, wvv), lambda bi, i: (bi, i, 0))],
        out_shape=[jax.ShapeDtypeStruct((b, t + lz, wqk), BF16)] * 2
        + [jax.ShapeDtypeStruct((b, t + lz, wvv), BF16)],
        compiler_params=_cparams("parallel", "parallel"),
        name="mlaprep",
    )(px, px, px, pz, pz, pz, qn, kvn, wq, wqs, wk, wv, cos, sin)


def _flash_kernel(q_ref, k_ref, v_ref, o_ref, m_scr, l_scr, acc_scr, *, heads, dk, dv, scale, nk):
    ki = pl.program_id(2)
    tq = q_ref.shape[1]
    tk = k_ref.shape[1]
    lanes = 2 * dv

    @pl.when(ki == 0)
    def _():
        m_scr[...] = jnp.full(m_scr.shape, -jnp.inf, F32)
        l_scr[...] = jnp.zeros(l_scr.shape, F32)
        acc_scr[...] = jnp.zeros(acc_scr.shape, F32)

    low = lax.broadcasted_iota(jnp.int32, (tq, lanes), 1) < dv
    for pr in range(heads // 2):
        vp = v_ref[0, :, pr * lanes:(pr + 1) * lanes].astype(BF16)
        alphas, pvs = [], []
        for h in (2 * pr, 2 * pr + 1):
            qh = q_ref[0, :, h * dk:(h + 1) * dk]
            if scale != 1.0:
                qh = qh * scale
            kh = k_ref[0, :, h * dk:(h + 1) * dk]
            s = lax.dot_general(qh.astype(BF16), kh.astype(BF16), (((1,), (1,)), ((), ())),
                                preferred_element_type=F32)
            cols = [s[:, j * lanes:(j + 1) * lanes] for j in range(tk // lanes)]
            m_prev = m_scr[h]
            m_tile = jnp.max(functools.reduce(jnp.maximum, cols), axis=-1, keepdims=True)
            m_new = jnp.maximum(m_prev, m_tile)
            alpha = jnp.exp2(m_prev - m_new)
            ps = [jnp.exp2(cj - m_new) for cj in cols]
            l_scr[h] = alpha * l_scr[h] + functools.reduce(jnp.add, ps)
            m_scr[h] = m_new
            p = jnp.concatenate([pj.astype(BF16) for pj in ps], axis=1)
            pvs.append(jnp.dot(p, vp, preferred_element_type=F32))
            alphas.append(alpha)
        acc_scr[pr] = acc_scr[pr] * jnp.where(low, alphas[0], alphas[1]) + jnp.where(low, pvs[0], pvs[1])

    @pl.when(ki == nk - 1)
    def _():
        for pr in range(heads // 2):
            l0 = jnp.sum(l_scr[2 * pr], axis=-1, keepdims=True)
            l1 = jnp.sum(l_scr[2 * pr + 1], axis=-1, keepdims=True)
            o_ref[0, :, pr * lanes:(pr + 1) * lanes] = (acc_scr[pr] / jnp.where(low, l0, l1)).astype(o_ref.dtype)


def _flash(q, k, v, qcol, kcol, vcol, heads, dk, dv, scale, tq_pref, tk_pref):
    b, tq_all, _ = q.shape
    tk_all = k.shape[1]
    tq = _tile(tq_all, tq_pref)
    tk = _tile(tk_all, tk_pref)
    nk = tk_all // tk
    assert heads % 2 == 0 and 2 * dv == 128 and tk % 128 == 0
    kern = functools.partial(_flash_kernel, heads=heads, dk=dk, dv=dv, scale=scale, nk=nk)
    return pl.pallas_call(
        kern,
        grid=(b, tq_all // tq, nk),
        in_specs=[
            pl.BlockSpec((1, tq, heads * dk), lambda bi, i, j: (bi, i, qcol)),
            pl.BlockSpec((1, tk, heads * dk), lambda bi, i, j: (bi, j, kcol)),
            pl.BlockSpec((1, tk, heads * dv), lambda bi, i, j: (bi, j, vcol)),
        ],
        out_specs=pl.BlockSpec((1, tq, heads * dv), lambda bi, i, j: (bi, i, 0)),
        out_shape=jax.ShapeDtypeStruct((b, tq_all, heads * dv), BF16),
        scratch_shapes=[
            pltpu.VMEM((heads, tq, 2 * dv), F32),
            pltpu.VMEM((heads, tq, 2 * dv), F32),
            pltpu.VMEM((heads // 2, tq, 2 * dv), F32),
        ],
        compiler_params=_cparams("parallel", "parallel", "arbitrary"),
        name="flash",
    )(q, k, v)


def _mla_attn_kernel(q_ref, k_ref, v_ref, o_ref, m_scr, l_scr, acc_scr, *, heads, nk):
    ki = pl.program_id(2)
    w = MLA_DK_PAD
    tq = q_ref.shape[1]

    @pl.when(ki == 0)
    def _():
        m_scr[...] = jnp.full(m_scr.shape, -jnp.inf, F32)
        l_scr[...] = jnp.zeros(l_scr.shape, F32)
        acc_scr[...] = jnp.zeros(acc_scr.shape, F32)

    low = lax.broadcasted_iota(jnp.int32, (tq, w), 1) < MLA_DV
    for pr in range(heads // 2):
        vp = v_ref[0, :, pr * w:(pr + 1) * w]
        alphas, pvs = [], []
        for h in (2 * pr, 2 * pr + 1):
            sl = slice(h * w, (h + 1) * w)
            s = lax.dot_general(q_ref[0, :, sl], k_ref[0, :, sl], (((1,), (1,)), ((), ())),
                                preferred_element_type=F32)
            cols = [s[:, j * w:(j + 1) * w] for j in range(s.shape[1] // w)]
            m_prev = m_scr[h]
            m_new = jnp.maximum(m_prev, jnp.max(functools.reduce(jnp.maximum, cols), axis=-1, keepdims=True))
            alpha = jnp.exp2(m_prev - m_new)
            ps = [jnp.exp2(cj - m_new) for cj in cols]
            l_scr[h] = alpha * l_scr[h] + functools.reduce(jnp.add, ps)
            m_scr[h] = m_new
            p = jnp.concatenate([pj.astype(BF16) for pj in ps], axis=1)
            pvs.append(jnp.dot(p, vp, preferred_element_type=F32))
            alphas.append(alpha)
        acc_scr[pr] = acc_scr[pr] * jnp.where(low, alphas[0], alphas[1]) + jnp.where(low, pvs[0], pvs[1])

    @pl.when(ki == nk - 1)
    def _():
        for pr in range(heads // 2):
            l0 = jnp.sum(l_scr[2 * pr], axis=-1, keepdims=True)
            l1 = jnp.sum(l_scr[2 * pr + 1], axis=-1, keepdims=True)
            o_ref[0, :, pr * w:(pr + 1) * w] = (acc_scr[pr] / jnp.where(low, l0, l1)).astype(o_ref.dtype)


def _mla_attn(q, k, v, q_start, q_len, k_start, k_len, tq_pref, tk_pref):
    b, _, wq = q.shape
    heads = wq // MLA_DK_PAD
    tq = _tile(q_len, tq_pref, 8)
    tk = _tile(k_len, tk_pref, 128)
    assert q_start % tq == 0 and k_start % tk == 0
    q_off, k_off = q_start // tq, k_start // tk
    nk = k_len // tk
    kern = functools.partial(_mla_attn_kernel, heads=heads, nk=nk)
    return pl.pallas_call(
        kern,
        grid=(b, q_len // tq, nk),
        in_specs=[
            pl.BlockSpec((1, tq, wq), lambda bi, i, j: (bi, q_off + i, 0)),
            pl.BlockSpec((1, tk, wq), lambda bi, i, j: (bi, k_off + j, 0)),
            pl.BlockSpec((1, tk, heads * MLA_DV), lambda bi, i, j: (bi, k_off + j, 0)),
        ],
        out_specs=pl.BlockSpec((1, tq, heads * MLA_DV), lambda bi, i, j: (bi, i, 0)),
        out_shape=jax.ShapeDtypeStruct((b, q_len, heads * MLA_DV), BF16),
        scratch_shapes=[
            pltpu.VMEM((heads, tq, MLA_DK_PAD), F32),
            pltpu.VMEM((heads, tq, MLA_DK_PAD), F32),
            pltpu.VMEM((heads // 2, tq, MLA_DK_PAD), F32),
        ],
        compiler_params=_cparams("parallel", "parallel", "arbitrary"),
        name="mla_attn",
    )(q, k, v)


def _layer_norm(r, gain, bias):
    mu = jnp.mean(r, axis=-1, keepdims=True)
    rc = r - mu
    var = jnp.mean(rc * rc, axis=-1, keepdims=True)
    return rc * lax.rsqrt(var + EPS) * gain + bias


def _post_kernel(yaf_ref, yab_ref, yb_ref, yc_ref, ga_ref, gb_ref, gc_ref, x_ref, g1_ref, sc_ref, sh_ref, g2_ref,
                 wa_ref, wb_ref, wc_ref, wo_ref, w1_ref, w2_ref, lng_ref, lnb_ref, o_ref, *, ff_chunk):
    ya = (yaf_ref[0] + yab_ref[0]).astype(BF16)
    y = (jax.nn.sigmoid(ga_ref[0]) * jnp.dot(ya, wa_ref[0], preferred_element_type=F32)
         + jax.nn.sigmoid(gb_ref[0]) * jnp.dot(yb_ref[0].astype(BF16), wb_ref[0], preferred_element_type=F32)
         + jax.nn.sigmoid(gc_ref[0]) * jnp.dot(yc_ref[0].astype(BF16), wc_ref[0], preferred_element_type=F32))
    mix = jnp.dot(y.astype(BF16), wo_ref[0], preferred_element_type=F32)
    x1 = _layer_norm(DEEPNORM_ALPHA * x_ref[0] + g1_ref[0, 0, 0] * mix, lng_ref[0, 0:1], lnb_ref[0, 0:1])
    h = (x1 * (1.0 + sc_ref[0, 0, 0]) + sh_ref[0, 0, 0]).astype(BF16)
    acc = jnp.zeros(x1.shape, F32)
    for j in range(D_FF // ff_chunk):
        u = jnp.dot(h, w1_ref[0, :, j * ff_chunk:(j + 1) * ff_chunk], preferred_element_type=F32)
        u = jnp.square(jnp.maximum(u, 0.0)).astype(BF16)
        acc = acc + jnp.dot(u, w2_ref[0, j * ff_chunk:(j + 1) * ff_chunk, :], preferred_element_type=F32)
    o_ref[0] = _layer_norm(DEEPNORM_ALPHA * x1 + g2_ref[0, 0, 0] * acc, lng_ref[0, 1:2], lnb_ref[0, 1:2])


def _post(yaf, yab, yb, yc, p, x, mod, ctx_row, wa, wb, wc, wo, w1, w2, lng, lnb, layer):
    b, t, d = x.shape
    tm = _tile(t, 512)
    wbr = 512
    row = lambda width, col: pl.BlockSpec((1, tm, width), lambda bi, i: (bi, i, col // width))
    mods = [_mod_spec(d, layer, which, ctx_row, 0) for which in (MOD_G1, MOD_SC2, MOD_SH2, MOD_G2)]
    kern = functools.partial(_post_kernel, ff_chunk=1024)
    return pl.pallas_call(
        kern,
        grid=(b, t // tm),
        in_specs=[
            row(wbr, 0), row(wbr, 0), row(wbr, 0), row(wbr, 0),
            row(d, C_GATE_A), row(d, C_GATE_B), row(d, C_GATE_C), row(d, 0),
            *mods,
            _layer_spec((wbr, d), layer), _layer_spec((wbr, d), layer), _layer_spec((wbr, d), layer),
            _layer_spec((d, d), layer), _layer_spec((d, D_FF), layer), _layer_spec((D_FF, d), layer),
            _layer_spec((2, d), layer), _layer_spec((2, d), layer),
        ],
        out_specs=row(d, 0),
        out_shape=jax.ShapeDtypeStruct((b, t, d), F32),
        compiler_params=_cparams("parallel", "parallel"),
        name="post",
    )(yaf, yab, yb, yc, p, p, p, x, mod, mod, mod, mod, wa, wb, wc, wo, w1, w2, lng, lnb)


def _rope_tables(n_tok, rot_dim):
    t = jnp.arange(n_tok)
    row = (t // GRID_W).astype(F32)
    col = (t % GRID_W).astype(F32)
    n_freq = rot_dim // 4
    inv_freq = ROPE_BASE ** (-2.0 * jnp.arange(n_freq, dtype=F32) / (rot_dim // 2))
    ang = jnp.concatenate([row[:, None] * inv_freq, col[:, None] * inv_freq], axis=-1)
    return jnp.cos(ang), jnp.sin(ang)


def _ret_rope_tables(n_tok):
    cos, sin = _rope_tables(n_tok, RET_DK)
    cos_h = jnp.concatenate([cos, cos], axis=1)
    sin_h = jnp.concatenate([-sin, sin], axis=1)
    return jnp.tile(cos_h, (1, 2)), jnp.tile(sin_h, (1, 2))


def _mla_rope_tables(n_tok):
    cos, sin = _rope_tables(n_tok, MLA_ROPE)
    ones = jnp.ones((n_tok, MLA_NOPE), F32)
    zeros = jnp.zeros((n_tok, MLA_NOPE), F32)
    pad = jnp.zeros((n_tok, MLA_DK_PAD - MLA_NOPE - MLA_ROPE), F32)
    cos_h = jnp.concatenate([ones, cos, cos, pad], axis=1)
    sin_h = jnp.concatenate([zeros, -sin, sin, pad], axis=1)
    return cos_h, sin_h


def _pack_w_in(w):
    w = w.astype(BF16)
    pad = jnp.zeros(w.shape[:2] + (P_WIDTH - 7200,), BF16)
    return jnp.concatenate([w[..., :4096], w[..., 4128:7200], w[..., 4096:4128], pad], axis=-1)


def _pack_mla_weights(w_qup, w_kvup):
    depth, r = w_qup.shape[:2]
    wq = w_qup.reshape(depth, r, MLA_HEADS, MLA_NOPE + MLA_ROPE)
    half = MLA_ROPE // 2
    wqs = jnp.concatenate([jnp.zeros((depth, r, MLA_HEADS, MLA_NOPE), wq.dtype), wq[..., MLA_NOPE + half:],
                           wq[..., MLA_NOPE:MLA_NOPE + half]], axis=-1)
    pad_q = ((0, 0), (0, 0), (0, 0), (0, MLA_DK_PAD - MLA_NOPE - MLA_ROPE))
    wq = jnp.pad(wq, pad_q).reshape(depth, r, MLA_HEADS * MLA_DK_PAD)
    wqs = jnp.pad(wqs, pad_q).reshape(depth, r, MLA_HEADS * MLA_DK_PAD)
    wkv = w_kvup.reshape(depth, r, MLA_HEADS, MLA_NOPE + MLA_DV)
    wk = jnp.pad(wkv[..., :MLA_NOPE], ((0, 0), (0, 0), (0, 0), (0, MLA_DK_PAD - MLA_NOPE)))
    wk = wk.reshape(depth, r, MLA_HEADS * MLA_DK_PAD)
    wv = wkv[..., MLA_NOPE:].reshape(depth, r, MLA_HEADS * MLA_DV)
    return wq.astype(BF16), wqs.astype(BF16), wk.astype(BF16), wv.astype(BF16)


def kernel(x, c, ctx, c_ctx, w_ada, b_ada, w_in, ret_log_decay, ret_gn_gain, na_rpb, mla_q_norm, mla_w_qup,
           mla_kv_norm, mla_w_kvup, w_branch_ret, w_branch_na, w_branch_mla, w_out, w_ff1, w_ff2, ln_gain, ln_bias):
    depth = w_ada.shape[0]
    b, t, d = x.shape
    lz = ctx.shape[1]
    rows = t // GRID_W

    cc = jnp.zeros((8, d), F32).at[:b].set(c).at[b].set(c_ctx)
    mod = _ada(cc, w_ada, b_ada).reshape(depth, 8, 6, 1, d)

    cos_r, sin_r = _ret_rope_tables(t)
    cos_m, sin_m = _mla_rope_tables(t)
    cos_m = jnp.concatenate([cos_m, jnp.ones((lz, MLA_DK_PAD), F32)], axis=0)
    sin_m = jnp.concatenate([sin_m, jnp.zeros((lz, MLA_DK_PAD), F32)], axis=0)
    cos_rz, sin_rz = cos_r[:lz], sin_r[:lz]
    na_bias = _na_bias_tables(na_rpb, rows)
    s_zero = jnp.zeros((b, 2, RET_HEADS * RET_DK, RET_DV), F32)

    w_in_p = _pack_w_in(w_in)
    wq, wqs, wk, wv = _pack_mla_weights(mla_w_qup, mla_w_kvup)
    wa = w_branch_ret.astype(BF16)
    wb = w_branch_na.astype(BF16)
    wc = w_branch_mla.astype(BF16)
    wo = w_out.astype(BF16)
    w1 = w_ff1.astype(BF16)
    w2 = w_ff2.astype(BF16)

    lg = jnp.log1p(-jnp.exp(ret_log_decay.astype(F32))).reshape(depth * 2 * RET_HEADS)
    gn_gain = ret_gn_gain.reshape(depth, 1, RET_HEADS * RET_DV)
    qn = mla_q_norm.reshape(depth, 1, MLA_RANK)
    kvn = mla_kv_norm.reshape(depth, 1, MLA_RANK)

    z = ctx
    for l in range(depth):
        need_ctx = l < depth - 1
        px = _inproj(x, mod, None, w_in_p, l)
        pz = _inproj(z, mod, b, w_in_p, l)

        yaf_z, yab_z, s_ctx = _retention(pz, lg, cos_rz, sin_rz, gn_gain, s_zero, False, l)
        yaf_x, yab_x, _ = _retention(px, lg, cos_r, sin_r, gn_gain, s_ctx, True, l)

        yb_x = _na(px, pz, na_bias, l)

        q_all, k_all, v_all = _mlaprep(px, pz, qn, kvn, wq, wqs, wk, wv, cos_m, sin_m, l)
        yc_x = _mla_attn(q_all, k_all, v_all, 0, t, 0, t + lz, 1024, 2816)

        x = _post(yaf_x, yab_x, yb_x, yc_x, px, x, mod, None, wa, wb, wc, wo, w1, w2, ln_gain, ln_bias, l)

        if need_ctx:
            wna = NA_HEADS * NA_DH
            yb_z = _flash(pz, pz, pz, C_NA_Q // wna, C_NA_K // wna, C_NA_V // wna,
                          NA_HEADS, NA_DH, NA_DH, NA_DH ** -0.5 * LOG2_E, 256, 256)
            yc_z = _mla_attn(q_all, k_all, v_all, t, lz, t, lz, lz, lz)
            z = _post(yaf_z, yab_z, yb_z, yc_z, pz, z, mod, b, wa, wb, wc, wo, w1, w2, ln_gain, ln_bias, l)
    return x
```

```python
import functools

import numpy as np
import jax
import jax.numpy as jnp
from jax import lax
from jax.experimental import pallas as pl
from jax.experimental.pallas import tpu as pltpu

F32 = jnp.float32
BF16 = jnp.bfloat16

D_MODEL = 1024
GRID_W = 64
RET_HEADS = 4
RET_DK = 64
RET_DV = 128
RET_CHUNK = 256
NA_HEADS = 8
NA_DH = 64
NA_KH = 8
NA_KW = 16
MLA_HEADS = 8
MLA_RANK = 256
MLA_NOPE = 64
MLA_ROPE = 32
MLA_DV = 64
MLA_DK_PAD = 128
D_FF = 4 * D_MODEL
ROPE_BASE = 10000.0
EPS = 1e-5
DEPTH_FOR_NORM = 4
DEEPNORM_ALPHA = (2 * DEPTH_FOR_NORM) ** 0.25
MASK_VALUE = -1e30
LOG2_E = 1.4426950408889634

C_RET_Q, C_RET_K, C_RET_V, C_RET_GF, C_RET_GB = 0, 256, 512, 1024, 1536
C_NA_Q, C_NA_K, C_NA_V = 2048, 2560, 3072
C_MLA_Q, C_MLA_KV = 3584, 3840
C_GATE_A, C_GATE_B, C_GATE_C = 4096, 5120, 6144
C_MLA_KR = 7168
IN_WIDTH = 7200
P_WIDTH = 7296
P_COL_TILE = 2432

NA_QROWS = 4
NA_KROWS = 12
NA_QB = NA_QROWS * GRID_W
NA_KB = NA_KROWS * GRID_W

VMEM_LIMIT = 56 * 1024 * 1024


def _cparams(*sem):
    return pltpu.CompilerParams(dimension_semantics=sem, vmem_limit_bytes=VMEM_LIMIT)


def _tile(n, pref, mult=1):
    t = min(n, pref) // mult * mult
    while n % t:
        t -= mult
    return t


def _const_spec(shape):
    nd = len(shape)
    return pl.BlockSpec(shape, lambda *_: (0,) * nd)


MOD_SH1, MOD_SC1, MOD_G1, MOD_SH2, MOD_SC2, MOD_G2 = range(6)


def _mod_spec(d, layer, which, ctx_row, batch_axis):
    def index(*g):
        return (layer, g[batch_axis] if ctx_row is None else ctx_row, which, 0, 0)
    return pl.BlockSpec((1, 1, 1, 1, d), index)


def _layer_spec(shape, layer):
    nd = len(shape)
    return pl.BlockSpec((1,) + tuple(shape), lambda *_: (layer,) + (0,) * nd)


def _ada_kernel(c_ref, w_ref, b_ref, o_ref):
    c = c_ref[...]
    a = c * jax.nn.sigmoid(c)
    o_ref[0] = jnp.dot(a.astype(BF16), w_ref[0].astype(BF16), preferred_element_type=F32) + b_ref[0]


def _ada(cc, w_ada, b_ada):
    depth, d, n = w_ada.shape
    tn = _tile(n, 3072, 128)
    return pl.pallas_call(
        _ada_kernel,
        grid=(depth, n // tn),
        in_specs=[
            pl.BlockSpec((8, d), lambda l, j: (0, 0)),
            pl.BlockSpec((1, d, tn), lambda l, j: (l, 0, j)),
            pl.BlockSpec((1, 1, tn), lambda l, j: (l, 0, j)),
        ],
        out_specs=pl.BlockSpec((1, 8, tn), lambda l, j: (l, 0, j)),
        out_shape=jax.ShapeDtypeStruct((depth, 8, n), F32),
        compiler_params=_cparams("parallel", "parallel"),
        name="ada",
    )(cc, w_ada, b_ada.reshape(depth, 1, n))


def _inproj_kernel(x_ref, sc_ref, sh_ref, w_ref, o_ref):
    h = x_ref[0] * (1.0 + sc_ref[0, 0, 0]) + sh_ref[0, 0, 0]
    o_ref[0] = jnp.dot(h.astype(BF16), w_ref[0], preferred_element_type=F32)


def _inproj(x, mod, ctx_row, w, layer):
    b, t, d = x.shape
    tm = _tile(t, 512)
    tn = P_COL_TILE
    return pl.pallas_call(
        _inproj_kernel,
        grid=(P_WIDTH // tn, b, t // tm),
        in_specs=[
            pl.BlockSpec((1, tm, d), lambda j, bi, i: (bi, i, 0)),
            _mod_spec(d, layer, MOD_SC1, ctx_row, 1),
            _mod_spec(d, layer, MOD_SH1, ctx_row, 1),
            pl.BlockSpec((1, d, tn), lambda j, bi, i: (layer, 0, j)),
        ],
        out_specs=pl.BlockSpec((1, tm, tn), lambda j, bi, i: (bi, i, j)),
        out_shape=jax.ShapeDtypeStruct((b, t, P_WIDTH), F32),
        compiler_params=_cparams("parallel", "parallel", "parallel"),
        name="inproj",
    )(x, mod, mod, w)


def _swap_halves(x, half):
    n = x.shape[-1]
    lane = lax.broadcasted_iota(jnp.int32, x.shape, x.ndim - 1)
    first = (lane % (2 * half)) < half
    return jnp.where(first, pltpu.roll(x, n - half, x.ndim - 1), pltpu.roll(x, half, x.ndim - 1))


def _ret_kernel(lg_ref, qf_ref, qb_ref, kf_ref, kb_ref, vf_ref, vb_ref, gf_ref, gb_ref,
                cosf_ref, cosb_ref, sinf_ref, sinb_ref, gain_ref, s0_ref,
                yf_ref, yb_ref, sf_ref, s_scr, *, use_rope, n_chunks, layer):
    c = pl.program_id(0)
    batch, cc = qf_ref.shape[:2]
    lg_base = layer * 2 * RET_HEADS

    @pl.when(c == 0)
    def _():
        s_scr[...] = s0_ref[...]

    gain = gain_ref[0]
    row = lax.broadcasted_iota(jnp.int32, (cc, cc), 0).astype(F32)
    col = lax.broadcasted_iota(jnp.int32, (cc, cc), 1).astype(F32)
    wqk = RET_HEADS * RET_DK
    pos = lax.broadcasted_iota(jnp.int32, (cc, wqk), 0).astype(F32)
    head_of_lane = lax.broadcasted_iota(jnp.int32, (cc, wqk), 1) // RET_DK
    dirs = (
        (qf_ref, kf_ref, vf_ref, gf_ref, cosf_ref, sinf_ref, yf_ref, row - col, pos + 1.0, cc - 1.0 - pos),
        (qb_ref, kb_ref, vb_ref, gb_ref, cosb_ref, sinb_ref, yb_ref, col - row, cc - pos, pos),
    )
    for d, (q_ref, k_ref, v_ref, g_ref, cos_ref, sin_ref, y_ref, diff, q_exp, k_exp) in enumerate(dirs):
        if use_rope:
            cos = jnp.concatenate([cos_ref[...], cos_ref[...]], axis=1)
            sin = jnp.concatenate([sin_ref[...], sin_ref[...]], axis=1)
        lgs = [lg_ref[lg_base + d * RET_HEADS + h] for h in range(RET_HEADS)]
        lg_lanes = jnp.full((cc, wqk), lgs[RET_HEADS - 1], F32)
        for h in range(RET_HEADS - 2, -1, -1):
            lg_lanes = jnp.where(head_of_lane == h, lgs[h], lg_lanes)
        q_decay = jnp.exp(lg_lanes * q_exp)
        k_decay = jnp.exp(lg_lanes * k_exp)
        for b in range(batch):
            q = q_ref[b]
            k = k_ref[b] * (RET_DK ** -0.5)
            if use_rope:
                q = q * cos + _swap_halves(q, RET_DK // 2) * sin
                k = k * cos + _swap_halves(k, RET_DK // 2) * sin
            k_bf = k.astype(BF16)
            kt_decayed = (k * k_decay).T.astype(BF16)
            q_decayed = q * q_decay
            state_bf = s_scr[b, d].astype(BF16)
            for h in range(RET_HEADS):
                mine = head_of_lane == h
                sl = slice(h * RET_DV, (h + 1) * RET_DV)
                rows = slice(h * RET_DK, (h + 1) * RET_DK)
                decay = jnp.where(diff >= 0, jnp.exp(lgs[h] * jnp.maximum(diff, 0.0)), 0.0)
                vh = v_ref[b, :, sl].astype(BF16)
                scores = lax.dot_general(jnp.where(mine, q, 0.0).astype(BF16), k_bf, (((1,), (1,)), ((), ())),
                                         preferred_element_type=F32) * decay
                inner = jnp.dot(scores.astype(BF16), vh, preferred_element_type=F32)
                cross = jnp.dot(jnp.where(mine, q_decayed, 0.0).astype(BF16), state_bf, preferred_element_type=F32)
                o = inner + cross
                kv = jnp.dot(kt_decayed[rows], vh, preferred_element_type=F32)
                chunk_decay = jnp.exp(lgs[h] * jnp.full((RET_DK, RET_DV), float(cc), F32))
                s_scr[b, d, rows] = s_scr[b, d, rows] * chunk_decay + kv
                mu = jnp.mean(o, axis=-1, keepdims=True)
                oc = o - mu
                var = jnp.mean(oc * oc, axis=-1, keepdims=True)
                gate = g_ref[b, :, sl]
                y_ref[b, :, sl] = (gate * jax.nn.sigmoid(gate)) * (oc * lax.rsqrt(var + EPS) * gain[:, sl])

    @pl.when(c == n_chunks - 1)
    def _():
        sf_ref[...] = s_scr[...]


def _retention(p, lg, cos, sin, gain, s0, use_rope, layer):
    b, t, _ = p.shape
    cc = _tile(t, RET_CHUNK, 8)
    n = t // cc
    wv = RET_HEADS * RET_DV

    def pspec(width, col, backward):
        if backward:
            return pl.BlockSpec((b, cc, width), lambda ci, lg_: (0, n - 1 - ci, col // width))
        return pl.BlockSpec((b, cc, width), lambda ci, lg_: (0, ci, col // width))

    def tspec(backward):
        if backward:
            return pl.BlockSpec((cc, 128), lambda ci, lg_: (n - 1 - ci, 0))
        return pl.BlockSpec((cc, 128), lambda ci, lg_: (ci, 0))

    state_spec = pl.BlockSpec((b, 2, RET_HEADS * RET_DK, RET_DV), lambda ci, lg_: (0, 0, 0, 0))
    kern = functools.partial(_ret_kernel, use_rope=use_rope, n_chunks=n, layer=layer)
    grid_spec = pltpu.PrefetchScalarGridSpec(
        num_scalar_prefetch=1,
        grid=(n,),
        in_specs=[
            pspec(256, C_RET_Q, False), pspec(256, C_RET_Q, True),
            pspec(256, C_RET_K, False), pspec(256, C_RET_K, True),
            pspec(wv, C_RET_V, False), pspec(wv, C_RET_V, True),
            pspec(wv, C_RET_GF, False), pspec(wv, C_RET_GB, True),
            tspec(False), tspec(True), tspec(False), tspec(True),
            pl.BlockSpec((1, 1, wv), lambda ci, lg_: (layer, 0, 0)),
            state_spec,
        ],
        out_specs=[
            pl.BlockSpec((b, cc, wv), lambda ci, lg_: (0, ci, 0)),
            pl.BlockSpec((b, cc, wv), lambda ci, lg_: (0, n - 1 - ci, 0)),
            state_spec,
        ],
        scratch_shapes=[pltpu.VMEM((b, 2, RET_HEADS * RET_DK, RET_DV), F32)],
    )
    return pl.pallas_call(
        kern,
        grid_spec=grid_spec,
        out_shape=[
            jax.ShapeDtypeStruct((b, t, wv), F32),
            jax.ShapeDtypeStruct((b, t, wv), F32),
            jax.ShapeDtypeStruct((b, 2, RET_HEADS * RET_DK, RET_DV), F32),
        ],
        compiler_params=_cparams("arbitrary"),
        name="retention",
    )(lg, p, p, p, p, p, p, p, p, cos, cos, sin, sin, gain, s0)


def _na_tables(rows):
    groups = rows // NA_QROWS
    cols = np.arange(GRID_W)
    c0 = np.clip(cols - NA_KW // 2, 0, GRID_W - NA_KW)
    col_ok = (cols[None, :] >= c0[:, None]) & (cols[None, :] < c0[:, None] + NA_KW)
    dc = cols[None, :] - cols[:, None] + (NA_KW - 1)
    onehot = (dc[None] == np.arange(2 * NA_KW - 1)[:, None, None]) & col_ok[None]
    row_bias = np.full((3, NA_QROWS, NA_KROWS), 2 * NA_KH - 1, np.int64)
    for ti, g in enumerate((0, 1, groups - 1)):
        ws = int(np.clip(NA_QROWS * g - NA_KH // 2, 0, rows - NA_KROWS))
        for lr in range(NA_QROWS):
            r = NA_QROWS * g + lr
            r0 = int(np.clip(r - NA_KH // 2, 0, rows - NA_KH))
            for kr in range(NA_KROWS):
                if r0 <= ws + kr < r0 + NA_KH:
                    row_bias[ti, lr, kr] = ws + kr - r + (NA_KH - 1)
    return onehot, col_ok, row_bias


def _na_bias_tables(na_rpb, rows):
    onehot, col_ok, row_bias = _na_tables(rows)
    depth, heads = na_rpb.shape[:2]
    nd = 2 * NA_KH
    toep = jnp.einsum('lhdj,jck->lhdck', na_rpb, onehot.astype(np.float32), precision=lax.Precision.HIGHEST)
    toep = jnp.where(col_ok, toep * LOG2_E, MASK_VALUE)
    masked = jnp.full(toep.shape[:2] + (1, GRID_W, GRID_W), MASK_VALUE, F32)
    toep = jnp.concatenate([toep, masked], axis=2)
    sides = jnp.concatenate([jnp.pad(toep, ((0, 0),) * 4 + ((0, GRID_W),)),
                             jnp.pad(toep, ((0, 0),) * 4 + ((GRID_W, 0),))], axis=2)
    pairs = row_bias.reshape(-1, 2)
    pick = ((pairs[:, :1] == np.arange(nd)[None, :]).astype(np.float32),
            (pairs[:, 1:] == np.arange(nd)[None, :]).astype(np.float32))
    pick = np.concatenate(pick, axis=1)
    tab = jnp.einsum('nd,lhdck->lhnck', pick, sides, precision=lax.Precision.HIGHEST)
    return tab.reshape(depth, heads, 3, NA_QROWS * NA_KROWS // 2, GRID_W, 2 * GRID_W)


def _na_kernel(q_ref, k0_ref, k1_ref, k2_ref, v0_ref, v1_ref, v2_ref, kz_ref, vz_ref, bias_ref, o_ref):
    lanes = 2 * NA_DH
    npair = NA_KROWS // 2
    low = lax.broadcasted_iota(jnp.int32, (q_ref.shape[1], lanes), 1) < NA_DH
    for pr in range(NA_HEADS // 2):
        sl = slice(pr * lanes, (pr + 1) * lanes)
        qp = q_ref[0, :, sl] * (NA_DH ** -0.5 * LOG2_E)
        qm = (jnp.where(low, qp, 0.0).astype(BF16), jnp.where(low, 0.0, qp).astype(BF16))
        ks = [r[0, :, sl].astype(BF16) for r in (k0_ref, k1_ref, k2_ref, kz_ref)]
        vs = [r[0, :, sl].astype(BF16) for r in (v0_ref, v1_ref, v2_ref, vz_ref)]
        outs = []
        for hh in range(2):
            h = 2 * pr + hh
            cols = []
            for j in range(4):
                sj = lax.dot_general(qm[hh], ks[j], (((1,), (1,)), ((), ())), preferred_element_type=F32)
                halves = [sj[:, i * lanes:(i + 1) * lanes] for i in range(sj.shape[1] // lanes)]
                if j < 3:
                    halves = [hv + jnp.concatenate([bias_ref[0, h, 0, lr * npair + j * len(halves) + i]
                                                    for lr in range(NA_QROWS)], axis=0)
                              for i, hv in enumerate(halves)]
                cols += halves
            m = jnp.max(functools.reduce(jnp.maximum, cols), axis=-1, keepdims=True)
            ps = [jnp.exp2(cj - m) for cj in cols]
            l = jnp.sum(functools.reduce(jnp.add, ps), axis=-1, keepdims=True)
            o = None
            per = len(cols) // 4
            for j in range(4):
                pj = jnp.concatenate([pc.astype(BF16) for pc in ps[j * per:(j + 1) * per]], axis=1)
                oj = jnp.dot(pj, vs[j], preferred_element_type=F32)
                o = oj if o is None else o + oj
            outs.append(o / l)
        o_ref[0, :, sl] = jnp.where(low, outs[0], outs[1]).astype(o_ref.dtype)


def _na(px, pz, bias, layer):
    b, t, _ = px.shape
    lz = pz.shape[1]
    groups = t // NA_QB
    w = NA_HEADS * NA_DH
    kblk = NA_KB // 3
    assert lz == kblk and groups >= 3

    def kspec(col, off):
        return pl.BlockSpec((1, kblk, w), lambda bi, g: (bi, jnp.clip(g - 1, 0, groups - 3) + off, col // w))

    def tab(g):
        return jnp.where(g == 0, 0, jnp.where(g == groups - 1, 2, 1))

    return pl.pallas_call(
        _na_kernel,
        grid=(b, groups),
        in_specs=[
            pl.BlockSpec((1, NA_QB, w), lambda bi, g: (bi, g, C_NA_Q // w)),
            kspec(C_NA_K, 0), kspec(C_NA_K, 1), kspec(C_NA_K, 2),
            kspec(C_NA_V, 0), kspec(C_NA_V, 1), kspec(C_NA_V, 2),
            pl.BlockSpec((1, lz, w), lambda bi, g: (bi, 0, C_NA_K // w)),
            pl.BlockSpec((1, lz, w), lambda bi, g: (bi, 0, C_NA_V // w)),
            pl.BlockSpec((1, NA_HEADS, 1) + bias.shape[3:], lambda bi, g: (layer, 0, tab(g), 0, 0, 0)),
        ],
        out_specs=pl.BlockSpec((1, NA_QB, w), lambda bi, g: (bi, g, 0)),
        out_shape=jax.ShapeDtypeStruct((b, t, w), BF16),
        compiler_params=_cparams("parallel", "arbitrary"),
        name="na",
    )(px, px, px, px, px, px, px, pz, pz, bias)


def _rms(x, gain):
    return x * lax.rsqrt(jnp.mean(x * x, axis=-1, keepdims=True) + EPS) * gain


def _mlaprep_kernel(xq_ref, xkv_ref, xkr_ref, zq_ref, zkv_ref, zkr_ref, qn_ref, kvn_ref, wq_ref, wqs_ref, wk_ref,
                    wv_ref, cos_ref, sin_ref, q_ref, k_ref, v_ref, *, nx):
    latent = pl.program_id(1) < nx
    pq = jnp.where(latent, xq_ref[0], zq_ref[0])
    pkv = jnp.where(latent, xkv_ref[0], zkv_ref[0])
    pkr = jnp.where(latent, xkr_ref[0], zkr_ref[0])
    hq = _rms(pq, qn_ref[0]).astype(BF16)
    hkv = _rms(pkv, kvn_ref[0]).astype(BF16)
    q = jnp.dot(hq, wq_ref[0], preferred_element_type=F32)
    q_swapped = jnp.dot(hq, wqs_ref[0], preferred_element_type=F32)
    k = jnp.dot(hkv, wk_ref[0], preferred_element_type=F32)
    v = jnp.dot(hkv, wv_ref[0], preferred_element_type=F32)
    kr = pltpu.roll(pkr, MLA_NOPE, 1)
    k = k + jnp.concatenate([kr] * MLA_HEADS, axis=1)
    k_swapped = jnp.concatenate([_swap_halves(kr, MLA_ROPE // 2)] * MLA_HEADS, axis=1)
    cos = jnp.concatenate([cos_ref[...]] * MLA_HEADS, axis=1)
    sin = jnp.concatenate([sin_ref[...]] * MLA_HEADS, axis=1)
    q = q * cos + q_swapped * sin
    k = k * cos + k_swapped * sin
    q_ref[0] = (q * ((MLA_NOPE + MLA_ROPE) ** -0.5 * LOG2_E)).astype(BF16)
    k_ref[0] = k.astype(BF16)
    v_ref[0] = v.astype(BF16)


def _mlaprep(px, pz, qn, kvn, wq, wqs, wk, wv, cos, sin, layer):
    b, t, _ = px.shape
    lz = pz.shape[1]
    tm = lz
    nx = t // tm
    wqk = MLA_HEADS * MLA_DK_PAD
    wvv = MLA_HEADS * MLA_DV

    def xspec(width, col):
        return pl.BlockSpec((1, tm, width), lambda bi, i: (bi, jnp.minimum(i, nx - 1), col // width))

    def zspec(width, col):
        return pl.BlockSpec((1, tm, width), lambda bi, i: (bi, 0, col // width))

    out_spec = pl.BlockSpec((1, tm, wqk), lambda bi, i: (bi, i, 0))
    kern = functools.partial(_mlaprep_kernel, nx=nx)
    return pl.pallas_call(
        kern,
        grid=(b, nx + 1),
        in_specs=[
            xspec(MLA_RANK, C_MLA_Q), xspec(MLA_RANK, C_MLA_KV), xspec(128, C_MLA_KR),
            zspec(MLA_RANK, C_MLA_Q), zspec(MLA_RANK, C_MLA_KV), zspec(128, C_MLA_KR),
            _layer_spec((1, MLA_RANK), layer), _layer_spec((1, MLA_RANK), layer),
            _layer_spec((MLA_RANK, wqk), layer), _layer_spec((MLA_RANK, wqk), layer),
            _layer_spec((MLA_RANK, wqk), layer), _layer_spec((MLA_RANK, wvv), layer),
            pl.BlockSpec((tm, 128), lambda bi, i: (i, 0)),
            pl.BlockSpec((tm, 128), lambda bi, i: (i, 0)),
        ],
        out_specs=[out_spec, out_spec, pl.BlockSpec((1, tm, wvv), lambda bi, i: (bi, i, 0))],
        out_shape=[jax.ShapeDtypeStruct((b, t + lz, wqk), BF16)] * 2
        + [jax.ShapeDtypeStruct((b, t + lz, wvv), BF16)],
        compiler_params=_cparams("parallel", "parallel"),
        name="mlaprep",
    )(px, px, px, pz, pz, pz, qn, kvn, wq, wqs, wk, wv, cos, sin)


def _flash_kernel(q_ref, k_ref, v_ref, o_ref, m_scr, l_scr, acc_scr, *, heads, dk, dv, scale, nk):
    ki = pl.program_id(2)
    tq = q_ref.shape[1]
    tk = k_ref.shape[1]
    lanes = 2 * dv

    @pl.when(ki == 0)
    def _():
        m_scr[...] = jnp.full(m_scr.shape, -jnp.inf, F32)
        l_scr[...] = jnp.zeros(l_scr.shape, F32)
        acc_scr[...] = jnp.zeros(acc_scr.shape, F32)

    low = lax.broadcasted_iota(jnp.int32, (tq, lanes), 1) < dv
    for pr in range(heads // 2):
        vp = v_ref[0, :, pr * lanes:(pr + 1) * lanes].astype(BF16)
        alphas, pvs = [], []
        for h in (2 * pr, 2 * pr + 1):
            qh = q_ref[0, :, h * dk:(h + 1) * dk]
            if scale != 1.0:
                qh = qh * scale
            kh = k_ref[0, :, h * dk:(h + 1) * dk]
            s = lax.dot_general(qh.astype(BF16), kh.astype(BF16), (((1,), (1,)), ((), ())),
                                preferred_element_type=F32)
            cols = [s[:, j * lanes:(j + 1) * lanes] for j in range(tk // lanes)]
            m_prev = m_scr[h]
            m_tile = jnp.max(functools.reduce(jnp.maximum, cols), axis=-1, keepdims=True)
            m_new = jnp.maximum(m_prev, m_tile)
            alpha = jnp.exp2(m_prev - m_new)
            ps = [jnp.exp2(cj - m_new) for cj in cols]
            l_scr[h] = alpha * l_scr[h] + functools.reduce(jnp.add, ps)
            m_scr[h] = m_new
            p = jnp.concatenate([pj.astype(BF16) for pj in ps], axis=1)
            pvs.append(jnp.dot(p, vp, preferred_element_type=F32))
            alphas.append(alpha)
        acc_scr[pr] = acc_scr[pr] * jnp.where(low, alphas[0], alphas[1]) + jnp.where(low, pvs[0], pvs[1])

    @pl.when(ki == nk - 1)
    def _():
        for pr in range(heads // 2):
            l0 = jnp.sum(l_scr[2 * pr], axis=-1, keepdims=True)
            l1 = jnp.sum(l_scr[2 * pr + 1], axis=-1, keepdims=True)
            o_ref[0, :, pr * lanes:(pr + 1) * lanes] = (acc_scr[pr] / jnp.where(low, l0, l1)).astype(o_ref.dtype)


def _flash(q, k, v, qcol, kcol, vcol, heads, dk, dv, scale, tq_pref, tk_pref):
    b, tq_all, _ = q.shape
    tk_all = k.shape[1]
    tq = _tile(tq_all, tq_pref)
    tk = _tile(tk_all, tk_pref)
    nk = tk_all // tk
    assert heads % 2 == 0 and 2 * dv == 128 and tk % 128 == 0
    kern = functools.partial(_flash_kernel, heads=heads, dk=dk, dv=dv, scale=scale, nk=nk)
    return pl.pallas_call(
        kern,
        grid=(b, tq_all // tq, nk),
        in_specs=[
            pl.BlockSpec((1, tq, heads * dk), lambda bi, i, j: (bi, i, qcol)),
            pl.BlockSpec((1, tk, heads * dk), lambda bi, i, j: (bi, j, kcol)),
            pl.BlockSpec((1, tk, heads * dv), lambda bi, i, j: (bi, j, vcol)),
        ],
        out_specs=pl.BlockSpec((1, tq, heads * dv), lambda bi, i, j: (bi, i, 0)),
        out_shape=jax.ShapeDtypeStruct((b, tq_all, heads * dv), BF16),
        scratch_shapes=[
            pltpu.VMEM((heads, tq, 2 * dv), F32),
            pltpu.VMEM((heads, tq, 2 * dv), F32),
            pltpu.VMEM((heads // 2, tq, 2 * dv), F32),
        ],
        compiler_params=_cparams("parallel", "parallel", "arbitrary"),
        name="flash",
    )(q, k, v)


def _mla_attn_kernel(q_ref, k_ref, v_ref, o_ref, m_scr, l_scr, acc_scr, *, heads, nk):
    ki = pl.program_id(2)
    w = MLA_DK_PAD
    tq = q_ref.shape[1]

    @pl.when(ki == 0)
    def _():
        m_scr[...] = jnp.full(m_scr.shape, -jnp.inf, F32)
        l_scr[...] = jnp.zeros(l_scr.shape, F32)
        acc_scr[...] = jnp.zeros(acc_scr.shape, F32)

    low = lax.broadcasted_iota(jnp.int32, (tq, w), 1) < MLA_DV
    for pr in range(heads // 2):
        vp = v_ref[0, :, pr * w:(pr + 1) * w]
        alphas, pvs = [], []
        for h in (2 * pr, 2 * pr + 1):
            sl = slice(h * w, (h + 1) * w)
            s = lax.dot_general(q_ref[0, :, sl], k_ref[0, :, sl], (((1,), (1,)), ((), ())),
                                preferred_element_type=F32)
            cols = [s[:, j * w:(j + 1) * w] for j in range(s.shape[1] // w)]
            m_prev = m_scr[h]
            m_new = jnp.maximum(m_prev, jnp.max(functools.reduce(jnp.maximum, cols), axis=-1, keepdims=True))
            alpha = jnp.exp2(m_prev - m_new)
            ps = [jnp.exp2(cj - m_new) for cj in cols]
            l_scr[h] = alpha * l_scr[h] + functools.reduce(jnp.add, ps)
            m_scr[h] = m_new
            p = jnp.concatenate([pj.astype(BF16) for pj in ps], axis=1)
            pvs.append(jnp.dot(p, vp, preferred_element_type=F32))
            alphas.append(alpha)
        acc_scr[pr] = acc_scr[pr] * jnp.where(low, alphas[0], alphas[1]) + jnp.where(low, pvs[0], pvs[1])

    @pl.when(ki == nk - 1)
    def _():
        for pr in range(heads // 2):
            l0 = jnp.sum(l_scr[2 * pr], axis=-1, keepdims=True)
            l1 = jnp.sum(l_scr[2 * pr + 1], axis=-1, keepdims=True)
            o_ref[0, :, pr * w:(pr + 1) * w] = (acc_scr[pr] / jnp.where(low, l0, l1)).astype(o_ref.dtype)


def _mla_attn(q, k, v, q_start, q_len, k_start, k_len, tq_pref, tk_pref):
    b, _, wq = q.shape
    heads = wq // MLA_DK_PAD
    tq = _tile(q_len, tq_pref, 8)
    tk = _tile(k_len, tk_pref, 128)
    assert q_start % tq == 0 and k_start % tk == 0
    q_off, k_off = q_start // tq, k_start // tk
    nk = k_len // tk
    kern = functools.partial(_mla_attn_kernel, heads=heads, nk=nk)
    return pl.pallas_call(
        kern,
        grid=(b, q_len // tq, nk),
        in_specs=[
            pl.BlockSpec((1, tq, wq), lambda bi, i, j: (bi, q_off + i, 0)),
            pl.BlockSpec((1, tk, wq), lambda bi, i, j: (bi, k_off + j, 0)),
            pl.BlockSpec((1, tk, heads * MLA_DV), lambda bi, i, j: (bi, k_off + j, 0)),
        ],
        out_specs=pl.BlockSpec((1, tq, heads * MLA_DV), lambda bi, i, j: (bi, i, 0)),
        out_shape=jax.ShapeDtypeStruct((b, q_len, heads * MLA_DV), BF16),
        scratch_shapes=[
            pltpu.VMEM((heads, tq, MLA_DK_PAD), F32),
            pltpu.VMEM((heads, tq, MLA_DK_PAD), F32),
            pltpu.VMEM((heads // 2, tq, MLA_DK_PAD), F32),
        ],
        compiler_params=_cparams("parallel", "parallel", "arbitrary"),
        name="mla_attn",
    )(q, k, v)


def _layer_norm(r, gain, bias):
    mu = jnp.mean(r, axis=-1, keepdims=True)
    rc = r - mu
    var = jnp.mean(rc * rc, axis=-1, keepdims=True)
    return rc * lax.rsqrt(var + EPS) * gain + bias


def _post_kernel(yaf_ref, yab_ref, yb_ref, yc_ref, ga_ref, gb_ref, gc_ref, x_ref, g1_ref, sc_ref, sh_ref, g2_ref,
                 wa_ref, wb_ref, wc_ref, wo_ref, w1_ref, w2_ref, lng_ref, lnb_ref, o_ref, *, ff_chunk):
    ya = (yaf_ref[0] + yab_ref[0]).astype(BF16)
    y = (jax.nn.sigmoid(ga_ref[0]) * jnp.dot(ya, wa_ref[0], preferred_element_type=F32)
         + jax.nn.sigmoid(gb_ref[0]) * jnp.dot(yb_ref[0].astype(BF16), wb_ref[0], preferred_element_type=F32)
         + jax.nn.sigmoid(gc_ref[0]) * jnp.dot(yc_ref[0].astype(BF16), wc_ref[0], preferred_element_type=F32))
    mix = jnp.dot(y.astype(BF16), wo_ref[0], preferred_element_type=F32)
    x1 = _layer_norm(DEEPNORM_ALPHA * x_ref[0] + g1_ref[0, 0, 0] * mix, lng_ref[0, 0:1], lnb_ref[0, 0:1])
    h = (x1 * (1.0 + sc_ref[0, 0, 0]) + sh_ref[0, 0, 0]).astype(BF16)
    acc = jnp.zeros(x1.shape, F32)
    for j in range(D_FF // ff_chunk):
        u = jnp.dot(h, w1_ref[0, :, j * ff_chunk:(j + 1) * ff_chunk], preferred_element_type=F32)
        u = jnp.square(jnp.maximum(u, 0.0)).astype(BF16)
        acc = acc + jnp.dot(u, w2_ref[0, j * ff_chunk:(j + 1) * ff_chunk, :], preferred_element_type=F32)
    o_ref[0] = _layer_norm(DEEPNORM_ALPHA * x1 + g2_ref[0, 0, 0] * acc, lng_ref[0, 1:2], lnb_ref[0, 1:2])


def _post(yaf, yab, yb, yc, p, x, mod, ctx_row, wa, wb, wc, wo, w1, w2, lng, lnb, layer):
    b, t, d = x.shape
    tm = _tile(t, 512)
    wbr = 512
    row = lambda width, col: pl.BlockSpec((1, tm, width), lambda bi, i: (bi, i, col // width))
    mods = [_mod_spec(d, layer, which, ctx_row, 0) for which in (MOD_G1, MOD_SC2, MOD_SH2, MOD_G2)]
    kern = functools.partial(_post_kernel, ff_chunk=1024)
    return pl.pallas_call(
        kern,
        grid=(b, t // tm),
        in_specs=[
            row(wbr, 0), row(wbr, 0), row(wbr, 0), row(wbr, 0),
            row(d, C_GATE_A), row(d, C_GATE_B), row(d, C_GATE_C), row(d, 0),
            *mods,
            _layer_spec((wbr, d), layer), _layer_spec((wbr, d), layer), _layer_spec((wbr, d), layer),
            _layer_spec((d, d), layer), _layer_spec((d, D_FF), layer), _layer_spec((D_FF, d), layer),
            _layer_spec((2, d), layer), _layer_spec((2, d), layer),
        ],
        out_specs=row(d, 0),
        out_shape=jax.ShapeDtypeStruct((b, t, d), F32),
        compiler_params=_cparams("parallel", "parallel"),
        name="post",
    )(yaf, yab, yb, yc, p, p, p, x, mod, mod, mod, mod, wa, wb, wc, wo, w1, w2, lng, lnb)


def _rope_tables(n_tok, rot_dim):
    t = jnp.arange(n_tok)
    row = (t // GRID_W).astype(F32)
    col = (t % GRID_W).astype(F32)
    n_freq = rot_dim // 4
    inv_freq = ROPE_BASE ** (-2.0 * jnp.arange(n_freq, dtype=F32) / (rot_dim // 2))
    ang = jnp.concatenate([row[:, None] * inv_freq, col[:, None] * inv_freq], axis=-1)
    return jnp.cos(ang), jnp.sin(ang)


def _ret_rope_tables(n_tok):
    cos, sin = _rope_tables(n_tok, RET_DK)
    cos_h = jnp.concatenate([cos, cos], axis=1)
    sin_h = jnp.concatenate([-sin, sin], axis=1)
    return jnp.tile(cos_h, (1, 2)), jnp.tile(sin_h, (1, 2))


def _mla_rope_tables(n_tok):
    cos, sin = _rope_tables(n_tok, MLA_ROPE)
    ones = jnp.ones((n_tok, MLA_NOPE), F32)
    zeros = jnp.zeros((n_tok, MLA_NOPE), F32)
    pad = jnp.zeros((n_tok, MLA_DK_PAD - MLA_NOPE - MLA_ROPE), F32)
    cos_h = jnp.concatenate([ones, cos, cos, pad], axis=1)
    sin_h = jnp.concatenate([zeros, -sin, sin, pad], axis=1)
    return cos_h, sin_h


def _pack_w_in_kernel(w_ref, o_ref):
    w = w_ref[0]
    pad = jnp.zeros((w.shape[0], P_WIDTH - IN_WIDTH), F32)
    o_ref[0] = jnp.concatenate([w[:, :C_GATE_A], w[:, C_GATE_A + MLA_ROPE:], w[:, C_GATE_A:C_GATE_A + MLA_ROPE], pad],
                               axis=1).astype(BF16)


def _pack_w_in(w):
    depth, d, n = w.shape
    assert n == IN_WIDTH
    tr = _tile(d, 256, 8)
    return pl.pallas_call(
        _pack_w_in_kernel,
        grid=(depth, d // tr),
        in_specs=[pl.BlockSpec((1, tr, n), lambda l, i: (l, i, 0))],
        out_specs=pl.BlockSpec((1, tr, P_WIDTH), lambda l, i: (l, i, 0)),
        out_shape=jax.ShapeDtypeStruct((depth, d, P_WIDTH), BF16),
        compiler_params=_cparams("parallel", "parallel"),
        name="pack_w_in",
    )(w)


def _pack_mla_weights(w_qup, w_kvup):
    depth, r = w_qup.shape[:2]
    wq = w_qup.reshape(depth, r, MLA_HEADS, MLA_NOPE + MLA_ROPE)
    half = MLA_ROPE // 2
    wqs = jnp.concatenate([jnp.zeros((depth, r, MLA_HEADS, MLA_NOPE), wq.dtype), wq[..., MLA_NOPE + half:],
                           wq[..., MLA_NOPE:MLA_NOPE + half]], axis=-1)
    pad_q = ((0, 0), (0, 0), (0, 0), (0, MLA_DK_PAD - MLA_NOPE - MLA_ROPE))
    wq = jnp.pad(wq, pad_q).reshape(depth, r, MLA_HEADS * MLA_DK_PAD)
    wqs = jnp.pad(wqs, pad_q).reshape(depth, r, MLA_HEADS * MLA_DK_PAD)
    wkv = w_kvup.reshape(depth, r, MLA_HEADS, MLA_NOPE + MLA_DV)
    wk = jnp.pad(wkv[..., :MLA_NOPE], ((0, 0), (0, 0), (0, 0), (0, MLA_DK_PAD - MLA_NOPE)))
    wk = wk.reshape(depth, r, MLA_HEADS * MLA_DK_PAD)
    wv = wkv[..., MLA_NOPE:].reshape(depth, r, MLA_HEADS * MLA_DV)
    return wq.astype(BF16), wqs.astype(BF16), wk.astype(BF16), wv.astype(BF16)


def kernel(x, c, ctx, c_ctx, w_ada, b_ada, w_in, ret_log_decay, ret_gn_gain, na_rpb, mla_q_norm, mla_w_qup,
           mla_kv_norm, mla_w_kvup, w_branch_ret, w_branch_na, w_branch_mla, w_out, w_ff1, w_ff2, ln_gain, ln_bias):
    depth = w_ada.shape[0]
    b, t, d = x.shape
    lz = ctx.shape[1]
    rows = t // GRID_W

    cc = jnp.zeros((8, d), F32).at[:b].set(c).at[b].set(c_ctx)
    mod = _ada(cc, w_ada, b_ada).reshape(depth, 8, 6, 1, d)

    cos_r, sin_r = _ret_rope_tables(t)
    cos_m, sin_m = _mla_rope_tables(t)
    cos_m = jnp.concatenate([cos_m, jnp.ones((lz, MLA_DK_PAD), F32)], axis=0)
    sin_m = jnp.concatenate([sin_m, jnp.zeros((lz, MLA_DK_PAD), F32)], axis=0)
    cos_rz, sin_rz = cos_r[:lz], sin_r[:lz]
    na_bias = _na_bias_tables(na_rpb, rows)
    s_zero = jnp.zeros((b, 2, RET_HEADS * RET_DK, RET_DV), F32)

    w_in_p = _pack_w_in(w_in)
    wq, wqs, wk, wv = _pack_mla_weights(mla_w_qup, mla_w_kvup)
    wa = w_branch_ret.astype(BF16)
    wb = w_branch_na.astype(BF16)
    wc = w_branch_mla.astype(BF16)
    wo = w_out.astype(BF16)
    w1 = w_ff1.astype(BF16)
    w2 = w_ff2.astype(BF16)

    lg = jnp.log1p(-jnp.exp(ret_log_decay.astype(F32))).reshape(depth * 2 * RET_HEADS)
    gn_gain = ret_gn_gain.reshape(depth, 1, RET_HEADS * RET_DV)
    qn = mla_q_norm.reshape(depth, 1, MLA_RANK)
    kvn = mla_kv_norm.reshape(depth, 1, MLA_RANK)

    z = ctx
    for l in range(depth):
        need_ctx = l < depth - 1
        px = _inproj(x, mod, None, w_in_p, l)
        pz = _inproj(z, mod, b, w_in_p, l)

        yaf_z, yab_z, s_ctx = _retention(pz, lg, cos_rz, sin_rz, gn_gain, s_zero, False, l)
        yaf_x, yab_x, _ = _retention(px, lg, cos_r, sin_r, gn_gain, s_ctx, True, l)

        yb_x = _na(px, pz, na_bias, l)

        q_all, k_all, v_all = _mlaprep(px, pz, qn, kvn, wq, wqs, wk, wv, cos_m, sin_m, l)
        yc_x = _mla_attn(q_all, k_all, v_all, 0, t, 0, t + lz, 1024, 2816)

        x = _post(yaf_x, yab_x, yb_x, yc_x, px, x, mod, None, wa, wb, wc, wo, w1, w2, ln_gain, ln_bias, l)

        if need_ctx:
            wna = NA_HEADS * NA_DH
            yb_z = _flash(pz, pz, pz, C_NA_Q // wna, C_NA_K // wna, C_NA_V // wna,
                          NA_HEADS, NA_DH, NA_DH, NA_DH ** -0.5 * LOG2_E, 256, 256)
            yc_z = _mla_attn(q_all, k_all, v_all, t, lz, t, lz, lz, lz)
            z = _post(yaf_z, yab_z, yb_z, yc_z, pz, z, mod, b, wa, wb, wc, wo, w1, w2, ln_gain, ln_bias, l)
    return x
```

```python
import functools

import numpy as np
import jax
import jax.numpy as jnp
from jax import lax
from jax.experimental import pallas as pl
from jax.experimental.pallas import tpu as pltpu

F32 = jnp.float32
BF16 = jnp.bfloat16

D_MODEL = 1024
GRID_W = 64
RET_HEADS = 4
RET_DK = 64
RET_DV = 128
RET_CHUNK = 256
NA_HEADS = 8
NA_DH = 64
NA_KH = 8
NA_KW = 16
MLA_HEADS = 8
MLA_RANK = 256
MLA_NOPE = 64
MLA_ROPE = 32
MLA_DV = 64
MLA_DK_PAD = 128
D_FF = 4 * D_MODEL
ROPE_BASE = 10000.0
EPS = 1e-5
DEPTH_FOR_NORM = 4
DEEPNORM_ALPHA = (2 * DEPTH_FOR_NORM) ** 0.25
MASK_VALUE = -1e30
LOG2_E = 1.4426950408889634

C_RET_Q, C_RET_K, C_RET_V, C_RET_GF, C_RET_GB = 0, 256, 512, 1024, 1536
C_NA_Q, C_NA_K, C_NA_V = 2048, 2560, 3072
C_MLA_Q, C_MLA_KV = 3584, 3840
C_GATE_A, C_GATE_B, C_GATE_C = 4096, 5120, 6144
C_MLA_KR = 7168
IN_WIDTH = 7200
P_WIDTH = 7296
P_COL_TILE = 2432

NA_QROWS = 4
NA_KROWS = 12
NA_QB = NA_QROWS * GRID_W
NA_KB = NA_KROWS * GRID_W
NA_STEP_GROUPS = 1

VMEM_LIMIT = 56 * 1024 * 1024


def _cparams(*sem):
    return pltpu.CompilerParams(dimension_semantics=sem, vmem_limit_bytes=VMEM_LIMIT)


def _tile(n, pref, mult=1):
    t = min(n, pref) // mult * mult
    while n % t:
        t -= mult
    return t


def _const_spec(shape):
    nd = len(shape)
    return pl.BlockSpec(shape, lambda *_: (0,) * nd)


MOD_SH1, MOD_SC1, MOD_G1, MOD_SH2, MOD_SC2, MOD_G2 = range(6)


def _mod_spec(d, layer, which, ctx_row, batch_axis):
    def index(*g):
        return (layer, g[batch_axis] if ctx_row is None else ctx_row, which, 0, 0)
    return pl.BlockSpec((1, 1, 1, 1, d), index)


def _layer_spec(shape, layer):
    nd = len(shape)
    return pl.BlockSpec((1,) + tuple(shape), lambda *_: (layer,) + (0,) * nd)


def _ada_kernel(c_ref, w_ref, b_ref, o_ref):
    c = c_ref[...]
    a = c * jax.nn.sigmoid(c)
    o_ref[0] = jnp.dot(a.astype(BF16), w_ref[0].astype(BF16), preferred_element_type=F32) + b_ref[0]


def _ada(cc, w_ada, b_ada):
    depth, d, n = w_ada.shape
    tn = _tile(n, 3072, 128)
    return pl.pallas_call(
        _ada_kernel,
        grid=(depth, n // tn),
        in_specs=[
            pl.BlockSpec((8, d), lambda l, j: (0, 0)),
            pl.BlockSpec((1, d, tn), lambda l, j: (l, 0, j)),
            pl.BlockSpec((1, 1, tn), lambda l, j: (l, 0, j)),
        ],
        out_specs=pl.BlockSpec((1, 8, tn), lambda l, j: (l, 0, j)),
        out_shape=jax.ShapeDtypeStruct((depth, 8, n), F32),
        compiler_params=_cparams("parallel", "parallel"),
        name="ada",
    )(cc, w_ada, b_ada.reshape(depth, 1, n))


def _inproj_kernel(x_ref, sc_ref, sh_ref, w_ref, o_ref):
    h = x_ref[0] * (1.0 + sc_ref[0, 0, 0]) + sh_ref[0, 0, 0]
    o_ref[0] = jnp.dot(h.astype(BF16), w_ref[0], preferred_element_type=F32)


def _inproj(x, mod, ctx_row, w, layer):
    b, t, d = x.shape
    tm = _tile(t, 1024)
    tn = P_COL_TILE
    return pl.pallas_call(
        _inproj_kernel,
        grid=(P_WIDTH // tn, b, t // tm),
        in_specs=[
            pl.BlockSpec((1, tm, d), lambda j, bi, i: (bi, i, 0)),
            _mod_spec(d, layer, MOD_SC1, ctx_row, 1),
            _mod_spec(d, layer, MOD_SH1, ctx_row, 1),
            pl.BlockSpec((1, d, tn), lambda j, bi, i: (layer, 0, j)),
        ],
        out_specs=pl.BlockSpec((1, tm, tn), lambda j, bi, i: (bi, i, j)),
        out_shape=jax.ShapeDtypeStruct((b, t, P_WIDTH), F32),
        compiler_params=_cparams("parallel", "parallel", "parallel"),
        name="inproj",
    )(x, mod, mod, w)


def _swap_halves(x, half):
    n = x.shape[-1]
    lane = lax.broadcasted_iota(jnp.int32, x.shape, x.ndim - 1)
    first = (lane % (2 * half)) < half
    return jnp.where(first, pltpu.roll(x, n - half, x.ndim - 1), pltpu.roll(x, half, x.ndim - 1))


def _ret_kernel(lg_ref, qf_ref, qb_ref, kf_ref, kb_ref, vf_ref, vb_ref, gf_ref, gb_ref,
                cosf_ref, cosb_ref, sinf_ref, sinb_ref, gain_ref, s0_ref,
                yf_ref, yb_ref, sf_ref, s_scr, *, use_rope, n_chunks, layer):
    c = pl.program_id(0)
    batch, cc = qf_ref.shape[:2]
    lg_base = layer * 2 * RET_HEADS

    @pl.when(c == 0)
    def _():
        s_scr[...] = s0_ref[...]

    gain = gain_ref[0]
    row = lax.broadcasted_iota(jnp.int32, (cc, cc), 0).astype(F32)
    col = lax.broadcasted_iota(jnp.int32, (cc, cc), 1).astype(F32)
    wqk = RET_HEADS * RET_DK
    pos = lax.broadcasted_iota(jnp.int32, (cc, wqk), 0).astype(F32)
    head_of_lane = lax.broadcasted_iota(jnp.int32, (cc, wqk), 1) // RET_DK
    dirs = (
        (qf_ref, kf_ref, vf_ref, gf_ref, cosf_ref, sinf_ref, yf_ref, row - col, pos + 1.0, cc - 1.0 - pos),
        (qb_ref, kb_ref, vb_ref, gb_ref, cosb_ref, sinb_ref, yb_ref, col - row, cc - pos, pos),
    )
    for d, (q_ref, k_ref, v_ref, g_ref, cos_ref, sin_ref, y_ref, diff, q_exp, k_exp) in enumerate(dirs):
        if use_rope:
            cos = jnp.concatenate([cos_ref[...], cos_ref[...]], axis=1)
            sin = jnp.concatenate([sin_ref[...], sin_ref[...]], axis=1)
        lgs = [lg_ref[lg_base + d * RET_HEADS + h] for h in range(RET_HEADS)]
        lg_lanes = jnp.full((cc, wqk), lgs[RET_HEADS - 1], F32)
        for h in range(RET_HEADS - 2, -1, -1):
            lg_lanes = jnp.where(head_of_lane == h, lgs[h], lg_lanes)
        q_decay = jnp.exp(lg_lanes * q_exp)
        k_decay = jnp.exp(lg_lanes * k_exp)
        for b in range(batch):
            q = q_ref[b]
            k = k_ref[b] * (RET_DK ** -0.5)
            if use_rope:
                q = q * cos + _swap_halves(q, RET_DK // 2) * sin
                k = k * cos + _swap_halves(k, RET_DK // 2) * sin
            k_bf = k.astype(BF16)
            kt_decayed = (k * k_decay).T.astype(BF16)
            q_decayed = q * q_decay
            state_bf = s_scr[b, d].astype(BF16)
            for h in range(RET_HEADS):
                mine = head_of_lane == h
                sl = slice(h * RET_DV, (h + 1) * RET_DV)
                rows = slice(h * RET_DK, (h + 1) * RET_DK)
                decay = jnp.where(diff >= 0, jnp.exp(lgs[h] * jnp.maximum(diff, 0.0)), 0.0)
                vh = v_ref[b, :, sl].astype(BF16)
                scores = lax.dot_general(jnp.where(mine, q, 0.0).astype(BF16), k_bf, (((1,), (1,)), ((), ())),
                                         preferred_element_type=F32) * decay
                inner = jnp.dot(scores.astype(BF16), vh, preferred_element_type=F32)
                cross = jnp.dot(jnp.where(mine, q_decayed, 0.0).astype(BF16), state_bf, preferred_element_type=F32)
                o = inner + cross
                kv = jnp.dot(kt_decayed[rows], vh, preferred_element_type=F32)
                chunk_decay = jnp.exp(lgs[h] * jnp.full((RET_DK, RET_DV), float(cc), F32))
                s_scr[b, d, rows] = s_scr[b, d, rows] * chunk_decay + kv
                mu = jnp.mean(o, axis=-1, keepdims=True)
                oc = o - mu
                var = jnp.mean(oc * oc, axis=-1, keepdims=True)
                gate = g_ref[b, :, sl]
                y_ref[b, :, sl] = (gate * jax.nn.sigmoid(gate)) * (oc * lax.rsqrt(var + EPS) * gain[:, sl])

    @pl.when(c == n_chunks - 1)
    def _():
        sf_ref[...] = s_scr[...]


def _retention(p, lg, cos, sin, gain, s0, use_rope, layer):
    b, t, _ = p.shape
    cc = _tile(t, RET_CHUNK, 8)
    n = t // cc
    wv = RET_HEADS * RET_DV

    def pspec(width, col, backward):
        if backward:
            return pl.BlockSpec((b, cc, width), lambda ci, lg_: (0, n - 1 - ci, col // width))
        return pl.BlockSpec((b, cc, width), lambda ci, lg_: (0, ci, col // width))

    def tspec(backward):
        if backward:
            return pl.BlockSpec((cc, 128), lambda ci, lg_: (n - 1 - ci, 0))
        return pl.BlockSpec((cc, 128), lambda ci, lg_: (ci, 0))

    state_spec = pl.BlockSpec((b, 2, RET_HEADS * RET_DK, RET_DV), lambda ci, lg_: (0, 0, 0, 0))
    kern = functools.partial(_ret_kernel, use_rope=use_rope, n_chunks=n, layer=layer)
    grid_spec = pltpu.PrefetchScalarGridSpec(
        num_scalar_prefetch=1,
        grid=(n,),
        in_specs=[
            pspec(256, C_RET_Q, False), pspec(256, C_RET_Q, True),
            pspec(256, C_RET_K, False), pspec(256, C_RET_K, True),
            pspec(wv, C_RET_V, False), pspec(wv, C_RET_V, True),
            pspec(wv, C_RET_GF, False), pspec(wv, C_RET_GB, True),
            tspec(False), tspec(True), tspec(False), tspec(True),
            pl.BlockSpec((1, 1, wv), lambda ci, lg_: (layer, 0, 0)),
            state_spec,
        ],
        out_specs=[
            pl.BlockSpec((b, cc, wv), lambda ci, lg_: (0, ci, 0)),
            pl.BlockSpec((b, cc, wv), lambda ci, lg_: (0, n - 1 - ci, 0)),
            state_spec,
        ],
        scratch_shapes=[pltpu.VMEM((b, 2, RET_HEADS * RET_DK, RET_DV), F32)],
    )
    return pl.pallas_call(
        kern,
        grid_spec=grid_spec,
        out_shape=[
            jax.ShapeDtypeStruct((b, t, wv), F32),
            jax.ShapeDtypeStruct((b, t, wv), F32),
            jax.ShapeDtypeStruct((b, 2, RET_HEADS * RET_DK, RET_DV), F32),
        ],
        compiler_params=_cparams("arbitrary"),
        name="retention",
    )(lg, p, p, p, p, p, p, p, p, cos, cos, sin, sin, gain, s0)


def _na_tables(rows):
    groups = rows // NA_QROWS
    cols = np.arange(GRID_W)
    c0 = np.clip(cols - NA_KW // 2, 0, GRID_W - NA_KW)
    col_ok = (cols[None, :] >= c0[:, None]) & (cols[None, :] < c0[:, None] + NA_KW)
    dc = cols[None, :] - cols[:, None] + (NA_KW - 1)
    onehot = (dc[None] == np.arange(2 * NA_KW - 1)[:, None, None]) & col_ok[None]
    row_bias = np.full((3, NA_QROWS, NA_KROWS), 2 * NA_KH - 1, np.int64)
    for ti, g in enumerate((0, 1, groups - 1)):
        ws = int(np.clip(NA_QROWS * g - NA_KH // 2, 0, rows - NA_KROWS))
        for lr in range(NA_QROWS):
            r = NA_QROWS * g + lr
            r0 = int(np.clip(r - NA_KH // 2, 0, rows - NA_KH))
            for kr in range(NA_KROWS):
                if r0 <= ws + kr < r0 + NA_KH:
                    row_bias[ti, lr, kr] = ws + kr - r + (NA_KH - 1)
    return onehot, col_ok, row_bias


def _na_bias_tables(na_rpb, rows):
    onehot, col_ok, row_bias = _na_tables(rows)
    depth, heads = na_rpb.shape[:2]
    nd = 2 * NA_KH
    toep = jnp.einsum('lhdj,jck->lhdck', na_rpb, onehot.astype(np.float32), precision=lax.Precision.HIGHEST)
    toep = jnp.where(col_ok, toep * LOG2_E, MASK_VALUE)
    masked = jnp.full(toep.shape[:2] + (1, GRID_W, GRID_W), MASK_VALUE, F32)
    toep = jnp.concatenate([toep, masked], axis=2)
    sides = jnp.concatenate([jnp.pad(toep, ((0, 0),) * 4 + ((0, GRID_W),)),
                             jnp.pad(toep, ((0, 0),) * 4 + ((GRID_W, 0),))], axis=2)
    pairs = row_bias.reshape(-1, 2)
    pick = ((pairs[:, :1] == np.arange(nd)[None, :]).astype(np.float32),
            (pairs[:, 1:] == np.arange(nd)[None, :]).astype(np.float32))
    pick = np.concatenate(pick, axis=1)
    tab = jnp.einsum('nd,lhdck->lhnck', pick, sides, precision=lax.Precision.HIGHEST)
    return tab.reshape(depth, heads, 3, NA_QROWS * NA_KROWS // 2, GRID_W, 2 * GRID_W)


def _na_kernel(*refs):
    kz_ref, vz_ref, o_ref = refs[-3:]
    lanes = 2 * NA_DH
    npair = NA_KROWS // 2
    low = lax.broadcasted_iota(jnp.int32, (NA_QB, lanes), 1) < NA_DH
    for gi in range(NA_STEP_GROUPS):
        q_ref, k0_ref, k1_ref, k2_ref, v0_ref, v1_ref, v2_ref, bias_ref = refs[8 * gi:8 * gi + 8]
        for pr in range(NA_HEADS // 2):
            sl = slice(pr * lanes, (pr + 1) * lanes)
            qp = q_ref[0, :, sl] * (NA_DH ** -0.5 * LOG2_E)
            qm = (jnp.where(low, qp, 0.0).astype(BF16), jnp.where(low, 0.0, qp).astype(BF16))
            ks = [r[0, :, sl].astype(BF16) for r in (k0_ref, k1_ref, k2_ref, kz_ref)]
            vs = [r[0, :, sl].astype(BF16) for r in (v0_ref, v1_ref, v2_ref, vz_ref)]
            outs = []
            for hh in range(2):
                h = 2 * pr + hh
                cols = []
                for j in range(4):
                    sj = lax.dot_general(qm[hh], ks[j], (((1,), (1,)), ((), ())), preferred_element_type=F32)
                    halves = [sj[:, i * lanes:(i + 1) * lanes] for i in range(sj.shape[1] // lanes)]
                    if j < 3:
                        halves = [hv + jnp.concatenate([bias_ref[0, h, 0, lr * npair + j * len(halves) + i]
                                                        for lr in range(NA_QROWS)], axis=0)
                                  for i, hv in enumerate(halves)]
                    cols += halves
                m = jnp.max(functools.reduce(jnp.maximum, cols), axis=-1, keepdims=True)
                ps = [jnp.exp2(cj - m) for cj in cols]
                l = jnp.sum(functools.reduce(jnp.add, ps), axis=-1, keepdims=True)
                o = None
                per = len(cols) // 4
                for j in range(4):
                    pj = jnp.concatenate([pc.astype(BF16) for pc in ps[j * per:(j + 1) * per]], axis=1)
                    oj = jnp.dot(pj, vs[j], preferred_element_type=F32)
                    o = oj if o is None else o + oj
                outs.append(o / l)
            o_ref[0, gi * NA_QB:(gi + 1) * NA_QB, sl] = jnp.where(low, outs[0], outs[1]).astype(o_ref.dtype)


def _na(px, pz, bias, layer):
    b, t, _ = px.shape
    lz = pz.shape[1]
    groups = t // NA_QB
    w = NA_HEADS * NA_DH
    kblk = NA_KB // 3
    ng = NA_STEP_GROUPS
    assert lz == kblk and groups >= 3 and groups % ng == 0

    def group_specs(gi):
        def kspec(col, off):
            return pl.BlockSpec((1, kblk, w),
                                lambda bi, s: (bi, jnp.clip(ng * s + gi - 1, 0, groups - 3) + off, col // w))

        def tab(s):
            g = ng * s + gi
            return jnp.where(g == 0, 0, jnp.where(g == groups - 1, 2, 1))

        return [
            pl.BlockSpec((1, NA_QB, w), lambda bi, s: (bi, ng * s + gi, C_NA_Q // w)),
            kspec(C_NA_K, 0), kspec(C_NA_K, 1), kspec(C_NA_K, 2),
            kspec(C_NA_V, 0), kspec(C_NA_V, 1), kspec(C_NA_V, 2),
            pl.BlockSpec((1, NA_HEADS, 1) + bias.shape[3:], lambda bi, s: (layer, 0, tab(s), 0, 0, 0)),
        ]

    in_specs, args = [], []
    for gi in range(ng):
        in_specs += group_specs(gi)
        args += [px] * 7 + [bias]
    in_specs += [pl.BlockSpec((1, lz, w), lambda bi, s: (bi, 0, C_NA_K // w)),
                 pl.BlockSpec((1, lz, w), lambda bi, s: (bi, 0, C_NA_V // w))]
    args += [pz, pz]
    return pl.pallas_call(
        _na_kernel,
        grid=(b, groups // ng),
        in_specs=in_specs,
        out_specs=pl.BlockSpec((1, ng * NA_QB, w), lambda bi, s: (bi, s, 0)),
        out_shape=jax.ShapeDtypeStruct((b, t, w), BF16),
        compiler_params=_cparams("parallel", "arbitrary"),
        name="na",
    )(*args)


def _rms(x, gain):
    return x * lax.rsqrt(jnp.mean(x * x, axis=-1, keepdims=True) + EPS) * gain


def _mlaprep_kernel(xq_ref, xkv_ref, xkr_ref, zq_ref, zkv_ref, zkr_ref, qn_ref, kvn_ref, wq_ref, wqs_ref, wk_ref,
                    wv_ref, cos_ref, sin_ref, q_ref, k_ref, v_ref, *, nx):
    latent = pl.program_id(1) < nx
    pq = jnp.where(latent, xq_ref[0], zq_ref[0])
    pkv = jnp.where(latent, xkv_ref[0], zkv_ref[0])
    pkr = jnp.where(latent, xkr_ref[0], zkr_ref[0])
    hq = _rms(pq, qn_ref[0]).astype(BF16)
    hkv = _rms(pkv, kvn_ref[0]).astype(BF16)
    q = jnp.dot(hq, wq_ref[0], preferred_element_type=F32)
    q_swapped = jnp.dot(hq, wqs_ref[0], preferred_element_type=F32)
    k = jnp.dot(hkv, wk_ref[0], preferred_element_type=F32)
    v = jnp.dot(hkv, wv_ref[0], preferred_element_type=F32)
    kr = pltpu.roll(pkr, MLA_NOPE, 1)
    k = k + jnp.concatenate([kr] * MLA_HEADS, axis=1)
    k_swapped = jnp.concatenate([_swap_halves(kr, MLA_ROPE // 2)] * MLA_HEADS, axis=1)
    cos = jnp.concatenate([cos_ref[...]] * MLA_HEADS, axis=1)
    sin = jnp.concatenate([sin_ref[...]] * MLA_HEADS, axis=1)
    q = q * cos + q_swapped * sin
    k = k * cos + k_swapped * sin
    q_ref[0] = (q * ((MLA_NOPE + MLA_ROPE) ** -0.5 * LOG2_E)).astype(BF16)
    k_ref[0] = k.astype(BF16)
    v_ref[0] = v.astype(BF16)


def _mlaprep(px, pz, qn, kvn, wq, wqs, wk, wv, cos, sin, layer):
    b, t, _ = px.shape
    lz = pz.shape[1]
    tm = lz
    nx = t // tm
    wqk = MLA_HEADS * MLA_DK_PAD
    wvv = MLA_HEADS * MLA_DV

    def xspec(width, col):
        return pl.BlockSpec((1, tm, width), lambda bi, i: (bi, jnp.minimum(i, nx - 1), col // width))

    def zspec(width, col):
        return pl.BlockSpec((1, tm, width), lambda bi, i: (bi, 0, col // width))

    out_spec = pl.BlockSpec((1, tm, wqk), lambda bi, i: (bi, i, 0))
    kern = functools.partial(_mlaprep_kernel, nx=nx)
    return pl.pallas_call(
        kern,
        grid=(b, nx + 1),
        in_specs=[
            xspec(MLA_RANK, C_MLA_Q), xspec(MLA_RANK, C_MLA_KV), xspec(128, C_MLA_KR),
            zspec(MLA_RANK, C_MLA_Q), zspec(MLA_RANK, C_MLA_KV), zspec(128, C_MLA_KR),
            _layer_spec((1, MLA_RANK), layer), _layer_spec((1, MLA_RANK), layer),
            _layer_spec((MLA_RANK, wqk), layer), _layer_spec((MLA_RANK, wqk), layer),
            _layer_spec((MLA_RANK, wqk), layer), _layer_spec((MLA_RANK, wvv), layer),
            pl.BlockSpec((tm, 128), lambda bi, i: (i, 0)),
            pl.BlockSpec((tm, 128), lambda bi, i: (i, 0)),
        ],
        out_specs=[out_spec, out_spec, pl.BlockSpec((1, tm, wvv), lambda bi, i: (bi, i, 0))],
        out_shape=[jax.ShapeDtypeStruct((b, t + lz, wqk), BF16)] * 2
        + [jax.ShapeDtypeStruct((b, t + lz, wvv), BF16)],
        compiler_params=_cparams("parallel", "parallel"),
        name="mlaprep",
    )(px, px, px, pz, pz, pz, qn, kvn, wq, wqs, wk, wv, cos, sin)


def _flash_kernel(q_ref, k_ref, v_ref, o_ref, m_scr, l_scr, acc_scr, *, heads, dk, dv, scale, nk):
    ki = pl.program_id(2)
    tq = q_ref.shape[1]
    tk = k_ref.shape[1]
    lanes = 2 * dv

    @pl.when(ki == 0)
    def _():
        m_scr[...] = jnp.full(m_scr.shape, -jnp.inf, F32)
        l_scr[...] = jnp.zeros(l_scr.shape, F32)
        acc_scr[...] = jnp.zeros(acc_scr.shape, F32)

    low = lax.broadcasted_iota(jnp.int32, (tq, lanes), 1) < dv
    for pr in range(heads // 2):
        vp = v_ref[0, :, pr * lanes:(pr + 1) * lanes].astype(BF16)
        alphas, pvs = [], []
        for h in (2 * pr, 2 * pr + 1):
            qh = q_ref[0, :, h * dk:(h + 1) * dk]
            if scale != 1.0:
                qh = qh * scale
            kh = k_ref[0, :, h * dk:(h + 1) * dk]
            s = lax.dot_general(qh.astype(BF16), kh.astype(BF16), (((1,), (1,)), ((), ())),
                                preferred_element_type=F32)
            cols = [s[:, j * lanes:(j + 1) * lanes] for j in range(tk // lanes)]
            m_prev = m_scr[h]
            m_tile = jnp.max(functools.reduce(jnp.maximum, cols), axis=-1, keepdims=True)
            m_new = jnp.maximum(m_prev, m_tile)
            alpha = jnp.exp2(m_prev - m_new)
            ps = [jnp.exp2(cj - m_new) for cj in cols]
            l_scr[h] = alpha * l_scr[h] + functools.reduce(jnp.add, ps)
            m_scr[h] = m_new
            p = jnp.concatenate([pj.astype(BF16) for pj in ps], axis=1)
            pvs.append(jnp.dot(p, vp, preferred_element_type=F32))
            alphas.append(alpha)
        acc_scr[pr] = acc_scr[pr] * jnp.where(low, alphas[0], alphas[1]) + jnp.where(low, pvs[0], pvs[1])

    @pl.when(ki == nk - 1)
    def _():
        for pr in range(heads // 2):
            l0 = jnp.sum(l_scr[2 * pr], axis=-1, keepdims=True)
            l1 = jnp.sum(l_scr[2 * pr + 1], axis=-1, keepdims=True)
            o_ref[0, :, pr * lanes:(pr + 1) * lanes] = (acc_scr[pr] / jnp.where(low, l0, l1)).astype(o_ref.dtype)


def _flash(q, k, v, qcol, kcol, vcol, heads, dk, dv, scale, tq_pref, tk_pref):
    b, tq_all, _ = q.shape
    tk_all = k.shape[1]
    tq = _tile(tq_all, tq_pref)
    tk = _tile(tk_all, tk_pref)
    nk = tk_all // tk
    assert heads % 2 == 0 and 2 * dv == 128 and tk % 128 == 0
    kern = functools.partial(_flash_kernel, heads=heads, dk=dk, dv=dv, scale=scale, nk=nk)
    return pl.pallas_call(
        kern,
        grid=(b, tq_all // tq, nk),
        in_specs=[
            pl.BlockSpec((1, tq, heads * dk), lambda bi, i, j: (bi, i, qcol)),
            pl.BlockSpec((1, tk, heads * dk), lambda bi, i, j: (bi, j, kcol)),
            pl.BlockSpec((1, tk, heads * dv), lambda bi, i, j: (bi, j, vcol)),
        ],
        out_specs=pl.BlockSpec((1, tq, heads * dv), lambda bi, i, j: (bi, i, 0)),
        out_shape=jax.ShapeDtypeStruct((b, tq_all, heads * dv), BF16),
        scratch_shapes=[
            pltpu.VMEM((heads, tq, 2 * dv), F32),
            pltpu.VMEM((heads, tq, 2 * dv), F32),
            pltpu.VMEM((heads // 2, tq, 2 * dv), F32),
        ],
        compiler_params=_cparams("parallel", "parallel", "arbitrary"),
        name="flash",
    )(q, k, v)


def _mla_attn_kernel(q_ref, k_ref, v_ref, o_ref, m_scr, l_scr, acc_scr, *, heads, nk):
    ki = pl.program_id(2)
    w = MLA_DK_PAD
    tq = q_ref.shape[1]

    @pl.when(ki == 0)
    def _():
        m_scr[...] = jnp.full(m_scr.shape, -jnp.inf, F32)
        l_scr[...] = jnp.zeros(l_scr.shape, F32)
        acc_scr[...] = jnp.zeros(acc_scr.shape, F32)

    low = lax.broadcasted_iota(jnp.int32, (tq, w), 1) < MLA_DV
    for pr in range(heads // 2):
        vp = v_ref[0, :, pr * w:(pr + 1) * w]
        alphas, pvs = [], []
        for h in (2 * pr, 2 * pr + 1):
            sl = slice(h * w, (h + 1) * w)
            s = lax.dot_general(q_ref[0, :, sl], k_ref[0, :, sl], (((1,), (1,)), ((), ())),
                                preferred_element_type=F32)
            cols = [s[:, j * w:(j + 1) * w] for j in range(s.shape[1] // w)]
            m_prev = m_scr[h]
            m_new = jnp.maximum(m_prev, jnp.max(functools.reduce(jnp.maximum, cols), axis=-1, keepdims=True))
            alpha = jnp.exp2(m_prev - m_new)
            ps = [jnp.exp2(cj - m_new) for cj in cols]
            l_scr[h] = alpha * l_scr[h] + functools.reduce(jnp.add, ps)
            m_scr[h] = m_new
            p = jnp.concatenate([pj.astype(BF16) for pj in ps], axis=1)
            pvs.append(jnp.dot(p, vp, preferred_element_type=F32))
            alphas.append(alpha)
        acc_scr[pr] = acc_scr[pr] * jnp.where(low, alphas[0], alphas[1]) + jnp.where(low, pvs[0], pvs[1])

    @pl.when(ki == nk - 1)
    def _():
        for pr in range(heads // 2):
            l0 = jnp.sum(l_scr[2 * pr], axis=-1, keepdims=True)
            l1 = jnp.sum(l_scr[2 * pr + 1], axis=-1, keepdims=True)
            o_ref[0, :, pr * w:(pr + 1) * w] = (acc_scr[pr] / jnp.where(low, l0, l1)).astype(o_ref.dtype)


def _mla_attn(q, k, v, q_start, q_len, k_start, k_len, tq_pref, tk_pref):
    b, _, wq = q.shape
    heads = wq // MLA_DK_PAD
    tq = _tile(q_len, tq_pref, 8)
    tk = _tile(k_len, tk_pref, 128)
    assert q_start % tq == 0 and k_start % tk == 0
    q_off, k_off = q_start // tq, k_start // tk
    nk = k_len // tk
    kern = functools.partial(_mla_attn_kernel, heads=heads, nk=nk)
    return pl.pallas_call(
        kern,
        grid=(b, q_len // tq, nk),
        in_specs=[
            pl.BlockSpec((1, tq, wq), lambda bi, i, j: (bi, q_off + i, 0)),
            pl.BlockSpec((1, tk, wq), lambda bi, i, j: (bi, k_off + j, 0)),
            pl.BlockSpec((1, tk, heads * MLA_DV), lambda bi, i, j: (bi, k_off + j, 0)),
        ],
        out_specs=pl.BlockSpec((1, tq, heads * MLA_DV), lambda bi, i, j: (bi, i, 0)),
        out_shape=jax.ShapeDtypeStruct((b, q_len, heads * MLA_DV), BF16),
        scratch_shapes=[
            pltpu.VMEM((heads, tq, MLA_DK_PAD), F32),
            pltpu.VMEM((heads, tq, MLA_DK_PAD), F32),
            pltpu.VMEM((heads // 2, tq, MLA_DK_PAD), F32),
        ],
        compiler_params=_cparams("parallel", "parallel", "arbitrary"),
        name="mla_attn",
    )(q, k, v)


def _layer_norm(r, gain, bias):
    mu = jnp.mean(r, axis=-1, keepdims=True)
    rc = r - mu
    var = jnp.mean(rc * rc, axis=-1, keepdims=True)
    return rc * lax.rsqrt(var + EPS) * gain + bias


def _post_kernel(yaf_ref, yab_ref, yb_ref, yc_ref, ga_ref, gb_ref, gc_ref, x_ref, g1_ref, sc_ref, sh_ref, g2_ref,
                 wa_ref, wb_ref, wc_ref, wo_ref, w1_ref, w2_ref, lng_ref, lnb_ref, o_ref, *, ff_chunk):
    ya = (yaf_ref[0] + yab_ref[0]).astype(BF16)
    y = (jax.nn.sigmoid(ga_ref[0]) * jnp.dot(ya, wa_ref[0], preferred_element_type=F32)
         + jax.nn.sigmoid(gb_ref[0]) * jnp.dot(yb_ref[0].astype(BF16), wb_ref[0], preferred_element_type=F32)
         + jax.nn.sigmoid(gc_ref[0]) * jnp.dot(yc_ref[0].astype(BF16), wc_ref[0], preferred_element_type=F32))
    mix = jnp.dot(y.astype(BF16), wo_ref[0], preferred_element_type=F32)
    x1 = _layer_norm(DEEPNORM_ALPHA * x_ref[0] + g1_ref[0, 0, 0] * mix, lng_ref[0, 0:1], lnb_ref[0, 0:1])
    h = (x1 * (1.0 + sc_ref[0, 0, 0]) + sh_ref[0, 0, 0]).astype(BF16)
    acc = jnp.zeros(x1.shape, F32)
    for j in range(D_FF // ff_chunk):
        u = jnp.dot(h, w1_ref[0, :, j * ff_chunk:(j + 1) * ff_chunk], preferred_element_type=F32)
        u = jnp.square(jnp.maximum(u, 0.0)).astype(BF16)
        acc = acc + jnp.dot(u, w2_ref[0, j * ff_chunk:(j + 1) * ff_chunk, :], preferred_element_type=F32)
    o_ref[0] = _layer_norm(DEEPNORM_ALPHA * x1 + g2_ref[0, 0, 0] * acc, lng_ref[0, 1:2], lnb_ref[0, 1:2])


def _post(yaf, yab, yb, yc, p, x, mod, ctx_row, wa, wb, wc, wo, w1, w2, lng, lnb, layer):
    b, t, d = x.shape
    tm = _tile(t, 512)
    wbr = 512
    row = lambda width, col: pl.BlockSpec((1, tm, width), lambda bi, i: (bi, i, col // width))
    mods = [_mod_spec(d, layer, which, ctx_row, 0) for which in (MOD_G1, MOD_SC2, MOD_SH2, MOD_G2)]
    kern = functools.partial(_post_kernel, ff_chunk=1024)
    return pl.pallas_call(
        kern,
        grid=(b, t // tm),
        in_specs=[
            row(wbr, 0), row(wbr, 0), row(wbr, 0), row(wbr, 0),
            row(d, C_GATE_A), row(d, C_GATE_B), row(d, C_GATE_C), row(d, 0),
            *mods,
            _layer_spec((wbr, d), layer), _layer_spec((wbr, d), layer), _layer_spec((wbr, d), layer),
            _layer_spec((d, d), layer), _layer_spec((d, D_FF), layer), _layer_spec((D_FF, d), layer),
            _layer_spec((2, d), layer), _layer_spec((2, d), layer),
        ],
        out_specs=row(d, 0),
        out_shape=jax.ShapeDtypeStruct((b, t, d), F32),
        compiler_params=_cparams("parallel", "parallel"),
        name="post",
    )(yaf, yab, yb, yc, p, p, p, x, mod, mod, mod, mod, wa, wb, wc, wo, w1, w2, lng, lnb)


def _rope_tables(n_tok, rot_dim):
    t = jnp.arange(n_tok)
    row = (t // GRID_W).astype(F32)
    col = (t % GRID_W).astype(F32)
    n_freq = rot_dim // 4
    inv_freq = ROPE_BASE ** (-2.0 * jnp.arange(n_freq, dtype=F32) / (rot_dim // 2))
    ang = jnp.concatenate([row[:, None] * inv_freq, col[:, None] * inv_freq], axis=-1)
    return jnp.cos(ang), jnp.sin(ang)


def _ret_rope_tables(n_tok):
    cos, sin = _rope_tables(n_tok, RET_DK)
    cos_h = jnp.concatenate([cos, cos], axis=1)
    sin_h = jnp.concatenate([-sin, sin], axis=1)
    return jnp.tile(cos_h, (1, 2)), jnp.tile(sin_h, (1, 2))


def _mla_rope_tables(n_tok):
    cos, sin = _rope_tables(n_tok, MLA_ROPE)
    ones = jnp.ones((n_tok, MLA_NOPE), F32)
    zeros = jnp.zeros((n_tok, MLA_NOPE), F32)
    pad = jnp.zeros((n_tok, MLA_DK_PAD - MLA_NOPE - MLA_ROPE), F32)
    cos_h = jnp.concatenate([ones, cos, cos, pad], axis=1)
    sin_h = jnp.concatenate([zeros, -sin, sin, pad], axis=1)
    return cos_h, sin_h


def _pack_w_in_kernel(w_ref, o_ref):
    w = w_ref[0]
    pad = jnp.zeros((w.shape[0], P_WIDTH - IN_WIDTH), F32)
    o_ref[0] = jnp.concatenate([w[:, :C_GATE_A], w[:, C_GATE_A + MLA_ROPE:], w[:, C_GATE_A:C_GATE_A + MLA_ROPE], pad],
                               axis=1).astype(BF16)


def _pack_w_in(w):
    depth, d, n = w.shape
    assert n == IN_WIDTH
    tr = _tile(d, 256, 8)
    return pl.pallas_call(
        _pack_w_in_kernel,
        grid=(depth, d // tr),
        in_specs=[pl.BlockSpec((1, tr, n), lambda l, i: (l, i, 0))],
        out_specs=pl.BlockSpec((1, tr, P_WIDTH), lambda l, i: (l, i, 0)),
        out_shape=jax.ShapeDtypeStruct((depth, d, P_WIDTH), BF16),
        compiler_params=_cparams("parallel", "parallel"),
        name="pack_w_in",
    )(w)


def _pack_mla_weights(w_qup, w_kvup):
    depth, r = w_qup.shape[:2]
    wq = w_qup.reshape(depth, r, MLA_HEADS, MLA_NOPE + MLA_ROPE)
    half = MLA_ROPE // 2
    wqs = jnp.concatenate([jnp.zeros((depth, r, MLA_HEADS, MLA_NOPE), wq.dtype), wq[..., MLA_NOPE + half:],
                           wq[..., MLA_NOPE:MLA_NOPE + half]], axis=-1)
    pad_q = ((0, 0), (0, 0), (0, 0), (0, MLA_DK_PAD - MLA_NOPE - MLA_ROPE))
    wq = jnp.pad(wq, pad_q).reshape(depth, r, MLA_HEADS * MLA_DK_PAD)
    wqs = jnp.pad(wqs, pad_q).reshape(depth, r, MLA_HEADS * MLA_DK_PAD)
    wkv = w_kvup.reshape(depth, r, MLA_HEADS, MLA_NOPE + MLA_DV)
    wk = jnp.pad(wkv[..., :MLA_NOPE], ((0, 0), (0, 0), (0, 0), (0, MLA_DK_PAD - MLA_NOPE)))
    wk = wk.reshape(depth, r, MLA_HEADS * MLA_DK_PAD)
    wv = wkv[..., MLA_NOPE:].reshape(depth, r, MLA_HEADS * MLA_DV)
    return wq.astype(BF16), wqs.astype(BF16), wk.astype(BF16), wv.astype(BF16)


def kernel(x, c, ctx, c_ctx, w_ada, b_ada, w_in, ret_log_decay, ret_gn_gain, na_rpb, mla_q_norm, mla_w_qup,
           mla_kv_norm, mla_w_kvup, w_branch_ret, w_branch_na, w_branch_mla, w_out, w_ff1, w_ff2, ln_gain, ln_bias):
    depth = w_ada.shape[0]
    b, t, d = x.shape
    lz = ctx.shape[1]
    rows = t // GRID_W

    cc = jnp.zeros((8, d), F32).at[:b].set(c).at[b].set(c_ctx)
    mod = _ada(cc, w_ada, b_ada).reshape(depth, 8, 6, 1, d)

    cos_r, sin_r = _ret_rope_tables(t)
    cos_m, sin_m = _mla_rope_tables(t)
    cos_m = jnp.concatenate([cos_m, jnp.ones((lz, MLA_DK_PAD), F32)], axis=0)
    sin_m = jnp.concatenate([sin_m, jnp.zeros((lz, MLA_DK_PAD), F32)], axis=0)
    cos_rz, sin_rz = cos_r[:lz], sin_r[:lz]
    na_bias = _na_bias_tables(na_rpb, rows)
    s_zero = jnp.zeros((b, 2, RET_HEADS * RET_DK, RET_DV), F32)

    w_in_p = _pack_w_in(w_in)
    wq, wqs, wk, wv = _pack_mla_weights(mla_w_qup, mla_w_kvup)
    wa = w_branch_ret.astype(BF16)
    wb = w_branch_na.astype(BF16)
    wc = w_branch_mla.astype(BF16)
    wo = w_out.astype(BF16)
    w1 = w_ff1.astype(BF16)
    w2 = w_ff2.astype(BF16)

    lg = jnp.log1p(-jnp.exp(ret_log_decay.astype(F32))).reshape(depth * 2 * RET_HEADS)
    gn_gain = ret_gn_gain.reshape(depth, 1, RET_HEADS * RET_DV)
    qn = mla_q_norm.reshape(depth, 1, MLA_RANK)
    kvn = mla_kv_norm.reshape(depth, 1, MLA_RANK)

    z = ctx
    for l in range(depth):
        need_ctx = l < depth - 1
        px = _inproj(x, mod, None, w_in_p, l)
        pz = _inproj(z, mod, b, w_in_p, l)

        yaf_z, yab_z, s_ctx = _retention(pz, lg, cos_rz, sin_rz, gn_gain, s_zero, False, l)
        yaf_x, yab_x, _ = _retention(px, lg, cos_r, sin_r, gn_gain, s_ctx, True, l)

        yb_x = _na(px, pz, na_bias, l)

        q_all, k_all, v_all = _mlaprep(px, pz, qn, kvn, wq, wqs, wk, wv, cos_m, sin_m, l)
        yc_x = _mla_attn(q_all, k_all, v_all, 0, t, 0, t + lz, 1024, 2816)

        x = _post(yaf_x, yab_x, yb_x, yc_x, px, x, mod, None, wa, wb, wc, wo, w1, w2, ln_gain, ln_bias, l)

        if need_ctx:
            wna = NA_HEADS * NA_DH
            yb_z = _flash(pz, pz, pz, C_NA_Q // wna, C_NA_K // wna, C_NA_V // wna,
                          NA_HEADS, NA_DH, NA_DH, NA_DH ** -0.5 * LOG2_E, 256, 256)
            yc_z = _mla_attn(q_all, k_all, v_all, t, lz, t, lz, lz, lz)
            z = _post(yaf_z, yab_z, yb_z, yc_z, pz, z, mod, b, wa, wb, wc, wo, w1, w2, ln_gain, ln_bias, l)
    return x
```

```python
import functools

import numpy as np
import jax
import jax.numpy as jnp
from jax import lax
from jax.experimental import pallas as pl
from jax.experimental.pallas import tpu as pltpu

F32 = jnp.float32
BF16 = jnp.bfloat16

D_MODEL = 1024
GRID_W = 64
RET_HEADS = 4
RET_DK = 64
RET_DV = 128
RET_CHUNK = 256
NA_HEADS = 8
NA_DH = 64
NA_KH = 8
NA_KW = 16
MLA_HEADS = 8
MLA_RANK = 256
MLA_NOPE = 64
MLA_ROPE = 32
MLA_DV = 64
MLA_DK_PAD = 128
D_FF = 4 * D_MODEL
ROPE_BASE = 10000.0
EPS = 1e-5
DEPTH_FOR_NORM = 4
DEEPNORM_ALPHA = (2 * DEPTH_FOR_NORM) ** 0.25
MASK_VALUE = -1e30
LOG2_E = 1.4426950408889634

C_RET_Q, C_RET_K, C_RET_V, C_RET_GF, C_RET_GB = 0, 256, 512, 1024, 1536
C_NA_Q, C_NA_K, C_NA_V = 2048, 2560, 3072
C_MLA_Q, C_MLA_KV = 3584, 3840
C_GATE_A, C_GATE_B, C_GATE_C = 4096, 5120, 6144
C_MLA_KR = 7168
IN_WIDTH = 7200
P_WIDTH = 7296
P_COL_TILE = 2432

NA_QROWS = 4
NA_KROWS = 12
NA_QB = NA_QROWS * GRID_W
NA_KB = NA_KROWS * GRID_W
NA_STEP_GROUPS = 1

VMEM_LIMIT = 56 * 1024 * 1024


def _cparams(*sem):
    return pltpu.CompilerParams(dimension_semantics=sem, vmem_limit_bytes=VMEM_LIMIT)


def _tile(n, pref, mult=1):
    t = min(n, pref) // mult * mult
    while n % t:
        t -= mult
    return t


def _const_spec(shape):
    nd = len(shape)
    return pl.BlockSpec(shape, lambda *_: (0,) * nd)


MOD_SH1, MOD_SC1, MOD_G1, MOD_SH2, MOD_SC2, MOD_G2 = range(6)


def _mod_spec(d, layer, which, ctx_row, batch_axis):
    def index(*g):
        return (layer, g[batch_axis] if ctx_row is None else ctx_row, which, 0, 0)
    return pl.BlockSpec((1, 1, 1, 1, d), index)


def _layer_spec(shape, layer):
    nd = len(shape)
    return pl.BlockSpec((1,) + tuple(shape), lambda *_: (layer,) + (0,) * nd)


def _ada_kernel(c_ref, w_ref, b_ref, o_ref):
    c = c_ref[...]
    a = c * jax.nn.sigmoid(c)
    o_ref[0] = jnp.dot(a.astype(BF16), w_ref[0].astype(BF16), preferred_element_type=F32) + b_ref[0]


def _ada(cc, w_ada, b_ada):
    depth, d, n = w_ada.shape
    tn = _tile(n, 3072, 128)
    return pl.pallas_call(
        _ada_kernel,
        grid=(depth, n // tn),
        in_specs=[
            pl.BlockSpec((8, d), lambda l, j: (0, 0)),
            pl.BlockSpec((1, d, tn), lambda l, j: (l, 0, j)),
            pl.BlockSpec((1, 1, tn), lambda l, j: (l, 0, j)),
        ],
        out_specs=pl.BlockSpec((1, 8, tn), lambda l, j: (l, 0, j)),
        out_shape=jax.ShapeDtypeStruct((depth, 8, n), F32),
        compiler_params=_cparams("parallel", "parallel"),
        name="ada",
    )(cc, w_ada, b_ada.reshape(depth, 1, n))


def _inproj_kernel(x_ref, sc_ref, sh_ref, w_ref, o_ref):
    h = x_ref[0] * (1.0 + sc_ref[0, 0, 0]) + sh_ref[0, 0, 0]
    o_ref[0] = jnp.dot(h.astype(BF16), w_ref[0], preferred_element_type=F32)


def _inproj(x, mod, ctx_row, w, layer):
    b, t, d = x.shape
    tm = _tile(t, 1024)
    tn = P_COL_TILE
    return pl.pallas_call(
        _inproj_kernel,
        grid=(P_WIDTH // tn, b, t // tm),
        in_specs=[
            pl.BlockSpec((1, tm, d), lambda j, bi, i: (bi, i, 0)),
            _mod_spec(d, layer, MOD_SC1, ctx_row, 1),
            _mod_spec(d, layer, MOD_SH1, ctx_row, 1),
            pl.BlockSpec((1, d, tn), lambda j, bi, i: (layer, 0, j)),
        ],
        out_specs=pl.BlockSpec((1, tm, tn), lambda j, bi, i: (bi, i, j)),
        out_shape=jax.ShapeDtypeStruct((b, t, P_WIDTH), F32),
        compiler_params=_cparams("parallel", "parallel", "parallel"),
        name="inproj",
    )(x, mod, mod, w)


def _swap_halves(x, half):
    n = x.shape[-1]
    lane = lax.broadcasted_iota(jnp.int32, x.shape, x.ndim - 1)
    first = (lane % (2 * half)) < half
    return jnp.where(first, pltpu.roll(x, n - half, x.ndim - 1), pltpu.roll(x, half, x.ndim - 1))


def _ret_kernel(lg_ref, qf_ref, qb_ref, kf_ref, kb_ref, vf_ref, vb_ref, gf_ref, gb_ref,
                cosf_ref, cosb_ref, sinf_ref, sinb_ref, gain_ref, s0_ref,
                yf_ref, yb_ref, sf_ref, s_scr, *, use_rope, n_chunks, layer):
    c = pl.program_id(0)
    batch, cc = qf_ref.shape[:2]
    lg_base = layer * 2 * RET_HEADS

    @pl.when(c == 0)
    def _():
        s_scr[...] = s0_ref[...]

    gain = gain_ref[0]
    row = lax.broadcasted_iota(jnp.int32, (cc, cc), 0).astype(F32)
    col = lax.broadcasted_iota(jnp.int32, (cc, cc), 1).astype(F32)
    wqk = RET_HEADS * RET_DK
    pos = lax.broadcasted_iota(jnp.int32, (cc, wqk), 0).astype(F32)
    head_of_lane = lax.broadcasted_iota(jnp.int32, (cc, wqk), 1) // RET_DK
    dirs = (
        (qf_ref, kf_ref, vf_ref, gf_ref, cosf_ref, sinf_ref, yf_ref, row - col, pos + 1.0, cc - 1.0 - pos),
        (qb_ref, kb_ref, vb_ref, gb_ref, cosb_ref, sinb_ref, yb_ref, col - row, cc - pos, pos),
    )
    for d, (q_ref, k_ref, v_ref, g_ref, cos_ref, sin_ref, y_ref, diff, q_exp, k_exp) in enumerate(dirs):
        if use_rope:
            cos = jnp.concatenate([cos_ref[...], cos_ref[...]], axis=1)
            sin = jnp.concatenate([sin_ref[...], sin_ref[...]], axis=1)
        lgs = [lg_ref[lg_base + d * RET_HEADS + h] for h in range(RET_HEADS)]
        lg_lanes = jnp.full((cc, wqk), lgs[RET_HEADS - 1], F32)
        for h in range(RET_HEADS - 2, -1, -1):
            lg_lanes = jnp.where(head_of_lane == h, lgs[h], lg_lanes)
        q_decay = jnp.exp(lg_lanes * q_exp)
        k_decay = jnp.exp(lg_lanes * k_exp)
        for b in range(batch):
            q = q_ref[b]
            k = k_ref[b] * (RET_DK ** -0.5)
            if use_rope:
                q = q * cos + _swap_halves(q, RET_DK // 2) * sin
                k = k * cos + _swap_halves(k, RET_DK // 2) * sin
            k_bf = k.astype(BF16)
            kt_decayed = (k * k_decay).T.astype(BF16)
            q_decayed = q * q_decay
            state_bf = s_scr[b, d].astype(BF16)
            for h in range(RET_HEADS):
                mine = head_of_lane == h
                sl = slice(h * RET_DV, (h + 1) * RET_DV)
                rows = slice(h * RET_DK, (h + 1) * RET_DK)
                decay = jnp.where(diff >= 0, jnp.exp(lgs[h] * jnp.maximum(diff, 0.0)), 0.0)
                vh = v_ref[b, :, sl].astype(BF16)
                scores = lax.dot_general(jnp.where(mine, q, 0.0).astype(BF16), k_bf, (((1,), (1,)), ((), ())),
                                         preferred_element_type=F32) * decay
                inner = jnp.dot(scores.astype(BF16), vh, preferred_element_type=F32)
                cross = jnp.dot(jnp.where(mine, q_decayed, 0.0).astype(BF16), state_bf, preferred_element_type=F32)
                o = inner + cross
                kv = jnp.dot(kt_decayed[rows], vh, preferred_element_type=F32)
                chunk_decay = jnp.exp(lgs[h] * jnp.full((RET_DK, RET_DV), float(cc), F32))
                s_scr[b, d, rows] = s_scr[b, d, rows] * chunk_decay + kv
                mu = jnp.mean(o, axis=-1, keepdims=True)
                oc = o - mu
                var = jnp.mean(oc * oc, axis=-1, keepdims=True)
                gate = g_ref[b, :, sl]
                y_ref[b, :, sl] = (gate * jax.nn.sigmoid(gate)) * (oc * lax.rsqrt(var + EPS) * gain[:, sl])

    @pl.when(c == n_chunks - 1)
    def _():
        sf_ref[...] = s_scr[...]


def _retention(p, lg, cos, sin, gain, s0, use_rope, layer):
    b, t, _ = p.shape
    cc = _tile(t, RET_CHUNK, 8)
    n = t // cc
    wv = RET_HEADS * RET_DV

    def pspec(width, col, backward):
        if backward:
            return pl.BlockSpec((b, cc, width), lambda ci, lg_: (0, n - 1 - ci, col // width))
        return pl.BlockSpec((b, cc, width), lambda ci, lg_: (0, ci, col // width))

    def tspec(backward):
        if backward:
            return pl.BlockSpec((cc, 128), lambda ci, lg_: (n - 1 - ci, 0))
        return pl.BlockSpec((cc, 128), lambda ci, lg_: (ci, 0))

    state_spec = pl.BlockSpec((b, 2, RET_HEADS * RET_DK, RET_DV), lambda ci, lg_: (0, 0, 0, 0))
    kern = functools.partial(_ret_kernel, use_rope=use_rope, n_chunks=n, layer=layer)
    grid_spec = pltpu.PrefetchScalarGridSpec(
        num_scalar_prefetch=1,
        grid=(n,),
        in_specs=[
            pspec(256, C_RET_Q, False), pspec(256, C_RET_Q, True),
            pspec(256, C_RET_K, False), pspec(256, C_RET_K, True),
            pspec(wv, C_RET_V, False), pspec(wv, C_RET_V, True),
            pspec(wv, C_RET_GF, False), pspec(wv, C_RET_GB, True),
            tspec(False), tspec(True), tspec(False), tspec(True),
            pl.BlockSpec((1, 1, wv), lambda ci, lg_: (layer, 0, 0)),
            state_spec,
        ],
        out_specs=[
            pl.BlockSpec((b, cc, wv), lambda ci, lg_: (0, ci, 0)),
            pl.BlockSpec((b, cc, wv), lambda ci, lg_: (0, n - 1 - ci, 0)),
            state_spec,
        ],
        scratch_shapes=[pltpu.VMEM((b, 2, RET_HEADS * RET_DK, RET_DV), F32)],
    )
    return pl.pallas_call(
        kern,
        grid_spec=grid_spec,
        out_shape=[
            jax.ShapeDtypeStruct((b, t, wv), F32),
            jax.ShapeDtypeStruct((b, t, wv), F32),
            jax.ShapeDtypeStruct((b, 2, RET_HEADS * RET_DK, RET_DV), F32),
        ],
        compiler_params=_cparams("arbitrary"),
        name="retention",
    )(lg, p, p, p, p, p, p, p, p, cos, cos, sin, sin, gain, s0)


def _na_tables(rows):
    groups = rows // NA_QROWS
    cols = np.arange(GRID_W)
    c0 = np.clip(cols - NA_KW // 2, 0, GRID_W - NA_KW)
    col_ok = (cols[None, :] >= c0[:, None]) & (cols[None, :] < c0[:, None] + NA_KW)
    dc = cols[None, :] - cols[:, None] + (NA_KW - 1)
    onehot = (dc[None] == np.arange(2 * NA_KW - 1)[:, None, None]) & col_ok[None]
    row_bias = np.full((3, NA_QROWS, NA_KROWS), 2 * NA_KH - 1, np.int64)
    for ti, g in enumerate((0, 1, groups - 1)):
        ws = int(np.clip(NA_QROWS * g - NA_KH // 2, 0, rows - NA_KROWS))
        for lr in range(NA_QROWS):
            r = NA_QROWS * g + lr
            r0 = int(np.clip(r - NA_KH // 2, 0, rows - NA_KH))
            for kr in range(NA_KROWS):
                if r0 <= ws + kr < r0 + NA_KH:
                    row_bias[ti, lr, kr] = ws + kr - r + (NA_KH - 1)
    return onehot, col_ok, row_bias


def _na_bias_tables(na_rpb, rows):
    onehot, col_ok, row_bias = _na_tables(rows)
    depth, heads = na_rpb.shape[:2]
    nd = 2 * NA_KH
    toep = jnp.einsum('lhdj,jck->lhdck', na_rpb, onehot.astype(np.float32), precision=lax.Precision.HIGHEST)
    toep = jnp.where(col_ok, toep * LOG2_E, MASK_VALUE)
    masked = jnp.full(toep.shape[:2] + (1, GRID_W, GRID_W), MASK_VALUE, F32)
    toep = jnp.concatenate([toep, masked], axis=2)
    sides = jnp.concatenate([jnp.pad(toep, ((0, 0),) * 4 + ((0, GRID_W),)),
                             jnp.pad(toep, ((0, 0),) * 4 + ((GRID_W, 0),))], axis=2)
    pairs = row_bias.reshape(-1, 2)
    pick = ((pairs[:, :1] == np.arange(nd)[None, :]).astype(np.float32),
            (pairs[:, 1:] == np.arange(nd)[None, :]).astype(np.float32))
    pick = np.concatenate(pick, axis=1)
    tab = jnp.einsum('nd,lhdck->lhnck', pick, sides, precision=lax.Precision.HIGHEST)
    return tab.reshape(depth, heads, 3, NA_QROWS * NA_KROWS // 2, GRID_W, 2 * GRID_W)


def _na_kernel(*refs):
    kz_ref, vz_ref, o_ref = refs[-3:]
    lanes = 2 * NA_DH
    npair = NA_KROWS // 2
    low = lax.broadcasted_iota(jnp.int32, (NA_QB, lanes), 1) < NA_DH
    for gi in range(NA_STEP_GROUPS):
        q_ref, k0_ref, k1_ref, k2_ref, v0_ref, v1_ref, v2_ref, bias_ref = refs[8 * gi:8 * gi + 8]
        for pr in range(NA_HEADS // 2):
            sl = slice(pr * lanes, (pr + 1) * lanes)
            qp = q_ref[0, :, sl] * (NA_DH ** -0.5 * LOG2_E)
            qm = (jnp.where(low, qp, 0.0).astype(BF16), jnp.where(low, 0.0, qp).astype(BF16))
            ks = [r[0, :, sl].astype(BF16) for r in (k0_ref, k1_ref, k2_ref, kz_ref)]
            vs = [r[0, :, sl].astype(BF16) for r in (v0_ref, v1_ref, v2_ref, vz_ref)]
            outs = []
            for hh in range(2):
                h = 2 * pr + hh
                cols = []
                for j in range(4):
                    sj = lax.dot_general(qm[hh], ks[j], (((1,), (1,)), ((), ())), preferred_element_type=F32)
                    halves = [sj[:, i * lanes:(i + 1) * lanes] for i in range(sj.shape[1] // lanes)]
                    if j < 3:
                        halves = [hv + jnp.concatenate([bias_ref[0, h, 0, lr * npair + j * len(halves) + i]
                                                        for lr in range(NA_QROWS)], axis=0)
                                  for i, hv in enumerate(halves)]
                    cols += halves
                m = jnp.max(functools.reduce(jnp.maximum, cols), axis=-1, keepdims=True)
                ps = [jnp.exp2(cj - m) for cj in cols]
                l = jnp.sum(functools.reduce(jnp.add, ps), axis=-1, keepdims=True)
                o = None
                per = len(cols) // 4
                for j in range(4):
                    pj = jnp.concatenate([pc.astype(BF16) for pc in ps[j * per:(j + 1) * per]], axis=1)
                    oj = jnp.dot(pj, vs[j], preferred_element_type=F32)
                    o = oj if o is None else o + oj
                outs.append(o / l)
            o_ref[0, gi * NA_QB:(gi + 1) * NA_QB, sl] = jnp.where(low, outs[0], outs[1]).astype(o_ref.dtype)


def _na(px, pz, bias, layer):
    b, t, _ = px.shape
    lz = pz.shape[1]
    groups = t // NA_QB
    w = NA_HEADS * NA_DH
    kblk = NA_KB // 3
    ng = NA_STEP_GROUPS
    assert lz == kblk and groups >= 3 and groups % ng == 0

    def group_specs(gi):
        def kspec(col, off):
            return pl.BlockSpec((1, kblk, w),
                                lambda bi, s: (bi, jnp.clip(ng * s + gi - 1, 0, groups - 3) + off, col // w))

        def tab(s):
            g = ng * s + gi
            return jnp.where(g == 0, 0, jnp.where(g == groups - 1, 2, 1))

        return [
            pl.BlockSpec((1, NA_QB, w), lambda bi, s: (bi, ng * s + gi, C_NA_Q // w)),
            kspec(C_NA_K, 0), kspec(C_NA_K, 1), kspec(C_NA_K, 2),
            kspec(C_NA_V, 0), kspec(C_NA_V, 1), kspec(C_NA_V, 2),
            pl.BlockSpec((1, NA_HEADS, 1) + bias.shape[3:], lambda bi, s: (layer, 0, tab(s), 0, 0, 0)),
        ]

    in_specs, args = [], []
    for gi in range(ng):
        in_specs += group_specs(gi)
        args += [px] * 7 + [bias]
    in_specs += [pl.BlockSpec((1, lz, w), lambda bi, s: (bi, 0, C_NA_K // w)),
                 pl.BlockSpec((1, lz, w), lambda bi, s: (bi, 0, C_NA_V // w))]
    args += [pz, pz]
    return pl.pallas_call(
        _na_kernel,
        grid=(b, groups // ng),
        in_specs=in_specs,
        out_specs=pl.BlockSpec((1, ng * NA_QB, w), lambda bi, s: (bi, s, 0)),
        out_shape=jax.ShapeDtypeStruct((b, t, w), BF16),
        compiler_params=_cparams("parallel", "arbitrary"),
        name="na",
    )(*args)


def _rms(x, gain):
    return x * lax.rsqrt(jnp.mean(x * x, axis=-1, keepdims=True) + EPS) * gain


def _mlaprep_kernel(xq_ref, xkv_ref, xkr_ref, zq_ref, zkv_ref, zkr_ref, qn_ref, kvn_ref, wq_ref, wqs_ref, wk_ref,
                    wv_ref, cos_ref, sin_ref, q_ref, k_ref, v_ref, *, nx):
    latent = pl.program_id(1) < nx
    pq = jnp.where(latent, xq_ref[0], zq_ref[0])
    pkv = jnp.where(latent, xkv_ref[0], zkv_ref[0])
    pkr = jnp.where(latent, xkr_ref[0], zkr_ref[0])
    hq = _rms(pq, qn_ref[0]).astype(BF16)
    hkv = _rms(pkv, kvn_ref[0]).astype(BF16)
    q = jnp.dot(hq, wq_ref[0], preferred_element_type=F32)
    q_swapped = jnp.dot(hq, wqs_ref[0], preferred_element_type=F32)
    k = jnp.dot(hkv, wk_ref[0], preferred_element_type=F32)
    v = jnp.dot(hkv, wv_ref[0], preferred_element_type=F32)
    kr = pltpu.roll(pkr, MLA_NOPE, 1)
    k = k + jnp.concatenate([kr] * MLA_HEADS, axis=1)
    k_swapped = jnp.concatenate([_swap_halves(kr, MLA_ROPE // 2)] * MLA_HEADS, axis=1)
    cos = jnp.concatenate([cos_ref[...]] * MLA_HEADS, axis=1)
    sin = jnp.concatenate([sin_ref[...]] * MLA_HEADS, axis=1)
    q = q * cos + q_swapped * sin
    k = k * cos + k_swapped * sin
    q_ref[0] = (q * ((MLA_NOPE + MLA_ROPE) ** -0.5 * LOG2_E)).astype(BF16)
    k_ref[0] = k.astype(BF16)
    lane = lax.broadcasted_iota(jnp.int32, v.shape, 1)
    v_ref[0] = jnp.where(lane % (2 * MLA_DK_PAD) == MLA_DK_PAD, 1.0, v).astype(BF16)


def _mlaprep(px, pz, qn, kvn, wq, wqs, wk, wv, cos, sin, layer):
    b, t, _ = px.shape
    lz = pz.shape[1]
    tm = lz
    nx = t // tm
    wqk = MLA_HEADS * MLA_DK_PAD
    wvv = wqk

    def xspec(width, col):
        return pl.BlockSpec((1, tm, width), lambda bi, i: (bi, jnp.minimum(i, nx - 1), col // width))

    def zspec(width, col):
        return pl.BlockSpec((1, tm, width), lambda bi, i: (bi, 0, col // width))

    out_spec = pl.BlockSpec((1, tm, wqk), lambda bi, i: (bi, i, 0))
    kern = functools.partial(_mlaprep_kernel, nx=nx)
    return pl.pallas_call(
        kern,
        grid=(b, nx + 1),
        in_specs=[
            xspec(MLA_RANK, C_MLA_Q), xspec(MLA_RANK, C_MLA_KV), xspec(128, C_MLA_KR),
            zspec(MLA_RANK, C_MLA_Q), zspec(MLA_RANK, C_MLA_KV), zspec(128, C_MLA_KR),
            _layer_spec((1, MLA_RANK), layer), _layer_spec((1, MLA_RANK), layer),
            _layer_spec((MLA_RANK, wqk), layer), _layer_spec((MLA_RANK, wqk), layer),
            _layer_spec((MLA_RANK, wqk), layer), _layer_spec((MLA_RANK, wvv), layer),
            pl.BlockSpec((tm, 128), lambda bi, i: (i, 0)),
            pl.BlockSpec((tm, 128), lambda bi, i: (i, 0)),
        ],
        out_specs=[out_spec, out_spec, pl.BlockSpec((1, tm, wvv), lambda bi, i: (bi, i, 0))],
        out_shape=[jax.ShapeDtypeStruct((b, t + lz, wqk), BF16)] * 2
        + [jax.ShapeDtypeStruct((b, t + lz, wvv), BF16)],
        compiler_params=_cparams("parallel", "parallel"),
        name="mlaprep",
    )(px, px, px, pz, pz, pz, qn, kvn, wq, wqs, wk, wv, cos, sin)


def _flash_kernel(q_ref, k_ref, v_ref, o_ref, m_scr, l_scr, acc_scr, *, heads, dk, dv, scale, nk):
    ki = pl.program_id(2)
    tq = q_ref.shape[1]
    tk = k_ref.shape[1]
    lanes = 2 * dv

    @pl.when(ki == 0)
    def _():
        m_scr[...] = jnp.full(m_scr.shape, -jnp.inf, F32)
        l_scr[...] = jnp.zeros(l_scr.shape, F32)
        acc_scr[...] = jnp.zeros(acc_scr.shape, F32)

    low = lax.broadcasted_iota(jnp.int32, (tq, lanes), 1) < dv
    for pr in range(heads // 2):
        vp = v_ref[0, :, pr * lanes:(pr + 1) * lanes].astype(BF16)
        alphas, pvs = [], []
        for h in (2 * pr, 2 * pr + 1):
            qh = q_ref[0, :, h * dk:(h + 1) * dk]
            if scale != 1.0:
                qh = qh * scale
            kh = k_ref[0, :, h * dk:(h + 1) * dk]
            s = lax.dot_general(qh.astype(BF16), kh.astype(BF16), (((1,), (1,)), ((), ())),
                                preferred_element_type=F32)
            cols = [s[:, j * lanes:(j + 1) * lanes] for j in range(tk // lanes)]
            m_prev = m_scr[h]
            m_tile = jnp.max(functools.reduce(jnp.maximum, cols), axis=-1, keepdims=True)
            m_new = jnp.maximum(m_prev, m_tile)
            alpha = jnp.exp2(m_prev - m_new)
            ps = [jnp.exp2(cj - m_new) for cj in cols]
            l_scr[h] = alpha * l_scr[h] + functools.reduce(jnp.add, ps)
            m_scr[h] = m_new
            p = jnp.concatenate([pj.astype(BF16) for pj in ps], axis=1)
            pvs.append(jnp.dot(p, vp, preferred_element_type=F32))
            alphas.append(alpha)
        acc_scr[pr] = acc_scr[pr] * jnp.where(low, alphas[0], alphas[1]) + jnp.where(low, pvs[0], pvs[1])

    @pl.when(ki == nk - 1)
    def _():
        for pr in range(heads // 2):
            l0 = jnp.sum(l_scr[2 * pr], axis=-1, keepdims=True)
            l1 = jnp.sum(l_scr[2 * pr + 1], axis=-1, keepdims=True)
            o_ref[0, :, pr * lanes:(pr + 1) * lanes] = (acc_scr[pr] / jnp.where(low, l0, l1)).astype(o_ref.dtype)


def _flash(q, k, v, qcol, kcol, vcol, heads, dk, dv, scale, tq_pref, tk_pref):
    b, tq_all, _ = q.shape
    tk_all = k.shape[1]
    tq = _tile(tq_all, tq_pref)
    tk = _tile(tk_all, tk_pref)
    nk = tk_all // tk
    assert heads % 2 == 0 and 2 * dv == 128 and tk % 128 == 0
    kern = functools.partial(_flash_kernel, heads=heads, dk=dk, dv=dv, scale=scale, nk=nk)
    return pl.pallas_call(
        kern,
        grid=(b, tq_all // tq, nk),
        in_specs=[
            pl.BlockSpec((1, tq, heads * dk), lambda bi, i, j: (bi, i, qcol)),
            pl.BlockSpec((1, tk, heads * dk), lambda bi, i, j: (bi, j, kcol)),
            pl.BlockSpec((1, tk, heads * dv), lambda bi, i, j: (bi, j, vcol)),
        ],
        out_specs=pl.BlockSpec((1, tq, heads * dv), lambda bi, i, j: (bi, i, 0)),
        out_shape=jax.ShapeDtypeStruct((b, tq_all, heads * dv), BF16),
        scratch_shapes=[
            pltpu.VMEM((heads, tq, 2 * dv), F32),
            pltpu.VMEM((heads, tq, 2 * dv), F32),
            pltpu.VMEM((heads // 2, tq, 2 * dv), F32),
        ],
        compiler_params=_cparams("parallel", "parallel", "arbitrary"),
        name="flash",
    )(q, k, v)


def _mla_attn_kernel(q_ref, k_ref, v_ref, o_ref, m_scr, l_scr, acc_scr, *, heads, nk):
    ki = pl.program_id(2)
    w = MLA_DK_PAD
    tq = q_ref.shape[1]

    @pl.when(ki == 0)
    def _():
        m_scr[...] = jnp.full(m_scr.shape, -jnp.inf, F32)
        l_scr[...] = jnp.zeros(l_scr.shape, F32)
        acc_scr[...] = jnp.zeros(acc_scr.shape, F32)

    low = lax.broadcasted_iota(jnp.int32, (tq, w), 1) < MLA_DV
    for pr in range(heads // 2):
        vp = v_ref[0, :, pr * 2 * w:(pr + 1) * 2 * w]
        alphas, pvs = [], []
        for h in (2 * pr, 2 * pr + 1):
            sl = slice(h * w, (h + 1) * w)
            s = lax.dot_general(q_ref[0, :, sl], k_ref[0, :, sl], (((1,), (1,)), ((), ())),
                                preferred_element_type=F32)
            cols = [s[:, j * w:(j + 1) * w] for j in range(s.shape[1] // w)]
            m_prev = m_scr[h]
            m_new = jnp.maximum(m_prev, jnp.max(functools.reduce(jnp.maximum, cols), axis=-1, keepdims=True))
            alpha = jnp.exp2(m_prev - m_new)
            m_scr[h] = m_new
            p = jnp.concatenate([jnp.exp2(cj - m_new).astype(BF16) for cj in cols], axis=1)
            pv = jnp.dot(p, vp, preferred_element_type=F32)
            l_scr[h] = alpha * l_scr[h] + pv[:, w:]
            pvs.append(pv[:, :w])
            alphas.append(alpha)
        acc_scr[pr] = acc_scr[pr] * jnp.where(low, alphas[0], alphas[1]) + jnp.where(low, pvs[0], pvs[1])

    @pl.when(ki == nk - 1)
    def _():
        for pr in range(heads // 2):
            l0 = jnp.sum(l_scr[2 * pr], axis=-1, keepdims=True)
            l1 = jnp.sum(l_scr[2 * pr + 1], axis=-1, keepdims=True)
            o_ref[0, :, pr * w:(pr + 1) * w] = (acc_scr[pr] / jnp.where(low, l0, l1)).astype(o_ref.dtype)


def _mla_attn(q, k, v, q_start, q_len, k_start, k_len, tq_pref, tk_pref):
    b, _, wq = q.shape
    heads = wq // MLA_DK_PAD
    tq = _tile(q_len, tq_pref, 8)
    tk = _tile(k_len, tk_pref, 128)
    assert q_start % tq == 0 and k_start % tk == 0
    q_off, k_off = q_start // tq, k_start // tk
    nk = k_len // tk
    kern = functools.partial(_mla_attn_kernel, heads=heads, nk=nk)
    return pl.pallas_call(
        kern,
        grid=(b, q_len // tq, nk),
        in_specs=[
            pl.BlockSpec((1, tq, wq), lambda bi, i, j: (bi, q_off + i, 0)),
            pl.BlockSpec((1, tk, wq), lambda bi, i, j: (bi, k_off + j, 0)),
            pl.BlockSpec((1, tk, heads * MLA_DK_PAD), lambda bi, i, j: (bi, k_off + j, 0)),
        ],
        out_specs=pl.BlockSpec((1, tq, heads * MLA_DV), lambda bi, i, j: (bi, i, 0)),
        out_shape=jax.ShapeDtypeStruct((b, q_len, heads * MLA_DV), BF16),
        scratch_shapes=[
            pltpu.VMEM((heads, tq, MLA_DK_PAD), F32),
            pltpu.VMEM((heads, tq, MLA_DK_PAD), F32),
            pltpu.VMEM((heads // 2, tq, MLA_DK_PAD), F32),
        ],
        compiler_params=_cparams("parallel", "parallel", "arbitrary"),
        name="mla_attn",
    )(q, k, v)


def _layer_norm(r, gain, bias):
    mu = jnp.mean(r, axis=-1, keepdims=True)
    rc = r - mu
    var = jnp.mean(rc * rc, axis=-1, keepdims=True)
    return rc * lax.rsqrt(var + EPS) * gain + bias


def _post_kernel(yaf_ref, yab_ref, yb_ref, yc_ref, ga_ref, gb_ref, gc_ref, x_ref, g1_ref, sc_ref, sh_ref, g2_ref,
                 wa_ref, wb_ref, wc_ref, wo_ref, w1_ref, w2_ref, lng_ref, lnb_ref, o_ref, *, ff_chunk):
    ya = (yaf_ref[0] + yab_ref[0]).astype(BF16)
    y = (jax.nn.sigmoid(ga_ref[0]) * jnp.dot(ya, wa_ref[0], preferred_element_type=F32)
         + jax.nn.sigmoid(gb_ref[0]) * jnp.dot(yb_ref[0].astype(BF16), wb_ref[0], preferred_element_type=F32)
         + jax.nn.sigmoid(gc_ref[0]) * jnp.dot(yc_ref[0].astype(BF16), wc_ref[0], preferred_element_type=F32))
    mix = jnp.dot(y.astype(BF16), wo_ref[0], preferred_element_type=F32)
    x1 = _layer_norm(DEEPNORM_ALPHA * x_ref[0] + g1_ref[0, 0, 0] * mix, lng_ref[0, 0:1], lnb_ref[0, 0:1])
    h = (x1 * (1.0 + sc_ref[0, 0, 0]) + sh_ref[0, 0, 0]).astype(BF16)
    acc = jnp.zeros(x1.shape, F32)
    for j in range(D_FF // ff_chunk):
        u = jnp.dot(h, w1_ref[0, :, j * ff_chunk:(j + 1) * ff_chunk], preferred_element_type=F32)
        u = jnp.square(jnp.maximum(u, 0.0)).astype(BF16)
        acc = acc + jnp.dot(u, w2_ref[0, j * ff_chunk:(j + 1) * ff_chunk, :], preferred_element_type=F32)
    o_ref[0] = _layer_norm(DEEPNORM_ALPHA * x1 + g2_ref[0, 0, 0] * acc, lng_ref[0, 1:2], lnb_ref[0, 1:2])


def _post(yaf, yab, yb, yc, p, x, mod, ctx_row, wa, wb, wc, wo, w1, w2, lng, lnb, layer):
    b, t, d = x.shape
    tm = _tile(t, 512)
    wbr = 512
    row = lambda width, col: pl.BlockSpec((1, tm, width), lambda bi, i: (bi, i, col // width))
    mods = [_mod_spec(d, layer, which, ctx_row, 0) for which in (MOD_G1, MOD_SC2, MOD_SH2, MOD_G2)]
    kern = functools.partial(_post_kernel, ff_chunk=1024)
    return pl.pallas_call(
        kern,
        grid=(b, t // tm),
        in_specs=[
            row(wbr, 0), row(wbr, 0), row(wbr, 0), row(wbr, 0),
            row(d, C_GATE_A), row(d, C_GATE_B), row(d, C_GATE_C), row(d, 0),
            *mods,
            _layer_spec((wbr, d), layer), _layer_spec((wbr, d), layer), _layer_spec((wbr, d), layer),
            _layer_spec((d, d), layer), _layer_spec((d, D_FF), layer), _layer_spec((D_FF, d), layer),
            _layer_spec((2, d), layer), _layer_spec((2, d), layer),
        ],
        out_specs=row(d, 0),
        out_shape=jax.ShapeDtypeStruct((b, t, d), F32),
        compiler_params=_cparams("parallel", "parallel"),
        name="post",
    )(yaf, yab, yb, yc, p, p, p, x, mod, mod, mod, mod, wa, wb, wc, wo, w1, w2, lng, lnb)


def _rope_tables(n_tok, rot_dim):
    t = jnp.arange(n_tok)
    row = (t // GRID_W).astype(F32)
    col = (t % GRID_W).astype(F32)
    n_freq = rot_dim // 4
    inv_freq = ROPE_BASE ** (-2.0 * jnp.arange(n_freq, dtype=F32) / (rot_dim // 2))
    ang = jnp.concatenate([row[:, None] * inv_freq, col[:, None] * inv_freq], axis=-1)
    return jnp.cos(ang), jnp.sin(ang)


def _ret_rope_tables(n_tok):
    cos, sin = _rope_tables(n_tok, RET_DK)
    cos_h = jnp.concatenate([cos, cos], axis=1)
    sin_h = jnp.concatenate([-sin, sin], axis=1)
    return jnp.tile(cos_h, (1, 2)), jnp.tile(sin_h, (1, 2))


def _mla_rope_tables(n_tok):
    cos, sin = _rope_tables(n_tok, MLA_ROPE)
    ones = jnp.ones((n_tok, MLA_NOPE), F32)
    zeros = jnp.zeros((n_tok, MLA_NOPE), F32)
    pad = jnp.zeros((n_tok, MLA_DK_PAD - MLA_NOPE - MLA_ROPE), F32)
    cos_h = jnp.concatenate([ones, cos, cos, pad], axis=1)
    sin_h = jnp.concatenate([zeros, -sin, sin, pad], axis=1)
    return cos_h, sin_h


def _pack_w_in_kernel(w_ref, o_ref):
    w = w_ref[0]
    pad = jnp.zeros((w.shape[0], P_WIDTH - IN_WIDTH), F32)
    o_ref[0] = jnp.concatenate([w[:, :C_GATE_A], w[:, C_GATE_A + MLA_ROPE:], w[:, C_GATE_A:C_GATE_A + MLA_ROPE], pad],
                               axis=1).astype(BF16)


def _pack_w_in(w):
    depth, d, n = w.shape
    assert n == IN_WIDTH
    tr = _tile(d, 256, 8)
    return pl.pallas_call(
        _pack_w_in_kernel,
        grid=(depth, d // tr),
        in_specs=[pl.BlockSpec((1, tr, n), lambda l, i: (l, i, 0))],
        out_specs=pl.BlockSpec((1, tr, P_WIDTH), lambda l, i: (l, i, 0)),
        out_shape=jax.ShapeDtypeStruct((depth, d, P_WIDTH), BF16),
        compiler_params=_cparams("parallel", "parallel"),
        name="pack_w_in",
    )(w)


def _pack_mla_weights(w_qup, w_kvup):
    depth, r = w_qup.shape[:2]
    wq = w_qup.reshape(depth, r, MLA_HEADS, MLA_NOPE + MLA_ROPE)
    half = MLA_ROPE // 2
    wqs = jnp.concatenate([jnp.zeros((depth, r, MLA_HEADS, MLA_NOPE), wq.dtype), wq[..., MLA_NOPE + half:],
                           wq[..., MLA_NOPE:MLA_NOPE + half]], axis=-1)
    pad_q = ((0, 0), (0, 0), (0, 0), (0, MLA_DK_PAD - MLA_NOPE - MLA_ROPE))
    wq = jnp.pad(wq, pad_q).reshape(depth, r, MLA_HEADS * MLA_DK_PAD)
    wqs = jnp.pad(wqs, pad_q).reshape(depth, r, MLA_HEADS * MLA_DK_PAD)
    wkv = w_kvup.reshape(depth, r, MLA_HEADS, MLA_NOPE + MLA_DV)
    wk = jnp.pad(wkv[..., :MLA_NOPE], ((0, 0), (0, 0), (0, 0), (0, MLA_DK_PAD - MLA_NOPE)))
    wk = wk.reshape(depth, r, MLA_HEADS * MLA_DK_PAD)
    wv = wkv[..., MLA_NOPE:].reshape(depth, r, MLA_HEADS // 2, 2 * MLA_DV)
    wv = jnp.pad(wv, ((0, 0), (0, 0), (0, 0), (0, 2 * MLA_DK_PAD - 2 * MLA_DV))).reshape(depth, r, MLA_HEADS * MLA_DK_PAD)
    return wq.astype(BF16), wqs.astype(BF16), wk.astype(BF16), wv.astype(BF16)


def kernel(x, c, ctx, c_ctx, w_ada, b_ada, w_in, ret_log_decay, ret_gn_gain, na_rpb, mla_q_norm, mla_w_qup,
           mla_kv_norm, mla_w_kvup, w_branch_ret, w_branch_na, w_branch_mla, w_out, w_ff1, w_ff2, ln_gain, ln_bias):
    depth = w_ada.shape[0]
    b, t, d = x.shape
    lz = ctx.shape[1]
    rows = t // GRID_W

    cc = jnp.zeros((8, d), F32).at[:b].set(c).at[b].set(c_ctx)
    mod = _ada(cc, w_ada, b_ada).reshape(depth, 8, 6, 1, d)

    cos_r, sin_r = _ret_rope_tables(t)
    cos_m, sin_m = _mla_rope_tables(t)
    cos_m = jnp.concatenate([cos_m, jnp.ones((lz, MLA_DK_PAD), F32)], axis=0)
    sin_m = jnp.concatenate([sin_m, jnp.zeros((lz, MLA_DK_PAD), F32)], axis=0)
    cos_rz, sin_rz = cos_r[:lz], sin_r[:lz]
    na_bias = _na_bias_tables(na_rpb, rows)
    s_zero = jnp.zeros((b, 2, RET_HEADS * RET_DK, RET_DV), F32)

    w_in_p = _pack_w_in(w_in)
    wq, wqs, wk, wv = _pack_mla_weights(mla_w_qup, mla_w_kvup)
    wa = w_branch_ret.astype(BF16)
    wb = w_branch_na.astype(BF16)
    wc = w_branch_mla.astype(BF16)
    wo = w_out.astype(BF16)
    w1 = w_ff1.astype(BF16)
    w2 = w_ff2.astype(BF16)

    lg = jnp.log1p(-jnp.exp(ret_log_decay.astype(F32))).reshape(depth * 2 * RET_HEADS)
    gn_gain = ret_gn_gain.reshape(depth, 1, RET_HEADS * RET_DV)
    qn = mla_q_norm.reshape(depth, 1, MLA_RANK)
    kvn = mla_kv_norm.reshape(depth, 1, MLA_RANK)

    z = ctx
    for l in range(depth):
        need_ctx = l < depth - 1
        px = _inproj(x, mod, None, w_in_p, l)
        pz = _inproj(z, mod, b, w_in_p, l)

        yaf_z, yab_z, s_ctx = _retention(pz, lg, cos_rz, sin_rz, gn_gain, s_zero, False, l)
        yaf_x, yab_x, _ = _retention(px, lg, cos_r, sin_r, gn_gain, s_ctx, True, l)

        yb_x = _na(px, pz, na_bias, l)

        q_all, k_all, v_all = _mlaprep(px, pz, qn, kvn, wq, wqs, wk, wv, cos_m, sin_m, l)
        yc_x = _mla_attn(q_all, k_all, v_all, 0, t, 0, t + lz, 1024, 2816)

        x = _post(yaf_x, yab_x, yb_x, yc_x, px, x, mod, None, wa, wb, wc, wo, w1, w2, ln_gain, ln_bias, l)

        if need_ctx:
            wna = NA_HEADS * NA_DH
            yb_z = _flash(pz, pz, pz, C_NA_Q // wna, C_NA_K // wna, C_NA_V // wna,
                          NA_HEADS, NA_DH, NA_DH, NA_DH ** -0.5 * LOG2_E, 256, 256)
            yc_z = _mla_attn(q_all, k_all, v_all, t, lz, t, lz, lz, lz)
            z = _post(yaf_z, yab_z, yb_z, yc_z, pz, z, mod, b, wa, wb, wc, wo, w1, w2, ln_gain, ln_bias, l)
    return x
```

```python
import functools

import numpy as np
import jax
import jax.numpy as jnp
from jax import lax
from jax.experimental import pallas as pl
from jax.experimental.pallas import tpu as pltpu

F32 = jnp.float32
BF16 = jnp.bfloat16

D_MODEL = 1024
GRID_W = 64
RET_HEADS = 4
RET_DK = 64
RET_DV = 128
RET_CHUNK = 256
NA_HEADS = 8
NA_DH = 64
NA_KH = 8
NA_KW = 16
MLA_HEADS = 8
MLA_RANK = 256
MLA_NOPE = 64
MLA_ROPE = 32
MLA_DV = 64
MLA_DK_PAD = 128
D_FF = 4 * D_MODEL
ROPE_BASE = 10000.0
EPS = 1e-5
DEPTH_FOR_NORM = 4
DEEPNORM_ALPHA = (2 * DEPTH_FOR_NORM) ** 0.25
MASK_VALUE = -1e30
LOG2_E = 1.4426950408889634

C_RET_Q, C_RET_K, C_RET_V, C_RET_GF, C_RET_GB = 0, 256, 512, 1024, 1536
C_NA_Q, C_NA_K, C_NA_V = 2048, 2560, 3072
C_MLA_Q, C_MLA_KV = 3584, 3840
C_GATE_A, C_GATE_B, C_GATE_C = 4096, 5120, 6144
C_MLA_KR = 7168
IN_WIDTH = 7200
P_WIDTH = 7296
P_COL_TILE = 2432

NA_QROWS = 4
NA_KROWS = 12
NA_QB = NA_QROWS * GRID_W
NA_KB = NA_KROWS * GRID_W
NA_STEP_GROUPS = 1

VMEM_LIMIT = 56 * 1024 * 1024


def _cparams(*sem):
    return pltpu.CompilerParams(dimension_semantics=sem, vmem_limit_bytes=VMEM_LIMIT)


def _tile(n, pref, mult=1):
    t = min(n, pref) // mult * mult
    while n % t:
        t -= mult
    return t


MOD_SH1, MOD_SC1, MOD_G1, MOD_SH2, MOD_SC2, MOD_G2 = range(6)


def _mod_spec(d, layer, which, ctx_row, batch_axis):
    def index(*g):
        return (layer, g[batch_axis] if ctx_row is None else ctx_row, which, 0, 0)
    return pl.BlockSpec((1, 1, 1, 1, d), index)


def _layer_spec(shape, layer):
    nd = len(shape)
    return pl.BlockSpec((1,) + tuple(shape), lambda *_: (layer,) + (0,) * nd)


def _ada_kernel(c_ref, w_ref, b_ref, o_ref):
    c = c_ref[...]
    a = c * jax.nn.sigmoid(c)
    o_ref[0] = jnp.dot(a.astype(BF16), w_ref[0].astype(BF16), preferred_element_type=F32) + b_ref[0]


def _ada(cc, w_ada, b_ada):
    depth, d, n = w_ada.shape
    tn = _tile(n, 3072, 128)
    return pl.pallas_call(
        _ada_kernel,
        grid=(depth, n // tn),
        in_specs=[
            pl.BlockSpec((8, d), lambda l, j: (0, 0)),
            pl.BlockSpec((1, d, tn), lambda l, j: (l, 0, j)),
            pl.BlockSpec((1, 1, tn), lambda l, j: (l, 0, j)),
        ],
        out_specs=pl.BlockSpec((1, 8, tn), lambda l, j: (l, 0, j)),
        out_shape=jax.ShapeDtypeStruct((depth, 8, n), F32),
        compiler_params=_cparams("parallel", "parallel"),
        name="ada",
    )(cc, w_ada, b_ada.reshape(depth, 1, n))


def _inproj_kernel(x_ref, sc_ref, sh_ref, w_ref, o_ref):
    h = x_ref[0] * (1.0 + sc_ref[0, 0, 0]) + sh_ref[0, 0, 0]
    o_ref[0] = jnp.dot(h.astype(BF16), w_ref[0], preferred_element_type=F32)


def _inproj(x, mod, ctx_row, w, layer):
    b, t, d = x.shape
    tm = _tile(t, 1024)
    tn = P_COL_TILE
    return pl.pallas_call(
        _inproj_kernel,
        grid=(P_WIDTH // tn, b, t // tm),
        in_specs=[
            pl.BlockSpec((1, tm, d), lambda j, bi, i: (bi, i, 0)),
            _mod_spec(d, layer, MOD_SC1, ctx_row, 1),
            _mod_spec(d, layer, MOD_SH1, ctx_row, 1),
            pl.BlockSpec((1, d, tn), lambda j, bi, i: (layer, 0, j)),
        ],
        out_specs=pl.BlockSpec((1, tm, tn), lambda j, bi, i: (bi, i, j)),
        out_shape=jax.ShapeDtypeStruct((b, t, P_WIDTH), F32),
        compiler_params=_cparams("parallel", "parallel", "parallel"),
        name="inproj",
    )(x, mod, mod, w)


def _swap_halves(x, half):
    n = x.shape[-1]
    lane = lax.broadcasted_iota(jnp.int32, x.shape, x.ndim - 1)
    first = (lane % (2 * half)) < half
    return jnp.where(first, pltpu.roll(x, n - half, x.ndim - 1), pltpu.roll(x, half, x.ndim - 1))


def _ret_kernel(lg_ref, qf_ref, qb_ref, kf_ref, kb_ref, vf_ref, vb_ref, gf_ref, gb_ref,
                cosf_ref, cosb_ref, sinf_ref, sinb_ref, gain_ref, s0_ref,
                yf_ref, yb_ref, sf_ref, s_scr, *, use_rope, n_chunks, layer):
    c = pl.program_id(0)
    batch, cc = qf_ref.shape[:2]
    lg_base = layer * 2 * RET_HEADS

    @pl.when(c == 0)
    def _():
        s_scr[...] = s0_ref[...]

    gain = gain_ref[0]
    row = lax.broadcasted_iota(jnp.int32, (cc, cc), 0).astype(F32)
    col = lax.broadcasted_iota(jnp.int32, (cc, cc), 1).astype(F32)
    wqk = RET_HEADS * RET_DK
    pos = lax.broadcasted_iota(jnp.int32, (cc, wqk), 0).astype(F32)
    head_of_lane = lax.broadcasted_iota(jnp.int32, (cc, wqk), 1) // RET_DK
    dirs = (
        (qf_ref, kf_ref, vf_ref, gf_ref, cosf_ref, sinf_ref, yf_ref, row - col, pos + 1.0, cc - 1.0 - pos),
        (qb_ref, kb_ref, vb_ref, gb_ref, cosb_ref, sinb_ref, yb_ref, col - row, cc - pos, pos),
    )
    for d, (q_ref, k_ref, v_ref, g_ref, cos_ref, sin_ref, y_ref, diff, q_exp, k_exp) in enumerate(dirs):
        if use_rope:
            cos = jnp.concatenate([cos_ref[...], cos_ref[...]], axis=1)
            sin = jnp.concatenate([sin_ref[...], sin_ref[...]], axis=1)
        lgs = [lg_ref[lg_base + d * RET_HEADS + h] for h in range(RET_HEADS)]
        lg_lanes = jnp.full((cc, wqk), lgs[RET_HEADS - 1], F32)
        for h in range(RET_HEADS - 2, -1, -1):
            lg_lanes = jnp.where(head_of_lane == h, lgs[h], lg_lanes)
        q_decay = jnp.exp(lg_lanes * q_exp)
        k_decay = jnp.exp(lg_lanes * k_exp)
        for b in range(batch):
            q = q_ref[b]
            k = k_ref[b] * (RET_DK ** -0.5)
            if use_rope:
                q = q * cos + _swap_halves(q, RET_DK // 2) * sin
                k = k * cos + _swap_halves(k, RET_DK // 2) * sin
            k_bf = k.astype(BF16)
            kt_decayed = (k * k_decay).T.astype(BF16)
            q_decayed = q * q_decay
            state_bf = s_scr[b, d].astype(BF16)
            for h in range(RET_HEADS):
                mine = head_of_lane == h
                sl = slice(h * RET_DV, (h + 1) * RET_DV)
                rows = slice(h * RET_DK, (h + 1) * RET_DK)
                decay = jnp.where(diff >= 0, jnp.exp(lgs[h] * jnp.maximum(diff, 0.0)), 0.0)
                vh = v_ref[b, :, sl].astype(BF16)
                scores = lax.dot_general(jnp.where(mine, q, 0.0).astype(BF16), k_bf, (((1,), (1,)), ((), ())),
                                         preferred_element_type=F32) * decay
                inner = jnp.dot(scores.astype(BF16), vh, preferred_element_type=F32)
                cross = jnp.dot(jnp.where(mine, q_decayed, 0.0).astype(BF16), state_bf, preferred_element_type=F32)
                o = inner + cross
                kv = jnp.dot(kt_decayed[rows], vh, preferred_element_type=F32)
                chunk_decay = jnp.exp(lgs[h] * jnp.full((RET_DK, RET_DV), float(cc), F32))
                s_scr[b, d, rows] = s_scr[b, d, rows] * chunk_decay + kv
                mu = jnp.mean(o, axis=-1, keepdims=True)
                oc = o - mu
                var = jnp.mean(oc * oc, axis=-1, keepdims=True)
                gate = g_ref[b, :, sl]
                y_ref[b, :, sl] = (gate * jax.nn.sigmoid(gate)) * (oc * lax.rsqrt(var + EPS) * gain[:, sl])

    @pl.when(c == n_chunks - 1)
    def _():
        sf_ref[...] = s_scr[...]


def _retention(p, lg, cos, sin, gain, s0, use_rope, layer):
    b, t, _ = p.shape
    cc = _tile(t, RET_CHUNK, 8)
    n = t // cc
    wv = RET_HEADS * RET_DV

    def pspec(width, col, backward):
        if backward:
            return pl.BlockSpec((b, cc, width), lambda ci, lg_: (0, n - 1 - ci, col // width))
        return pl.BlockSpec((b, cc, width), lambda ci, lg_: (0, ci, col // width))

    def tspec(backward):
        if backward:
            return pl.BlockSpec((cc, 128), lambda ci, lg_: (n - 1 - ci, 0))
        return pl.BlockSpec((cc, 128), lambda ci, lg_: (ci, 0))

    state_spec = pl.BlockSpec((b, 2, RET_HEADS * RET_DK, RET_DV), lambda ci, lg_: (0, 0, 0, 0))
    kern = functools.partial(_ret_kernel, use_rope=use_rope, n_chunks=n, layer=layer)
    grid_spec = pltpu.PrefetchScalarGridSpec(
        num_scalar_prefetch=1,
        grid=(n,),
        in_specs=[
            pspec(256, C_RET_Q, False), pspec(256, C_RET_Q, True),
            pspec(256, C_RET_K, False), pspec(256, C_RET_K, True),
            pspec(wv, C_RET_V, False), pspec(wv, C_RET_V, True),
            pspec(wv, C_RET_GF, False), pspec(wv, C_RET_GB, True),
            tspec(False), tspec(True), tspec(False), tspec(True),
            pl.BlockSpec((1, 1, wv), lambda ci, lg_: (layer, 0, 0)),
            state_spec,
        ],
        out_specs=[
            pl.BlockSpec((b, cc, wv), lambda ci, lg_: (0, ci, 0)),
            pl.BlockSpec((b, cc, wv), lambda ci, lg_: (0, n - 1 - ci, 0)),
            state_spec,
        ],
        scratch_shapes=[pltpu.VMEM((b, 2, RET_HEADS * RET_DK, RET_DV), F32)],
    )
    return pl.pallas_call(
        kern,
        grid_spec=grid_spec,
        out_shape=[
            jax.ShapeDtypeStruct((b, t, wv), F32),
            jax.ShapeDtypeStruct((b, t, wv), F32),
            jax.ShapeDtypeStruct((b, 2, RET_HEADS * RET_DK, RET_DV), F32),
        ],
        compiler_params=_cparams("arbitrary"),
        name="retention",
    )(lg, p, p, p, p, p, p, p, p, cos, cos, sin, sin, gain, s0)


def _na_tables(rows):
    groups = rows // NA_QROWS
    cols = np.arange(GRID_W)
    c0 = np.clip(cols - NA_KW // 2, 0, GRID_W - NA_KW)
    col_ok = (cols[None, :] >= c0[:, None]) & (cols[None, :] < c0[:, None] + NA_KW)
    dc = cols[None, :] - cols[:, None] + (NA_KW - 1)
    onehot = (dc[None] == np.arange(2 * NA_KW - 1)[:, None, None]) & col_ok[None]
    row_bias = np.full((3, NA_QROWS, NA_KROWS), 2 * NA_KH - 1, np.int64)
    for ti, g in enumerate((0, 1, groups - 1)):
        ws = int(np.clip(NA_QROWS * g - NA_KH // 2, 0, rows - NA_KROWS))
        for lr in range(NA_QROWS):
            r = NA_QROWS * g + lr
            r0 = int(np.clip(r - NA_KH // 2, 0, rows - NA_KH))
            for kr in range(NA_KROWS):
                if r0 <= ws + kr < r0 + NA_KH:
                    row_bias[ti, lr, kr] = ws + kr - r + (NA_KH - 1)
    return onehot, col_ok, row_bias


def _na_bias_tables(na_rpb, rows):
    onehot, col_ok, row_bias = _na_tables(rows)
    depth, heads = na_rpb.shape[:2]
    nd = 2 * NA_KH
    toep = jnp.einsum('lhdj,jck->lhdck', na_rpb, onehot.astype(np.float32), precision=lax.Precision.HIGHEST)
    toep = jnp.where(col_ok, toep * LOG2_E, MASK_VALUE)
    masked = jnp.full(toep.shape[:2] + (1, GRID_W, GRID_W), MASK_VALUE, F32)
    toep = jnp.concatenate([toep, masked], axis=2)
    sides = jnp.concatenate([jnp.pad(toep, ((0, 0),) * 4 + ((0, GRID_W),)),
                             jnp.pad(toep, ((0, 0),) * 4 + ((GRID_W, 0),))], axis=2)
    pairs = row_bias.reshape(-1, 2)
    pick = ((pairs[:, :1] == np.arange(nd)[None, :]).astype(np.float32),
            (pairs[:, 1:] == np.arange(nd)[None, :]).astype(np.float32))
    pick = np.concatenate(pick, axis=1)
    tab = jnp.einsum('nd,lhdck->lhnck', pick, sides, precision=lax.Precision.HIGHEST)
    return tab.reshape(depth, heads, 3, NA_QROWS * NA_KROWS // 2, GRID_W, 2 * GRID_W)


def _na_kernel(*refs):
    kz_ref, vz_ref, o_ref = refs[-3:]
    lanes = 2 * NA_DH
    npair = NA_KROWS // 2
    low = lax.broadcasted_iota(jnp.int32, (NA_QB, lanes), 1) < NA_DH
    ones_col = jnp.where(lax.broadcasted_iota(jnp.int32, (kz_ref.shape[1], lanes), 1) == 0, 1.0, 0.0).astype(BF16)
    for gi in range(NA_STEP_GROUPS):
        q_ref, k0_ref, k1_ref, k2_ref, v0_ref, v1_ref, v2_ref, bias_ref = refs[8 * gi:8 * gi + 8]
        for pr in range(NA_HEADS // 2):
            sl = slice(pr * lanes, (pr + 1) * lanes)
            qp = q_ref[0, :, sl] * (NA_DH ** -0.5 * LOG2_E)
            qm = (jnp.where(low, qp, 0.0).astype(BF16), jnp.where(low, 0.0, qp).astype(BF16))
            ks = [r[0, :, sl].astype(BF16) for r in (k0_ref, k1_ref, k2_ref, kz_ref)]
            vs = [jnp.concatenate([r[0, :, sl].astype(BF16), ones_col], axis=1)
                  for r in (v0_ref, v1_ref, v2_ref, vz_ref)]
            outs = []
            for hh in range(2):
                h = 2 * pr + hh
                cols = []
                for j in range(4):
                    sj = lax.dot_general(qm[hh], ks[j], (((1,), (1,)), ((), ())), preferred_element_type=F32)
                    halves = [sj[:, i * lanes:(i + 1) * lanes] for i in range(sj.shape[1] // lanes)]
                    if j < 3:
                        halves = [hv + jnp.concatenate([bias_ref[0, h, 0, lr * npair + j * len(halves) + i]
                                                        for lr in range(NA_QROWS)], axis=0)
                                  for i, hv in enumerate(halves)]
                    cols += halves
                m = jnp.max(functools.reduce(jnp.maximum, cols), axis=-1, keepdims=True)
                o = None
                per = len(cols) // 4
                for j in range(4):
                    pj = jnp.concatenate([jnp.exp2(cj - m).astype(BF16) for cj in cols[j * per:(j + 1) * per]], axis=1)
                    oj = jnp.dot(pj, vs[j], preferred_element_type=F32)
                    o = oj if o is None else o + oj
                l = jnp.sum(o[:, lanes:], axis=-1, keepdims=True)
                outs.append(o[:, :lanes] / l)
            o_ref[0, gi * NA_QB:(gi + 1) * NA_QB, sl] = jnp.where(low, outs[0], outs[1]).astype(o_ref.dtype)


def _na(px, pz, bias, layer):
    b, t, _ = px.shape
    lz = pz.shape[1]
    groups = t // NA_QB
    w = NA_HEADS * NA_DH
    kblk = NA_KB // 3
    ng = NA_STEP_GROUPS
    assert lz == kblk and groups >= 3 and groups % ng == 0

    def group_specs(gi):
        def kspec(col, off):
            return pl.BlockSpec((1, kblk, w),
                                lambda bi, s: (bi, jnp.clip(ng * s + gi - 1, 0, groups - 3) + off, col // w))

        def tab(s):
            g = ng * s + gi
            return jnp.where(g == 0, 0, jnp.where(g == groups - 1, 2, 1))

        return [
            pl.BlockSpec((1, NA_QB, w), lambda bi, s: (bi, ng * s + gi, C_NA_Q // w)),
            kspec(C_NA_K, 0), kspec(C_NA_K, 1), kspec(C_NA_K, 2),
            kspec(C_NA_V, 0), kspec(C_NA_V, 1), kspec(C_NA_V, 2),
            pl.BlockSpec((1, NA_HEADS, 1) + bias.shape[3:], lambda bi, s: (layer, 0, tab(s), 0, 0, 0)),
        ]

    in_specs, args = [], []
    for gi in range(ng):
        in_specs += group_specs(gi)
        args += [px] * 7 + [bias]
    in_specs += [pl.BlockSpec((1, lz, w), lambda bi, s: (bi, 0, C_NA_K // w)),
                 pl.BlockSpec((1, lz, w), lambda bi, s: (bi, 0, C_NA_V // w))]
    args += [pz, pz]
    return pl.pallas_call(
        _na_kernel,
        grid=(b, groups // ng),
        in_specs=in_specs,
        out_specs=pl.BlockSpec((1, ng * NA_QB, w), lambda bi, s: (bi, s, 0)),
        out_shape=jax.ShapeDtypeStruct((b, t, w), BF16),
        compiler_params=_cparams("parallel", "arbitrary"),
        name="na",
    )(*args)


def _rms(x, gain):
    return x * lax.rsqrt(jnp.mean(x * x, axis=-1, keepdims=True) + EPS) * gain


def _mlaprep_kernel(xq_ref, xkv_ref, xkr_ref, zq_ref, zkv_ref, zkr_ref, qn_ref, kvn_ref, wq_ref, wqs_ref, wk_ref,
                    wv_ref, cos_ref, sin_ref, q_ref, k_ref, v_ref, *, nx):
    latent = pl.program_id(1) < nx
    pq = jnp.where(latent, xq_ref[0], zq_ref[0])
    pkv = jnp.where(latent, xkv_ref[0], zkv_ref[0])
    pkr = jnp.where(latent, xkr_ref[0], zkr_ref[0])
    hq = _rms(pq, qn_ref[0]).astype(BF16)
    hkv = _rms(pkv, kvn_ref[0]).astype(BF16)
    q = jnp.dot(hq, wq_ref[0], preferred_element_type=F32)
    q_swapped = jnp.dot(hq, wqs_ref[0], preferred_element_type=F32)
    k = jnp.dot(hkv, wk_ref[0], preferred_element_type=F32)
    v = jnp.dot(hkv, wv_ref[0], preferred_element_type=F32)
    kr = pltpu.roll(pkr, MLA_NOPE, 1)
    k = k + jnp.concatenate([kr] * MLA_HEADS, axis=1)
    k_swapped = jnp.concatenate([_swap_halves(kr, MLA_ROPE // 2)] * MLA_HEADS, axis=1)
    cos = jnp.concatenate([cos_ref[...]] * MLA_HEADS, axis=1)
    sin = jnp.concatenate([sin_ref[...]] * MLA_HEADS, axis=1)
    q = q * cos + q_swapped * sin
    k = k * cos + k_swapped * sin
    q_ref[0] = (q * ((MLA_NOPE + MLA_ROPE) ** -0.5 * LOG2_E)).astype(BF16)
    k_ref[0] = k.astype(BF16)
    lane = lax.broadcasted_iota(jnp.int32, v.shape, 1)
    v_ref[0] = jnp.where(lane % (2 * MLA_DK_PAD) == MLA_DK_PAD, 1.0, v).astype(BF16)


def _mlaprep(px, pz, qn, kvn, wq, wqs, wk, wv, cos, sin, layer):
    b, t, _ = px.shape
    lz = pz.shape[1]
    tm = lz
    nx = t // tm
    wqk = MLA_HEADS * MLA_DK_PAD
    wvv = wqk

    def xspec(width, col):
        return pl.BlockSpec((1, tm, width), lambda bi, i: (bi, jnp.minimum(i, nx - 1), col // width))

    def zspec(width, col):
        return pl.BlockSpec((1, tm, width), lambda bi, i: (bi, 0, col // width))

    out_spec = pl.BlockSpec((1, tm, wqk), lambda bi, i: (bi, i, 0))
    kern = functools.partial(_mlaprep_kernel, nx=nx)
    return pl.pallas_call(
        kern,
        grid=(b, nx + 1),
        in_specs=[
            xspec(MLA_RANK, C_MLA_Q), xspec(MLA_RANK, C_MLA_KV), xspec(128, C_MLA_KR),
            zspec(MLA_RANK, C_MLA_Q), zspec(MLA_RANK, C_MLA_KV), zspec(128, C_MLA_KR),
            _layer_spec((1, MLA_RANK), layer), _layer_spec((1, MLA_RANK), layer),
            _layer_spec((MLA_RANK, wqk), layer), _layer_spec((MLA_RANK, wqk), layer),
            _layer_spec((MLA_RANK, wqk), layer), _layer_spec((MLA_RANK, wvv), layer),
            pl.BlockSpec((tm, 128), lambda bi, i: (i, 0)),
            pl.BlockSpec((tm, 128), lambda bi, i: (i, 0)),
        ],
        out_specs=[out_spec, out_spec, pl.BlockSpec((1, tm, wvv), lambda bi, i: (bi, i, 0))],
        out_shape=[jax.ShapeDtypeStruct((b, t + lz, wqk), BF16)] * 2
        + [jax.ShapeDtypeStruct((b, t + lz, wvv), BF16)],
        compiler_params=_cparams("parallel", "parallel"),
        name="mlaprep",
    )(px, px, px, pz, pz, pz, qn, kvn, wq, wqs, wk, wv, cos, sin)


def _flash_kernel(q_ref, k_ref, v_ref, o_ref, m_scr, l_scr, acc_scr, *, heads, dk, dv, scale, nk):
    ki = pl.program_id(2)
    tq = q_ref.shape[1]
    tk = k_ref.shape[1]
    lanes = 2 * dv

    @pl.when(ki == 0)
    def _():
        m_scr[...] = jnp.full(m_scr.shape, -jnp.inf, F32)
        l_scr[...] = jnp.zeros(l_scr.shape, F32)
        acc_scr[...] = jnp.zeros(acc_scr.shape, F32)

    low = lax.broadcasted_iota(jnp.int32, (tq, lanes), 1) < dv
    for pr in range(heads // 2):
        vp = v_ref[0, :, pr * lanes:(pr + 1) * lanes].astype(BF16)
        alphas, pvs = [], []
        for h in (2 * pr, 2 * pr + 1):
            qh = q_ref[0, :, h * dk:(h + 1) * dk]
            if scale != 1.0:
                qh = qh * scale
            kh = k_ref[0, :, h * dk:(h + 1) * dk]
            s = lax.dot_general(qh.astype(BF16), kh.astype(BF16), (((1,), (1,)), ((), ())),
                                preferred_element_type=F32)
            cols = [s[:, j * lanes:(j + 1) * lanes] for j in range(tk // lanes)]
            m_prev = m_scr[h]
            m_tile = jnp.max(functools.reduce(jnp.maximum, cols), axis=-1, keepdims=True)
            m_new = jnp.maximum(m_prev, m_tile)
            alpha = jnp.exp2(m_prev - m_new)
            ps = [jnp.exp2(cj - m_new) for cj in cols]
            l_scr[h] = alpha * l_scr[h] + functools.reduce(jnp.add, ps)
            m_scr[h] = m_new
            p = jnp.concatenate([pj.astype(BF16) for pj in ps], axis=1)
            pvs.append(jnp.dot(p, vp, preferred_element_type=F32))
            alphas.append(alpha)
        acc_scr[pr] = acc_scr[pr] * jnp.where(low, alphas[0], alphas[1]) + jnp.where(low, pvs[0], pvs[1])

    @pl.when(ki == nk - 1)
    def _():
        for pr in range(heads // 2):
            l0 = jnp.sum(l_scr[2 * pr], axis=-1, keepdims=True)
            l1 = jnp.sum(l_scr[2 * pr + 1], axis=-1, keepdims=True)
            o_ref[0, :, pr * lanes:(pr + 1) * lanes] = (acc_scr[pr] / jnp.where(low, l0, l1)).astype(o_ref.dtype)


def _flash(q, k, v, qcol, kcol, vcol, heads, dk, dv, scale, tq_pref, tk_pref):
    b, tq_all, _ = q.shape
    tk_all = k.shape[1]
    tq = _tile(tq_all, tq_pref)
    tk = _tile(tk_all, tk_pref)
    nk = tk_all // tk
    assert heads % 2 == 0 and 2 * dv == 128 and tk % 128 == 0
    kern = functools.partial(_flash_kernel, heads=heads, dk=dk, dv=dv, scale=scale, nk=nk)
    return pl.pallas_call(
        kern,
        grid=(b, tq_all // tq, nk),
        in_specs=[
            pl.BlockSpec((1, tq, heads * dk), lambda bi, i, j: (bi, i, qcol)),
            pl.BlockSpec((1, tk, heads * dk), lambda bi, i, j: (bi, j, kcol)),
            pl.BlockSpec((1, tk, heads * dv), lambda bi, i, j: (bi, j, vcol)),
        ],
        out_specs=pl.BlockSpec((1, tq, heads * dv), lambda bi, i, j: (bi, i, 0)),
        out_shape=jax.ShapeDtypeStruct((b, tq_all, heads * dv), BF16),
        scratch_shapes=[
            pltpu.VMEM((heads, tq, 2 * dv), F32),
            pltpu.VMEM((heads, tq, 2 * dv), F32),
            pltpu.VMEM((heads // 2, tq, 2 * dv), F32),
        ],
        compiler_params=_cparams("parallel", "parallel", "arbitrary"),
        name="flash",
    )(q, k, v)


def _mla_attn_kernel(q_ref, k_ref, v_ref, o_ref, m_scr, l_scr, acc_scr, *, heads, nk):
    ki = pl.program_id(2)
    w = MLA_DK_PAD
    tq = q_ref.shape[1]

    @pl.when(ki == 0)
    def _():
        m_scr[...] = jnp.full(m_scr.shape, -jnp.inf, F32)
        l_scr[...] = jnp.zeros(l_scr.shape, F32)
        acc_scr[...] = jnp.zeros(acc_scr.shape, F32)

    low = lax.broadcasted_iota(jnp.int32, (tq, w), 1) < MLA_DV
    for pr in range(heads // 2):
        vp = v_ref[0, :, pr * 2 * w:(pr + 1) * 2 * w]
        alphas, pvs = [], []
        for h in (2 * pr, 2 * pr + 1):
            sl = slice(h * w, (h + 1) * w)
            s = lax.dot_general(q_ref[0, :, sl], k_ref[0, :, sl], (((1,), (1,)), ((), ())),
                                preferred_element_type=F32)
            cols = [s[:, j * w:(j + 1) * w] for j in range(s.shape[1] // w)]
            m_prev = m_scr[h]
            m_new = jnp.maximum(m_prev, jnp.max(functools.reduce(jnp.maximum, cols), axis=-1, keepdims=True))
            alpha = jnp.exp2(m_prev - m_new)
            m_scr[h] = m_new
            p = jnp.concatenate([jnp.exp2(cj - m_new).astype(BF16) for cj in cols], axis=1)
            pv = jnp.dot(p, vp, preferred_element_type=F32)
            l_scr[h] = alpha * l_scr[h] + pv[:, w:]
            pvs.append(pv[:, :w])
            alphas.append(alpha)
        acc_scr[pr] = acc_scr[pr] * jnp.where(low, alphas[0], alphas[1]) + jnp.where(low, pvs[0], pvs[1])

    @pl.when(ki == nk - 1)
    def _():
        for pr in range(heads // 2):
            l0 = jnp.sum(l_scr[2 * pr], axis=-1, keepdims=True)
            l1 = jnp.sum(l_scr[2 * pr + 1], axis=-1, keepdims=True)
            o_ref[0, :, pr * w:(pr + 1) * w] = (acc_scr[pr] / jnp.where(low, l0, l1)).astype(o_ref.dtype)


def _mla_attn(q, k, v, q_start, q_len, k_start, k_len, tq_pref, tk_pref):
    b, _, wq = q.shape
    heads = wq // MLA_DK_PAD
    tq = _tile(q_len, tq_pref, 8)
    tk = _tile(k_len, tk_pref, 128)
    assert q_start % tq == 0 and k_start % tk == 0
    q_off, k_off = q_start // tq, k_start // tk
    nk = k_len // tk
    kern = functools.partial(_mla_attn_kernel, heads=heads, nk=nk)
    return pl.pallas_call(
        kern,
        grid=(b, q_len // tq, nk),
        in_specs=[
            pl.BlockSpec((1, tq, wq), lambda bi, i, j: (bi, q_off + i, 0)),
            pl.BlockSpec((1, tk, wq), lambda bi, i, j: (bi, k_off + j, 0)),
            pl.BlockSpec((1, tk, heads * MLA_DK_PAD), lambda bi, i, j: (bi, k_off + j, 0)),
        ],
        out_specs=pl.BlockSpec((1, tq, heads * MLA_DV), lambda bi, i, j: (bi, i, 0)),
        out_shape=jax.ShapeDtypeStruct((b, q_len, heads * MLA_DV), BF16),
        scratch_shapes=[
            pltpu.VMEM((heads, tq, MLA_DK_PAD), F32),
            pltpu.VMEM((heads, tq, MLA_DK_PAD), F32),
            pltpu.VMEM((heads // 2, tq, MLA_DK_PAD), F32),
        ],
        compiler_params=_cparams("parallel", "parallel", "arbitrary"),
        name="mla_attn",
    )(q, k, v)


def _layer_norm(r, gain, bias):
    mu = jnp.mean(r, axis=-1, keepdims=True)
    rc = r - mu
    var = jnp.mean(rc * rc, axis=-1, keepdims=True)
    return rc * lax.rsqrt(var + EPS) * gain + bias


def _post_kernel(yaf_ref, yab_ref, yb_ref, yc_ref, ga_ref, gb_ref, gc_ref, x_ref, g1_ref, sc_ref, sh_ref, g2_ref,
                 wa_ref, wb_ref, wc_ref, wo_ref, w1_ref, w2_ref, lng_ref, lnb_ref, o_ref, *, ff_chunk):
    ya = (yaf_ref[0] + yab_ref[0]).astype(BF16)
    y = (jax.nn.sigmoid(ga_ref[0]) * jnp.dot(ya, wa_ref[0], preferred_element_type=F32)
         + jax.nn.sigmoid(gb_ref[0]) * jnp.dot(yb_ref[0].astype(BF16), wb_ref[0], preferred_element_type=F32)
         + jax.nn.sigmoid(gc_ref[0]) * jnp.dot(yc_ref[0].astype(BF16), wc_ref[0], preferred_element_type=F32))
    mix = jnp.dot(y.astype(BF16), wo_ref[0], preferred_element_type=F32)
    x1 = _layer_norm(DEEPNORM_ALPHA * x_ref[0] + g1_ref[0, 0, 0] * mix, lng_ref[0, 0:1], lnb_ref[0, 0:1])
    h = (x1 * (1.0 + sc_ref[0, 0, 0]) + sh_ref[0, 0, 0]).astype(BF16)
    acc = jnp.zeros(x1.shape, F32)
    for j in range(D_FF // ff_chunk):
        u = jnp.dot(h, w1_ref[0, :, j * ff_chunk:(j + 1) * ff_chunk], preferred_element_type=F32)
        u = jnp.square(jnp.maximum(u, 0.0)).astype(BF16)
        acc = acc + jnp.dot(u, w2_ref[0, j * ff_chunk:(j + 1) * ff_chunk, :], preferred_element_type=F32)
    o_ref[0] = _layer_norm(DEEPNORM_ALPHA * x1 + g2_ref[0, 0, 0] * acc, lng_ref[0, 1:2], lnb_ref[0, 1:2])


def _post(yaf, yab, yb, yc, p, x, mod, ctx_row, wa, wb, wc, wo, w1, w2, lng, lnb, layer):
    b, t, d = x.shape
    tm = _tile(t, 512)
    wbr = 512
    row = lambda width, col: pl.BlockSpec((1, tm, width), lambda bi, i: (bi, i, col // width))
    mods = [_mod_spec(d, layer, which, ctx_row, 0) for which in (MOD_G1, MOD_SC2, MOD_SH2, MOD_G2)]
    kern = functools.partial(_post_kernel, ff_chunk=2048)
    return pl.pallas_call(
        kern,
        grid=(b, t // tm),
        in_specs=[
            row(wbr, 0), row(wbr, 0), row(wbr, 0), row(wbr, 0),
            row(d, C_GATE_A), row(d, C_GATE_B), row(d, C_GATE_C), row(d, 0),
            *mods,
            _layer_spec((wbr, d), layer), _layer_spec((wbr, d), layer), _layer_spec((wbr, d), layer),
            _layer_spec((d, d), layer), _layer_spec((d, D_FF), layer), _layer_spec((D_FF, d), layer),
            _layer_spec((2, d), layer), _layer_spec((2, d), layer),
        ],
        out_specs=row(d, 0),
        out_shape=jax.ShapeDtypeStruct((b, t, d), F32),
        compiler_params=_cparams("parallel", "parallel"),
        name="post",
    )(yaf, yab, yb, yc, p, p, p, x, mod, mod, mod, mod, wa, wb, wc, wo, w1, w2, lng, lnb)


def _rope_tables(n_tok, rot_dim):
    t = jnp.arange(n_tok)
    row = (t // GRID_W).astype(F32)
    col = (t % GRID_W).astype(F32)
    n_freq = rot_dim // 4
    inv_freq = ROPE_BASE ** (-2.0 * jnp.arange(n_freq, dtype=F32) / (rot_dim // 2))
    ang = jnp.concatenate([row[:, None] * inv_freq, col[:, None] * inv_freq], axis=-1)
    return jnp.cos(ang), jnp.sin(ang)


def _ret_rope_tables(n_tok):
    cos, sin = _rope_tables(n_tok, RET_DK)
    cos_h = jnp.concatenate([cos, cos], axis=1)
    sin_h = jnp.concatenate([-sin, sin], axis=1)
    return jnp.tile(cos_h, (1, 2)), jnp.tile(sin_h, (1, 2))


def _mla_rope_tables(n_tok):
    cos, sin = _rope_tables(n_tok, MLA_ROPE)
    ones = jnp.ones((n_tok, MLA_NOPE), F32)
    zeros = jnp.zeros((n_tok, MLA_NOPE), F32)
    pad = jnp.zeros((n_tok, MLA_DK_PAD - MLA_NOPE - MLA_ROPE), F32)
    cos_h = jnp.concatenate([ones, cos, cos, pad], axis=1)
    sin_h = jnp.concatenate([zeros, -sin, sin, pad], axis=1)
    return cos_h, sin_h


def _pack_w_in_kernel(w_ref, o_ref):
    w = w_ref[0]
    pad = jnp.zeros((w.shape[0], P_WIDTH - IN_WIDTH), F32)
    o_ref[0] = jnp.concatenate([w[:, :C_GATE_A], w[:, C_GATE_A + MLA_ROPE:], w[:, C_GATE_A:C_GATE_A + MLA_ROPE], pad],
                               axis=1).astype(BF16)


def _pack_w_in(w):
    depth, d, n = w.shape
    assert n == IN_WIDTH
    tr = _tile(d, 256, 8)
    return pl.pallas_call(
        _pack_w_in_kernel,
        grid=(depth, d // tr),
        in_specs=[pl.BlockSpec((1, tr, n), lambda l, i: (l, i, 0))],
        out_specs=pl.BlockSpec((1, tr, P_WIDTH), lambda l, i: (l, i, 0)),
        out_shape=jax.ShapeDtypeStruct((depth, d, P_WIDTH), BF16),
        compiler_params=_cparams("parallel", "parallel"),
        name="pack_w_in",
    )(w)


def _pack_mla_weights(w_qup, w_kvup):
    depth, r = w_qup.shape[:2]
    wq = w_qup.reshape(depth, r, MLA_HEADS, MLA_NOPE + MLA_ROPE)
    half = MLA_ROPE // 2
    wqs = jnp.concatenate([jnp.zeros((depth, r, MLA_HEADS, MLA_NOPE), wq.dtype), wq[..., MLA_NOPE + half:],
                           wq[..., MLA_NOPE:MLA_NOPE + half]], axis=-1)
    pad_q = ((0, 0), (0, 0), (0, 0), (0, MLA_DK_PAD - MLA_NOPE - MLA_ROPE))
    wq = jnp.pad(wq, pad_q).reshape(depth, r, MLA_HEADS * MLA_DK_PAD)
    wqs = jnp.pad(wqs, pad_q).reshape(depth, r, MLA_HEADS * MLA_DK_PAD)
    wkv = w_kvup.reshape(depth, r, MLA_HEADS, MLA_NOPE + MLA_DV)
    wk = jnp.pad(wkv[..., :MLA_NOPE], ((0, 0), (0, 0), (0, 0), (0, MLA_DK_PAD - MLA_NOPE)))
    wk = wk.reshape(depth, r, MLA_HEADS * MLA_DK_PAD)
    wv = wkv[..., MLA_NOPE:].reshape(depth, r, MLA_HEADS // 2, 2 * MLA_DV)
    wv = jnp.pad(wv, ((0, 0), (0, 0), (0, 0), (0, 2 * MLA_DK_PAD - 2 * MLA_DV))).reshape(depth, r, MLA_HEADS * MLA_DK_PAD)
    return wq.astype(BF16), wqs.astype(BF16), wk.astype(BF16), wv.astype(BF16)


def kernel(x, c, ctx, c_ctx, w_ada, b_ada, w_in, ret_log_decay, ret_gn_gain, na_rpb, mla_q_norm, mla_w_qup,
           mla_kv_norm, mla_w_kvup, w_branch_ret, w_branch_na, w_branch_mla, w_out, w_ff1, w_ff2, ln_gain, ln_bias):
    depth = w_ada.shape[0]
    b, t, d = x.shape
    lz = ctx.shape[1]
    rows = t // GRID_W

    cc = jnp.zeros((8, d), F32).at[:b].set(c).at[b].set(c_ctx)
    mod = _ada(cc, w_ada, b_ada).reshape(depth, 8, 6, 1, d)

    cos_r, sin_r = _ret_rope_tables(t)
    cos_m, sin_m = _mla_rope_tables(t)
    cos_m = jnp.concatenate([cos_m, jnp.ones((lz, MLA_DK_PAD), F32)], axis=0)
    sin_m = jnp.concatenate([sin_m, jnp.zeros((lz, MLA_DK_PAD), F32)], axis=0)
    cos_rz, sin_rz = cos_r[:lz], sin_r[:lz]
    na_bias = _na_bias_tables(na_rpb, rows)
    s_zero = jnp.zeros((b, 2, RET_HEADS * RET_DK, RET_DV), F32)

    w_in_p = _pack_w_in(w_in)
    wq, wqs, wk, wv = _pack_mla_weights(mla_w_qup, mla_w_kvup)
    wa = w_branch_ret.astype(BF16)
    wb = w_branch_na.astype(BF16)
    wc = w_branch_mla.astype(BF16)
    wo = w_out.astype(BF16)
    w1 = w_ff1.astype(BF16)
    w2 = w_ff2.astype(BF16)

    lg = jnp.log1p(-jnp.exp(ret_log_decay.astype(F32))).reshape(depth * 2 * RET_HEADS)
    gn_gain = ret_gn_gain.reshape(depth, 1, RET_HEADS * RET_DV)
    qn = mla_q_norm.reshape(depth, 1, MLA_RANK)
    kvn = mla_kv_norm.reshape(depth, 1, MLA_RANK)

    z = ctx
    for l in range(depth):
        need_ctx = l < depth - 1
        px = _inproj(x, mod, None, w_in_p, l)
        pz = _inproj(z, mod, b, w_in_p, l)

        yaf_z, yab_z, s_ctx = _retention(pz, lg, cos_rz, sin_rz, gn_gain, s_zero, False, l)
        yaf_x, yab_x, _ = _retention(px, lg, cos_r, sin_r, gn_gain, s_ctx, True, l)

        yb_x = _na(px, pz, na_bias, l)

        q_all, k_all, v_all = _mlaprep(px, pz, qn, kvn, wq, wqs, wk, wv, cos_m, sin_m, l)
        yc_x = _mla_attn(q_all, k_all, v_all, 0, t, 0, t + lz, 1024, 1408)

        x = _post(yaf_x, yab_x, yb_x, yc_x, px, x, mod, None, wa, wb, wc, wo, w1, w2, ln_gain, ln_bias, l)

        if need_ctx:
            wna = NA_HEADS * NA_DH
            yb_z = _flash(pz, pz, pz, C_NA_Q // wna, C_NA_K // wna, C_NA_V // wna,
                          NA_HEADS, NA_DH, NA_DH, NA_DH ** -0.5 * LOG2_E, 256, 256)
            yc_z = _mla_attn(q_all, k_all, v_all, t, lz, t, lz, lz, lz)
            z = _post(yaf_z, yab_z, yb_z, yc_z, pz, z, mod, b, wa, wb, wc, wo, w1, w2, ln_gain, ln_bias, l)
    return x
```

```python
import functools

import numpy as np
import jax
import jax.numpy as jnp
from jax import lax
from jax.experimental import pallas as pl
from jax.experimental.pallas import tpu as pltpu

F32 = jnp.float32
BF16 = jnp.bfloat16

D_MODEL = 1024
GRID_W = 64
RET_HEADS = 4
RET_DK = 64
RET_DV = 128
RET_CHUNK = 256
NA_HEADS = 8
NA_DH = 64
NA_KH = 8
NA_KW = 16
MLA_HEADS = 8
MLA_RANK = 256
MLA_NOPE = 64
MLA_ROPE = 32
MLA_DV = 64
MLA_DK_PAD = 128
D_FF = 4 * D_MODEL
ROPE_BASE = 10000.0
EPS = 1e-5
DEPTH_FOR_NORM = 4
DEEPNORM_ALPHA = (2 * DEPTH_FOR_NORM) ** 0.25
MASK_VALUE = -1e30
LOG2_E = 1.4426950408889634

C_RET_Q, C_RET_K, C_RET_V, C_RET_GF, C_RET_GB = 0, 256, 512, 1024, 1536
C_NA_Q, C_NA_K, C_NA_V = 2048, 2560, 3072
C_MLA_Q, C_MLA_KV = 3584, 3840
C_GATE_A, C_GATE_B, C_GATE_C = 4096, 5120, 6144
C_MLA_KR = 7168
IN_WIDTH = 7200
P_WIDTH = 7296
P_COL_TILE = 2432

NA_QROWS = 4
NA_KROWS = 12
NA_QB = NA_QROWS * GRID_W
NA_KB = NA_KROWS * GRID_W
NA_STEP_GROUPS = 1

VMEM_LIMIT = 56 * 1024 * 1024


def _cparams(*sem):
    return pltpu.CompilerParams(dimension_semantics=sem, vmem_limit_bytes=VMEM_LIMIT)


def _tile(n, pref, mult=1):
    t = min(n, pref) // mult * mult
    while n % t:
        t -= mult
    return t


MOD_SH1, MOD_SC1, MOD_G1, MOD_SH2, MOD_SC2, MOD_G2 = range(6)


def _mod_spec(d, layer, which, ctx_row, batch_axis):
    def index(*g):
        return (layer, g[batch_axis] if ctx_row is None else ctx_row, which, 0, 0)
    return pl.BlockSpec((1, 1, 1, 1, d), index)


def _layer_spec(shape, layer):
    nd = len(shape)
    return pl.BlockSpec((1,) + tuple(shape), lambda *_: (layer,) + (0,) * nd)


def _ada_kernel(c_ref, w_ref, b_ref, o_ref):
    c = c_ref[...]
    a = c * jax.nn.sigmoid(c)
    o_ref[0] = jnp.dot(a.astype(BF16), w_ref[0].astype(BF16), preferred_element_type=F32) + b_ref[0]


def _ada(cc, w_ada, b_ada):
    depth, d, n = w_ada.shape
    tn = _tile(n, 3072, 128)
    return pl.pallas_call(
        _ada_kernel,
        grid=(depth, n // tn),
        in_specs=[
            pl.BlockSpec((8, d), lambda l, j: (0, 0)),
            pl.BlockSpec((1, d, tn), lambda l, j: (l, 0, j)),
            pl.BlockSpec((1, 1, tn), lambda l, j: (l, 0, j)),
        ],
        out_specs=pl.BlockSpec((1, 8, tn), lambda l, j: (l, 0, j)),
        out_shape=jax.ShapeDtypeStruct((depth, 8, n), F32),
        compiler_params=_cparams("parallel", "parallel"),
        name="ada",
    )(cc, w_ada, b_ada.reshape(depth, 1, n))


def _inproj_kernel(x_ref, sc_ref, sh_ref, w_ref, o_ref):
    h = x_ref[0] * (1.0 + sc_ref[0, 0, 0]) + sh_ref[0, 0, 0]
    o_ref[0] = jnp.dot(h.astype(BF16), w_ref[0], preferred_element_type=F32)


def _inproj(x, mod, ctx_row, w, layer):
    b, t, d = x.shape
    tm = _tile(t, 1024)
    tn = P_COL_TILE
    return pl.pallas_call(
        _inproj_kernel,
        grid=(P_WIDTH // tn, b, t // tm),
        in_specs=[
            pl.BlockSpec((1, tm, d), lambda j, bi, i: (bi, i, 0)),
            _mod_spec(d, layer, MOD_SC1, ctx_row, 1),
            _mod_spec(d, layer, MOD_SH1, ctx_row, 1),
            pl.BlockSpec((1, d, tn), lambda j, bi, i: (layer, 0, j)),
        ],
        out_specs=pl.BlockSpec((1, tm, tn), lambda j, bi, i: (bi, i, j)),
        out_shape=jax.ShapeDtypeStruct((b, t, P_WIDTH), F32),
        compiler_params=_cparams("parallel", "parallel", "parallel"),
        name="inproj",
    )(x, mod, mod, w)


def _swap_halves(x, half):
    n = x.shape[-1]
    lane = lax.broadcasted_iota(jnp.int32, x.shape, x.ndim - 1)
    first = (lane % (2 * half)) < half
    return jnp.where(first, pltpu.roll(x, n - half, x.ndim - 1), pltpu.roll(x, half, x.ndim - 1))


def _ret_kernel(lg_ref, qf_ref, qb_ref, kf_ref, kb_ref, vf_ref, vb_ref, gf_ref, gb_ref,
                cosf_ref, cosb_ref, sinf_ref, sinb_ref, gain_ref, s0_ref,
                yf_ref, yb_ref, sf_ref, s_scr, *, use_rope, n_chunks, layer):
    c = pl.program_id(0)
    batch, cc = qf_ref.shape[:2]
    lg_base = layer * 2 * RET_HEADS

    @pl.when(c == 0)
    def _():
        s_scr[...] = s0_ref[...]

    gain = gain_ref[0]
    row = lax.broadcasted_iota(jnp.int32, (cc, cc), 0).astype(F32)
    col = lax.broadcasted_iota(jnp.int32, (cc, cc), 1).astype(F32)
    wqk = RET_HEADS * RET_DK
    pos = lax.broadcasted_iota(jnp.int32, (cc, wqk), 0).astype(F32)
    head_of_lane = lax.broadcasted_iota(jnp.int32, (cc, wqk), 1) // RET_DK
    dirs = (
        (qf_ref, kf_ref, vf_ref, gf_ref, cosf_ref, sinf_ref, yf_ref, row - col, pos + 1.0, cc - 1.0 - pos),
        (qb_ref, kb_ref, vb_ref, gb_ref, cosb_ref, sinb_ref, yb_ref, col - row, cc - pos, pos),
    )
    for d, (q_ref, k_ref, v_ref, g_ref, cos_ref, sin_ref, y_ref, diff, q_exp, k_exp) in enumerate(dirs):
        if use_rope:
            cos = jnp.concatenate([cos_ref[...], cos_ref[...]], axis=1)
            sin = jnp.concatenate([sin_ref[...], sin_ref[...]], axis=1)
        lgs = [lg_ref[lg_base + d * RET_HEADS + h] for h in range(RET_HEADS)]
        lg_lanes = jnp.full((cc, wqk), lgs[RET_HEADS - 1], F32)
        for h in range(RET_HEADS - 2, -1, -1):
            lg_lanes = jnp.where(head_of_lane == h, lgs[h], lg_lanes)
        q_decay = jnp.exp(lg_lanes * q_exp)
        k_decay = jnp.exp(lg_lanes * k_exp)
        for b in range(batch):
            q = q_ref[b]
            k = k_ref[b] * (RET_DK ** -0.5)
            if use_rope:
                q = q * cos + _swap_halves(q, RET_DK // 2) * sin
                k = k * cos + _swap_halves(k, RET_DK // 2) * sin
            k_bf = k.astype(BF16)
            kt_decayed = (k * k_decay).T.astype(BF16)
            q_decayed = q * q_decay
            state_bf = s_scr[b, d].astype(BF16)
            for h in range(RET_HEADS):
                mine = head_of_lane == h
                sl = slice(h * RET_DV, (h + 1) * RET_DV)
                rows = slice(h * RET_DK, (h + 1) * RET_DK)
                decay = jnp.where(diff >= 0, jnp.exp(lgs[h] * jnp.maximum(diff, 0.0)), 0.0)
                vh = v_ref[b, :, sl].astype(BF16)
                scores = lax.dot_general(jnp.where(mine, q, 0.0).astype(BF16), k_bf, (((1,), (1,)), ((), ())),
                                         preferred_element_type=F32) * decay
                inner = jnp.dot(scores.astype(BF16), vh, preferred_element_type=F32)
                cross = jnp.dot(jnp.where(mine, q_decayed, 0.0).astype(BF16), state_bf, preferred_element_type=F32)
                o = inner + cross
                kv = jnp.dot(kt_decayed[rows], vh, preferred_element_type=F32)
                chunk_decay = jnp.exp(lgs[h] * jnp.full((RET_DK, RET_DV), float(cc), F32))
                s_scr[b, d, rows] = s_scr[b, d, rows] * chunk_decay + kv
                mu = jnp.mean(o, axis=-1, keepdims=True)
                oc = o - mu
                var = jnp.mean(oc * oc, axis=-1, keepdims=True)
                gate = g_ref[b, :, sl]
                y_ref[b, :, sl] = (gate * jax.nn.sigmoid(gate)) * (oc * lax.rsqrt(var + EPS) * gain[:, sl])

    @pl.when(c == n_chunks - 1)
    def _():
        sf_ref[...] = s_scr[...]


def _retention(p, lg, cos, sin, gain, s0, use_rope, layer):
    b, t, _ = p.shape
    cc = _tile(t, RET_CHUNK, 8)
    n = t // cc
    wv = RET_HEADS * RET_DV

    def pspec(width, col, backward):
        if backward:
            return pl.BlockSpec((b, cc, width), lambda ci, lg_: (0, n - 1 - ci, col // width))
        return pl.BlockSpec((b, cc, width), lambda ci, lg_: (0, ci, col // width))

    def tspec(backward):
        if backward:
            return pl.BlockSpec((cc, 128), lambda ci, lg_: (n - 1 - ci, 0))
        return pl.BlockSpec((cc, 128), lambda ci, lg_: (ci, 0))

    state_spec = pl.BlockSpec((b, 2, RET_HEADS * RET_DK, RET_DV), lambda ci, lg_: (0, 0, 0, 0))
    kern = functools.partial(_ret_kernel, use_rope=use_rope, n_chunks=n, layer=layer)
    grid_spec = pltpu.PrefetchScalarGridSpec(
        num_scalar_prefetch=1,
        grid=(n,),
        in_specs=[
            pspec(256, C_RET_Q, False), pspec(256, C_RET_Q, True),
            pspec(256, C_RET_K, False), pspec(256, C_RET_K, True),
            pspec(wv, C_RET_V, False), pspec(wv, C_RET_V, True),
            pspec(wv, C_RET_GF, False), pspec(wv, C_RET_GB, True),
            tspec(False), tspec(True), tspec(False), tspec(True),
            pl.BlockSpec((1, 1, wv), lambda ci, lg_: (layer, 0, 0)),
            state_spec,
        ],
        out_specs=[
            pl.BlockSpec((b, cc, wv), lambda ci, lg_: (0, ci, 0)),
            pl.BlockSpec((b, cc, wv), lambda ci, lg_: (0, n - 1 - ci, 0)),
            state_spec,
        ],
        scratch_shapes=[pltpu.VMEM((b, 2, RET_HEADS * RET_DK, RET_DV), F32)],
    )
    return pl.pallas_call(
        kern,
        grid_spec=grid_spec,
        out_shape=[
            jax.ShapeDtypeStruct((b, t, wv), F32),
            jax.ShapeDtypeStruct((b, t, wv), F32),
            jax.ShapeDtypeStruct((b, 2, RET_HEADS * RET_DK, RET_DV), F32),
        ],
        compiler_params=_cparams("arbitrary"),
        name="retention",
    )(lg, p, p, p, p, p, p, p, p, cos, cos, sin, sin, gain, s0)


def _na_tables(rows):
    groups = rows // NA_QROWS
    cols = np.arange(GRID_W)
    c0 = np.clip(cols - NA_KW // 2, 0, GRID_W - NA_KW)
    col_ok = (cols[None, :] >= c0[:, None]) & (cols[None, :] < c0[:, None] + NA_KW)
    dc = cols[None, :] - cols[:, None] + (NA_KW - 1)
    onehot = (dc[None] == np.arange(2 * NA_KW - 1)[:, None, None]) & col_ok[None]
    row_bias = np.full((3, NA_QROWS, NA_KROWS), 2 * NA_KH - 1, np.int64)
    for ti, g in enumerate((0, 1, groups - 1)):
        ws = int(np.clip(NA_QROWS * g - NA_KH // 2, 0, rows - NA_KROWS))
        for lr in range(NA_QROWS):
            r = NA_QROWS * g + lr
            r0 = int(np.clip(r - NA_KH // 2, 0, rows - NA_KH))
            for kr in range(NA_KROWS):
                if r0 <= ws + kr < r0 + NA_KH:
                    row_bias[ti, lr, kr] = ws + kr - r + (NA_KH - 1)
    return onehot, col_ok, row_bias


def _na_bias_tables(na_rpb, rows):
    onehot, col_ok, row_bias = _na_tables(rows)
    depth, heads = na_rpb.shape[:2]
    nd = 2 * NA_KH
    toep = jnp.einsum('lhdj,jck->lhdck', na_rpb, onehot.astype(np.float32), precision=lax.Precision.HIGHEST)
    toep = jnp.where(col_ok, toep * LOG2_E, MASK_VALUE)
    masked = jnp.full(toep.shape[:2] + (1, GRID_W, GRID_W), MASK_VALUE, F32)
    toep = jnp.concatenate([toep, masked], axis=2)
    sides = jnp.concatenate([jnp.pad(toep, ((0, 0),) * 4 + ((0, GRID_W),)),
                             jnp.pad(toep, ((0, 0),) * 4 + ((GRID_W, 0),))], axis=2)
    pairs = row_bias.reshape(-1, 2)
    pick = ((pairs[:, :1] == np.arange(nd)[None, :]).astype(np.float32),
            (pairs[:, 1:] == np.arange(nd)[None, :]).astype(np.float32))
    pick = np.concatenate(pick, axis=1)
    tab = jnp.einsum('nd,lhdck->lhnck', pick, sides, precision=lax.Precision.HIGHEST)
    return tab.reshape(depth, heads, 3, NA_QROWS * NA_KROWS // 2, GRID_W, 2 * GRID_W)


def _na_kernel(*refs):
    kz_ref, vz_ref, o_ref = refs[-3:]
    lanes = 2 * NA_DH
    npair = NA_KROWS // 2
    low = lax.broadcasted_iota(jnp.int32, (NA_QB, lanes), 1) < NA_DH
    ones_col = jnp.where(lax.broadcasted_iota(jnp.int32, (kz_ref.shape[1], lanes), 1) == 0, 1.0, 0.0).astype(BF16)
    for gi in range(NA_STEP_GROUPS):
        q_ref, k0_ref, k1_ref, k2_ref, v0_ref, v1_ref, v2_ref, bias_ref = refs[8 * gi:8 * gi + 8]
        for pr in range(NA_HEADS // 2):
            sl = slice(pr * lanes, (pr + 1) * lanes)
            qp = q_ref[0, :, sl] * (NA_DH ** -0.5 * LOG2_E)
            qm = (jnp.where(low, qp, 0.0).astype(BF16), jnp.where(low, 0.0, qp).astype(BF16))
            ks = [r[0, :, sl].astype(BF16) for r in (k0_ref, k1_ref, k2_ref, kz_ref)]
            vs = [jnp.concatenate([r[0, :, sl].astype(BF16), ones_col], axis=1)
                  for r in (v0_ref, v1_ref, v2_ref, vz_ref)]
            outs = []
            for hh in range(2):
                h = 2 * pr + hh
                cols = []
                for j in range(4):
                    sj = lax.dot_general(qm[hh], ks[j], (((1,), (1,)), ((), ())), preferred_element_type=F32)
                    halves = [sj[:, i * lanes:(i + 1) * lanes] for i in range(sj.shape[1] // lanes)]
                    if j < 3:
                        halves = [hv + jnp.concatenate([bias_ref[0, h, 0, lr * npair + j * len(halves) + i]
                                                        for lr in range(NA_QROWS)], axis=0)
                                  for i, hv in enumerate(halves)]
                    cols += halves
                m = jnp.max(functools.reduce(jnp.maximum, cols), axis=-1, keepdims=True)
                o = None
                per = len(cols) // 4
                for j in range(4):
                    pj = jnp.concatenate([jnp.exp2(cj - m).astype(BF16) for cj in cols[j * per:(j + 1) * per]], axis=1)
                    oj = jnp.dot(pj, vs[j], preferred_element_type=F32)
                    o = oj if o is None else o + oj
                l = jnp.sum(o[:, lanes:], axis=-1, keepdims=True)
                outs.append(o[:, :lanes] / l)
            o_ref[0, gi * NA_QB:(gi + 1) * NA_QB, sl] = jnp.where(low, outs[0], outs[1]).astype(o_ref.dtype)


def _na(px, pz, bias, layer):
    b, t, _ = px.shape
    lz = pz.shape[1]
    groups = t // NA_QB
    w = NA_HEADS * NA_DH
    kblk = NA_KB // 3
    ng = NA_STEP_GROUPS
    assert lz == kblk and groups >= 3 and groups % ng == 0

    def group_specs(gi):
        def kspec(col, off):
            return pl.BlockSpec((1, kblk, w),
                                lambda bi, s: (bi, jnp.clip(ng * s + gi - 1, 0, groups - 3) + off, col // w))

        def tab(s):
            g = ng * s + gi
            return jnp.where(g == 0, 0, jnp.where(g == groups - 1, 2, 1))

        return [
            pl.BlockSpec((1, NA_QB, w), lambda bi, s: (bi, ng * s + gi, C_NA_Q // w)),
            kspec(C_NA_K, 0), kspec(C_NA_K, 1), kspec(C_NA_K, 2),
            kspec(C_NA_V, 0), kspec(C_NA_V, 1), kspec(C_NA_V, 2),
            pl.BlockSpec((1, NA_HEADS, 1) + bias.shape[3:], lambda bi, s: (layer, 0, tab(s), 0, 0, 0)),
        ]

    in_specs, args = [], []
    for gi in range(ng):
        in_specs += group_specs(gi)
        args += [px] * 7 + [bias]
    in_specs += [pl.BlockSpec((1, lz, w), lambda bi, s: (bi, 0, C_NA_K // w)),
                 pl.BlockSpec((1, lz, w), lambda bi, s: (bi, 0, C_NA_V // w))]
    args += [pz, pz]
    return pl.pallas_call(
        _na_kernel,
        grid=(b, groups // ng),
        in_specs=in_specs,
        out_specs=pl.BlockSpec((1, ng * NA_QB, w), lambda bi, s: (bi, s, 0)),
        out_shape=jax.ShapeDtypeStruct((b, t, w), BF16),
        compiler_params=_cparams("parallel", "arbitrary"),
        name="na",
    )(*args)


def _rms(x, gain):
    return x * lax.rsqrt(jnp.mean(x * x, axis=-1, keepdims=True) + EPS) * gain


def _mlaprep_kernel(xq_ref, xkv_ref, xkr_ref, zq_ref, zkv_ref, zkr_ref, qn_ref, kvn_ref, wq_ref, wqs_ref, wk_ref,
                    wv_ref, cos_ref, sin_ref, q_ref, k_ref, v_ref, *, nx):
    latent = pl.program_id(1) < nx
    pq = jnp.where(latent, xq_ref[0], zq_ref[0])
    pkv = jnp.where(latent, xkv_ref[0], zkv_ref[0])
    pkr = jnp.where(latent, xkr_ref[0], zkr_ref[0])
    hq = _rms(pq, qn_ref[0]).astype(BF16)
    hkv = _rms(pkv, kvn_ref[0]).astype(BF16)
    q = jnp.dot(hq, wq_ref[0], preferred_element_type=F32)
    q_swapped = jnp.dot(hq, wqs_ref[0], preferred_element_type=F32)
    k = jnp.dot(hkv, wk_ref[0], preferred_element_type=F32)
    v = jnp.dot(hkv, wv_ref[0], preferred_element_type=F32)
    kr = pltpu.roll(pkr, MLA_NOPE, 1)
    k = k + jnp.concatenate([kr] * MLA_HEADS, axis=1)
    k_swapped = jnp.concatenate([_swap_halves(kr, MLA_ROPE // 2)] * MLA_HEADS, axis=1)
    cos = jnp.concatenate([cos_ref[...]] * MLA_HEADS, axis=1)
    sin = jnp.concatenate([sin_ref[...]] * MLA_HEADS, axis=1)
    q = q * cos + q_swapped * sin
    k = k * cos + k_swapped * sin
    q_ref[0] = (q * ((MLA_NOPE + MLA_ROPE) ** -0.5 * LOG2_E)).astype(BF16)
    k_ref[0] = k.astype(BF16)
    lane = lax.broadcasted_iota(jnp.int32, v.shape, 1)
    v_ref[0] = jnp.where(lane % (2 * MLA_DK_PAD) == MLA_DK_PAD, 1.0, v).astype(BF16)


def _mlaprep(px, pz, qn, kvn, wq, wqs, wk, wv, cos, sin, layer):
    b, t, _ = px.shape
    lz = pz.shape[1]
    tm = lz
    nx = t // tm
    wqk = MLA_HEADS * MLA_DK_PAD
    wvv = wqk

    def xspec(width, col):
        return pl.BlockSpec((1, tm, width), lambda bi, i: (bi, jnp.minimum(i, nx - 1), col // width))

    def zspec(width, col):
        return pl.BlockSpec((1, tm, width), lambda bi, i: (bi, 0, col // width))

    out_spec = pl.BlockSpec((1, tm, wqk), lambda bi, i: (bi, i, 0))
    kern = functools.partial(_mlaprep_kernel, nx=nx)
    return pl.pallas_call(
        kern,
        grid=(b, nx + 1),
        in_specs=[
            xspec(MLA_RANK, C_MLA_Q), xspec(MLA_RANK, C_MLA_KV), xspec(128, C_MLA_KR),
            zspec(MLA_RANK, C_MLA_Q), zspec(MLA_RANK, C_MLA_KV), zspec(128, C_MLA_KR),
            _layer_spec((1, MLA_RANK), layer), _layer_spec((1, MLA_RANK), layer),
            _layer_spec((MLA_RANK, wqk), layer), _layer_spec((MLA_RANK, wqk), layer),
            _layer_spec((MLA_RANK, wqk), layer), _layer_spec((MLA_RANK, wvv), layer),
            pl.BlockSpec((tm, 128), lambda bi, i: (i, 0)),
            pl.BlockSpec((tm, 128), lambda bi, i: (i, 0)),
        ],
        out_specs=[out_spec, out_spec, pl.BlockSpec((1, tm, wvv), lambda bi, i: (bi, i, 0))],
        out_shape=[jax.ShapeDtypeStruct((b, t + lz, wqk), BF16)] * 2
        + [jax.ShapeDtypeStruct((b, t + lz, wvv), BF16)],
        compiler_params=_cparams("parallel", "parallel"),
        name="mlaprep",
    )(px, px, px, pz, pz, pz, qn, kvn, wq, wqs, wk, wv, cos, sin)


def _flash_kernel(q_ref, k_ref, v_ref, o_ref, m_scr, l_scr, acc_scr, *, heads, dk, dv, scale, nk):
    ki = pl.program_id(2)
    tq = q_ref.shape[1]
    tk = k_ref.shape[1]
    lanes = 2 * dv

    @pl.when(ki == 0)
    def _():
        m_scr[...] = jnp.full(m_scr.shape, -jnp.inf, F32)
        l_scr[...] = jnp.zeros(l_scr.shape, F32)
        acc_scr[...] = jnp.zeros(acc_scr.shape, F32)

    low = lax.broadcasted_iota(jnp.int32, (tq, lanes), 1) < dv
    for pr in range(heads // 2):
        vp = v_ref[0, :, pr * lanes:(pr + 1) * lanes].astype(BF16)
        alphas, pvs = [], []
        for h in (2 * pr, 2 * pr + 1):
            qh = q_ref[0, :, h * dk:(h + 1) * dk]
            if scale != 1.0:
                qh = qh * scale
            kh = k_ref[0, :, h * dk:(h + 1) * dk]
            s = lax.dot_general(qh.astype(BF16), kh.astype(BF16), (((1,), (1,)), ((), ())),
                                preferred_element_type=F32)
            cols = [s[:, j * lanes:(j + 1) * lanes] for j in range(tk // lanes)]
            m_prev = m_scr[h]
            m_tile = jnp.max(functools.reduce(jnp.maximum, cols), axis=-1, keepdims=True)
            m_new = jnp.maximum(m_prev, m_tile)
            alpha = jnp.exp2(m_prev - m_new)
            ps = [jnp.exp2(cj - m_new) for cj in cols]
            l_scr[h] = alpha * l_scr[h] + functools.reduce(jnp.add, ps)
            m_scr[h] = m_new
            p = jnp.concatenate([pj.astype(BF16) for pj in ps], axis=1)
            pvs.append(jnp.dot(p, vp, preferred_element_type=F32))
            alphas.append(alpha)
        acc_scr[pr] = acc_scr[pr] * jnp.where(low, alphas[0], alphas[1]) + jnp.where(low, pvs[0], pvs[1])

    @pl.when(ki == nk - 1)
    def _():
        for pr in range(heads // 2):
            l0 = jnp.sum(l_scr[2 * pr], axis=-1, keepdims=True)
            l1 = jnp.sum(l_scr[2 * pr + 1], axis=-1, keepdims=True)
            o_ref[0, :, pr * lanes:(pr + 1) * lanes] = (acc_scr[pr] / jnp.where(low, l0, l1)).astype(o_ref.dtype)


def _flash(q, k, v, qcol, kcol, vcol, heads, dk, dv, scale, tq_pref, tk_pref):
    b, tq_all, _ = q.shape
    tk_all = k.shape[1]
    tq = _tile(tq_all, tq_pref)
    tk = _tile(tk_all, tk_pref)
    nk = tk_all // tk
    assert heads % 2 == 0 and 2 * dv == 128 and tk % 128 == 0
    kern = functools.partial(_flash_kernel, heads=heads, dk=dk, dv=dv, scale=scale, nk=nk)
    return pl.pallas_call(
        kern,
        grid=(b, tq_all // tq, nk),
        in_specs=[
            pl.BlockSpec((1, tq, heads * dk), lambda bi, i, j: (bi, i, qcol)),
            pl.BlockSpec((1, tk, heads * dk), lambda bi, i, j: (bi, j, kcol)),
            pl.BlockSpec((1, tk, heads * dv), lambda bi, i, j: (bi, j, vcol)),
        ],
        out_specs=pl.BlockSpec((1, tq, heads * dv), lambda bi, i, j: (bi, i, 0)),
        out_shape=jax.ShapeDtypeStruct((b, tq_all, heads * dv), BF16),
        scratch_shapes=[
            pltpu.VMEM((heads, tq, 2 * dv), F32),
            pltpu.VMEM((heads, tq, 2 * dv), F32),
            pltpu.VMEM((heads // 2, tq, 2 * dv), F32),
        ],
        compiler_params=_cparams("parallel", "parallel", "arbitrary"),
        name="flash",
    )(q, k, v)


def _mla_attn_kernel(q_ref, k_ref, v_ref, o_ref, m_scr, l_scr, acc_scr, *, heads, nk):
    ki = pl.program_id(2)
    w = MLA_DK_PAD
    tq = q_ref.shape[1]

    @pl.when(ki == 0)
    def _():
        m_scr[...] = jnp.full(m_scr.shape, -jnp.inf, F32)
        l_scr[...] = jnp.zeros(l_scr.shape, F32)
        acc_scr[...] = jnp.zeros(acc_scr.shape, F32)

    low = lax.broadcasted_iota(jnp.int32, (tq, w), 1) < MLA_DV
    for pr in range(heads // 2):
        vp = v_ref[0, :, pr * 2 * w:(pr + 1) * 2 * w]
        alphas, pvs = [], []
        for h in (2 * pr, 2 * pr + 1):
            sl = slice(h * w, (h + 1) * w)
            s = lax.dot_general(q_ref[0, :, sl], k_ref[0, :, sl], (((1,), (1,)), ((), ())),
                                preferred_element_type=F32)
            cols = [s[:, j * w:(j + 1) * w] for j in range(s.shape[1] // w)]
            m_prev = m_scr[h]
            m_new = jnp.maximum(m_prev, jnp.max(functools.reduce(jnp.maximum, cols), axis=-1, keepdims=True))
            alpha = jnp.exp2(m_prev - m_new)
            m_scr[h] = m_new
            p = jnp.concatenate([jnp.exp2(cj - m_new).astype(BF16) for cj in cols], axis=1)
            pv = jnp.dot(p, vp, preferred_element_type=F32)
            l_scr[h] = alpha * l_scr[h] + pv[:, w:]
            pvs.append(pv[:, :w])
            alphas.append(alpha)
        acc_scr[pr] = acc_scr[pr] * jnp.where(low, alphas[0], alphas[1]) + jnp.where(low, pvs[0], pvs[1])

    @pl.when(ki == nk - 1)
    def _():
        for pr in range(heads // 2):
            l0 = jnp.sum(l_scr[2 * pr], axis=-1, keepdims=True)
            l1 = jnp.sum(l_scr[2 * pr + 1], axis=-1, keepdims=True)
            o_ref[0, :, pr * w:(pr + 1) * w] = (acc_scr[pr] / jnp.where(low, l0, l1)).astype(o_ref.dtype)


def _mla_attn(q, k, v, q_start, q_len, k_start, k_len, tq_pref, tk_pref):
    b, _, wq = q.shape
    heads = wq // MLA_DK_PAD
    tq = _tile(q_len, tq_pref, 8)
    tk = _tile(k_len, tk_pref, 128)
    assert q_start % tq == 0 and k_start % tk == 0
    q_off, k_off = q_start // tq, k_start // tk
    nk = k_len // tk
    kern = functools.partial(_mla_attn_kernel, heads=heads, nk=nk)
    return pl.pallas_call(
        kern,
        grid=(b, q_len // tq, nk),
        in_specs=[
            pl.BlockSpec((1, tq, wq), lambda bi, i, j: (bi, q_off + i, 0)),
            pl.BlockSpec((1, tk, wq), lambda bi, i, j: (bi, k_off + j, 0)),
            pl.BlockSpec((1, tk, heads * MLA_DK_PAD), lambda bi, i, j: (bi, k_off + j, 0)),
        ],
        out_specs=pl.BlockSpec((1, tq, heads * MLA_DV), lambda bi, i, j: (bi, i, 0)),
        out_shape=jax.ShapeDtypeStruct((b, q_len, heads * MLA_DV), BF16),
        scratch_shapes=[
            pltpu.VMEM((heads, tq, MLA_DK_PAD), F32),
            pltpu.VMEM((heads, tq, MLA_DK_PAD), F32),
            pltpu.VMEM((heads // 2, tq, MLA_DK_PAD), F32),
        ],
        compiler_params=_cparams("parallel", "parallel", "arbitrary"),
        name="mla_attn",
    )(q, k, v)


def _layer_norm(r, gain, bias):
    mu = jnp.mean(r, axis=-1, keepdims=True)
    rc = r - mu
    var = jnp.mean(rc * rc, axis=-1, keepdims=True)
    return rc * lax.rsqrt(var + EPS) * gain + bias


def _post_kernel(yaf_ref, yab_ref, yb_ref, yc_ref, ga_ref, gb_ref, gc_ref, x_ref, g1_ref, sc_ref, sh_ref, g2_ref,
                 wa_ref, wb_ref, wc_ref, wo_ref, w1_ref, w2_ref, lng_ref, lnb_ref, o_ref, *, ff_chunk):
    ya = (yaf_ref[0] + yab_ref[0]).astype(BF16)
    y = (jax.nn.sigmoid(ga_ref[0]) * jnp.dot(ya, wa_ref[0], preferred_element_type=F32)
         + jax.nn.sigmoid(gb_ref[0]) * jnp.dot(yb_ref[0].astype(BF16), wb_ref[0], preferred_element_type=F32)
         + jax.nn.sigmoid(gc_ref[0]) * jnp.dot(yc_ref[0].astype(BF16), wc_ref[0], preferred_element_type=F32))
    mix = jnp.dot(y.astype(BF16), wo_ref[0], preferred_element_type=F32)
    x1 = _layer_norm(DEEPNORM_ALPHA * x_ref[0] + g1_ref[0, 0, 0] * mix, lng_ref[0, 0:1], lnb_ref[0, 0:1])
    h = (x1 * (1.0 + sc_ref[0, 0, 0]) + sh_ref[0, 0, 0]).astype(BF16)
    acc = jnp.zeros(x1.shape, F32)
    for j in range(D_FF // ff_chunk):
        u = jnp.dot(h, w1_ref[0, :, j * ff_chunk:(j + 1) * ff_chunk], preferred_element_type=F32)
        u = jnp.square(jnp.maximum(u, 0.0)).astype(BF16)
        acc = acc + jnp.dot(u, w2_ref[0, j * ff_chunk:(j + 1) * ff_chunk, :], preferred_element_type=F32)
    o_ref[0] = _layer_norm(DEEPNORM_ALPHA * x1 + g2_ref[0, 0, 0] * acc, lng_ref[0, 1:2], lnb_ref[0, 1:2])


def _post(yaf, yab, yb, yc, p, x, mod, ctx_row, wa, wb, wc, wo, w1, w2, lng, lnb, layer):
    b, t, d = x.shape
    tm = _tile(t, 512)
    wbr = 512
    row = lambda width, col: pl.BlockSpec((1, tm, width), lambda bi, i: (bi, i, col // width))
    mods = [_mod_spec(d, layer, which, ctx_row, 0) for which in (MOD_G1, MOD_SC2, MOD_SH2, MOD_G2)]
    kern = functools.partial(_post_kernel, ff_chunk=2048)
    return pl.pallas_call(
        kern,
        grid=(b, t // tm),
        in_specs=[
            row(wbr, 0), row(wbr, 0), row(wbr, 0), row(wbr, 0),
            row(d, C_GATE_A), row(d, C_GATE_B), row(d, C_GATE_C), row(d, 0),
            *mods,
            _layer_spec((wbr, d), layer), _layer_spec((wbr, d), layer), _layer_spec((wbr, d), layer),
            _layer_spec((d, d), layer), _layer_spec((d, D_FF), layer), _layer_spec((D_FF, d), layer),
            _layer_spec((2, d), layer), _layer_spec((2, d), layer),
        ],
        out_specs=row(d, 0),
        out_shape=jax.ShapeDtypeStruct((b, t, d), F32),
        compiler_params=_cparams("parallel", "parallel"),
        name="post",
    )(yaf, yab, yb, yc, p, p, p, x, mod, mod, mod, mod, wa, wb, wc, wo, w1, w2, lng, lnb)


def _rope_tables(n_tok, rot_dim):
    t = jnp.arange(n_tok)
    row = (t // GRID_W).astype(F32)
    col = (t % GRID_W).astype(F32)
    n_freq = rot_dim // 4
    inv_freq = ROPE_BASE ** (-2.0 * jnp.arange(n_freq, dtype=F32) / (rot_dim // 2))
    ang = jnp.concatenate([row[:, None] * inv_freq, col[:, None] * inv_freq], axis=-1)
    return jnp.cos(ang), jnp.sin(ang)


def _ret_rope_tables(n_tok):
    cos, sin = _rope_tables(n_tok, RET_DK)
    cos_h = jnp.concatenate([cos, cos], axis=1)
    sin_h = jnp.concatenate([-sin, sin], axis=1)
    return jnp.tile(cos_h, (1, 2)), jnp.tile(sin_h, (1, 2))


def _mla_rope_tables(n_tok):
    cos, sin = _rope_tables(n_tok, MLA_ROPE)
    ones = jnp.ones((n_tok, MLA_NOPE), F32)
    zeros = jnp.zeros((n_tok, MLA_NOPE), F32)
    pad = jnp.zeros((n_tok, MLA_DK_PAD - MLA_NOPE - MLA_ROPE), F32)
    cos_h = jnp.concatenate([ones, cos, cos, pad], axis=1)
    sin_h = jnp.concatenate([zeros, -sin, sin, pad], axis=1)
    return cos_h, sin_h


def _pack_w_in_kernel(w_ref, o_ref):
    w = w_ref[0]
    pad = jnp.zeros((w.shape[0], P_WIDTH - IN_WIDTH), F32)
    o_ref[0] = jnp.concatenate([w[:, :C_GATE_A], w[:, C_GATE_A + MLA_ROPE:], w[:, C_GATE_A:C_GATE_A + MLA_ROPE], pad],
                               axis=1).astype(BF16)


def _pack_w_in(w):
    depth, d, n = w.shape
    assert n == IN_WIDTH
    tr = _tile(d, 256, 8)
    return pl.pallas_call(
        _pack_w_in_kernel,
        grid=(depth, d // tr),
        in_specs=[pl.BlockSpec((1, tr, n), lambda l, i: (l, i, 0))],
        out_specs=pl.BlockSpec((1, tr, P_WIDTH), lambda l, i: (l, i, 0)),
        out_shape=jax.ShapeDtypeStruct((depth, d, P_WIDTH), BF16),
        compiler_params=_cparams("parallel", "parallel"),
        name="pack_w_in",
    )(w)


def _pack_mla_weights(w_qup, w_kvup):
    depth, r = w_qup.shape[:2]
    wq = w_qup.reshape(depth, r, MLA_HEADS, MLA_NOPE + MLA_ROPE)
    half = MLA_ROPE // 2
    wqs = jnp.concatenate([jnp.zeros((depth, r, MLA_HEADS, MLA_NOPE), wq.dtype), wq[..., MLA_NOPE + half:],
                           wq[..., MLA_NOPE:MLA_NOPE + half]], axis=-1)
    pad_q = ((0, 0), (0, 0), (0, 0), (0, MLA_DK_PAD - MLA_NOPE - MLA_ROPE))
    wq = jnp.pad(wq, pad_q).reshape(depth, r, MLA_HEADS * MLA_DK_PAD)
    wqs = jnp.pad(wqs, pad_q).reshape(depth, r, MLA_HEADS * MLA_DK_PAD)
    wkv = w_kvup.reshape(depth, r, MLA_HEADS, MLA_NOPE + MLA_DV)
    wk = jnp.pad(wkv[..., :MLA_NOPE], ((0, 0), (0, 0), (0, 0), (0, MLA_DK_PAD - MLA_NOPE)))
    wk = wk.reshape(depth, r, MLA_HEADS * MLA_DK_PAD)
    wv = wkv[..., MLA_NOPE:].reshape(depth, r, MLA_HEADS // 2, 2 * MLA_DV)
    wv = jnp.pad(wv, ((0, 0), (0, 0), (0, 0), (0, 2 * MLA_DK_PAD - 2 * MLA_DV))).reshape(depth, r, MLA_HEADS * MLA_DK_PAD)
    return wq.astype(BF16), wqs.astype(BF16), wk.astype(BF16), wv.astype(BF16)


def kernel(x, c, ctx, c_ctx, w_ada, b_ada, w_in, ret_log_decay, ret_gn_gain, na_rpb, mla_q_norm, mla_w_qup,
           mla_kv_norm, mla_w_kvup, w_branch_ret, w_branch_na, w_branch_mla, w_out, w_ff1, w_ff2, ln_gain, ln_bias):
    depth = w_ada.shape[0]
    b, t, d = x.shape
    lz = ctx.shape[1]
    rows = t // GRID_W

    cc = jnp.zeros((8, d), F32).at[:b].set(c).at[b].set(c_ctx)
    mod = _ada(cc, w_ada, b_ada).reshape(depth, 8, 6, 1, d)

    cos_r, sin_r = _ret_rope_tables(t)
    cos_m, sin_m = _mla_rope_tables(t)
    cos_m = jnp.concatenate([cos_m, jnp.ones((lz, MLA_DK_PAD), F32)], axis=0)
    sin_m = jnp.concatenate([sin_m, jnp.zeros((lz, MLA_DK_PAD), F32)], axis=0)
    cos_rz, sin_rz = cos_r[:lz], sin_r[:lz]
    na_bias = _na_bias_tables(na_rpb, rows)
    s_zero = jnp.zeros((b, 2, RET_HEADS * RET_DK, RET_DV), F32)

    w_in_p = _pack_w_in(w_in)
    wq, wqs, wk, wv = _pack_mla_weights(mla_w_qup, mla_w_kvup)
    wa = w_branch_ret.astype(BF16)
    wb = w_branch_na.astype(BF16)
    wc = w_branch_mla.astype(BF16)
    wo = w_out.astype(BF16)
    w1 = w_ff1.astype(BF16)
    w2 = w_ff2.astype(BF16)

    lg = jnp.log1p(-jnp.exp(ret_log_decay.astype(F32))).reshape(depth * 2 * RET_HEADS)
    gn_gain = ret_gn_gain.reshape(depth, 1, RET_HEADS * RET_DV)
    qn = mla_q_norm.reshape(depth, 1, MLA_RANK)
    kvn = mla_kv_norm.reshape(depth, 1, MLA_RANK)

    z = ctx
    for l in range(depth):
        need_ctx = l < depth - 1
        px = _inproj(x, mod, None, w_in_p, l)
        pz = _inproj(z, mod, b, w_in_p, l)

        yaf_z, yab_z, s_ctx = _retention(pz, lg, cos_rz, sin_rz, gn_gain, s_zero, False, l)
        yaf_x, yab_x, _ = _retention(px, lg, cos_r, sin_r, gn_gain, s_ctx, True, l)

        yb_x = _na(px, pz, na_bias, l)

        q_all, k_all, v_all = _mlaprep(px, pz, qn, kvn, wq, wqs, wk, wv, cos_m, sin_m, l)
        yc_x = _mla_attn(q_all, k_all, v_all, 0, t, 0, t + lz, 512, 2816)

        x = _post(yaf_x, yab_x, yb_x, yc_x, px, x, mod, None, wa, wb, wc, wo, w1, w2, ln_gain, ln_bias, l)

        if need_ctx:
            wna = NA_HEADS * NA_DH
            yb_z = _flash(pz, pz, pz, C_NA_Q // wna, C_NA_K // wna, C_NA_V // wna,
                          NA_HEADS, NA_DH, NA_DH, NA_DH ** -0.5 * LOG2_E, 256, 256)
            yc_z = _mla_attn(q_all, k_all, v_all, t, lz, t, lz, lz, lz)
            z = _post(yaf_z, yab_z, yb_z, yc_z, pz, z, mod, b, wa, wb, wc, wo, w1, w2, ln_gain, ln_bias, l)
    return x
```
